```python
import math
import jax, jax.numpy as jnp
from jax import lax
import numpy as np

D_MODEL = 1024
BATCH = 8
SEQ = 2048
DEPTH = 1

D_MIX = D_MODEL
S5_WIDTH = D_MIX // 2
S5_GROUP = 16
S5_GROUPS = S5_WIDTH // S5_GROUP
S5_STATE = 64
GDN_HEADS = 4
GDN_HEAD_DIM = 128
GDN_WIDTH = GDN_HEADS * GDN_HEAD_DIM
GDN_CONV = 5
GDN_CHUNK = 64
MEM_LEN = 256
XA_HEADS = 4
XA_HEAD_DIM = D_MODEL // XA_HEADS
MOE_GROUPS = 4
MOE_PER_GROUP = 8
MOE_EXPERTS = MOE_GROUPS * MOE_PER_GROUP
MOE_TOPK = 2
D_EXPERT = D_MODEL // 4
RMS_EPS = 1e-6
L2_EPS = 1e-6
IN_SPLITS = (S5_WIDTH, GDN_WIDTH, GDN_WIDTH, GDN_WIDTH, GDN_WIDTH, GDN_HEADS, GDN_HEADS, GDN_HEADS, GDN_HEADS)
D_IN = S5_WIDTH + 4 * GDN_WIDTH + 4 * GDN_HEADS
F32 = jnp.float32

kernel_name = 'hybrid_s5_gdn_memxattn_hiermoe_encoder'


def _rmsnorm(x, gain):
    xf = x.astype(F32)
    y = xf * lax.rsqrt(jnp.mean(xf * xf, axis=-1, keepdims=True) + RMS_EPS)
    return (y * gain.astype(F32)).astype(x.dtype)


def _l2norm(x):
    return x * lax.rsqrt(jnp.sum(x * x, axis=-1, keepdims=True) + L2_EPS)


def _split_cols(h):
    parts, off = [], 0
    for w in IN_SPLITS:
        parts.append(h[..., off:off + w])
        off += w
    return parts


def _s5_scan(u, lam_re, lam_im, log_step, b_re, b_im, c_re, c_im, reverse):
    lam = lax.complex(lam_re.astype(F32), lam_im.astype(F32))
    step = jnp.exp(log_step.astype(F32))[:, None]
    lam_bar = jnp.exp(lam * step)
    b = lax.complex(b_re.astype(F32), b_im.astype(F32))
    b_bar = ((lam_bar - 1.0) / lam)[:, :, None] * b
    bu = jnp.einsum('gpc,bsgc->bsgp', b_bar, u.astype(jnp.complex64))
    a = jnp.broadcast_to(lam_bar[None, None], (1, u.shape[1]) + lam_bar.shape)

    def combine(e1, e2):
        a1, b1 = e1
        a2, b2 = e2
        return a1 * a2, a2 * b1 + b2

    _, h = lax.associative_scan(combine, (a, bu), reverse=reverse, axis=1)
    return (jnp.einsum('bsgp,gcp->bsgc', h.real, c_re.astype(F32))
            - jnp.einsum('bsgp,gcp->bsgc', h.imag, c_im.astype(F32)))


def _s5_mixer(u, lam_re_f, lam_im_f, log_step_f, b_re_f, b_im_f, c_re_f, c_im_f,
              lam_re_b, lam_im_b, log_step_b, b_re_b, b_im_b, c_re_b, c_im_b,
              d, w_glu, b_glu, norm_w):
    bsz, s, _ = u.shape
    uf = u.astype(F32)
    ug = uf.reshape(bsz, s, S5_GROUPS, S5_GROUP)
    y = (_s5_scan(ug, lam_re_f, lam_im_f, log_step_f, b_re_f, b_im_f, c_re_f, c_im_f, False)
         + _s5_scan(ug, lam_re_b, lam_im_b, log_step_b, b_re_b, b_im_b, c_re_b, c_im_b, True))
    y = y.reshape(bsz, s, S5_WIDTH) + d.astype(F32) * uf
    g = jax.nn.gelu(y, approximate=False)
    g = g * jax.nn.sigmoid(g @ w_glu.astype(F32) + b_glu.astype(F32))
    return _rmsnorm(g, norm_w).astype(u.dtype)


def _centred_depthwise_conv(x, w):
    c = x.shape[-1]
    pad = (GDN_CONV - 1) // 2
    return lax.conv_general_dilated(x, w[:, None, :].astype(x.dtype), window_strides=(1,),
                                    padding=[(pad, pad)], dimension_numbers=('NWC', 'WIO', 'NWC'),
                                    feature_group_count=c)


def _gated_delta_chunked(q, k, v, g, beta):
    bsz, s, nh, dk = q.shape
    dv = v.shape[-1]
    n = s // GDN_CHUNK
    q = q * (dk ** -0.5)

    def chunks(t):
        return jnp.transpose(t.reshape(bsz, n, GDN_CHUNK, nh, t.shape[-1]), (0, 3, 1, 2, 4))

    qc, kc, vc = chunks(q), chunks(k), chunks(v)
    bc = chunks(beta[..., None])
    gc = jnp.cumsum(chunks(g[..., None])[..., 0], axis=-1)
    idx = jnp.arange(GDN_CHUNK)
    lower = idx[:, None] >= idx[None, :]
    strict = idx[:, None] > idx[None, :]
    diff = gc[..., :, None] - gc[..., None, :]
    decay = jnp.where(lower, jnp.exp(jnp.where(lower, diff, 0.0)), 0.0)
    k_beta = kc * bc
    v_beta = vc * bc
    l_mat = jnp.where(strict, jnp.einsum('bhncd,bhnsd->bhncs', k_beta, kc) * decay, 0.0)
    u_c = lax.linalg.triangular_solve(l_mat, v_beta, left_side=True, lower=True, unit_diagonal=True)
    w_c = lax.linalg.triangular_solve(l_mat, k_beta * jnp.exp(gc)[..., None], left_side=True,
                                      lower=True, unit_diagonal=True)
    qk = jnp.where(lower, jnp.einsum('bhncd,bhnsd->bhncs', qc, kc) * decay, 0.0)

    def step(state, inp):
        q_i, k_i, u_i, w_i, g_i, qk_i = inp
        v_new = u_i - jnp.einsum('bhcd,bhdv->bhcv', w_i, state)
        o_i = (jnp.einsum('bhcd,bhdv->bhcv', q_i * jnp.exp(g_i)[..., None], state)
               + jnp.einsum('bhcs,bhsv->bhcv', qk_i, v_new))
        g_last = g_i[..., -1]
        state = (state * jnp.exp(g_last)[..., None, None]
                 + jnp.einsum('bhcd,bhcv->bhdv', k_i * jnp.exp(g_last[..., None] - g_i)[..., None], v_new))
        return state, o_i

    xs = tuple(jnp.moveaxis(t, 2, 0) for t in (qc, kc, u_c, w_c, gc, qk))
    state0 = jnp.zeros((bsz, nh, dk, dv), F32)
    _, o = lax.scan(step, state0, xs)
    return jnp.transpose(o, (1, 0, 3, 2, 4)).reshape(bsz, s, nh, dv)


def _gdn_mixer(q, k, v, z, beta_f, beta_b, a_f, a_b, conv_w, a_log_f, dt_bias_f, a_log_b, dt_bias_b, norm_w):
    bsz, s, _ = q.shape
    qkv = jax.nn.silu(_centred_depthwise_conv(jnp.concatenate([q, k, v], axis=-1), conv_w))
    qkv = qkv.astype(F32).reshape(bsz, s, 3, GDN_HEADS, GDN_HEAD_DIM)
    qh = _l2norm(qkv[:, :, 0])
    kh = _l2norm(qkv[:, :, 1])
    vh = qkv[:, :, 2]

    def log_decay(a, a_log, dt_bias):
        return -jnp.exp(a_log.astype(F32)) * jax.nn.softplus(a.astype(F32) + dt_bias.astype(F32))

    g_f = log_decay(a_f, a_log_f, dt_bias_f)
    g_b = log_decay(a_b, a_log_b, dt_bias_b)
    bt_f = jax.nn.sigmoid(beta_f.astype(F32))
    bt_b = jax.nn.sigmoid(beta_b.astype(F32))
    flip = lambda t: jnp.flip(t, axis=1)
    o = (_gated_delta_chunked(qh, kh, vh, g_f, bt_f)
         + flip(_gated_delta_chunked(flip(qh), flip(kh), flip(vh), flip(g_b), flip(bt_b))))
    o = o * lax.rsqrt(jnp.mean(o * o, axis=-1, keepdims=True) + RMS_EPS) * norm_w.astype(F32)
    o = o * jax.nn.silu(z.astype(F32).reshape(bsz, s, GDN_HEADS, GDN_HEAD_DIM))
    return o.reshape(bsz, s, GDN_WIDTH).astype(q.dtype)


def _mem_xattn(xn, memn, wq, wk, wv, wo):
    bsz, s, _ = xn.shape
    m = memn.shape[1]
    q = (xn @ wq).reshape(bsz, s, XA_HEADS, XA_HEAD_DIM)
    k = (memn @ wk).reshape(bsz, m, XA_HEADS, XA_HEAD_DIM)
    v = (memn @ wv).reshape(bsz, m, XA_HEADS, XA_HEAD_DIM)
    sc = jnp.einsum('bshd,bmhd->bhsm', q, k).astype(F32) * (XA_HEAD_DIM ** -0.5)
    p = jax.nn.softmax(sc, axis=-1).astype(v.dtype)
    o = jnp.einsum('bhsm,bmhd->bshd', p, v).reshape(bsz, s, XA_HEADS * XA_HEAD_DIM)
    return o @ wo


def _hier_moe(xn, wg, bg, we, be, w_gate, w_up, w_down):
    bsz, s, d = xn.shape
    t = xn.reshape(bsz * s, d)
    g_logits = (t @ wg).astype(F32) + bg.astype(F32)
    p_top, g_idx = lax.top_k(jax.nn.softmax(g_logits, axis=-1), 1)
    e_logits = ((t @ we).astype(F32) + be.astype(F32)).reshape(-1, MOE_GROUPS, MOE_PER_GROUP)
    e_sel = jnp.take_along_axis(e_logits, g_idx[:, :, None], axis=1)[:, 0]
    e_val, e_idx = lax.top_k(e_sel, MOE_TOPK)
    e_w = jax.nn.softmax(e_val, axis=-1) * p_top
    within = jnp.einsum('tk,tke->te', e_w, jax.nn.one_hot(e_idx, MOE_PER_GROUP, dtype=F32))
    combine = (jax.nn.one_hot(g_idx[:, 0], MOE_GROUPS, dtype=F32)[:, :, None]
               * within[:, None, :]).astype(xn.dtype)
    out = jnp.zeros_like(t)
    for grp in range(MOE_GROUPS):
        sl = slice(grp * MOE_PER_GROUP, (grp + 1) * MOE_PER_GROUP)
        h = (jax.nn.silu(jnp.einsum('td,edf->tef', t, w_gate[sl]))
             * jnp.einsum('td,edf->tef', t, w_up[sl]))
        h = h * combine[:, grp][..., None]
        out = out + jnp.einsum('tef,efd->td', h, w_down[sl])
    return out.reshape(bsz, s, d)


def setup_inputs(seed: int = 0) -> dict:
    key = jax.random.key(seed)
    k = jax.random.split(key, 48)
    L = DEPTH

    def nrm(i, shape, scale):
        return jax.random.normal(k[i], shape, F32) * scale

    def gain(i, shape):
        return 1.0 + nrm(i, shape, 0.02)

    def uni(i, shape, lo, hi):
        return jax.random.uniform(k[i], shape, F32, minval=lo, maxval=hi)

    n_idx = jnp.arange(S5_STATE, dtype=F32)
    dt_f = jnp.exp(uni(25, (L, GDN_HEADS), math.log(1e-3), math.log(1e-1)))
    dt_b = jnp.exp(uni(27, (L, GDN_HEADS), math.log(1e-3), math.log(1e-1)))
    return {
        'x': nrm(0, (BATCH, SEQ, D_MODEL), 1.0),
        'mem': nrm(1, (BATCH, MEM_LEN, D_MODEL), 1.0),
        'norm_mix': gain(2, (L, D_MODEL)),
        'w_in': nrm(3, (L, D_MODEL, D_IN), D_MODEL ** -0.5),
        'w_out': nrm(4, (L, D_MIX, D_MODEL), D_MIX ** -0.5),
        's5_lam_re_f': -0.5 + nrm(5, (L, S5_GROUPS, S5_STATE), 0.01),
        's5_lam_im_f': math.pi * n_idx + nrm(6, (L, S5_GROUPS, S5_STATE), 0.01),
        's5_log_step_f': uni(7, (L, S5_GROUPS), math.log(1e-3), math.log(1e-1)),
        's5_b_re_f': nrm(8, (L, S5_GROUPS, S5_STATE, S5_GROUP), (2 * S5_GROUP) ** -0.5),
        's5_b_im_f': nrm(9, (L, S5_GROUPS, S5_STATE, S5_GROUP), (2 * S5_GROUP) ** -0.5),
        's5_c_re_f': nrm(10, (L, S5_GROUPS, S5_GROUP, S5_STATE), S5_STATE ** -0.5),
        's5_c_im_f': nrm(11, (L, S5_GROUPS, S5_GROUP, S5_STATE), S5_STATE ** -0.5),
        's5_lam_re_b': -0.5 + nrm(12, (L, S5_GROUPS, S5_STATE), 0.01),
        's5_lam_im_b': math.pi * n_idx + nrm(13, (L, S5_GROUPS, S5_STATE), 0.01),
        's5_log_step_b': uni(14, (L, S5_GROUPS), math.log(1e-3), math.log(1e-1)),
        's5_b_re_b': nrm(15, (L, S5_GROUPS, S5_STATE, S5_GROUP), (2 * S5_GROUP) ** -0.5),
        's5_b_im_b': nrm(16, (L, S5_GROUPS, S5_STATE, S5_GROUP), (2 * S5_GROUP) ** -0.5),
        's5_c_re_b': nrm(17, (L, S5_GROUPS, S5_GROUP, S5_STATE), S5_STATE ** -0.5),
        's5_c_im_b': nrm(18, (L, S5_GROUPS, S5_GROUP, S5_STATE), S5_STATE ** -0.5),
        's5_d': nrm(19, (L, S5_WIDTH), 1.0),
        's5_w_glu': nrm(20, (L, S5_WIDTH, S5_WIDTH), S5_WIDTH ** -0.5),
        's5_b_glu': nrm(21, (L, S5_WIDTH), 0.02),
        's5_norm': gain(22, (L, S5_WIDTH)),
        'gdn_conv': nrm(23, (L, GDN_CONV, 3 * GDN_WIDTH), GDN_CONV ** -0.5),
        'gdn_a_log_f': jnp.log(uni(24, (L, GDN_HEADS), 1.0, 16.0)),
        'gdn_dt_bias_f': dt_f + jnp.log(-jnp.expm1(-dt_f)),
        'gdn_a_log_b': jnp.log(uni(26, (L, GDN_HEADS), 1.0, 16.0)),
        'gdn_dt_bias_b': dt_b + jnp.log(-jnp.expm1(-dt_b)),
        'gdn_norm': gain(28, (L, GDN_HEAD_DIM)),
        'norm_xattn': gain(29, (L, D_MODEL)),
        'norm_mem': gain(30, (L, D_MODEL)),
        'xa_wq': nrm(31, (L, D_MODEL, XA_HEADS * XA_HEAD_DIM), D_MODEL ** -0.5),
        'xa_wk': nrm(32, (L, D_MODEL, XA_HEADS * XA_HEAD_DIM), D_MODEL ** -0.5),
        'xa_wv': nrm(33, (L, D_MODEL, XA_HEADS * XA_HEAD_DIM), D_MODEL ** -0.5),
        'xa_wo': nrm(34, (L, XA_HEADS * XA_HEAD_DIM, D_MODEL), D_MODEL ** -0.5),
        'norm_moe': gain(35, (L, D_MODEL)),
        'router_group_w': nrm(36, (L, D_MODEL, MOE_GROUPS), D_MODEL ** -0.5),
        'router_group_b': nrm(37, (L, MOE_GROUPS), 0.01),
        'router_expert_w': nrm(38, (L, D_MODEL, MOE_EXPERTS), D_MODEL ** -0.5),
        'router_expert_b': nrm(39, (L, MOE_EXPERTS), 0.01),
        'moe_w_gate': nrm(40, (L, MOE_EXPERTS, D_MODEL, D_EXPERT), D_MODEL ** -0.5),
        'moe_w_up': nrm(41, (L, MOE_EXPERTS, D_MODEL, D_EXPERT), D_MODEL ** -0.5),
        'moe_w_down': nrm(42, (L, MOE_EXPERTS, D_EXPERT, D_MODEL), D_EXPERT ** -0.5),
        'norm_final': gain(43, (D_MODEL,)),
    }


def reference(x, mem, norm_mix, w_in, w_out,
              s5_lam_re_f, s5_lam_im_f, s5_log_step_f, s5_b_re_f, s5_b_im_f, s5_c_re_f, s5_c_im_f,
              s5_lam_re_b, s5_lam_im_b, s5_log_step_b, s5_b_re_b, s5_b_im_b, s5_c_re_b, s5_c_im_b,
              s5_d, s5_w_glu, s5_b_glu, s5_norm,
              gdn_conv, gdn_a_log_f, gdn_dt_bias_f, gdn_a_log_b, gdn_dt_bias_b, gdn_norm,
              norm_xattn, norm_mem, xa_wq, xa_wk, xa_wv, xa_wo,
              norm_moe, router_group_w, router_group_b, router_expert_w, router_expert_b,
              moe_w_gate, moe_w_up, moe_w_down, norm_final):
    for l in range(DEPTH):
        h = _rmsnorm(x, norm_mix[l])
        u, q, k, v, z, beta_f, beta_b, a_f, a_b = _split_cols(h @ w_in[l])
        y_s5 = _s5_mixer(u, s5_lam_re_f[l], s5_lam_im_f[l], s5_log_step_f[l], s5_b_re_f[l], s5_b_im_f[l],
                         s5_c_re_f[l], s5_c_im_f[l], s5_lam_re_b[l], s5_lam_im_b[l], s5_log_step_b[l],
                         s5_b_re_b[l], s5_b_im_b[l], s5_c_re_b[l], s5_c_im_b[l],
                         s5_d[l], s5_w_glu[l], s5_b_glu[l], s5_norm[l])
        y_gdn = _gdn_mixer(q, k, v, z, beta_f, beta_b, a_f, a_b, gdn_conv[l], gdn_a_log_f[l], gdn_dt_bias_f[l],
                           gdn_a_log_b[l], gdn_dt_bias_b[l], gdn_norm[l])
        x = x + jnp.concatenate([y_s5, y_gdn], axis=-1) @ w_out[l]
        x = x + _mem_xattn(_rmsnorm(x, norm_xattn[l]), _rmsnorm(mem, norm_mem[l]),
                           xa_wq[l], xa_wk[l], xa_wv[l], xa_wo[l])
        x = x + _hier_moe(_rmsnorm(x, norm_moe[l]), router_group_w[l], router_group_b[l],
                          router_expert_w[l], router_expert_b[l], moe_w_gate[l], moe_w_up[l], moe_w_down[l])
    return _rmsnorm(x, norm_final)
```

```python
import functools
import math

import jax
import jax.numpy as jnp
from jax import lax
from jax.experimental import pallas as pl
from jax.experimental.pallas import tpu as pltpu

F32 = jnp.float32
BF16 = jnp.bfloat16
I32 = jnp.int32

D_MODEL = 1024
S5_WIDTH = 512
S5_GROUP = 16
S5_GROUPS = 32
S5_STATE = 64
S5_CHUNK = 128
GDN_HEADS = 4
GDN_HEAD_DIM = 128
GDN_WIDTH = 512
GDN_CONV = 5
GDN_CHUNK = 64
XA_HEADS = 4
XA_HEAD_DIM = 256
MOE_GROUPS = 4
MOE_PER_GROUP = 8
MOE_EXPERTS = 32
D_EXPERT = 256
RMS_EPS = 1e-6
L2_EPS = 1e-6
LANES = 128
VMEM_LIMIT = 56 * 1024 * 1024


def _cparams(*sem):
    return pltpu.CompilerParams(dimension_semantics=tuple(sem), vmem_limit_bytes=VMEM_LIMIT)


def _rms(x, gain):
    return x * lax.rsqrt(jnp.mean(x * x, axis=-1, keepdims=True) + RMS_EPS) * gain


def _dot(a, b):
    return jnp.dot(a, b, preferred_element_type=F32)


def _dot_nt(a, b):
    return lax.dot_general(a, b, (((1,), (1,)), ((), ())), preferred_element_type=F32)


def _dot_tn(a, b):
    return lax.dot_general(a, b, (((0,), (0,)), ((), ())), preferred_element_type=F32)


def _in_proj_body(x_ref, g_ref, wut_ref, wqkvz_ref, wgt_ref, u_ref, qkvz_ref, gates_ref):
    h = _rms(x_ref[...], g_ref[...]).astype(BF16)
    ut = _dot_nt(wut_ref[...], h)
    for j in range(u_ref.shape[0]):
        u_ref[j] = ut[:, j * S5_CHUNK:(j + 1) * S5_CHUNK]
    qkvz_ref[...] = _dot(h, wqkvz_ref[...])
    gates_ref[...] = _dot(h, wgt_ref[...])


def _in_proj(x2d, norm_w, wut, wqkvz, wgt, tm=512):
    t = x2d.shape[0]
    nck = tm // S5_CHUNK
    full = lambda shape: pl.BlockSpec(shape, lambda i: (0,) * len(shape))
    return pl.pallas_call(
        _in_proj_body,
        grid=(t // tm,),
        in_specs=[pl.BlockSpec((tm, D_MODEL), lambda i: (i, 0)),
                  full((1, D_MODEL)), full(wut.shape), full(wqkvz.shape), full(wgt.shape)],
        out_specs=[pl.BlockSpec((nck, S5_WIDTH, S5_CHUNK), lambda i: (i, 0, 0)),
                   pl.BlockSpec((tm, wqkvz.shape[1]), lambda i: (i, 0)),
                   pl.BlockSpec((tm, LANES), lambda i: (i, 0))],
        out_shape=[jax.ShapeDtypeStruct((t // S5_CHUNK, S5_WIDTH, S5_CHUNK), F32),
                   jax.ShapeDtypeStruct((t, wqkvz.shape[1]), F32),
                   jax.ShapeDtypeStruct((t, LANES), F32)],
        compiler_params=_cparams("parallel"),
        name="in_proj",
    )(x2d, norm_w, wut, wqkvz, wgt)


def _cpow(re_col, im_col, step_col, expo):
    mag = jnp.exp(expo * (step_col * re_col))
    ang = expo * (step_col * im_col)
    return mag * jnp.cos(ang), mag * jnp.sin(ang)


def _zoh_coef(re, im, step):
    er = jnp.exp(step * re)
    lr = er * jnp.cos(step * im) - 1.0
    li = er * jnp.sin(step * im)
    den = re * re + im * im
    return (lr * re + li * im) / den, (li * re - lr * im) / den


def _s5_body(u_ref, lrow_ref, lcol_ref, l256_ref, b_ref, bt_ref, c_ref, ct_ref, d_ref,
             o_ref, vf_ref, vb_ref, m_ref, win_ref, wout_ref, sf_ref, sb_ref, hf_ref, hb_ref, *, nchunk):
    L = S5_CHUNK
    P = S5_STATE
    n_rows = u_ref.shape[0]
    nb = n_rows // nchunk
    lane = lax.broadcasted_iota(I32, (1, L), 1).astype(F32)

    lcol = lcol_ref[...]
    ref_c, imf_c, stf_c = lcol[:, 0:1], lcol[:, 1:2], lcol[:, 2:3]
    reb_c, imb_c, stb_c = lcol[:, 3:4], lcol[:, 4:5], lcol[:, 5:6]
    lrow = lrow_ref[...]
    ref_r, imf_r, stf_r = lrow[0:1], lrow[1:2], lrow[2:3]
    reb_r, imb_r, stb_r = lrow[3:4], lrow[4:5], lrow[5:6]

    pwf_r, pwf_i = _cpow(ref_c, imf_c, stf_c, lane)
    rvf_r, rvf_i = _cpow(ref_c, imf_c, stf_c, (L - 1.0) - lane)
    pwb_r, pwb_i = _cpow(reb_c, imb_c, stb_c, lane)
    rvb_r, rvb_i = _cpow(reb_c, imb_c, stb_c, float(L) - lane)
    lf_r, lf_i = _cpow(ref_c, imf_c, stf_c, 1.0)
    nxf_r = pwf_r * lf_r - pwf_i * lf_i
    nxf_i = pwf_r * lf_i + pwf_i * lf_r

    kfr_c, kfi_c = _zoh_coef(ref_c, imf_c, stf_c)
    kbr_c, kbi_c = _zoh_coef(reb_c, imb_c, stb_c)
    kfr_r, kfi_r = _zoh_coef(ref_r, imf_r, stf_r)
    kbr_r, kbi_r = _zoh_coef(reb_r, imb_r, stb_r)
    bf_r = kfr_c * b_ref[0] - kfi_c * b_ref[1]
    bf_i = kfr_c * b_ref[1] + kfi_c * b_ref[0]
    bb_r = kbr_c * b_ref[2] - kbi_c * b_ref[3]
    bb_i = kbr_c * b_ref[3] + kbi_c * b_ref[2]
    btf_r = kfr_r * bt_ref[0] - kfi_r * bt_ref[1]
    btf_i = kfr_r * bt_ref[1] + kfi_r * bt_ref[0]
    btb_r = kbr_r * bt_ref[2] - kbi_r * bt_ref[3]
    btb_i = kbr_r * bt_ref[3] + kbi_r * bt_ref[2]

    def taps(c_r, c_i, bt_r, bt_i, pw_r, pw_i):
        cb_r = (bt_r[:, None, :] * c_r[None, :, :] - bt_i[:, None, :] * c_i[None, :, :]).reshape(256, P)
        cb_i = (bt_r[:, None, :] * c_i[None, :, :] + bt_i[:, None, :] * c_r[None, :, :]).reshape(256, P)
        k = (jnp.dot(cb_r, pw_r, preferred_element_type=F32, precision=lax.Precision.HIGHEST)
             - jnp.dot(cb_i, pw_i, preferred_element_type=F32, precision=lax.Precision.HIGHEST))
        return k, jnp.sum(cb_r, axis=1, keepdims=True)

    kf, _ = taps(c_ref[0], c_ref[1], btf_r, btf_i, pwf_r, pwf_i)
    kb, kb0 = taps(c_ref[2], c_ref[3], btb_r, btb_i, rvb_r, rvb_i)
    is0 = lane == 0.0
    vf_ref[...] = kf + jnp.where(is0, kb0, 0.0)
    vb_ref[...] = jnp.where(is0, 0.0, kb)

    row_i = lax.broadcasted_iota(I32, (L, L), 0)
    col_i = lax.broadcasted_iota(I32, (L, L), 1)
    upper = col_i >= row_i

    def build_ci(ci, carry):
        for co in range(S5_GROUP):
            r = ci * S5_GROUP + co
            tf = pltpu.roll(jnp.broadcast_to(vf_ref[pl.ds(r, 1), :], (L, L)), 0, 1, stride=1, stride_axis=0)
            tb = pltpu.roll(jnp.broadcast_to(vb_ref[pl.ds(r, 1), :], (L, L)), 0, 1, stride=1, stride_axis=0)
            m_ref[pl.ds(pl.multiple_of(ci * L, L), L), co * L:(co + 1) * L] = jnp.where(upper, tf, tb).astype(BF16)
        return carry

    lax.fori_loop(0, S5_GROUP, build_ci, 0)

    for ci in range(S5_GROUP):
        sl = slice(ci * L, (ci + 1) * L)
        br, bi = bf_r[:, ci:ci + 1], bf_i[:, ci:ci + 1]
        win_ref[0 * P:1 * P, sl] = (rvf_r * br - rvf_i * bi).astype(BF16)
        win_ref[1 * P:2 * P, sl] = (rvf_r * bi + rvf_i * br).astype(BF16)
        br, bi = bb_r[:, ci:ci + 1], bb_i[:, ci:ci + 1]
        win_ref[2 * P:3 * P, sl] = (pwb_r * br - pwb_i * bi).astype(BF16)
        win_ref[3 * P:4 * P, sl] = (pwb_r * bi + pwb_i * br).astype(BF16)
    for co in range(S5_GROUP):
        sl = slice(co * L, (co + 1) * L)
        cr, ci_ = ct_ref[0][:, co:co + 1], ct_ref[1][:, co:co + 1]
        wout_ref[0 * P:1 * P, sl] = (cr * nxf_r - ci_ * nxf_i).astype(BF16)
        wout_ref[1 * P:2 * P, sl] = (-(cr * nxf_i + ci_ * nxf_r)).astype(BF16)
        cr, ci_ = ct_ref[2][:, co:co + 1], ct_ref[3][:, co:co + 1]
        wout_ref[2 * P:3 * P, sl] = (cr * rvb_r - ci_ * rvb_i).astype(BF16)
        wout_ref[3 * P:4 * P, sl] = (-(cr * rvb_i + ci_ * rvb_r)).astype(BF16)

    ucat = jnp.concatenate([u_ref[:, ci, :].astype(BF16) for ci in range(S5_GROUP)], axis=1)

    summ = _dot_nt(ucat, win_ref[...])
    sf_ref[...] = summ[:, :2 * P]
    sb_ref[...] = summ[:, 2 * P:]
    l256 = l256_ref[...]
    mag = jnp.exp(float(L) * l256[2:3] * l256[0:1])
    ang = float(L) * l256[2:3] * l256[1:2]
    lane256 = lax.broadcasted_iota(I32, (1, 4 * P), 1)
    sign = jnp.where((lane256 // P) % 2 == 0, -1.0, 1.0)
    a_mul = mag * jnp.cos(ang)
    b_mul = sign * mag * jnp.sin(ang)

    hf = jnp.zeros((nb, 2 * P), F32)
    hb = jnp.zeros((nb, 2 * P), F32)
    for c in range(nchunk):
        cr = nchunk - 1 - c
        rows_f = pl.ds(c, nb, stride=nchunk)
        rows_b = pl.ds(cr, nb, stride=nchunk)
        hf_ref[rows_f, :] = hf
        hb_ref[rows_b, :] = hb
        hf = a_mul[:, :2 * P] * hf + b_mul[:, :2 * P] * pltpu.roll(hf, P, 1) + sf_ref[rows_f, :]
        hb = a_mul[:, 2 * P:] * hb + b_mul[:, 2 * P:] * pltpu.roll(hb, P, 1) + sb_ref[rows_b, :]
    hprev = jnp.concatenate([hf_ref[...], hb_ref[...]], axis=1).astype(BF16)

    nblk = 2 * L
    for j in range(S5_GROUP * L // nblk):
        y = _dot(ucat, m_ref[:, j * nblk:(j + 1) * nblk]) + _dot(hprev, wout_ref[:, j * nblk:(j + 1) * nblk])
        for q in range(nblk // L):
            co = j * (nblk // L) + q
            yc = y[:, q * L:(q + 1) * L] + d_ref[co:co + 1, :] * u_ref[:, co, :]
            o_ref[:, co, :] = 0.5 * yc * (1.0 + lax.erf(yc * (2.0 ** -0.5)))


def _s5_scan(u3, p, nchunk):
    n = u3.shape[0]
    g, grp, st, L = S5_GROUPS, S5_GROUP, S5_STATE, S5_CHUNK
    step_f = jnp.exp(p["log_step_f"])[:, None] * jnp.ones((1, st), F32)
    step_b = jnp.exp(p["log_step_b"])[:, None] * jnp.ones((1, st), F32)
    zeros = jnp.zeros((g, st), F32)
    lrow = jnp.stack([p["lam_re_f"], p["lam_im_f"], step_f, p["lam_re_b"], p["lam_im_b"], step_b, zeros, zeros], axis=1)
    lcol = jnp.swapaxes(lrow, 1, 2)
    cat4 = lambda f, b: jnp.concatenate([f, f, b, b], axis=1)
    z256 = jnp.zeros((g, 4 * st), F32)
    l256 = jnp.stack([cat4(p["lam_re_f"], p["lam_re_b"]), cat4(p["lam_im_f"], p["lam_im_b"]),
                      cat4(step_f, step_b)] + [z256] * 5, axis=1)
    b4 = jnp.stack([p["b_re_f"], p["b_im_f"], p["b_re_b"], p["b_im_b"]], axis=1)
    bt4 = jnp.swapaxes(b4, 2, 3)
    c4 = jnp.stack([p["c_re_f"], p["c_im_f"], p["c_re_b"], p["c_im_b"]], axis=1)
    ct4 = jnp.swapaxes(c4, 2, 3)
    dbc = jnp.broadcast_to(p["d"].reshape(g, grp, 1), (g, grp, L))
    per_g = lambda *shape: pl.BlockSpec((None,) + shape, lambda i: (i,) + (0,) * len(shape))
    return pl.pallas_call(
        functools.partial(_s5_body, nchunk=nchunk),
        grid=(g,),
        in_specs=[pl.BlockSpec((n, grp, L), lambda i: (0, i, 0)),
                  per_g(8, st), per_g(st, 8), per_g(8, 4 * st), per_g(4, st, grp), per_g(4, grp, st),
                  per_g(4, grp, st), per_g(4, st, grp), per_g(grp, L)],
        out_specs=pl.BlockSpec((n, grp, L), lambda i: (0, i, 0)),
        out_shape=jax.ShapeDtypeStruct(u3.shape, F32),
        scratch_shapes=[pltpu.VMEM((grp * grp, L), F32), pltpu.VMEM((grp * grp, L), F32),
                        pltpu.VMEM((grp * L, grp * L), BF16),
                        pltpu.VMEM((4 * st, grp * L), BF16), pltpu.VMEM((4 * st, grp * L), BF16),
                        pltpu.VMEM((n, 2 * st), F32), pltpu.VMEM((n, 2 * st), F32),
                        pltpu.VMEM((n, 2 * st), F32), pltpu.VMEM((n, 2 * st), F32)],
        compiler_params=_cparams("parallel"),
        name="s5_scan",
    )(u3, lrow, lcol, l256, b4, bt4, c4, ct4, dbc)


def _s5_post_body(g_ref, wt_ref, b_ref, nw_ref, o_ref):
    for j in range(g_ref.shape[0]):
        g = g_ref[j]
        z = _dot(wt_ref[...], g.astype(BF16)) + b_ref[...]
        y = g * jax.nn.sigmoid(z)
        y = y * lax.rsqrt(jnp.mean(y * y, axis=0, keepdims=True) + RMS_EPS) * nw_ref[...]
        o_ref[j * S5_CHUNK:(j + 1) * S5_CHUNK, :] = y.T.astype(BF16)


def _s5_post(g3, w_glu_t, b_glu_col, norm_col, nck=4):
    n = g3.shape[0]
    full = lambda shape: pl.BlockSpec(shape, lambda i: (0,) * len(shape))
    return pl.pallas_call(
        _s5_post_body,
        grid=(n // nck,),
        in_specs=[pl.BlockSpec((nck, S5_WIDTH, S5_CHUNK), lambda i: (i, 0, 0)),
                  full(w_glu_t.shape), full(b_glu_col.shape), full(norm_col.shape)],
        out_specs=pl.BlockSpec((nck * S5_CHUNK, S5_WIDTH), lambda i: (i, 0)),
        out_shape=jax.ShapeDtypeStruct((n * S5_CHUNK, S5_WIDTH), BF16),
        compiler_params=_cparams("parallel"),
        name="s5_post",
    )(g3, w_glu_t, b_glu_col, norm_col)


GDN_BLOCK = 128


def _unit_tri_inverse(lmat, eye, bd16, rings):
    d = jnp.where(bd16, lmat, 0.0)
    d1 = d.astype(BF16)
    d2 = _dot(d1, d1)
    d2b = d2.astype(BF16)
    d4 = _dot(d2b, d2b)
    d4b = d4.astype(BF16)
    d8b = _dot(d4b, d4b).astype(BF16)
    x = eye - d
    x = x + _dot(x.astype(BF16), d2b)
    x = x + _dot(x.astype(BF16), d4b)
    x = x + _dot(x.astype(BF16), d8b)
    for ring in rings:
        xb = x.astype(BF16)
        x = x - _dot(_dot(xb, jnp.where(ring, lmat, 0.0).astype(BF16)).astype(BF16), xb)
    return x


def _gdn_body(q_ref, k_ref, v_ref, z_ref, wq_ref, wk_ref, wv_ref, g_ref, hp_ref, nw_ref, o_ref,
              qs, ks, vs, os_, gcf, gcb, btf, btb):
    seq = q_ref.shape[0]
    C = GDN_BLOCK
    nck = seq // C
    h = pl.program_id(1)
    row = lax.broadcasted_iota(I32, (seq, 1), 0)

    def conv_silu(x_ref, w_ref):
        x = x_ref[...]
        w = w_ref[...]
        acc = x * w[2:3]
        for j in (0, 1, 3, 4):
            dlt = j - (GDN_CONV - 1) // 2
            sh = pltpu.roll(x, (-dlt) % seq, 0)
            ok = jnp.logical_and(row + dlt >= 0, row + dlt < seq)
            acc = acc + jnp.where(ok, sh, 0.0) * w[j:j + 1]
        return acc * jax.nn.sigmoid(acc)

    def l2n(x):
        return x * lax.rsqrt(jnp.sum(x * x, axis=-1, keepdims=True) + L2_EPS)

    qs[...] = l2n(conv_silu(q_ref, wq_ref)) * (GDN_HEAD_DIM ** -0.5)
    ks[...] = l2n(conv_silu(k_ref, wk_ref))
    vs[...] = conv_silu(v_ref, wv_ref)
    os_[...] = jnp.zeros_like(os_)

    g = pltpu.roll(g_ref[...], lax.rem(LANES - 8 * h, LANES), 1)
    hp = hp_ref[...]
    pos = lax.rem(row, C)

    def softplus(x):
        return jnp.maximum(x, 0.0) + jnp.log1p(jnp.exp(-jnp.abs(x)))

    def gate(col, a_log, dt_bias, reverse):
        a = jnp.broadcast_to(g[:, col:col + 1], (seq, LANES))
        gl = -jnp.exp(a_log) * softplus(a + dt_bias)
        sh = 1
        while sh < C:
            if reverse:
                gl = gl + jnp.where(pos < C - sh, pltpu.roll(gl, seq - sh, 0), 0.0)
            else:
                gl = gl + jnp.where(pos >= sh, pltpu.roll(gl, sh, 0), 0.0)
            sh *= 2
        return gl

    gcf[...] = gate(2, hp[0:1], hp[1:2], False)
    gcb[...] = gate(3, hp[2:3], hp[3:4], True)
    btf[...] = jnp.broadcast_to(jax.nn.sigmoid(g[:, 0:1]), (seq, LANES))
    btb[...] = jnp.broadcast_to(jax.nn.sigmoid(g[:, 1:2]), (seq, LANES))

    ri = lax.broadcasted_iota(I32, (C, C), 0)
    ci = lax.broadcasted_iota(I32, (C, C), 1)
    eye = jnp.where(ri == ci, 1.0, 0.0).astype(F32)
    same = lambda w: (ri // w) == (ci // w)
    bd16 = same(16)
    rings = []
    w = 32
    while w <= C:
        rings.append(jnp.logical_and(same(w), jnp.logical_not(same(w // 2))))
        w *= 2

    def chunk_step(c, state, reverse):
        sl = pl.ds(pl.multiple_of(c * C, C), C)
        q, k, v = qs[sl, :], ks[sl, :], vs[sl, :]
        gc = (gcb if reverse else gcf)[sl, :]
        beta = (btb if reverse else btf)[sl, :]
        incl = (ri <= ci) if reverse else (ri >= ci)
        strict = (ri < ci) if reverse else (ri > ci)
        diff = gc - gc.T
        decay = jnp.where(incl, jnp.exp(jnp.where(incl, diff, 0.0)), 0.0)
        kb = k * beta
        kbf = k.astype(BF16)
        aq = _dot_nt(jnp.concatenate([kb, q], axis=0).astype(BF16), kbf)
        lmat = jnp.where(strict, aq[:C] * decay, 0.0)
        qk = aq[C:] * decay
        tinv = _unit_tri_inverse(lmat, eye, bd16, rings)
        eg = jnp.exp(gc)
        uw = _dot(tinv.astype(BF16), jnp.concatenate([v * beta, kb * eg], axis=1).astype(BF16))
        glast = gc[0:1] if reverse else gc[C - 1:C]
        sb = state.astype(BF16)
        ws_qs = _dot(jnp.concatenate([uw[:, C:], q * eg], axis=0).astype(BF16), sb)
        v_new = uw[:, :C] - ws_qs[:C]
        vnb = v_new.astype(BF16)
        os_[sl, :] += ws_qs[C:] + _dot(qk.astype(BF16), vnb)
        kd = (k * jnp.exp(glast - gc)).astype(BF16)
        return state * jnp.exp(glast) + _dot_tn(kd, vnb)

    def body(i, carry):
        sf, sb = carry
        return chunk_step(i, sf, False), chunk_step(nck - 1 - i, sb, True)

    zero = jnp.zeros((GDN_HEAD_DIM, GDN_HEAD_DIM), F32)
    lax.fori_loop(0, nck, body, (zero, zero))

    o = os_[...]
    o = o * lax.rsqrt(jnp.mean(o * o, axis=-1, keepdims=True) + RMS_EPS) * nw_ref[...]
    z = z_ref[...]
    o_ref[...] = (o * (z * jax.nn.sigmoid(z))).astype(BF16)


def _gdn(qkvz, gates, conv_w, head_par, norm_w, bsz, seq):
    t = bsz * seq
    hd = GDN_HEAD_DIM
    nh = GDN_HEADS
    col = lambda off: pl.BlockSpec((seq, hd), lambda b, h: (b, off + h))
    wcol = lambda off: pl.BlockSpec((GDN_CONV, hd), lambda b, h: (0, off + h))
    scr = lambda: pltpu.VMEM((seq, hd), F32)
    return pl.pallas_call(
        _gdn_body,
        grid=(bsz, nh),
        in_specs=[col(0), col(nh), col(2 * nh), col(3 * nh), wcol(0), wcol(nh), wcol(2 * nh),
                  pl.BlockSpec((seq, LANES), lambda b, h: (b, 0)),
                  pl.BlockSpec((None, 8, LANES), lambda b, h: (h, 0, 0)),
                  pl.BlockSpec((1, hd), lambda b, h: (0, 0))],
        out_specs=pl.BlockSpec((seq, hd), lambda b, h: (b, h)),
        out_shape=jax.ShapeDtypeStruct((t, nh * hd), BF16),
        scratch_shapes=[scr() for _ in range(8)],
        compiler_params=_cparams("parallel", "parallel"),
        name="gdn",
    )(qkvz, qkvz, qkvz, qkvz, conv_w, conv_w, conv_w, gates, head_par, norm_w)


def _kv_body(m_ref, g_ref, wk_ref, wv_ref, k_ref, v_ref):
    mn = _rms(m_ref[...], g_ref[...]).astype(BF16)
    k_ref[...] = _dot(mn, wk_ref[...]).astype(BF16)
    v_ref[...] = _dot(mn, wv_ref[...]).astype(BF16)


def _mem_kv(mem2d, norm_w, wk, wv, tm=512):
    r, d = mem2d.shape
    full = lambda shape: pl.BlockSpec(shape, lambda i: (0,) * len(shape))
    tile = pl.BlockSpec((tm, d), lambda i: (i, 0))
    return pl.pallas_call(
        _kv_body,
        grid=(r // tm,),
        in_specs=[tile, full((1, d)), full(wk.shape), full(wv.shape)],
        out_specs=[tile, tile],
        out_shape=[jax.ShapeDtypeStruct((r, d), BF16)] * 2,
        compiler_params=_cparams("parallel"),
        name="mem_kv",
    )(mem2d, norm_w, wk, wv)


def _mix_xattn_body(x_ref, y5_ref, yg_ref, wo5_ref, wog_ref, g_ref, wq_ref, k_ref, v_ref, wo_ref, o_ref):
    x1 = x_ref[...] + _dot(y5_ref[...], wo5_ref[...]) + _dot(yg_ref[...], wog_ref[...])
    xn = _rms(x1, g_ref[...]).astype(BF16)
    q = (_dot(xn, wq_ref[...]) * (XA_HEAD_DIM ** -0.5)).astype(BF16)
    heads = []
    for h in range(XA_HEADS):
        sl = slice(h * XA_HEAD_DIM, (h + 1) * XA_HEAD_DIM)
        s = _dot_nt(q[:, sl], k_ref[:, sl])
        p = jnp.exp(s - jnp.max(s, axis=-1, keepdims=True))
        p = p / jnp.sum(p, axis=-1, keepdims=True)
        heads.append(_dot(p.astype(BF16), v_ref[:, sl]).astype(BF16))
    o_ref[...] = x1 + _dot(jnp.concatenate(heads, axis=1), wo_ref[...])


def _mix_xattn(x2d, y5, yg, wo5, wog, norm_w, wq, kmem, vmem, wo, seq, mem_len, tm=512):
    t, d = x2d.shape
    per_b = seq // tm
    full = lambda shape: pl.BlockSpec(shape, lambda i: (0,) * len(shape))
    return pl.pallas_call(
        _mix_xattn_body,
        grid=(t // tm,),
        in_specs=[pl.BlockSpec((tm, d), lambda i: (i, 0)),
                  pl.BlockSpec((tm, S5_WIDTH), lambda i: (i, 0)),
                  pl.BlockSpec((tm, GDN_WIDTH), lambda i: (i, 0)),
                  full(wo5.shape), full(wog.shape), full((1, d)), full(wq.shape),
                  pl.BlockSpec((mem_len, d), lambda i: (i // per_b, 0)),
                  pl.BlockSpec((mem_len, d), lambda i: (i // per_b, 0)),
                  full(wo.shape)],
        out_specs=pl.BlockSpec((tm, d), lambda i: (i, 0)),
        out_shape=jax.ShapeDtypeStruct((t, d), F32),
        compiler_params=_cparams("parallel"),
        name="mix_xattn",
    )(x2d, y5, yg, wo5, wog, norm_w, wq, kmem, vmem, wo)


ROUTE_EXPERT_LANE0 = 4


def _router_body(x_ref, g_ref, w_ref, b_ref, tri_ref, xn_ref, mi_ref, mf_ref, cnt_ref, carry):
    @pl.when(pl.program_id(0) == 0)
    def _():
        carry[...] = jnp.zeros_like(carry)

    xn = _rms(x_ref[...], g_ref[...])
    xn_ref[...] = xn
    logits = jnp.dot(xn, w_ref[...], preferred_element_type=F32, precision=lax.Precision.HIGHEST) + b_ref[...]
    tm = logits.shape[0]
    lane = lax.broadcasted_iota(I32, (tm, LANES), 1)
    neg = jnp.float32(-jnp.inf)
    big = jnp.int32(LANES)

    def top(vals):
        m = jnp.max(vals, axis=-1, keepdims=True)
        idx = jnp.min(jnp.where(vals == m, lane, big), axis=-1, keepdims=True)
        return m, idx

    is_g = lane < MOE_GROUPS
    gl = jnp.where(is_g, logits, neg)
    gmax, gidx = top(gl)
    p_top = 1.0 / jnp.sum(jnp.where(is_g, jnp.exp(gl - gmax), 0.0), axis=-1, keepdims=True)
    elane = lane - ROUTE_EXPERT_LANE0
    in_grp = jnp.logical_and(jnp.logical_and(elane >= 0, elane < MOE_EXPERTS), (elane // MOE_PER_GROUP) == gidx)
    es = jnp.where(in_grp, logits, neg)
    m1, i1 = top(es)
    m2, i2 = top(jnp.where(lane == i1, neg, es))
    e21 = jnp.exp(m2 - m1)
    w1 = p_top / (1.0 + e21)
    w2 = p_top * e21 / (1.0 + e21)
    e1 = i1 - ROUTE_EXPERT_LANE0
    e2 = i2 - ROUTE_EXPERT_LANE0

    a1 = (lane == e1).astype(F32)
    a2 = (lane == e2).astype(F32)
    both = a1 + a2
    before = _dot(tri_ref[...], both.astype(BF16)) - both + carry[...]
    r1 = jnp.sum(a1 * before, axis=-1, keepdims=True).astype(I32)
    r2 = jnp.sum(a2 * before, axis=-1, keepdims=True).astype(I32)
    carry[...] = carry[...] + jnp.sum(both, axis=0, keepdims=True)
    cnt_ref[...] = jnp.broadcast_to(carry[...], cnt_ref.shape)

    mi_ref[...] = jnp.where(lane == 0, e1, jnp.where(lane == 1, e2, jnp.where(lane == 2, r1, jnp.where(lane == 3, r2, 0))))
    mf_ref[...] = jnp.where(lane == 0, w1, jnp.where(lane == 1, w2, 0.0))


def _router(x2d, norm_w, w_route, b_route, tm=512):
    t, d = x2d.shape
    tri = jnp.tril(jnp.ones((tm, tm), BF16))
    full = lambda shape: pl.BlockSpec(shape, lambda i: (0,) * len(shape))
    return pl.pallas_call(
        _router_body,
        grid=(t // tm,),
        in_specs=[pl.BlockSpec((tm, d), lambda i: (i, 0)), full((1, d)), full(w_route.shape), full((1, LANES)),
                  full((tm, tm))],
        out_specs=[pl.BlockSpec((tm, d), lambda i: (i, 0)),
                   pl.BlockSpec((tm, LANES), lambda i: (i, 0)),
                   pl.BlockSpec((tm, LANES), lambda i: (i, 0)),
                   pl.BlockSpec((8, LANES), lambda i: (0, 0))],
        out_shape=[jax.ShapeDtypeStruct((t, d), F32),
                   jax.ShapeDtypeStruct((t, LANES), I32),
                   jax.ShapeDtypeStruct((t, LANES), F32),
                   jax.ShapeDtypeStruct((8, LANES), F32)],
        scratch_shapes=[pltpu.VMEM((1, LANES), F32)],
        compiler_params=_cparams("arbitrary"),
        name="moe_router",
    )(x2d, norm_w, w_route, b_route, tri)


def _dispatch_body(dest_ref, xn_ref, xs_in_ref, xs_ref, sem):
    del xs_in_ref
    tm = xn_ref.shape[0]

    def copy(r, k):
        return pltpu.make_async_copy(xn_ref.at[pl.ds(r, 1), :], xs_ref.at[pl.ds(dest_ref[0, 2 * r + k], 1), :], sem)

    def start(r, c):
        copy(r, 0).start()
        copy(r, 1).start()
        return c

    def wait(r, c):
        copy(r, 0).wait()
        copy(r, 1).wait()
        return c

    lax.fori_loop(0, tm, start, 0)
    lax.fori_loop(0, tm, wait, 0)


def _dispatch(dest3, xn, xs_init, tm=256):
    t, d = xn.shape
    return pl.pallas_call(
        _dispatch_body,
        grid=(t // tm,),
        in_specs=[pl.BlockSpec((None, 1, 2 * tm), lambda i: (i, 0, 0), memory_space=pltpu.SMEM),
                  pl.BlockSpec((tm, d), lambda i: (i, 0)),
                  pl.BlockSpec(memory_space=pl.ANY)],
        out_specs=pl.BlockSpec(memory_space=pl.ANY),
        out_shape=jax.ShapeDtypeStruct(xs_init.shape, xs_init.dtype),
        scratch_shapes=[pltpu.SemaphoreType.DMA(())],
        input_output_aliases={2: 0},
        compiler_params=_cparams("arbitrary"),
        name="moe_dispatch",
    )(dest3, xn, xs_init)


def _experts_body(te_ref, na_ref, x_ref, wg_ref, wu_ref, wd_ref, y_ref):
    del te_ref

    @pl.when(pl.program_id(0) < na_ref[0])
    def _():
        x = x_ref[...].astype(BF16)
        gt = _dot(x, wg_ref[...])
        up = _dot(x, wu_ref[...])
        hid = (gt * jax.nn.sigmoid(gt) * up).astype(BF16)
        y_ref[...] = _dot(hid, wd_ref[...])

    @pl.when(pl.program_id(0) >= na_ref[0])
    def _():
        y_ref[...] = jnp.zeros_like(y_ref)


def _experts(tile_expert, n_active, xs, w_gate, w_up, w_down, tm):
    r, d = xs.shape
    f = w_gate.shape[2]
    grid_spec = pltpu.PrefetchScalarGridSpec(
        num_scalar_prefetch=2,
        grid=(r // tm,),
        in_specs=[pl.BlockSpec((tm, d), lambda i, te, na: (i, 0)),
                  pl.BlockSpec((None, d, f), lambda i, te, na: (te[i], 0, 0)),
                  pl.BlockSpec((None, d, f), lambda i, te, na: (te[i], 0, 0)),
                  pl.BlockSpec((None, f, d), lambda i, te, na: (te[i], 0, 0))],
        out_specs=pl.BlockSpec((tm, d), lambda i, te, na: (i, 0)),
    )
    return pl.pallas_call(
        _experts_body,
        grid_spec=grid_spec,
        out_shape=jax.ShapeDtypeStruct((r, d), F32),
        compiler_params=_cparams("arbitrary"),
        name="moe_experts",
    )(tile_expert, n_active, xs, w_gate, w_up, w_down)


def _combine_body(dest_ref, x_ref, mf_ref, g_ref, ys_ref, o_ref, buf, sem):
    tm = x_ref.shape[0]

    def copy(r, k):
        return pltpu.make_async_copy(ys_ref.at[pl.ds(dest_ref[0, 2 * r + k], 1), :], buf.at[k, pl.ds(r, 1), :], sem)

    def start(r, c):
        copy(r, 0).start()
        copy(r, 1).start()
        return c

    def wait(r, c):
        copy(r, 0).wait()
        copy(r, 1).wait()
        return c

    lax.fori_loop(0, tm, start, 0)
    lax.fori_loop(0, tm, wait, 0)
    mf = mf_ref[...]
    y = x_ref[...] + mf[:, 0:1] * buf[0] + mf[:, 1:2] * buf[1]
    o_ref[...] = _rms(y, g_ref[...])


def _combine(dest3, x2d, mf, norm_w, ys, tm=256):
    t, d = x2d.shape
    return pl.pallas_call(
        _combine_body,
        grid=(t // tm,),
        in_specs=[pl.BlockSpec((None, 1, 2 * tm), lambda i: (i, 0, 0), memory_space=pltpu.SMEM),
                  pl.BlockSpec((tm, d), lambda i: (i, 0)),
                  pl.BlockSpec((tm, LANES), lambda i: (i, 0)),
                  pl.BlockSpec((1, d), lambda i: (0, 0)),
                  pl.BlockSpec(memory_space=pl.ANY)],
        out_specs=pl.BlockSpec((tm, d), lambda i: (i, 0)),
        out_shape=jax.ShapeDtypeStruct((t, d), F32),
        scratch_shapes=[pltpu.VMEM((2, tm, d), F32), pltpu.SemaphoreType.DMA(())],
        compiler_params=_cparams("arbitrary"),
        name="moe_combine",
    )(dest3, x2d, mf, norm_w, ys)


MOE_ROW_TILE = 256


def _moe(x2d, norm_w, wg, bg, we, be, w_gate, w_up, w_down, norm_final):
    t, d = x2d.shape
    tm = MOE_ROW_TILE
    w_route = jnp.pad(jnp.concatenate([wg, we], axis=1), ((0, 0), (0, LANES - MOE_GROUPS - MOE_EXPERTS)))
    b_route = jnp.pad(jnp.concatenate([bg, be]), (0, LANES - MOE_GROUPS - MOE_EXPERTS))[None]
    xn, mi, mf, cnt = _router(x2d, norm_w, w_route, b_route)
    counts = cnt[0, :MOE_EXPERTS].astype(I32)
    padded = ((counts + tm - 1) // tm) * tm
    ends = jnp.cumsum(padded)
    offsets = ends - padded
    n_tiles = (2 * t + MOE_EXPERTS * (tm - 1)) // tm
    tile_expert = jnp.minimum(jnp.searchsorted(ends, jnp.arange(n_tiles, dtype=I32) * tm, side="right"),
                              MOE_EXPERTS - 1).astype(I32)
    n_active = (ends[-1:] // tm).astype(I32)
    dest = offsets[mi[:, 0:2]] + mi[:, 2:4]
    dest3 = dest.reshape(t // tm, 1, 2 * tm)
    xs = _dispatch(dest3, xn, jnp.zeros((n_tiles * tm, d), F32), tm)
    ys = _experts(tile_expert, n_active, xs, w_gate, w_up, w_down, tm)
    return _combine(dest3, x2d, mf, norm_final, ys, tm)


def kernel(x, mem, norm_mix, w_in, w_out,
           s5_lam_re_f, s5_lam_im_f, s5_log_step_f, s5_b_re_f, s5_b_im_f, s5_c_re_f, s5_c_im_f,
           s5_lam_re_b, s5_lam_im_b, s5_log_step_b, s5_b_re_b, s5_b_im_b, s5_c_re_b, s5_c_im_b,
           s5_d, s5_w_glu, s5_b_glu, s5_norm,
           gdn_conv, gdn_a_log_f, gdn_dt_bias_f, gdn_a_log_b, gdn_dt_bias_b, gdn_norm,
           norm_xattn, norm_mem, xa_wq, xa_wk, xa_wv, xa_wo,
           norm_moe, router_group_w, router_group_b, router_expert_w, router_expert_b,
           moe_w_gate, moe_w_up, moe_w_down, norm_final):
    bsz, seq, d = x.shape
    t = bsz * seq
    l = 0
    x2d = x.reshape(t, d)
    wi = w_in[l]
    wut = wi[:, :S5_WIDTH].T.astype(BF16)
    wqkvz = wi[:, S5_WIDTH:S5_WIDTH + 4 * GDN_WIDTH].astype(BF16)
    wg = wi[:, S5_WIDTH + 4 * GDN_WIDTH:].reshape(d, 4, GDN_HEADS)
    wg = jnp.pad(jnp.swapaxes(wg, 1, 2), ((0, 0), (0, 0), (0, 4))).reshape(d, GDN_HEADS * 8)
    wgt = jnp.pad(wg, ((0, 0), (0, LANES - GDN_HEADS * 8))).astype(BF16)
    u3, qkvz, gates = _in_proj(x2d, norm_mix[l][None], wut, wqkvz, wgt)
    s5p = dict(lam_re_f=s5_lam_re_f[l], lam_im_f=s5_lam_im_f[l], log_step_f=s5_log_step_f[l],
               b_re_f=s5_b_re_f[l], b_im_f=s5_b_im_f[l], c_re_f=s5_c_re_f[l], c_im_f=s5_c_im_f[l],
               lam_re_b=s5_lam_re_b[l], lam_im_b=s5_lam_im_b[l], log_step_b=s5_log_step_b[l],
               b_re_b=s5_b_re_b[l], b_im_b=s5_b_im_b[l], c_re_b=s5_c_re_b[l], c_im_b=s5_c_im_b[l], d=s5_d[l])
    g3 = _s5_scan(u3, s5p, seq // S5_CHUNK)
    y_s5 = _s5_post(g3, s5_w_glu[l].T.astype(BF16), s5_b_glu[l][:, None], s5_norm[l][:, None])

    head_par = jnp.stack([gdn_a_log_f[l], gdn_dt_bias_f[l], gdn_a_log_b[l], gdn_dt_bias_b[l]], axis=1)
    head_par = jnp.broadcast_to(jnp.pad(head_par, ((0, 0), (0, 4)))[:, :, None], (GDN_HEADS, 8, LANES))
    y_gdn = _gdn(qkvz, gates, gdn_conv[l], head_par, gdn_norm[l][None], bsz, seq)

    mem_len = mem.shape[1]
    kmem, vmem = _mem_kv(mem.reshape(bsz * mem_len, d), norm_mem[l][None],
                         xa_wk[l].astype(BF16), xa_wv[l].astype(BF16))
    wo = w_out[l].astype(BF16)
    x2 = _mix_xattn(x2d, y_s5, y_gdn, wo[:S5_WIDTH], wo[S5_WIDTH:], norm_xattn[l][None],
                    xa_wq[l].astype(BF16), kmem, vmem, xa_wo[l].astype(BF16), seq, mem_len)
    y = _moe(x2, norm_moe[l][None], router_group_w[l], router_group_b[l], router_expert_w[l], router_expert_b[l],
             moe_w_gate[l].astype(BF16), moe_w_up[l].astype(BF16), moe_w_down[l].astype(BF16), norm_final[None])
    return y.reshape(bsz, seq, d)
```

```python
import functools
import math

import jax
import jax.numpy as jnp
from jax import lax
from jax.experimental import pallas as pl
from jax.experimental.pallas import tpu as pltpu

F32 = jnp.float32
BF16 = jnp.bfloat16
I32 = jnp.int32

D_MODEL = 1024
S5_WIDTH = 512
S5_GROUP = 16
S5_GROUPS = 32
S5_STATE = 64
S5_CHUNK = 128
GDN_HEADS = 4
GDN_HEAD_DIM = 128
GDN_WIDTH = 512
GDN_CONV = 5
GDN_CHUNK = 64
XA_HEADS = 4
XA_HEAD_DIM = 256
MOE_GROUPS = 4
MOE_PER_GROUP = 8
MOE_EXPERTS = 32
MOE_TOPK = 2
D_EXPERT = 256
RMS_EPS = 1e-6
L2_EPS = 1e-6
LANES = 128
VMEM_LIMIT = 56 * 1024 * 1024


def _cparams(*sem):
    return pltpu.CompilerParams(dimension_semantics=tuple(sem), vmem_limit_bytes=VMEM_LIMIT)


def _rms(x, gain):
    return x * lax.rsqrt(jnp.mean(x * x, axis=-1, keepdims=True) + RMS_EPS) * gain


def _dot(a, b):
    return jnp.dot(a, b, preferred_element_type=F32)


def _dot_nt(a, b):
    return lax.dot_general(a, b, (((1,), (1,)), ((), ())), preferred_element_type=F32)


def _dot_tn(a, b):
    return lax.dot_general(a, b, (((0,), (0,)), ((), ())), preferred_element_type=F32)


def _in_proj_body(x_ref, g_ref, wut_ref, wqkvz_ref, wgt_ref, u_ref, qkvz_ref, gates_ref):
    h = _rms(x_ref[...], g_ref[...]).astype(BF16)
    ut = _dot_nt(wut_ref[...], h)
    for j in range(u_ref.shape[0]):
        u_ref[j] = ut[:, j * S5_CHUNK:(j + 1) * S5_CHUNK]
    qkvz_ref[...] = _dot(h, wqkvz_ref[...])
    gates_ref[...] = _dot(h, wgt_ref[...])


def _in_proj(x2d, norm_w, wut, wqkvz, wgt, tm=512):
    t = x2d.shape[0]
    nck = tm // S5_CHUNK
    full = lambda shape: pl.BlockSpec(shape, lambda i: (0,) * len(shape))
    return pl.pallas_call(
        _in_proj_body,
        grid=(t // tm,),
        in_specs=[pl.BlockSpec((tm, D_MODEL), lambda i: (i, 0)),
                  full((1, D_MODEL)), full(wut.shape), full(wqkvz.shape), full(wgt.shape)],
        out_specs=[pl.BlockSpec((nck, S5_WIDTH, S5_CHUNK), lambda i: (i, 0, 0)),
                   pl.BlockSpec((tm, wqkvz.shape[1]), lambda i: (i, 0)),
                   pl.BlockSpec((tm, LANES), lambda i: (i, 0))],
        out_shape=[jax.ShapeDtypeStruct((t // S5_CHUNK, S5_WIDTH, S5_CHUNK), F32),
                   jax.ShapeDtypeStruct((t, wqkvz.shape[1]), F32),
                   jax.ShapeDtypeStruct((t, LANES), F32)],
        compiler_params=_cparams("parallel"),
        name="in_proj",
    )(x2d, norm_w, wut, wqkvz, wgt)


def _cpow(re_col, im_col, step_col, expo):
    mag = jnp.exp(expo * (step_col * re_col))
    ang = expo * (step_col * im_col)
    return mag * jnp.cos(ang), mag * jnp.sin(ang)


def _zoh_coef(re, im, step):
    er = jnp.exp(step * re)
    lr = er * jnp.cos(step * im) - 1.0
    li = er * jnp.sin(step * im)
    den = re * re + im * im
    return (lr * re + li * im) / den, (li * re - lr * im) / den


def _s5_body(u_ref, lrow_ref, lcol_ref, l256_ref, b_ref, bt_ref, c_ref, ct_ref, d_ref,
             o_ref, vf_ref, vb_ref, m_ref, win_ref, wout_ref, sf_ref, sb_ref, hf_ref, hb_ref, *, nchunk):
    L = S5_CHUNK
    P = S5_STATE
    n_rows = u_ref.shape[0]
    nb = n_rows // nchunk
    lane = lax.broadcasted_iota(I32, (1, L), 1).astype(F32)

    lcol = lcol_ref[...]
    ref_c, imf_c, stf_c = lcol[:, 0:1], lcol[:, 1:2], lcol[:, 2:3]
    reb_c, imb_c, stb_c = lcol[:, 3:4], lcol[:, 4:5], lcol[:, 5:6]
    lrow = lrow_ref[...]
    ref_r, imf_r, stf_r = lrow[0:1], lrow[1:2], lrow[2:3]
    reb_r, imb_r, stb_r = lrow[3:4], lrow[4:5], lrow[5:6]

    pwf_r, pwf_i = _cpow(ref_c, imf_c, stf_c, lane)
    rvf_r, rvf_i = _cpow(ref_c, imf_c, stf_c, (L - 1.0) - lane)
    pwb_r, pwb_i = _cpow(reb_c, imb_c, stb_c, lane)
    rvb_r, rvb_i = _cpow(reb_c, imb_c, stb_c, float(L) - lane)
    lf_r, lf_i = _cpow(ref_c, imf_c, stf_c, 1.0)
    nxf_r = pwf_r * lf_r - pwf_i * lf_i
    nxf_i = pwf_r * lf_i + pwf_i * lf_r

    kfr_c, kfi_c = _zoh_coef(ref_c, imf_c, stf_c)
    kbr_c, kbi_c = _zoh_coef(reb_c, imb_c, stb_c)
    kfr_r, kfi_r = _zoh_coef(ref_r, imf_r, stf_r)
    kbr_r, kbi_r = _zoh_coef(reb_r, imb_r, stb_r)
    bf_r = kfr_c * b_ref[0] - kfi_c * b_ref[1]
    bf_i = kfr_c * b_ref[1] + kfi_c * b_ref[0]
    bb_r = kbr_c * b_ref[2] - kbi_c * b_ref[3]
    bb_i = kbr_c * b_ref[3] + kbi_c * b_ref[2]
    btf_r = kfr_r * bt_ref[0] - kfi_r * bt_ref[1]
    btf_i = kfr_r * bt_ref[1] + kfi_r * bt_ref[0]
    btb_r = kbr_r * bt_ref[2] - kbi_r * bt_ref[3]
    btb_i = kbr_r * bt_ref[3] + kbi_r * bt_ref[2]

    def taps(c_r, c_i, bt_r, bt_i, pw_r, pw_i):
        cb_r = (bt_r[:, None, :] * c_r[None, :, :] - bt_i[:, None, :] * c_i[None, :, :]).reshape(256, P)
        cb_i = (bt_r[:, None, :] * c_i[None, :, :] + bt_i[:, None, :] * c_r[None, :, :]).reshape(256, P)
        k = (jnp.dot(cb_r, pw_r, preferred_element_type=F32, precision=lax.Precision.HIGHEST)
             - jnp.dot(cb_i, pw_i, preferred_element_type=F32, precision=lax.Precision.HIGHEST))
        return k, jnp.sum(cb_r, axis=1, keepdims=True)

    kf, _ = taps(c_ref[0], c_ref[1], btf_r, btf_i, pwf_r, pwf_i)
    kb, kb0 = taps(c_ref[2], c_ref[3], btb_r, btb_i, rvb_r, rvb_i)
    is0 = lane == 0.0
    vf_ref[...] = kf + jnp.where(is0, kb0, 0.0)
    vb_ref[...] = jnp.where(is0, 0.0, kb)

    row_i = lax.broadcasted_iota(I32, (L, L), 0)
    col_i = lax.broadcasted_iota(I32, (L, L), 1)
    upper = col_i >= row_i

    def build_ci(ci, carry):
        for co in range(S5_GROUP):
            r = ci * S5_GROUP + co
            tf = pltpu.roll(jnp.broadcast_to(vf_ref[pl.ds(r, 1), :], (L, L)), 0, 1, stride=1, stride_axis=0)
            tb = pltpu.roll(jnp.broadcast_to(vb_ref[pl.ds(r, 1), :], (L, L)), 0, 1, stride=1, stride_axis=0)
            m_ref[pl.ds(pl.multiple_of(ci * L, L), L), co * L:(co + 1) * L] = jnp.where(upper, tf, tb).astype(BF16)
        return carry

    lax.fori_loop(0, S5_GROUP, build_ci, 0)

    for ci in range(S5_GROUP):
        sl = slice(ci * L, (ci + 1) * L)
        br, bi = bf_r[:, ci:ci + 1], bf_i[:, ci:ci + 1]
        win_ref[0 * P:1 * P, sl] = (rvf_r * br - rvf_i * bi).astype(BF16)
        win_ref[1 * P:2 * P, sl] = (rvf_r * bi + rvf_i * br).astype(BF16)
        br, bi = bb_r[:, ci:ci + 1], bb_i[:, ci:ci + 1]
        win_ref[2 * P:3 * P, sl] = (pwb_r * br - pwb_i * bi).astype(BF16)
        win_ref[3 * P:4 * P, sl] = (pwb_r * bi + pwb_i * br).astype(BF16)
    for co in range(S5_GROUP):
        sl = slice(co * L, (co + 1) * L)
        cr, ci_ = ct_ref[0][:, co:co + 1], ct_ref[1][:, co:co + 1]
        wout_ref[0 * P:1 * P, sl] = (cr * nxf_r - ci_ * nxf_i).astype(BF16)
        wout_ref[1 * P:2 * P, sl] = (-(cr * nxf_i + ci_ * nxf_r)).astype(BF16)
        cr, ci_ = ct_ref[2][:, co:co + 1], ct_ref[3][:, co:co + 1]
        wout_ref[2 * P:3 * P, sl] = (cr * rvb_r - ci_ * rvb_i).astype(BF16)
        wout_ref[3 * P:4 * P, sl] = (-(cr * rvb_i + ci_ * rvb_r)).astype(BF16)

    ucat = jnp.concatenate([u_ref[:, ci, :].astype(BF16) for ci in range(S5_GROUP)], axis=1)

    summ = _dot_nt(ucat, win_ref[...])
    sf_ref[...] = summ[:, :2 * P]
    sb_ref[...] = summ[:, 2 * P:]
    l256 = l256_ref[...]
    mag = jnp.exp(float(L) * l256[2:3] * l256[0:1])
    ang = float(L) * l256[2:3] * l256[1:2]
    lane256 = lax.broadcasted_iota(I32, (1, 4 * P), 1)
    sign = jnp.where((lane256 // P) % 2 == 0, -1.0, 1.0)
    a_mul = mag * jnp.cos(ang)
    b_mul = sign * mag * jnp.sin(ang)

    hf = jnp.zeros((nb, 2 * P), F32)
    hb = jnp.zeros((nb, 2 * P), F32)
    for c in range(nchunk):
        cr = nchunk - 1 - c
        rows_f = pl.ds(c, nb, stride=nchunk)
        rows_b = pl.ds(cr, nb, stride=nchunk)
        hf_ref[rows_f, :] = hf
        hb_ref[rows_b, :] = hb
        hf = a_mul[:, :2 * P] * hf + b_mul[:, :2 * P] * pltpu.roll(hf, P, 1) + sf_ref[rows_f, :]
        hb = a_mul[:, 2 * P:] * hb + b_mul[:, 2 * P:] * pltpu.roll(hb, P, 1) + sb_ref[rows_b, :]
    hprev = jnp.concatenate([hf_ref[...], hb_ref[...]], axis=1).astype(BF16)

    nblk = 2 * L
    for j in range(S5_GROUP * L // nblk):
        y = _dot(ucat, m_ref[:, j * nblk:(j + 1) * nblk]) + _dot(hprev, wout_ref[:, j * nblk:(j + 1) * nblk])
        for q in range(nblk // L):
            co = j * (nblk // L) + q
            yc = y[:, q * L:(q + 1) * L] + d_ref[co:co + 1, :] * u_ref[:, co, :]
            o_ref[:, co, :] = 0.5 * yc * (1.0 + lax.erf(yc * (2.0 ** -0.5)))


def _s5_scan(u3, p, nchunk):
    n = u3.shape[0]
    g, grp, st, L = S5_GROUPS, S5_GROUP, S5_STATE, S5_CHUNK
    step_f = jnp.exp(p["log_step_f"])[:, None] * jnp.ones((1, st), F32)
    step_b = jnp.exp(p["log_step_b"])[:, None] * jnp.ones((1, st), F32)
    zeros = jnp.zeros((g, st), F32)
    lrow = jnp.stack([p["lam_re_f"], p["lam_im_f"], step_f, p["lam_re_b"], p["lam_im_b"], step_b, zeros, zeros], axis=1)
    lcol = jnp.swapaxes(lrow, 1, 2)
    cat4 = lambda f, b: jnp.concatenate([f, f, b, b], axis=1)
    z256 = jnp.zeros((g, 4 * st), F32)
    l256 = jnp.stack([cat4(p["lam_re_f"], p["lam_re_b"]), cat4(p["lam_im_f"], p["lam_im_b"]),
                      cat4(step_f, step_b)] + [z256] * 5, axis=1)
    b4 = jnp.stack([p["b_re_f"], p["b_im_f"], p["b_re_b"], p["b_im_b"]], axis=1)
    bt4 = jnp.swapaxes(b4, 2, 3)
    c4 = jnp.stack([p["c_re_f"], p["c_im_f"], p["c_re_b"], p["c_im_b"]], axis=1)
    ct4 = jnp.swapaxes(c4, 2, 3)
    dbc = jnp.broadcast_to(p["d"].reshape(g, grp, 1), (g, grp, L))
    per_g = lambda *shape: pl.BlockSpec((None,) + shape, lambda i: (i,) + (0,) * len(shape))
    return pl.pallas_call(
        functools.partial(_s5_body, nchunk=nchunk),
        grid=(g,),
        in_specs=[pl.BlockSpec((n, grp, L), lambda i: (0, i, 0)),
                  per_g(8, st), per_g(st, 8), per_g(8, 4 * st), per_g(4, st, grp), per_g(4, grp, st),
                  per_g(4, grp, st), per_g(4, st, grp), per_g(grp, L)],
        out_specs=pl.BlockSpec((n, grp, L), lambda i: (0, i, 0)),
        out_shape=jax.ShapeDtypeStruct(u3.shape, F32),
        scratch_shapes=[pltpu.VMEM((grp * grp, L), F32), pltpu.VMEM((grp * grp, L), F32),
                        pltpu.VMEM((grp * L, grp * L), BF16),
                        pltpu.VMEM((4 * st, grp * L), BF16), pltpu.VMEM((4 * st, grp * L), BF16),
                        pltpu.VMEM((n, 2 * st), F32), pltpu.VMEM((n, 2 * st), F32),
                        pltpu.VMEM((n, 2 * st), F32), pltpu.VMEM((n, 2 * st), F32)],
        compiler_params=_cparams("parallel"),
        name="s5_scan",
    )(u3, lrow, lcol, l256, b4, bt4, c4, ct4, dbc)


def _s5_post_body(g_ref, wt_ref, b_ref, nw_ref, o_ref):
    for j in range(g_ref.shape[0]):
        g = g_ref[j]
        z = _dot(wt_ref[...], g.astype(BF16)) + b_ref[...]
        y = g * jax.nn.sigmoid(z)
        y = y * lax.rsqrt(jnp.mean(y * y, axis=0, keepdims=True) + RMS_EPS) * nw_ref[...]
        o_ref[j * S5_CHUNK:(j + 1) * S5_CHUNK, :] = y.T.astype(BF16)


def _s5_post(g3, w_glu_t, b_glu_col, norm_col, nck=4):
    n = g3.shape[0]
    full = lambda shape: pl.BlockSpec(shape, lambda i: (0,) * len(shape))
    return pl.pallas_call(
        _s5_post_body,
        grid=(n // nck,),
        in_specs=[pl.BlockSpec((nck, S5_WIDTH, S5_CHUNK), lambda i: (i, 0, 0)),
                  full(w_glu_t.shape), full(b_glu_col.shape), full(norm_col.shape)],
        out_specs=pl.BlockSpec((nck * S5_CHUNK, S5_WIDTH), lambda i: (i, 0)),
        out_shape=jax.ShapeDtypeStruct((n * S5_CHUNK, S5_WIDTH), BF16),
        compiler_params=_cparams("parallel"),
        name="s5_post",
    )(g3, w_glu_t, b_glu_col, norm_col)


GDN_BLOCK = 128
GDN_PREP_BATCH = 8


def _packed_tri_inverse(lps, low, upp, bd16, rings):
    def pk(xs, ys):
        outs = []
        for a, b in zip(xs, ys):
            lhs = jnp.concatenate([jnp.where(low, a, 0.0), jnp.where(upp, a, 0.0)], axis=1).astype(BF16)
            rhs = jnp.concatenate([jnp.where(low, b, 0.0), jnp.where(upp, b, 0.0)], axis=0).astype(BF16)
            outs.append(_dot(lhs, rhs))
        return outs

    d = [jnp.where(bd16, lp, 0.0) for lp in lps]
    d2 = pk(d, d)
    d4 = pk(d2, d2)
    d8 = pk(d4, d4)
    a = [y - x - p for x, y, p in zip(d, d2, pk(d, d2))]
    a = [x + y + p for x, y, p in zip(a, d4, pk(a, d4))]
    a = [x + y + p for x, y, p in zip(a, d8, pk(a, d8))]
    for ring in rings:
        n = [jnp.where(ring, lp, 0.0) for lp in lps]
        t = [x + p for x, p in zip(n, pk(a, n))]
        a = [x - y - p for x, y, p in zip(a, t, pk(t, a))]
    return a


def _gdn_body(q_ref, k_ref, v_ref, z_ref, wq_ref, wk_ref, wv_ref, g_ref, hp_ref, nw_ref, o_ref,
              qs, ks, vs, os_, gcf, gcb, btf, btb,
              uf, ub, wqf, wqb, qkf, qkb, kdtf, kdtb, eglf, eglb, xpad):
    seq = q_ref.shape[0]
    C = GDN_BLOCK
    nck = seq // C
    h = pl.program_id(1)
    row = lax.broadcasted_iota(I32, (seq, 1), 0)

    pad = 8
    half = (GDN_CONV - 1) // 2
    xpad[0:pad, :] = jnp.zeros((pad, LANES), F32)
    xpad[pad + seq:2 * pad + seq, :] = jnp.zeros((pad, LANES), F32)

    def conv_silu(x_ref, w_ref):
        xpad[pad:pad + seq, :] = x_ref[...]
        w = w_ref[...]
        acc = xpad[pad - half:pad - half + seq, :] * w[0:1]
        for j in range(1, GDN_CONV):
            acc = acc + xpad[pad - half + j:pad - half + j + seq, :] * w[j:j + 1]
        return acc * jax.nn.sigmoid(acc)

    def l2n(x):
        return x * lax.rsqrt(jnp.sum(x * x, axis=-1, keepdims=True) + L2_EPS)

    qs[...] = l2n(conv_silu(q_ref, wq_ref)) * (GDN_HEAD_DIM ** -0.5)
    ks[...] = l2n(conv_silu(k_ref, wk_ref))
    vs[...] = conv_silu(v_ref, wv_ref)
    os_[...] = jnp.zeros_like(os_)

    g = pltpu.roll(g_ref[...], lax.rem(LANES - 8 * h, LANES), 1)
    hp = hp_ref[...]
    pos = lax.rem(row, C)

    def softplus(x):
        return jnp.maximum(x, 0.0) + jnp.log1p(jnp.exp(-jnp.abs(x)))

    def gate(col, a_log, dt_bias, reverse):
        a = jnp.broadcast_to(g[:, col:col + 1], (seq, LANES))
        gl = -jnp.exp(a_log) * softplus(a + dt_bias)
        sh = 1
        while sh < C:
            if reverse:
                gl = gl + jnp.where(pos < C - sh, pltpu.roll(gl, seq - sh, 0), 0.0)
            else:
                gl = gl + jnp.where(pos >= sh, pltpu.roll(gl, sh, 0), 0.0)
            sh *= 2
        return gl

    gcf[...] = gate(2, hp[0:1], hp[1:2], False)
    gcb[...] = gate(3, hp[2:3], hp[3:4], True)
    btf[...] = jnp.broadcast_to(jax.nn.sigmoid(g[:, 0:1]), (seq, LANES))
    btb[...] = jnp.broadcast_to(jax.nn.sigmoid(g[:, 1:2]), (seq, LANES))

    ri = lax.broadcasted_iota(I32, (C, C), 0)
    ci = lax.broadcasted_iota(I32, (C, C), 1)
    low, upp = ri > ci, ri < ci
    low_i, upp_i = ri >= ci, ri <= ci
    same = lambda w: (ri // w) == (ci // w)
    bd16 = same(16)
    rings = []
    w = 32
    while w <= C:
        rings.append(jnp.logical_and(same(w), jnp.logical_not(same(w // 2))))
        w *= 2

    nbatch = math.gcd(GDN_PREP_BATCH, nck)

    def prepare(it, carry):
        cs = [it * nbatch + j for j in range(nbatch)]
        sls = [pl.ds(pl.multiple_of(c * C, C), C) for c in cs]
        dec_f, dec_b, kb_f, kb_b, aq = [], [], [], [], []
        for sl in sls:
            k, g_f, g_b = ks[sl, :], gcf[sl, :], gcb[sl, :]
            dec_f.append(jnp.where(low_i, jnp.exp(jnp.where(low_i, g_f - g_f.T, 0.0)), 0.0))
            dec_b.append(jnp.where(upp_i, jnp.exp(jnp.where(upp_i, g_b - g_b.T, 0.0)), 0.0))
            kb_f.append(k * btf[sl, :])
            kb_b.append(k * btb[sl, :])
            aq.append(_dot_nt(jnp.concatenate([kb_f[-1], kb_b[-1], qs[sl, :]], axis=0).astype(BF16), k.astype(BF16)))
        lps = [jnp.where(low, x[:C] * df, 0.0) + jnp.where(upp, x[C:2 * C] * db, 0.0)
               for x, df, db in zip(aq, dec_f, dec_b)]
        inv = _packed_tri_inverse(lps, low, upp, bd16, rings)
        for j, (c, sl) in enumerate(zip(cs, sls)):
            q, k, v = qs[sl, :], ks[sl, :], vs[sl, :]
            for rev, g_c, kb, beta, dec, msk, u_s, wq_s, qk_s, kdt_s, egl_s in (
                    (False, gcf[sl, :], kb_f[j], btf[sl, :], dec_f[j], low, uf, wqf, qkf, kdtf, eglf),
                    (True, gcb[sl, :], kb_b[j], btb[sl, :], dec_b[j], upp, ub, wqb, qkb, kdtb, eglb)):
                eg = jnp.exp(g_c)
                rhs = jnp.concatenate([v * beta, kb * eg], axis=1)
                uw = rhs + _dot(jnp.where(msk, inv[j], 0.0).astype(BF16), rhs.astype(BF16))
                glast = g_c[0:1] if rev else g_c[C - 1:C]
                u_s[sl, :] = uw[:, :C]
                wq_s[pl.ds(pl.multiple_of(c * 2 * C, 2 * C), 2 * C), :] = jnp.concatenate(
                    [uw[:, C:], q * eg], axis=0).astype(BF16)
                qk_s[sl, :] = (aq[j][2 * C:] * dec).astype(BF16)
                kdt_s[sl, :] = (k * jnp.exp(glast - g_c)).T.astype(BF16)
                egl_s[pl.ds(c, 1), :] = jnp.exp(glast)
        return carry

    lax.fori_loop(0, nck // nbatch, prepare, 0)

    def advance(c, state, u_s, wq_s, qk_s, kdt_s, egl_s):
        sl = pl.ds(pl.multiple_of(c * C, C), C)
        ws_qs = _dot(wq_s[pl.ds(pl.multiple_of(c * 2 * C, 2 * C), 2 * C), :], state.astype(BF16))
        vnb = (u_s[sl, :] - ws_qs[:C]).astype(BF16)
        os_[sl, :] += ws_qs[C:] + _dot(qk_s[sl, :], vnb)
        return state * egl_s[pl.ds(c, 1), :] + _dot(kdt_s[sl, :], vnb)

    def body(i, carry):
        sf, sb = carry
        return (advance(i, sf, uf, wqf, qkf, kdtf, eglf),
                advance(nck - 1 - i, sb, ub, wqb, qkb, kdtb, eglb))

    zero = jnp.zeros((GDN_HEAD_DIM, GDN_HEAD_DIM), F32)
    lax.fori_loop(0, nck, body, (zero, zero))

    o = os_[...]
    o = o * lax.rsqrt(jnp.mean(o * o, axis=-1, keepdims=True) + RMS_EPS) * nw_ref[...]
    z = z_ref[...]
    o_ref[...] = (o * (z * jax.nn.sigmoid(z))).astype(BF16)


def _gdn(qkvz, gates, conv_w, head_par, norm_w, bsz, seq):
    t = bsz * seq
    hd = GDN_HEAD_DIM
    nh = GDN_HEADS
    col = lambda off: pl.BlockSpec((seq, hd), lambda b, h: (b, off + h))
    wcol = lambda off: pl.BlockSpec((GDN_CONV, hd), lambda b, h: (0, off + h))
    scr = lambda: pltpu.VMEM((seq, hd), F32)
    return pl.pallas_call(
        _gdn_body,
        grid=(bsz, nh),
        in_specs=[col(0), col(nh), col(2 * nh), col(3 * nh), wcol(0), wcol(nh), wcol(2 * nh),
                  pl.BlockSpec((seq, LANES), lambda b, h: (b, 0)),
                  pl.BlockSpec((None, 8, LANES), lambda b, h: (h, 0, 0)),
                  pl.BlockSpec((1, hd), lambda b, h: (0, 0))],
        out_specs=pl.BlockSpec((seq, hd), lambda b, h: (b, h)),
        out_shape=jax.ShapeDtypeStruct((t, nh * hd), BF16),
        scratch_shapes=([scr() for _ in range(8)]
                        + [scr(), scr()]
                        + [pltpu.VMEM((2 * seq, hd), BF16)] * 2
                        + [pltpu.VMEM((seq, GDN_BLOCK), BF16)] * 2
                        + [pltpu.VMEM((seq, GDN_BLOCK), BF16)] * 2
                        + [pltpu.VMEM((seq // GDN_BLOCK, hd), F32)] * 2
                        + [pltpu.VMEM((seq + 16, hd), F32)]),
        compiler_params=_cparams("parallel", "parallel"),
        name="gdn",
    )(qkvz, qkvz, qkvz, qkvz, conv_w, conv_w, conv_w, gates, head_par, norm_w)


def _kv_body(m_ref, g_ref, wk_ref, wv_ref, k_ref, v_ref):
    mn = _rms(m_ref[...], g_ref[...]).astype(BF16)
    k_ref[...] = _dot(mn, wk_ref[...]).astype(BF16)
    v_ref[...] = _dot(mn, wv_ref[...]).astype(BF16)


def _mem_kv(mem2d, norm_w, wk, wv, tm=512):
    r, d = mem2d.shape
    full = lambda shape: pl.BlockSpec(shape, lambda i: (0,) * len(shape))
    tile = pl.BlockSpec((tm, d), lambda i: (i, 0))
    return pl.pallas_call(
        _kv_body,
        grid=(r // tm,),
        in_specs=[tile, full((1, d)), full(wk.shape), full(wv.shape)],
        out_specs=[tile, tile],
        out_shape=[jax.ShapeDtypeStruct((r, d), BF16)] * 2,
        compiler_params=_cparams("parallel"),
        name="mem_kv",
    )(mem2d, norm_w, wk, wv)


def _mix_xattn_body(x_ref, y5_ref, yg_ref, wo5_ref, wog_ref, g_ref, wq_ref, k_ref, v_ref, wo_ref, o_ref):
    x1 = x_ref[...] + _dot(y5_ref[...], wo5_ref[...]) + _dot(yg_ref[...], wog_ref[...])
    xn = _rms(x1, g_ref[...]).astype(BF16)
    q = (_dot(xn, wq_ref[...]) * (XA_HEAD_DIM ** -0.5)).astype(BF16)
    heads = []
    for h in range(XA_HEADS):
        sl = slice(h * XA_HEAD_DIM, (h + 1) * XA_HEAD_DIM)
        s = _dot_nt(q[:, sl], k_ref[:, sl])
        p = jnp.exp(s - jnp.max(s, axis=-1, keepdims=True))
        p = p / jnp.sum(p, axis=-1, keepdims=True)
        heads.append(_dot(p.astype(BF16), v_ref[:, sl]).astype(BF16))
    o_ref[...] = x1 + _dot(jnp.concatenate(heads, axis=1), wo_ref[...])


def _mix_xattn(x2d, y5, yg, wo5, wog, norm_w, wq, kmem, vmem, wo, seq, mem_len, tm=512):
    t, d = x2d.shape
    per_b = seq // tm
    full = lambda shape: pl.BlockSpec(shape, lambda i: (0,) * len(shape))
    return pl.pallas_call(
        _mix_xattn_body,
        grid=(t // tm,),
        in_specs=[pl.BlockSpec((tm, d), lambda i: (i, 0)),
                  pl.BlockSpec((tm, S5_WIDTH), lambda i: (i, 0)),
                  pl.BlockSpec((tm, GDN_WIDTH), lambda i: (i, 0)),
                  full(wo5.shape), full(wog.shape), full((1, d)), full(wq.shape),
                  pl.BlockSpec((mem_len, d), lambda i: (i // per_b, 0)),
                  pl.BlockSpec((mem_len, d), lambda i: (i // per_b, 0)),
                  full(wo.shape)],
        out_specs=pl.BlockSpec((tm, d), lambda i: (i, 0)),
        out_shape=jax.ShapeDtypeStruct((t, d), F32),
        compiler_params=_cparams("parallel"),
        name="mix_xattn",
    )(x2d, y5, yg, wo5, wog, norm_w, wq, kmem, vmem, wo)


ROUTE_EXPERT_LANE0 = 4


def _router_body(x_ref, g_ref, w_ref, b_ref, tri_ref, xn_ref, mi_ref, mf_ref, cnt_ref, carry):
    @pl.when(pl.program_id(0) == 0)
    def _():
        carry[...] = jnp.zeros_like(carry)

    xn = _rms(x_ref[...], g_ref[...])
    xn_ref[...] = xn
    logits = jnp.dot(xn, w_ref[...], preferred_element_type=F32, precision=lax.Precision.HIGHEST) + b_ref[...]
    tm = logits.shape[0]
    lane = lax.broadcasted_iota(I32, (tm, LANES), 1)
    neg = jnp.float32(-jnp.inf)
    big = jnp.int32(LANES)

    def top(vals):
        m = jnp.max(vals, axis=-1, keepdims=True)
        idx = jnp.min(jnp.where(vals == m, lane, big), axis=-1, keepdims=True)
        return m, idx

    is_g = lane < MOE_GROUPS
    gl = jnp.where(is_g, logits, neg)
    gmax, gidx = top(gl)
    p_top = 1.0 / jnp.sum(jnp.where(is_g, jnp.exp(gl - gmax), 0.0), axis=-1, keepdims=True)
    elane = lane - ROUTE_EXPERT_LANE0
    in_grp = jnp.logical_and(jnp.logical_and(elane >= 0, elane < MOE_EXPERTS), (elane // MOE_PER_GROUP) == gidx)
    es = jnp.where(in_grp, logits, neg)
    m1, i1 = top(es)
    m2, i2 = top(jnp.where(lane == i1, neg, es))
    e21 = jnp.exp(m2 - m1)
    w1 = p_top / (1.0 + e21)
    w2 = p_top * e21 / (1.0 + e21)
    e1 = i1 - ROUTE_EXPERT_LANE0
    e2 = i2 - ROUTE_EXPERT_LANE0

    a1 = (lane == e1).astype(F32)
    a2 = (lane == e2).astype(F32)
    both = a1 + a2
    before = _dot(tri_ref[...], both.astype(BF16)) - both + carry[...]
    r1 = jnp.sum(a1 * before, axis=-1, keepdims=True).astype(I32)
    r2 = jnp.sum(a2 * before, axis=-1, keepdims=True).astype(I32)
    carry[...] = carry[...] + jnp.sum(both, axis=0, keepdims=True)
    cnt_ref[...] = jnp.broadcast_to(carry[...], cnt_ref.shape)

    mi_ref[...] = jnp.where(lane == 0, e1, jnp.where(lane == 1, e2, jnp.where(lane == 2, r1, jnp.where(lane == 3, r2, 0))))
    mf_ref[...] = jnp.where(lane == 0, w1, jnp.where(lane == 1, w2, 0.0))


def _router(x2d, norm_w, w_route, b_route, tm=512):
    t, d = x2d.shape
    tri = jnp.tril(jnp.ones((tm, tm), BF16))
    full = lambda shape: pl.BlockSpec(shape, lambda i: (0,) * len(shape))
    return pl.pallas_call(
        _router_body,
        grid=(t // tm,),
        in_specs=[pl.BlockSpec((tm, d), lambda i: (i, 0)), full((1, d)), full(w_route.shape), full((1, LANES)),
                  full((tm, tm))],
        out_specs=[pl.BlockSpec((tm, d), lambda i: (i, 0)),
                   pl.BlockSpec((tm, LANES), lambda i: (i, 0)),
                   pl.BlockSpec((tm, LANES), lambda i: (i, 0)),
                   pl.BlockSpec((8, LANES), lambda i: (0, 0))],
        out_shape=[jax.ShapeDtypeStruct((t, d), F32),
                   jax.ShapeDtypeStruct((t, LANES), I32),
                   jax.ShapeDtypeStruct((t, LANES), F32),
                   jax.ShapeDtypeStruct((8, LANES), F32)],
        scratch_shapes=[pltpu.VMEM((1, LANES), F32)],
        compiler_params=_cparams("arbitrary"),
        name="moe_router",
    )(x2d, norm_w, w_route, b_route, tri)


def _dispatch_body(dest_ref, xn_ref, xs_in_ref, xs_ref, sem):
    del xs_in_ref
    tm = xn_ref.shape[0]

    def start(r, c):
        for k in range(MOE_TOPK):
            pltpu.make_async_copy(xn_ref.at[pl.ds(r, 1), :], xs_ref.at[pl.ds(dest_ref[0, MOE_TOPK * r + k], 1), :],
                                  sem.at[k]).start(priority=k)
        return c

    lax.fori_loop(0, tm, start, 0, unroll=8)
    for k in range(MOE_TOPK):
        pltpu.make_async_copy(xn_ref, xs_ref.at[pl.ds(0, tm), :], sem.at[k]).wait()


def _dispatch(dest3, xn, xs_init, tm=256):
    t, d = xn.shape
    return pl.pallas_call(
        _dispatch_body,
        grid=(t // tm,),
        in_specs=[pl.BlockSpec((None, 1, 2 * tm), lambda i: (i, 0, 0), memory_space=pltpu.SMEM),
                  pl.BlockSpec((tm, d), lambda i: (i, 0)),
                  pl.BlockSpec(memory_space=pl.ANY)],
        out_specs=pl.BlockSpec(memory_space=pl.ANY),
        out_shape=jax.ShapeDtypeStruct(xs_init.shape, xs_init.dtype),
        scratch_shapes=[pltpu.SemaphoreType.DMA((MOE_TOPK,))],
        input_output_aliases={2: 0},
        compiler_params=_cparams("arbitrary"),
        name="moe_dispatch",
    )(dest3, xn, xs_init)


def _experts_body(te_ref, na_ref, x_ref, wg_ref, wu_ref, wd_ref, y_ref):
    del te_ref

    @pl.when(pl.program_id(0) < na_ref[0])
    def _():
        x = x_ref[...].astype(BF16)
        gt = _dot(x, wg_ref[...])
        up = _dot(x, wu_ref[...])
        hid = (gt * jax.nn.sigmoid(gt) * up).astype(BF16)
        y_ref[...] = _dot(hid, wd_ref[...])

    @pl.when(pl.program_id(0) >= na_ref[0])
    def _():
        y_ref[...] = jnp.zeros_like(y_ref)


def _experts(tile_expert, n_active, xs, w_gate, w_up, w_down, tm):
    r, d = xs.shape
    f = w_gate.shape[2]
    grid_spec = pltpu.PrefetchScalarGridSpec(
        num_scalar_prefetch=2,
        grid=(r // tm,),
        in_specs=[pl.BlockSpec((tm, d), lambda i, te, na: (i, 0)),
                  pl.BlockSpec((None, d, f), lambda i, te, na: (te[i], 0, 0)),
                  pl.BlockSpec((None, d, f), lambda i, te, na: (te[i], 0, 0)),
                  pl.BlockSpec((None, f, d), lambda i, te, na: (te[i], 0, 0))],
        out_specs=pl.BlockSpec((tm, d), lambda i, te, na: (i, 0)),
    )
    return pl.pallas_call(
        _experts_body,
        grid_spec=grid_spec,
        out_shape=jax.ShapeDtypeStruct((r, d), F32),
        compiler_params=_cparams("arbitrary"),
        name="moe_experts",
    )(tile_expert, n_active, xs, w_gate, w_up, w_down)


def _combine_body(dest_ref, dnext_ref, x_ref, mf_ref, g_ref, ys_ref, o_ref, buf, sem):
    tm = x_ref.shape[0]
    i = pl.program_id(0)
    slot = lax.rem(i, 2)

    def gather(d_ref, s):
        def start(r, c):
            for k in range(MOE_TOPK):
                pltpu.make_async_copy(ys_ref.at[pl.ds(d_ref[0, MOE_TOPK * r + k], 1), :],
                                      buf.at[s, k, pl.ds(r, 1), :], sem.at[s, k]).start(priority=k)
            return c

        lax.fori_loop(0, tm, start, 0, unroll=8)

    @pl.when(i == 0)
    def _():
        gather(dest_ref, 0)

    @pl.when(i + 1 < pl.num_programs(0))
    def _():
        gather(dnext_ref, 1 - slot)

    for k in range(MOE_TOPK):
        pltpu.make_async_copy(ys_ref.at[pl.ds(0, tm), :], buf.at[slot, k], sem.at[slot, k]).wait()
    mf = mf_ref[...]
    y = x_ref[...] + mf[:, 0:1] * buf[slot, 0] + mf[:, 1:2] * buf[slot, 1]
    o_ref[...] = _rms(y, g_ref[...])


def _combine(dest3, x2d, mf, norm_w, ys, tm=256):
    t, d = x2d.shape
    last = t // tm - 1
    return pl.pallas_call(
        _combine_body,
        grid=(t // tm,),
        in_specs=[pl.BlockSpec((None, 1, 2 * tm), lambda i: (i, 0, 0), memory_space=pltpu.SMEM),
                  pl.BlockSpec((None, 1, 2 * tm), lambda i: (jnp.minimum(i + 1, last), 0, 0), memory_space=pltpu.SMEM),
                  pl.BlockSpec((tm, d), lambda i: (i, 0)),
                  pl.BlockSpec((tm, LANES), lambda i: (i, 0)),
                  pl.BlockSpec((1, d), lambda i: (0, 0)),
                  pl.BlockSpec(memory_space=pl.ANY)],
        out_specs=pl.BlockSpec((tm, d), lambda i: (i, 0)),
        out_shape=jax.ShapeDtypeStruct((t, d), F32),
        scratch_shapes=[pltpu.VMEM((2, MOE_TOPK, tm, d), F32), pltpu.SemaphoreType.DMA((2, MOE_TOPK))],
        compiler_params=_cparams("arbitrary"),
        name="moe_combine",
    )(dest3, dest3, x2d, mf, norm_w, ys)


MOE_ROW_TILE = 256


def _moe(x2d, norm_w, wg, bg, we, be, w_gate, w_up, w_down, norm_final):
    t, d = x2d.shape
    tm = MOE_ROW_TILE
    w_route = jnp.pad(jnp.concatenate([wg, we], axis=1), ((0, 0), (0, LANES - MOE_GROUPS - MOE_EXPERTS)))
    b_route = jnp.pad(jnp.concatenate([bg, be]), (0, LANES - MOE_GROUPS - MOE_EXPERTS))[None]
    xn, mi, mf, cnt = _router(x2d, norm_w, w_route, b_route)
    counts = cnt[0, :MOE_EXPERTS].astype(I32)
    padded = ((counts + tm - 1) // tm) * tm
    ends = jnp.cumsum(padded)
    offsets = ends - padded
    n_tiles = (2 * t + MOE_EXPERTS * (tm - 1)) // tm
    tile_start = jnp.arange(n_tiles, dtype=I32) * tm
    tile_expert = jnp.minimum(jnp.sum((ends[None, :] <= tile_start[:, None]).astype(I32), axis=1), MOE_EXPERTS - 1)
    n_active = (ends[-1:] // tm).astype(I32)
    dest = offsets[mi[:, 0:2]] + mi[:, 2:4]
    dest3 = dest.reshape(t // tm, 1, 2 * tm)
    xs = _dispatch(dest3, xn, jnp.zeros((n_tiles * tm, d), F32), tm)
    ys = _experts(tile_expert, n_active, xs, w_gate, w_up, w_down, tm)
    return _combine(dest3, x2d, mf, norm_final, ys, tm)


def kernel(x, mem, norm_mix, w_in, w_out,
           s5_lam_re_f, s5_lam_im_f, s5_log_step_f, s5_b_re_f, s5_b_im_f, s5_c_re_f, s5_c_im_f,
           s5_lam_re_b, s5_lam_im_b, s5_log_step_b, s5_b_re_b, s5_b_im_b, s5_c_re_b, s5_c_im_b,
           s5_d, s5_w_glu, s5_b_glu, s5_norm,
           gdn_conv, gdn_a_log_f, gdn_dt_bias_f, gdn_a_log_b, gdn_dt_bias_b, gdn_norm,
           norm_xattn, norm_mem, xa_wq, xa_wk, xa_wv, xa_wo,
           norm_moe, router_group_w, router_group_b, router_expert_w, router_expert_b,
           moe_w_gate, moe_w_up, moe_w_down, norm_final):
    bsz, seq, d = x.shape
    t = bsz * seq
    l = 0
    x2d = x.reshape(t, d)
    wi = w_in[l]
    wut = wi[:, :S5_WIDTH].T.astype(BF16)
    wqkvz = wi[:, S5_WIDTH:S5_WIDTH + 4 * GDN_WIDTH].astype(BF16)
    wg = wi[:, S5_WIDTH + 4 * GDN_WIDTH:].reshape(d, 4, GDN_HEADS)
    wg = jnp.pad(jnp.swapaxes(wg, 1, 2), ((0, 0), (0, 0), (0, 4))).reshape(d, GDN_HEADS * 8)
    wgt = jnp.pad(wg, ((0, 0), (0, LANES - GDN_HEADS * 8))).astype(BF16)
    u3, qkvz, gates = _in_proj(x2d, norm_mix[l][None], wut, wqkvz, wgt)
    s5p = dict(lam_re_f=s5_lam_re_f[l], lam_im_f=s5_lam_im_f[l], log_step_f=s5_log_step_f[l],
               b_re_f=s5_b_re_f[l], b_im_f=s5_b_im_f[l], c_re_f=s5_c_re_f[l], c_im_f=s5_c_im_f[l],
               lam_re_b=s5_lam_re_b[l], lam_im_b=s5_lam_im_b[l], log_step_b=s5_log_step_b[l],
               b_re_b=s5_b_re_b[l], b_im_b=s5_b_im_b[l], c_re_b=s5_c_re_b[l], c_im_b=s5_c_im_b[l], d=s5_d[l])
    g3 = _s5_scan(u3, s5p, seq // S5_CHUNK)
    y_s5 = _s5_post(g3, s5_w_glu[l].T.astype(BF16), s5_b_glu[l][:, None], s5_norm[l][:, None])

    head_par = jnp.stack([gdn_a_log_f[l], gdn_dt_bias_f[l], gdn_a_log_b[l], gdn_dt_bias_b[l]], axis=1)
    head_par = jnp.broadcast_to(jnp.pad(head_par, ((0, 0), (0, 4)))[:, :, None], (GDN_HEADS, 8, LANES))
    y_gdn = _gdn(qkvz, gates, gdn_conv[l], head_par, gdn_norm[l][None], bsz, seq)

    mem_len = mem.shape[1]
    kmem, vmem = _mem_kv(mem.reshape(bsz * mem_len, d), norm_mem[l][None],
                         xa_wk[l].astype(BF16), xa_wv[l].astype(BF16))
    wo = w_out[l].astype(BF16)
    x2 = _mix_xattn(x2d, y_s5, y_gdn, wo[:S5_WIDTH], wo[S5_WIDTH:], norm_xattn[l][None],
                    xa_wq[l].astype(BF16), kmem, vmem, xa_wo[l].astype(BF16), seq, mem_len)
    y = _moe(x2, norm_moe[l][None], router_group_w[l], router_group_b[l], router_expert_w[l], router_expert_b[l],
             moe_w_gate[l].astype(BF16), moe_w_up[l].astype(BF16), moe_w_down[l].astype(BF16), norm_final[None])
    return y.reshape(bsz, seq, d)
```

```python
import functools
import math

import jax
import jax.numpy as jnp
from jax import lax
from jax.experimental import pallas as pl
from jax.experimental.pallas import tpu as pltpu

F32 = jnp.float32
BF16 = jnp.bfloat16
I32 = jnp.int32

D_MODEL = 1024
S5_WIDTH = 512
S5_GROUP = 16
S5_GROUPS = 32
S5_STATE = 64
S5_CHUNK = 128
GDN_HEADS = 4
GDN_HEAD_DIM = 128
GDN_WIDTH = 512
GDN_CONV = 5
GDN_CHUNK = 64
XA_HEADS = 4
XA_HEAD_DIM = 256
MOE_GROUPS = 4
MOE_PER_GROUP = 8
MOE_EXPERTS = 32
MOE_TOPK = 2
D_EXPERT = 256
RMS_EPS = 1e-6
L2_EPS = 1e-6
LANES = 128
VMEM_LIMIT = 56 * 1024 * 1024


def _cparams(*sem):
    return pltpu.CompilerParams(dimension_semantics=tuple(sem), vmem_limit_bytes=VMEM_LIMIT)


def _rms(x, gain):
    return x * lax.rsqrt(jnp.mean(x * x, axis=-1, keepdims=True) + RMS_EPS) * gain


def _dot(a, b):
    return jnp.dot(a, b, preferred_element_type=F32)


def _dot_nt(a, b):
    return lax.dot_general(a, b, (((1,), (1,)), ((), ())), preferred_element_type=F32)


def _dot_tn(a, b):
    return lax.dot_general(a, b, (((0,), (0,)), ((), ())), preferred_element_type=F32)


def _in_proj_body(x_ref, g_ref, wut_ref, wqkvz_ref, wgt_ref, u_ref, qkvz_ref, gates_ref):
    h = _rms(x_ref[...], g_ref[...]).astype(BF16)
    ut = _dot_nt(wut_ref[...], h)
    gt = _dot_nt(wgt_ref[...], h)
    for j in range(u_ref.shape[0]):
        u_ref[j] = ut[:, j * S5_CHUNK:(j + 1) * S5_CHUNK]
        gates_ref[j] = gt[:, j * S5_CHUNK:(j + 1) * S5_CHUNK]
    qkvz_ref[...] = _dot(h, wqkvz_ref[...]).astype(BF16)


def _in_proj(x2d, norm_w, wut, wqkvz, wgt, tm=512):
    t = x2d.shape[0]
    nck = tm // S5_CHUNK
    full = lambda shape: pl.BlockSpec(shape, lambda i: (0,) * len(shape))
    return pl.pallas_call(
        _in_proj_body,
        grid=(t // tm,),
        in_specs=[pl.BlockSpec((tm, D_MODEL), lambda i: (i, 0)),
                  full((1, D_MODEL)), full(wut.shape), full(wqkvz.shape), full(wgt.shape)],
        out_specs=[pl.BlockSpec((nck, S5_WIDTH, S5_CHUNK), lambda i: (i, 0, 0)),
                   pl.BlockSpec((tm, wqkvz.shape[1]), lambda i: (i, 0)),
                   pl.BlockSpec((nck, wgt.shape[0], S5_CHUNK), lambda i: (i, 0, 0))],
        out_shape=[jax.ShapeDtypeStruct((t // S5_CHUNK, S5_WIDTH, S5_CHUNK), F32),
                   jax.ShapeDtypeStruct((t, wqkvz.shape[1]), BF16),
                   jax.ShapeDtypeStruct((t // S5_CHUNK, wgt.shape[0], S5_CHUNK), F32)],
        compiler_params=_cparams("parallel"),
        name="in_proj",
    )(x2d, norm_w, wut, wqkvz, wgt)


def _cmul(ar, ai, br, bi):
    return ar * br - ai * bi, ar * bi + ai * br


def _cpow_int(lr, li, expo, nbits):
    res_r = jnp.ones(jnp.broadcast_shapes(lr.shape, expo.shape), F32)
    res_i = jnp.zeros_like(res_r)
    for b in range(nbits):
        bit = ((expo >> b) & 1) == 1
        nr, ni = _cmul(res_r, res_i, lr, li)
        res_r = jnp.where(bit, nr, res_r)
        res_i = jnp.where(bit, ni, res_i)
        if b + 1 < nbits:
            lr, li = _cmul(lr, li, lr, li)
    return res_r, res_i


def _lam_bar(re, im, step):
    er = jnp.exp(step * re)
    return er * jnp.cos(step * im), er * jnp.sin(step * im)


def _zoh_coef(re, im, lr, li):
    den = re * re + im * im
    return ((lr - 1.0) * re + li * im) / den, (li * re - (lr - 1.0) * im) / den


def _s5_body(u_ref, lrow_ref, lcol_ref, l256_ref, b_ref, bt_ref, c_ref, ct_ref, d_ref,
             o_ref, vf_ref, vb_ref, m_ref, win_ref, wout_ref, sf_ref, sb_ref, hf_ref, hb_ref, uc_ref, *, nchunk):
    L = S5_CHUNK
    P = S5_STATE
    n_rows = u_ref.shape[0]
    nb = n_rows // nchunk
    lane_i = lax.broadcasted_iota(I32, (1, L), 1)

    lcol = lcol_ref[...]
    lbc_r, lbc_i = _lam_bar(lcol[:, 0:2], lcol[:, 2:4], lcol[:, 4:6])
    kc_r, kc_i = _zoh_coef(lcol[:, 0:2], lcol[:, 2:4], lbc_r, lbc_i)
    lrow = lrow_ref[...]
    lbr_r, lbr_i = _lam_bar(lrow[0:2], lrow[2:4], lrow[4:6])
    kr_r, kr_i = _zoh_coef(lrow[0:2], lrow[2:4], lbr_r, lbr_i)
    lf_r, lf_i, lb_r, lb_i = lbc_r[:, 0:1], lbc_i[:, 0:1], lbc_r[:, 1:2], lbc_i[:, 1:2]
    kfr_c, kfi_c, kbr_c, kbi_c = kc_r[:, 0:1], kc_i[:, 0:1], kc_r[:, 1:2], kc_i[:, 1:2]
    kfr_r, kfi_r, kbr_r, kbi_r = kr_r[0:1], kr_i[0:1], kr_r[1:2], kr_i[1:2]

    pwf_r, pwf_i = _cpow_int(lf_r, lf_i, lane_i, 7)
    rvf_r, rvf_i = _cpow_int(lf_r, lf_i, (L - 1) - lane_i, 7)
    pwb_r, pwb_i = _cpow_int(lb_r, lb_i, lane_i, 7)
    rvb_r, rvb_i = _cpow_int(lb_r, lb_i, L - lane_i, 8)
    nxf_r, nxf_i = _cmul(pwf_r, pwf_i, lf_r, lf_i)

    bf_r = kfr_c * b_ref[0] - kfi_c * b_ref[1]
    bf_i = kfr_c * b_ref[1] + kfi_c * b_ref[0]
    bb_r = kbr_c * b_ref[2] - kbi_c * b_ref[3]
    bb_i = kbr_c * b_ref[3] + kbi_c * b_ref[2]
    btf_r = kfr_r * bt_ref[0] - kfi_r * bt_ref[1]
    btf_i = kfr_r * bt_ref[1] + kfi_r * bt_ref[0]
    btb_r = kbr_r * bt_ref[2] - kbi_r * bt_ref[3]
    btb_i = kbr_r * bt_ref[3] + kbi_r * bt_ref[2]

    def taps(c_r, c_i, bt_r, bt_i, pw_r, pw_i):
        cb_r = (bt_r[:, None, :] * c_r[None, :, :] - bt_i[:, None, :] * c_i[None, :, :]).reshape(256, P)
        cb_i = (bt_r[:, None, :] * c_i[None, :, :] + bt_i[:, None, :] * c_r[None, :, :]).reshape(256, P)
        k = (jnp.dot(cb_r, pw_r, preferred_element_type=F32, precision=lax.Precision.HIGHEST)
             - jnp.dot(cb_i, pw_i, preferred_element_type=F32, precision=lax.Precision.HIGHEST))
        return k, jnp.sum(cb_r, axis=1, keepdims=True)

    kf, _ = taps(c_ref[0], c_ref[1], btf_r, btf_i, pwf_r, pwf_i)
    kb, kb0 = taps(c_ref[2], c_ref[3], btb_r, btb_i, rvb_r, rvb_i)
    is0 = lane_i == 0
    vf_ref[...] = kf + jnp.where(is0, kb0, 0.0)
    vb_ref[...] = jnp.where(is0, 0.0, kb)

    row_i = lax.broadcasted_iota(I32, (L, L), 0)
    col_i = lax.broadcasted_iota(I32, (L, L), 1)
    fwd_lane = col_i + row_i < L

    def build_ci(ci, carry):
        for co in range(S5_GROUP):
            r = ci * S5_GROUP + co
            taps_rows = jnp.where(fwd_lane, jnp.broadcast_to(vf_ref[pl.ds(r, 1), :], (L, L)),
                                  jnp.broadcast_to(vb_ref[pl.ds(r, 1), :], (L, L)))
            m_ref[pl.ds(pl.multiple_of(ci * L, L), L), co * L:(co + 1) * L] = pltpu.roll(
                taps_rows, 0, 1, stride=1, stride_axis=0).astype(BF16)
        return carry

    lax.fori_loop(0, S5_GROUP, build_ci, 0)

    for ci in range(S5_GROUP):
        sl = slice(ci * L, (ci + 1) * L)
        br, bi = bf_r[:, ci:ci + 1], bf_i[:, ci:ci + 1]
        win_ref[0 * P:1 * P, sl] = (rvf_r * br - rvf_i * bi).astype(BF16)
        win_ref[1 * P:2 * P, sl] = (rvf_r * bi + rvf_i * br).astype(BF16)
        br, bi = bb_r[:, ci:ci + 1], bb_i[:, ci:ci + 1]
        win_ref[2 * P:3 * P, sl] = (pwb_r * br - pwb_i * bi).astype(BF16)
        win_ref[3 * P:4 * P, sl] = (pwb_r * bi + pwb_i * br).astype(BF16)
    for co in range(S5_GROUP):
        sl = slice(co * L, (co + 1) * L)
        cr, ci_ = ct_ref[0][:, co:co + 1], ct_ref[1][:, co:co + 1]
        wout_ref[0 * P:1 * P, sl] = (cr * nxf_r - ci_ * nxf_i).astype(BF16)
        wout_ref[1 * P:2 * P, sl] = (-(cr * nxf_i + ci_ * nxf_r)).astype(BF16)
        cr, ci_ = ct_ref[2][:, co:co + 1], ct_ref[3][:, co:co + 1]
        wout_ref[2 * P:3 * P, sl] = (cr * rvb_r - ci_ * rvb_i).astype(BF16)
        wout_ref[3 * P:4 * P, sl] = (-(cr * rvb_i + ci_ * rvb_r)).astype(BF16)

    for ci in range(S5_GROUP):
        uc_ref[ci] = u_ref[:, ci, :]
    ucat = jnp.concatenate([uc_ref[ci].astype(BF16) for ci in range(S5_GROUP)], axis=1)

    summ = _dot_nt(ucat, win_ref[...])
    sf_ref[...] = summ[:, :2 * P]
    sb_ref[...] = summ[:, 2 * P:]
    l256 = l256_ref[...]
    a_mul, a_im = _lam_bar(l256[0:1], l256[1:2], l256[2:3])
    for _ in range(7):
        a_mul, a_im = _cmul(a_mul, a_im, a_mul, a_im)
    lane256 = lax.broadcasted_iota(I32, (1, 4 * P), 1)
    b_mul = jnp.where((lane256 // P) % 2 == 0, -a_im, a_im)

    hf = jnp.zeros((nb, 2 * P), F32)
    hb = jnp.zeros((nb, 2 * P), F32)
    for c in range(nchunk):
        cr = nchunk - 1 - c
        rows_f = pl.ds(c, nb, stride=nchunk)
        rows_b = pl.ds(cr, nb, stride=nchunk)
        hf_ref[rows_f, :] = hf
        hb_ref[rows_b, :] = hb
        hf = a_mul[:, :2 * P] * hf + b_mul[:, :2 * P] * pltpu.roll(hf, P, 1) + sf_ref[rows_f, :]
        hb = a_mul[:, 2 * P:] * hb + b_mul[:, 2 * P:] * pltpu.roll(hb, P, 1) + sb_ref[rows_b, :]
    hprev = jnp.concatenate([hf_ref[...], hb_ref[...]], axis=1).astype(BF16)

    nblk = 2 * L
    for j in range(S5_GROUP * L // nblk):
        y = _dot(ucat, m_ref[:, j * nblk:(j + 1) * nblk]) + _dot(hprev, wout_ref[:, j * nblk:(j + 1) * nblk])
        for q in range(nblk // L):
            co = j * (nblk // L) + q
            yc = y[:, q * L:(q + 1) * L] + d_ref[co:co + 1, :] * uc_ref[co]
            o_ref[:, co, :] = 0.5 * yc * (1.0 + lax.erf(yc * (2.0 ** -0.5)))


def _s5_scan(u3, p, nchunk):
    n = u3.shape[0]
    g, grp, st, L = S5_GROUPS, S5_GROUP, S5_STATE, S5_CHUNK
    step_f = jnp.exp(p["log_step_f"])[:, None] * jnp.ones((1, st), F32)
    step_b = jnp.exp(p["log_step_b"])[:, None] * jnp.ones((1, st), F32)
    zeros = jnp.zeros((g, st), F32)
    lrow = jnp.stack([p["lam_re_f"], p["lam_re_b"], p["lam_im_f"], p["lam_im_b"], step_f, step_b, zeros, zeros], axis=1)
    lcol = jnp.swapaxes(lrow, 1, 2)
    cat4 = lambda f, b: jnp.concatenate([f, f, b, b], axis=1)
    z256 = jnp.zeros((g, 4 * st), F32)
    l256 = jnp.stack([cat4(p["lam_re_f"], p["lam_re_b"]), cat4(p["lam_im_f"], p["lam_im_b"]),
                      cat4(step_f, step_b)] + [z256] * 5, axis=1)
    b4 = jnp.stack([p["b_re_f"], p["b_im_f"], p["b_re_b"], p["b_im_b"]], axis=1)
    bt4 = jnp.swapaxes(b4, 2, 3)
    c4 = jnp.stack([p["c_re_f"], p["c_im_f"], p["c_re_b"], p["c_im_b"]], axis=1)
    ct4 = jnp.swapaxes(c4, 2, 3)
    dbc = jnp.broadcast_to(p["d"].reshape(g, grp, 1), (g, grp, L))
    per_g = lambda *shape: pl.BlockSpec((None,) + shape, lambda i: (i,) + (0,) * len(shape))
    return pl.pallas_call(
        functools.partial(_s5_body, nchunk=nchunk),
        grid=(g,),
        in_specs=[pl.BlockSpec((n, grp, L), lambda i: (0, i, 0)),
                  per_g(8, st), per_g(st, 8), per_g(8, 4 * st), per_g(4, st, grp), per_g(4, grp, st),
                  per_g(4, grp, st), per_g(4, st, grp), per_g(grp, L)],
        out_specs=pl.BlockSpec((n, grp, L), lambda i: (0, i, 0)),
        out_shape=jax.ShapeDtypeStruct(u3.shape, F32),
        scratch_shapes=[pltpu.VMEM((grp * grp, L), F32), pltpu.VMEM((grp * grp, L), F32),
                        pltpu.VMEM((grp * L, grp * L), BF16),
                        pltpu.VMEM((4 * st, grp * L), BF16), pltpu.VMEM((4 * st, grp * L), BF16),
                        pltpu.VMEM((n, 2 * st), F32), pltpu.VMEM((n, 2 * st), F32),
                        pltpu.VMEM((n, 2 * st), F32), pltpu.VMEM((n, 2 * st), F32),
                        pltpu.VMEM((grp, n, L), F32)],
        compiler_params=_cparams("parallel"),
        name="s5_scan",
    )(u3, lrow, lcol, l256, b4, bt4, c4, ct4, dbc)


def _s5_post_body(g_ref, wt_ref, b_ref, nw_ref, o_ref):
    for j in range(g_ref.shape[0]):
        g = g_ref[j]
        z = _dot(wt_ref[...], g.astype(BF16)) + b_ref[...]
        y = g * jax.nn.sigmoid(z)
        y = y * lax.rsqrt(jnp.mean(y * y, axis=0, keepdims=True) + RMS_EPS) * nw_ref[...]
        o_ref[j * S5_CHUNK:(j + 1) * S5_CHUNK, :] = y.T.astype(BF16)


def _s5_post(g3, w_glu_t, b_glu_col, norm_col, nck=4):
    n = g3.shape[0]
    full = lambda shape: pl.BlockSpec(shape, lambda i: (0,) * len(shape))
    return pl.pallas_call(
        _s5_post_body,
        grid=(n // nck,),
        in_specs=[pl.BlockSpec((nck, S5_WIDTH, S5_CHUNK), lambda i: (i, 0, 0)),
                  full(w_glu_t.shape), full(b_glu_col.shape), full(norm_col.shape)],
        out_specs=pl.BlockSpec((nck * S5_CHUNK, S5_WIDTH), lambda i: (i, 0)),
        out_shape=jax.ShapeDtypeStruct((n * S5_CHUNK, S5_WIDTH), BF16),
        compiler_params=_cparams("parallel"),
        name="s5_post",
    )(g3, w_glu_t, b_glu_col, norm_col)


GDN_BLOCK = 128
GDN_PAIR = 2
GDN_PREP_BATCH = 8


def _packed_tri_inverse(lps, low, upp, bd16, rings):
    def pk(xs, ys):
        outs = []
        for a, b in zip(xs, ys):
            lhs = jnp.concatenate([jnp.where(low, a, 0.0), jnp.where(upp, a, 0.0)], axis=1).astype(BF16)
            rhs = jnp.concatenate([jnp.where(low, b, 0.0), jnp.where(upp, b, 0.0)], axis=0).astype(BF16)
            outs.append(_dot(lhs, rhs))
        return outs

    d = [jnp.where(bd16, lp, 0.0) for lp in lps]
    d2 = pk(d, d)
    d4 = pk(d2, d2)
    d8 = pk(d4, d4)
    a = [y - x - p for x, y, p in zip(d, d2, pk(d, d2))]
    a = [x + y + p for x, y, p in zip(a, d4, pk(a, d4))]
    a = [x + y + p for x, y, p in zip(a, d8, pk(a, d8))]
    for ring in rings:
        n = [jnp.where(ring, lp, 0.0) for lp in lps]
        t = [x + p for x, p in zip(n, pk(a, n))]
        a = [x - y - p for x, y, p in zip(a, t, pk(t, a))]
    return a


def _gdn_body(q_ref, k_ref, v_ref, z_ref, wq_ref, wk_ref, wv_ref, g_ref, hp_ref, nw_ref, o_ref,
              qs, ks, vs, os_, sg, cf, cb, uf, ub, wqf, wqb, qkf, qkb, kdtf, kdtb, eglf, eglb, xpad):
    seq = q_ref.shape[0]
    C = GDN_BLOCK
    hd = GDN_HEAD_DIM
    nck = seq // C
    heads = range(GDN_PAIR)

    pad = 8
    half = (GDN_CONV - 1) // 2
    xpad[0:pad, :] = jnp.zeros((pad, LANES), F32)
    xpad[pad + seq:2 * pad + seq, :] = jnp.zeros((pad, LANES), F32)

    def conv_silu(x_ref, w_ref, j):
        cols = pl.ds(pl.multiple_of(j * hd, hd), hd)
        xpad[pad:pad + seq, :] = x_ref[:, cols].astype(F32)
        w = w_ref[:, cols]
        acc = xpad[pad - half:pad - half + seq, :] * w[0:1]
        for tap in range(1, GDN_CONV):
            acc = acc + xpad[pad - half + tap:pad - half + tap + seq, :] * w[tap:tap + 1]
        return acc * jax.nn.sigmoid(acc)

    def l2n(x):
        return x * lax.rsqrt(jnp.sum(x * x, axis=-1, keepdims=True) + L2_EPS)

    def softplus(x):
        return jnp.maximum(x, 0.0) + jnp.log1p(jnp.exp(-jnp.abs(x)))

    lane = lax.broadcasted_iota(I32, (1, C), 1)

    def prologue(j, carry):
        qs[j] = l2n(conv_silu(q_ref, wq_ref, j)) * (hd ** -0.5)
        ks[j] = l2n(conv_silu(k_ref, wk_ref, j))
        vs[j] = conv_silu(v_ref, wv_ref, j)
        os_[j] = jnp.zeros((seq, hd), F32)
        g = g_ref[:, pl.ds(pl.multiple_of(8 * j, 8), 8), :].reshape(nck * 8, C)
        hp = hp_ref[j]
        sg[j] = jax.nn.sigmoid(g)
        gl_f = -jnp.exp(hp[0:1]) * softplus(g + hp[1:2])
        gl_b = -jnp.exp(hp[2:3]) * softplus(g + hp[3:4])
        sh = 1
        while sh < C:
            gl_f = gl_f + jnp.where(lane >= sh, pltpu.roll(gl_f, sh, 1), 0.0)
            gl_b = gl_b + jnp.where(lane < C - sh, pltpu.roll(gl_b, C - sh, 1), 0.0)
            sh *= 2
        cf[j] = gl_f
        cb[j] = gl_b
        return carry

    lax.fori_loop(0, GDN_PAIR, prologue, 0)

    ri = lax.broadcasted_iota(I32, (C, C), 0)
    ci = lax.broadcasted_iota(I32, (C, C), 1)
    low, upp = ri > ci, ri < ci
    low_i, upp_i = ri >= ci, ri <= ci
    same = lambda w: (ri // w) == (ci // w)
    bd16 = same(16)
    rings = []
    w = 32
    while w <= C:
        rings.append(jnp.logical_and(same(w), jnp.logical_not(same(w // 2))))
        w *= 2

    nbatch = math.gcd(GDN_PREP_BATCH // GDN_PAIR, nck)

    def column(ref, r):
        rows = jnp.broadcast_to(ref[pl.ds(r, 1), :], (C, C))
        return rows.T, rows

    def prepare(it, carry):
        items = [(j, it * nbatch + i) for i in range(nbatch) for j in heads]
        sls = [pl.ds(pl.multiple_of(c * C, C), C) for _, c in items]
        g_f, g_b, bt_f, bt_b, dec_f, dec_b, kb_f, kb_b, aq = [], [], [], [], [], [], [], [], []
        for (j, c), sl in zip(items, sls):
            k = ks[j, sl, :]
            gfc, gfr = column(cf.at[j], c * 8 + 2)
            gbc, gbr = column(cb.at[j], c * 8 + 3)
            g_f.append(gfc)
            g_b.append(gbc)
            bt_f.append(column(sg.at[j], c * 8)[0])
            bt_b.append(column(sg.at[j], c * 8 + 1)[0])
            dec_f.append(jnp.where(low_i, jnp.exp(jnp.where(low_i, gfc - gfr, 0.0)), 0.0))
            dec_b.append(jnp.where(upp_i, jnp.exp(jnp.where(upp_i, gbc - gbr, 0.0)), 0.0))
            kb_f.append(k * bt_f[-1])
            kb_b.append(k * bt_b[-1])
            aq.append(_dot_nt(jnp.concatenate([kb_f[-1], kb_b[-1], qs[j, sl, :]], axis=0).astype(BF16),
                              k.astype(BF16)))
        lps = [jnp.where(low, x[:C] * df, 0.0) + jnp.where(upp, x[C:2 * C] * db, 0.0)
               for x, df, db in zip(aq, dec_f, dec_b)]
        inv = _packed_tri_inverse(lps, low, upp, bd16, rings)
        for n, ((j, c), sl) in enumerate(zip(items, sls)):
            q, k, v = qs[j, sl, :], ks[j, sl, :], vs[j, sl, :]
            for rev, g_c, kb, beta, dec, msk, u_s, wq_s, qk_s, kdt_s, egl_s in (
                    (False, g_f[n], kb_f[n], bt_f[n], dec_f[n], low, uf, wqf, qkf, kdtf, eglf),
                    (True, g_b[n], kb_b[n], bt_b[n], dec_b[n], upp, ub, wqb, qkb, kdtb, eglb)):
                eg = jnp.exp(g_c)
                rhs = jnp.concatenate([v * beta, kb * eg], axis=1)
                uw = rhs + _dot(jnp.where(msk, inv[n], 0.0).astype(BF16), rhs.astype(BF16))
                glast = g_c[0:1] if rev else g_c[C - 1:C]
                u_s[j, sl, :] = uw[:, :C]
                wq_s[j, pl.ds(pl.multiple_of(c * 2 * C, 2 * C), 2 * C), :] = jnp.concatenate(
                    [uw[:, C:], q * eg], axis=0).astype(BF16)
                qk_s[j, sl, :] = (aq[n][2 * C:] * dec).astype(BF16)
                kdt_s[j, sl, :] = (k * jnp.exp(glast - g_c)).T.astype(BF16)
                egl_s[j, pl.ds(c, 1), :] = jnp.exp(glast)
        return carry

    lax.fori_loop(0, nck // nbatch, prepare, 0)

    def body(i, carry):
        chains = []
        for j in heads:
            chains.append((j, i, uf, wqf, qkf, kdtf, eglf))
            chains.append((j, nck - 1 - i, ub, wqb, qkb, kdtb, eglb))
        sls = [pl.ds(pl.multiple_of(c * C, C), C) for _, c, *_ in chains]
        ws_qs = [_dot(wq_s[j, pl.ds(pl.multiple_of(c * 2 * C, 2 * C), 2 * C), :], st.astype(BF16))
                 for (j, c, _, wq_s, *_), st in zip(chains, carry)]
        vnb = [(u_s[j, sl, :] - x[:C]).astype(BF16) for (j, _, u_s, *_), sl, x in zip(chains, sls, ws_qs)]
        new = [st * egl_s[j, pl.ds(c, 1), :] + _dot(kdt_s[j, sl, :], v)
               for (j, c, _, _, _, kdt_s, egl_s), sl, st, v in zip(chains, sls, carry, vnb)]
        for (j, _, _, _, qk_s, _, _), sl, x, v in zip(chains, sls, ws_qs, vnb):
            os_[j, sl, :] += x[C:] + _dot(qk_s[j, sl, :], v)
        return tuple(new)

    zero = jnp.zeros((hd, hd), F32)
    lax.fori_loop(0, nck, body, (zero,) * (2 * GDN_PAIR))

    for j in heads:
        o = os_[j]
        o = o * lax.rsqrt(jnp.mean(o * o, axis=-1, keepdims=True) + RMS_EPS) * nw_ref[...]
        z = z_ref[:, j * hd:(j + 1) * hd].astype(F32)
        o_ref[:, j * hd:(j + 1) * hd] = (o * (z * jax.nn.sigmoid(z))).astype(BF16)


def _gdn(qkvz, gates3, conv_w, head_par, norm_w, bsz, seq):
    t = bsz * seq
    hd = GDN_HEAD_DIM
    nh = GDN_HEADS
    np_ = GDN_PAIR
    wd = np_ * hd
    npairs = nh // np_
    nck = seq // GDN_BLOCK
    col = lambda off: pl.BlockSpec((seq, wd), lambda b, p: (b, off * npairs + p))
    wcol = lambda off: pl.BlockSpec((GDN_CONV, wd), lambda b, p: (0, off * npairs + p))
    per_head = lambda rows, dt: pltpu.VMEM((np_, rows, hd), dt)
    return pl.pallas_call(
        _gdn_body,
        grid=(bsz, npairs),
        in_specs=[col(0), col(1), col(2), col(3), wcol(0), wcol(1), wcol(2),
                  pl.BlockSpec((nck, 8 * np_, GDN_BLOCK), lambda b, p: (b, p, 0)),
                  pl.BlockSpec((np_, 8, LANES), lambda b, p: (p, 0, 0)),
                  pl.BlockSpec((1, hd), lambda b, p: (0, 0))],
        out_specs=pl.BlockSpec((seq, wd), lambda b, p: (b, p)),
        out_shape=jax.ShapeDtypeStruct((t, nh * hd), BF16),
        scratch_shapes=([per_head(seq, F32)] * 4
                        + [per_head(nck * 8, F32)] * 3
                        + [per_head(seq, F32)] * 2
                        + [per_head(2 * seq, BF16)] * 2
                        + [per_head(seq, BF16)] * 2
                        + [per_head(seq, BF16)] * 2
                        + [per_head(nck, F32)] * 2
                        + [pltpu.VMEM((seq + 16, hd), F32)]),
        compiler_params=_cparams("parallel", "parallel"),
        name="gdn",
    )(qkvz, qkvz, qkvz, qkvz, conv_w, conv_w, conv_w, gates3, head_par, norm_w)


def _kv_body(m_ref, g_ref, wk_ref, wv_ref, k_ref, v_ref):
    mn = _rms(m_ref[...], g_ref[...]).astype(BF16)
    k_ref[...] = _dot(mn, wk_ref[...]).astype(BF16)
    v_ref[...] = _dot(mn, wv_ref[...]).astype(BF16)


def _mem_kv(mem2d, norm_w, wk, wv, tm=512):
    r, d = mem2d.shape
    full = lambda shape: pl.BlockSpec(shape, lambda i: (0,) * len(shape))
    tile = pl.BlockSpec((tm, d), lambda i: (i, 0))
    return pl.pallas_call(
        _kv_body,
        grid=(r // tm,),
        in_specs=[tile, full((1, d)), full(wk.shape), full(wv.shape)],
        out_specs=[tile, tile],
        out_shape=[jax.ShapeDtypeStruct((r, d), BF16)] * 2,
        compiler_params=_cparams("parallel"),
        name="mem_kv",
    )(mem2d, norm_w, wk, wv)


def _mix_xattn_body(x_ref, y5_ref, yg_ref, wo5_ref, wog_ref, g_ref, wq_ref, k_ref, v_ref, wo_ref, o_ref):
    x1 = x_ref[...] + _dot(y5_ref[...], wo5_ref[...]) + _dot(yg_ref[...], wog_ref[...])
    xn = _rms(x1, g_ref[...]).astype(BF16)
    q = (_dot(xn, wq_ref[...]) * (XA_HEAD_DIM ** -0.5)).astype(BF16)
    heads = []
    for h in range(XA_HEADS):
        sl = slice(h * XA_HEAD_DIM, (h + 1) * XA_HEAD_DIM)
        s = _dot_nt(q[:, sl], k_ref[:, sl])
        p = jnp.exp(s - jnp.max(s, axis=-1, keepdims=True))
        p = p / jnp.sum(p, axis=-1, keepdims=True)
        heads.append(_dot(p.astype(BF16), v_ref[:, sl]).astype(BF16))
    o_ref[...] = x1 + _dot(jnp.concatenate(heads, axis=1), wo_ref[...])


def _mix_xattn(x2d, y5, yg, wo5, wog, norm_w, wq, kmem, vmem, wo, seq, mem_len, tm=512):
    t, d = x2d.shape
    per_b = seq // tm
    full = lambda shape: pl.BlockSpec(shape, lambda i: (0,) * len(shape))
    return pl.pallas_call(
        _mix_xattn_body,
        grid=(t // tm,),
        in_specs=[pl.BlockSpec((tm, d), lambda i: (i, 0)),
                  pl.BlockSpec((tm, S5_WIDTH), lambda i: (i, 0)),
                  pl.BlockSpec((tm, GDN_WIDTH), lambda i: (i, 0)),
                  full(wo5.shape), full(wog.shape), full((1, d)), full(wq.shape),
                  pl.BlockSpec((mem_len, d), lambda i: (i // per_b, 0)),
                  pl.BlockSpec((mem_len, d), lambda i: (i // per_b, 0)),
                  full(wo.shape)],
        out_specs=pl.BlockSpec((tm, d), lambda i: (i, 0)),
        out_shape=jax.ShapeDtypeStruct((t, d), F32),
        compiler_params=_cparams("parallel"),
        name="mix_xattn",
    )(x2d, y5, yg, wo5, wog, norm_w, wq, kmem, vmem, wo)


ROUTE_EXPERT_LANE0 = 4


def _router_body(x_ref, g_ref, whi_ref, wlo_ref, b_ref, tri_ref, xn_ref, mi_ref, mf_ref, cnt_ref, carry):
    @pl.when(pl.program_id(0) == 0)
    def _():
        carry[...] = jnp.zeros_like(carry)

    xn = _rms(x_ref[...], g_ref[...])
    xn_ref[...] = xn
    x_hi = xn.astype(BF16)
    x_lo = (xn - x_hi.astype(F32)).astype(BF16)
    logits = _dot(x_hi, whi_ref[...]) + _dot(x_lo, whi_ref[...]) + _dot(x_hi, wlo_ref[...]) + b_ref[...]
    tm = logits.shape[0]
    lane = lax.broadcasted_iota(I32, (tm, LANES), 1)
    neg = jnp.float32(-jnp.inf)
    big = jnp.int32(LANES)

    def top(vals):
        m = jnp.max(vals, axis=-1, keepdims=True)
        idx = jnp.min(jnp.where(vals == m, lane, big), axis=-1, keepdims=True)
        return m, idx

    is_g = lane < MOE_GROUPS
    gl = jnp.where(is_g, logits, neg)
    gmax, gidx = top(gl)
    p_top = 1.0 / jnp.sum(jnp.where(is_g, jnp.exp(gl - gmax), 0.0), axis=-1, keepdims=True)
    elane = lane - ROUTE_EXPERT_LANE0
    in_grp = jnp.logical_and(jnp.logical_and(elane >= 0, elane < MOE_EXPERTS), (elane // MOE_PER_GROUP) == gidx)
    es = jnp.where(in_grp, logits, neg)
    m1, i1 = top(es)
    m2, i2 = top(jnp.where(lane == i1, neg, es))
    e21 = jnp.exp(m2 - m1)
    w1 = p_top / (1.0 + e21)
    w2 = p_top * e21 / (1.0 + e21)
    e1 = i1 - ROUTE_EXPERT_LANE0
    e2 = i2 - ROUTE_EXPERT_LANE0

    a1 = (lane == e1).astype(F32)
    a2 = (lane == e2).astype(F32)
    both = a1 + a2
    before = _dot(tri_ref[...], both.astype(BF16)) - both + carry[...]
    r1 = jnp.sum(a1 * before, axis=-1, keepdims=True).astype(I32)
    r2 = jnp.sum(a2 * before, axis=-1, keepdims=True).astype(I32)
    carry[...] = carry[...] + jnp.sum(both, axis=0, keepdims=True)
    cnt_ref[...] = jnp.broadcast_to(carry[...], cnt_ref.shape)

    mi_ref[...] = jnp.where(lane == 0, e1, jnp.where(lane == 1, e2, jnp.where(lane == 2, r1, jnp.where(lane == 3, r2, 0))))
    mf_ref[...] = jnp.where(lane == 0, w1, jnp.where(lane == 1, w2, 0.0))


def _router(x2d, norm_w, w_route, b_route, tm=512):
    t, d = x2d.shape
    tri = jnp.tril(jnp.ones((tm, tm), BF16))
    w_hi = w_route.astype(BF16)
    w_lo = (w_route - w_hi.astype(F32)).astype(BF16)
    full = lambda shape: pl.BlockSpec(shape, lambda i: (0,) * len(shape))
    return pl.pallas_call(
        _router_body,
        grid=(t // tm,),
        in_specs=[pl.BlockSpec((tm, d), lambda i: (i, 0)), full((1, d)), full(w_route.shape), full(w_route.shape),
                  full((1, LANES)), full((tm, tm))],
        out_specs=[pl.BlockSpec((tm, d), lambda i: (i, 0)),
                   pl.BlockSpec((tm, LANES), lambda i: (i, 0)),
                   pl.BlockSpec((tm, LANES), lambda i: (i, 0)),
                   pl.BlockSpec((8, LANES), lambda i: (0, 0))],
        out_shape=[jax.ShapeDtypeStruct((t, d), F32),
                   jax.ShapeDtypeStruct((t, LANES), I32),
                   jax.ShapeDtypeStruct((t, LANES), F32),
                   jax.ShapeDtypeStruct((8, LANES), F32)],
        scratch_shapes=[pltpu.VMEM((1, LANES), F32)],
        compiler_params=_cparams("arbitrary"),
        name="moe_router",
    )(x2d, norm_w, w_hi, w_lo, b_route, tri)


def _dest_body(mi_ref, off_ref, da_ref, db_ref):
    mi = mi_ref[...]
    tm = mi.shape[0]
    lane = lax.broadcasted_iota(I32, (tm, LANES), 1)
    off = off_ref[...]
    d0 = jnp.sum(jnp.where(lane == mi[:, 0:1], off, 0), axis=-1, keepdims=True) + mi[:, 2:3]
    d1 = jnp.sum(jnp.where(lane == mi[:, 1:2], off, 0), axis=-1, keepdims=True) + mi[:, 3:4]
    dt = jnp.where(lane == 0, d0, jnp.where(lane == 1, d1, 0)).T
    da_ref[...] = dt[0:1]
    db_ref[...] = dt[1:2]


def _dest_rows(mi, offsets_row, tm):
    t = mi.shape[0]
    out = pl.BlockSpec((None, 1, tm), lambda i: (i, 0, 0))
    return pl.pallas_call(
        _dest_body,
        grid=(t // tm,),
        in_specs=[pl.BlockSpec((tm, LANES), lambda i: (i, 0)), pl.BlockSpec((1, LANES), lambda i: (0, 0))],
        out_specs=[out, out],
        out_shape=[jax.ShapeDtypeStruct((t // tm, 1, tm), I32)] * 2,
        compiler_params=_cparams("parallel"),
        name="moe_dest",
    )(mi, offsets_row)


def _dispatch_body(zs_ref, na_ref, da_ref, db_ref, xn_ref, xs_ref, zbuf, sem, zsem):
    tm = xn_ref.shape[0]
    n_tiles = xs_ref.shape[0] // tm

    @pl.when(pl.program_id(0) == 0)
    def _():
        zbuf[...] = jnp.zeros_like(zbuf)

        def fill(row0):
            return pltpu.make_async_copy(zbuf, xs_ref.at[pl.ds(pl.multiple_of(row0, tm), tm), :], zsem)

        def tail(e, c):
            @pl.when(zs_ref[e] >= 0)
            def _():
                fill(zs_ref[e]).start()
            return c

        def unused(j, c):
            fill(j * tm).start()
            return c

        def drain(j, c):
            fill(0).wait()
            return c

        lax.fori_loop(0, MOE_EXPERTS, tail, 0)
        lax.fori_loop(na_ref[0], n_tiles, unused, 0)
        lax.fori_loop(0, na_ref[1] + n_tiles - na_ref[0], drain, 0)

    def start(r, c):
        for k, d_ref in enumerate((da_ref, db_ref)):
            pltpu.make_async_copy(xn_ref.at[pl.ds(r, 1), :], xs_ref.at[pl.ds(d_ref[0, r], 1), :],
                                  sem.at[k]).start(priority=k)
        return c

    lax.fori_loop(0, tm, start, 0, unroll=8)
    for k in range(MOE_TOPK):
        pltpu.make_async_copy(xn_ref, xs_ref.at[pl.ds(0, tm), :], sem.at[k]).wait()


def _dispatch(zero_start, n_active, dest_a, dest_b, xn, n_rows, tm):
    t, dw = xn.shape
    smem_row = pl.BlockSpec((None, 1, tm), lambda i, zs, na: (i, 0, 0), memory_space=pltpu.SMEM)
    grid_spec = pltpu.PrefetchScalarGridSpec(
        num_scalar_prefetch=2,
        grid=(t // tm,),
        in_specs=[smem_row, smem_row, pl.BlockSpec((tm, dw), lambda i, zs, na: (i, 0))],
        out_specs=pl.BlockSpec(memory_space=pl.ANY),
        scratch_shapes=[pltpu.VMEM((tm, dw), xn.dtype), pltpu.SemaphoreType.DMA((MOE_TOPK,)),
                        pltpu.SemaphoreType.DMA(())],
    )
    return pl.pallas_call(
        _dispatch_body,
        grid_spec=grid_spec,
        out_shape=jax.ShapeDtypeStruct((n_rows, dw), xn.dtype),
        compiler_params=_cparams("arbitrary"),
        name="moe_dispatch",
    )(zero_start, n_active, dest_a, dest_b, xn)


def _experts_body(te_ref, na_ref, x_ref, wg_ref, wu_ref, wd_ref, y_ref):
    del te_ref

    @pl.when(pl.program_id(0) < na_ref[0])
    def _():
        x = x_ref[...].astype(BF16)
        gt = _dot(x, wg_ref[...])
        up = _dot(x, wu_ref[...])
        hid = (gt * jax.nn.sigmoid(gt) * up).astype(BF16)
        y_ref[...] = _dot(hid, wd_ref[...])

    @pl.when(pl.program_id(0) >= na_ref[0])
    def _():
        y_ref[...] = jnp.zeros_like(y_ref)


def _experts(tile_expert, n_active, xs, w_gate, w_up, w_down, tm):
    r, dw = xs.shape
    d, f = w_gate.shape[1], w_gate.shape[2]
    row_tile = lambda i, te, na: (jnp.minimum(i, na[0] - 1), 0)
    grid_spec = pltpu.PrefetchScalarGridSpec(
        num_scalar_prefetch=2,
        grid=(r // tm,),
        in_specs=[pl.BlockSpec((tm, dw), row_tile),
                  pl.BlockSpec((None, d, f), lambda i, te, na: (te[i], 0, 0)),
                  pl.BlockSpec((None, d, f), lambda i, te, na: (te[i], 0, 0)),
                  pl.BlockSpec((None, f, d), lambda i, te, na: (te[i], 0, 0))],
        out_specs=pl.BlockSpec((tm, dw), lambda i, te, na: (i, 0)),
    )
    return pl.pallas_call(
        _experts_body,
        grid_spec=grid_spec,
        out_shape=jax.ShapeDtypeStruct((r, dw), xs.dtype),
        compiler_params=_cparams("arbitrary"),
        name="moe_experts",
    )(tile_expert, n_active, xs, w_gate, w_up, w_down)


def _combine_body(da_ref, db_ref, na_ref, nb_ref, x_ref, mf_ref, g_ref, ys_ref, o_ref, buf, sem):
    tm = x_ref.shape[0]
    i = pl.program_id(0)
    slot = lax.rem(i, 2)

    def gather(d_refs, s):
        def start(r, c):
            for k, d_ref in enumerate(d_refs):
                pltpu.make_async_copy(ys_ref.at[pl.ds(d_ref[0, r], 1), :],
                                      buf.at[s, k, pl.ds(r, 1), :], sem.at[s, k]).start(priority=k)
            return c

        lax.fori_loop(0, tm, start, 0, unroll=8)

    @pl.when(i == 0)
    def _():
        gather((da_ref, db_ref), 0)

    @pl.when(i + 1 < pl.num_programs(0))
    def _():
        gather((na_ref, nb_ref), 1 - slot)

    for k in range(MOE_TOPK):
        pltpu.make_async_copy(ys_ref.at[pl.ds(0, tm), :], buf.at[slot, k], sem.at[slot, k]).wait()
    mf = mf_ref[...]
    y = x_ref[...] + mf[:, 0:1] * buf[slot, 0] + mf[:, 1:2] * buf[slot, 1]
    o_ref[...] = _rms(y, g_ref[...])


def _combine(dest_a, dest_b, x2d, mf, norm_w, ys, tm):
    t, d = x2d.shape
    last = t // tm - 1
    cur = pl.BlockSpec((None, 1, tm), lambda i: (i, 0, 0), memory_space=pltpu.SMEM)
    nxt = pl.BlockSpec((None, 1, tm), lambda i: (jnp.minimum(i + 1, last), 0, 0), memory_space=pltpu.SMEM)
    return pl.pallas_call(
        _combine_body,
        grid=(t // tm,),
        in_specs=[cur, cur, nxt, nxt,
                  pl.BlockSpec((tm, d), lambda i: (i, 0)),
                  pl.BlockSpec((tm, LANES), lambda i: (i, 0)),
                  pl.BlockSpec((1, d), lambda i: (0, 0)),
                  pl.BlockSpec(memory_space=pl.ANY)],
        out_specs=pl.BlockSpec((tm, d), lambda i: (i, 0)),
        out_shape=jax.ShapeDtypeStruct((t, d), F32),
        scratch_shapes=[pltpu.VMEM((2, MOE_TOPK, tm, ys.shape[1]), ys.dtype), pltpu.SemaphoreType.DMA((2, MOE_TOPK))],
        compiler_params=_cparams("arbitrary"),
        name="moe_combine",
    )(dest_a, dest_b, dest_a, dest_b, x2d, mf, norm_w, ys)


MOE_ROW_TILE = 256


def _moe(x2d, norm_w, wg, bg, we, be, w_gate, w_up, w_down, norm_final):
    t, d = x2d.shape
    tm = MOE_ROW_TILE
    w_route = jnp.pad(jnp.concatenate([wg, we], axis=1), ((0, 0), (0, LANES - MOE_GROUPS - MOE_EXPERTS)))
    b_route = jnp.pad(jnp.concatenate([bg, be]), (0, LANES - MOE_GROUPS - MOE_EXPERTS))[None]
    xn, mi, mf, cnt = _router(x2d, norm_w, w_route, b_route)
    counts = cnt[0, :MOE_EXPERTS].astype(I32)
    padded = ((counts + tm - 1) // tm) * tm
    ends = jnp.cumsum(padded)
    offsets = ends - padded
    n_tiles = (MOE_TOPK * t + MOE_EXPERTS * (tm - 1)) // tm
    tile_start = jnp.arange(n_tiles, dtype=I32) * tm
    tile_expert = jnp.minimum(jnp.sum((ends[None, :] <= tile_start[:, None]).astype(I32), axis=1), MOE_EXPERTS - 1)
    n_info = jnp.stack([ends[-1] // tm, jnp.sum((counts > 0).astype(I32))]).astype(I32)
    zero_start = jnp.where(counts > 0, ends - tm, -1).astype(I32)
    offsets_row = jnp.pad(offsets, (0, LANES - MOE_EXPERTS))[None]
    dest_a, dest_b = _dest_rows(mi, offsets_row, tm)
    xs = _dispatch(zero_start, n_info, dest_a, dest_b, xn, n_tiles * tm, tm)
    ys = _experts(tile_expert, n_info, xs, w_gate, w_up, w_down, tm)
    return _combine(dest_a, dest_b, x2d, mf, norm_final, ys, tm)


def kernel(x, mem, norm_mix, w_in, w_out,
           s5_lam_re_f, s5_lam_im_f, s5_log_step_f, s5_b_re_f, s5_b_im_f, s5_c_re_f, s5_c_im_f,
           s5_lam_re_b, s5_lam_im_b, s5_log_step_b, s5_b_re_b, s5_b_im_b, s5_c_re_b, s5_c_im_b,
           s5_d, s5_w_glu, s5_b_glu, s5_norm,
           gdn_conv, gdn_a_log_f, gdn_dt_bias_f, gdn_a_log_b, gdn_dt_bias_b, gdn_norm,
           norm_xattn, norm_mem, xa_wq, xa_wk, xa_wv, xa_wo,
           norm_moe, router_group_w, router_group_b, router_expert_w, router_expert_b,
           moe_w_gate, moe_w_up, moe_w_down, norm_final):
    bsz, seq, d = x.shape
    t = bsz * seq
    l = 0
    x2d = x.reshape(t, d)
    wi = w_in[l]
    wut = wi[:, :S5_WIDTH].T.astype(BF16)
    wqkvz = wi[:, S5_WIDTH:S5_WIDTH + 4 * GDN_WIDTH].astype(BF16)
    wg = wi[:, S5_WIDTH + 4 * GDN_WIDTH:].reshape(d, 4, GDN_HEADS)
    wg = jnp.pad(jnp.swapaxes(wg, 1, 2), ((0, 0), (0, 0), (0, 4))).reshape(d, GDN_HEADS * 8)
    wgt = wg.T.astype(BF16)
    u3, qkvz, gates = _in_proj(x2d, norm_mix[l][None], wut, wqkvz, wgt)
    s5p = dict(lam_re_f=s5_lam_re_f[l], lam_im_f=s5_lam_im_f[l], log_step_f=s5_log_step_f[l],
               b_re_f=s5_b_re_f[l], b_im_f=s5_b_im_f[l], c_re_f=s5_c_re_f[l], c_im_f=s5_c_im_f[l],
               lam_re_b=s5_lam_re_b[l], lam_im_b=s5_lam_im_b[l], log_step_b=s5_log_step_b[l],
               b_re_b=s5_b_re_b[l], b_im_b=s5_b_im_b[l], c_re_b=s5_c_re_b[l], c_im_b=s5_c_im_b[l], d=s5_d[l])
    g3 = _s5_scan(u3, s5p, seq // S5_CHUNK)
    y_s5 = _s5_post(g3, s5_w_glu[l].T.astype(BF16), s5_b_glu[l][:, None], s5_norm[l][:, None])

    head_par = jnp.stack([gdn_a_log_f[l], gdn_dt_bias_f[l], gdn_a_log_b[l], gdn_dt_bias_b[l]], axis=1)
    head_par = jnp.broadcast_to(jnp.pad(head_par, ((0, 0), (0, 4)))[:, :, None], (GDN_HEADS, 8, LANES))
    y_gdn = _gdn(qkvz, gates, gdn_conv[l], head_par, gdn_norm[l][None], bsz, seq)

    mem_len = mem.shape[1]
    kmem, vmem = _mem_kv(mem.reshape(bsz * mem_len, d), norm_mem[l][None],
                         xa_wk[l].astype(BF16), xa_wv[l].astype(BF16))
    wo = w_out[l].astype(BF16)
    x2 = _mix_xattn(x2d, y_s5, y_gdn, wo[:S5_WIDTH], wo[S5_WIDTH:], norm_xattn[l][None],
                    xa_wq[l].astype(BF16), kmem, vmem, xa_wo[l].astype(BF16), seq, mem_len)
    y = _moe(x2, norm_moe[l][None], router_group_w[l], router_group_b[l], router_expert_w[l], router_expert_b[l],
             moe_w_gate[l].astype(BF16), moe_w_up[l].astype(BF16), moe_w_down[l].astype(BF16), norm_final[None])
    return y.reshape(bsz, seq, d)
```

```python
import functools
import math

import jax
import jax.numpy as jnp
from jax import lax
from jax.experimental import pallas as pl
from jax.experimental.pallas import tpu as pltpu

F32 = jnp.float32
BF16 = jnp.bfloat16
I32 = jnp.int32

D_MODEL = 1024
S5_WIDTH = 512
S5_GROUP = 16
S5_GROUPS = 32
S5_STATE = 64
S5_CHUNK = 128
GDN_HEADS = 4
GDN_HEAD_DIM = 128
GDN_WIDTH = 512
GDN_CONV = 5
GDN_CHUNK = 64
XA_HEADS = 4
XA_HEAD_DIM = 256
MOE_GROUPS = 4
MOE_PER_GROUP = 8
MOE_EXPERTS = 32
MOE_TOPK = 2
D_EXPERT = 256
RMS_EPS = 1e-6
L2_EPS = 1e-6
LANES = 128
VMEM_LIMIT = 56 * 1024 * 1024


def _cparams(*sem):
    return pltpu.CompilerParams(dimension_semantics=tuple(sem), vmem_limit_bytes=VMEM_LIMIT)


def _rms(x, gain):
    return x * lax.rsqrt(jnp.mean(x * x, axis=-1, keepdims=True) + RMS_EPS) * gain


def _dot(a, b):
    return jnp.dot(a, b, preferred_element_type=F32)


def _dot_nt(a, b):
    return lax.dot_general(a, b, (((1,), (1,)), ((), ())), preferred_element_type=F32)


def _dot_tn(a, b):
    return lax.dot_general(a, b, (((0,), (0,)), ((), ())), preferred_element_type=F32)


def _in_proj_body(x_ref, g_ref, wut_ref, wqkvz_ref, wgt_ref, u_ref, qkvz_ref, gates_ref, wut_b, wqkvz_b, wgt_b):
    @pl.when(pl.program_id(0) == 0)
    def _():
        wut_b[...] = wut_ref[...].astype(BF16)
        wqkvz_b[...] = wqkvz_ref[...].astype(BF16)
        wgt_b[...] = wgt_ref[...].astype(BF16)

    h = _rms(x_ref[...], g_ref[...]).astype(BF16)
    ut = _dot_nt(wut_b[...], h)
    gt = _dot_nt(wgt_b[...], h)
    for j in range(u_ref.shape[0]):
        u_ref[j] = ut[:, j * S5_CHUNK:(j + 1) * S5_CHUNK]
        gates_ref[j] = gt[:, j * S5_CHUNK:(j + 1) * S5_CHUNK]
    qkvz_ref[...] = _dot(h, wqkvz_b[...]).astype(BF16)


def _in_proj(x2d, norm_w, wut, wqkvz, wgt, tm=512):
    t = x2d.shape[0]
    nck = tm // S5_CHUNK
    full = lambda shape: pl.BlockSpec(shape, lambda i: (0,) * len(shape))
    return pl.pallas_call(
        _in_proj_body,
        grid=(t // tm,),
        in_specs=[pl.BlockSpec((tm, D_MODEL), lambda i: (i, 0)),
                  full((1, D_MODEL)), full(wut.shape), full(wqkvz.shape), full(wgt.shape)],
        out_specs=[pl.BlockSpec((nck, S5_WIDTH, S5_CHUNK), lambda i: (i, 0, 0)),
                   pl.BlockSpec((tm, wqkvz.shape[1]), lambda i: (i, 0)),
                   pl.BlockSpec((nck, wgt.shape[0], S5_CHUNK), lambda i: (i, 0, 0))],
        out_shape=[jax.ShapeDtypeStruct((t // S5_CHUNK, S5_WIDTH, S5_CHUNK), F32),
                   jax.ShapeDtypeStruct((t, wqkvz.shape[1]), BF16),
                   jax.ShapeDtypeStruct((t // S5_CHUNK, wgt.shape[0], S5_CHUNK), F32)],
        scratch_shapes=[pltpu.VMEM(wut.shape, BF16), pltpu.VMEM(wqkvz.shape, BF16), pltpu.VMEM(wgt.shape, BF16)],
        compiler_params=_cparams("arbitrary"),
        name="in_proj",
    )(x2d, norm_w, wut, wqkvz, wgt)


def _cmul(ar, ai, br, bi):
    return ar * br - ai * bi, ar * bi + ai * br


def _cpow_int(lr, li, expo, nbits):
    res_r = jnp.ones(jnp.broadcast_shapes(lr.shape, expo.shape), F32)
    res_i = jnp.zeros_like(res_r)
    for b in range(nbits):
        bit = ((expo >> b) & 1) == 1
        nr, ni = _cmul(res_r, res_i, lr, li)
        res_r = jnp.where(bit, nr, res_r)
        res_i = jnp.where(bit, ni, res_i)
        if b + 1 < nbits:
            lr, li = _cmul(lr, li, lr, li)
    return res_r, res_i


def _lam_bar(re, im, step):
    er = jnp.exp(step * re)
    return er * jnp.cos(step * im), er * jnp.sin(step * im)


def _zoh_coef(re, im, lr, li):
    den = re * re + im * im
    return ((lr - 1.0) * re + li * im) / den, (li * re - (lr - 1.0) * im) / den


def _s5_body(u_ref, lrow_ref, lcol_ref, l256_ref, b_ref, bt_ref, c_ref, ct_ref, d_ref,
             o_ref, vf_ref, vb_ref, m_ref, win_ref, wout_ref, sf_ref, sb_ref, hf_ref, hb_ref, uc_ref, *, nchunk):
    L = S5_CHUNK
    P = S5_STATE
    n_rows = u_ref.shape[0]
    nb = n_rows // nchunk
    lane_i = lax.broadcasted_iota(I32, (1, L), 1)

    lcol = lcol_ref[...]
    lbc_r, lbc_i = _lam_bar(lcol[:, 0:2], lcol[:, 2:4], lcol[:, 4:6])
    kc_r, kc_i = _zoh_coef(lcol[:, 0:2], lcol[:, 2:4], lbc_r, lbc_i)
    lrow = lrow_ref[...]
    lbr_r, lbr_i = _lam_bar(lrow[0:2], lrow[2:4], lrow[4:6])
    kr_r, kr_i = _zoh_coef(lrow[0:2], lrow[2:4], lbr_r, lbr_i)
    lf_r, lf_i, lb_r, lb_i = lbc_r[:, 0:1], lbc_i[:, 0:1], lbc_r[:, 1:2], lbc_i[:, 1:2]
    kfr_c, kfi_c, kbr_c, kbi_c = kc_r[:, 0:1], kc_i[:, 0:1], kc_r[:, 1:2], kc_i[:, 1:2]
    kfr_r, kfi_r, kbr_r, kbi_r = kr_r[0:1], kr_i[0:1], kr_r[1:2], kr_i[1:2]

    pwf_r, pwf_i = _cpow_int(lf_r, lf_i, lane_i, 7)
    rvf_r, rvf_i = _cpow_int(lf_r, lf_i, (L - 1) - lane_i, 7)
    pwb_r, pwb_i = _cpow_int(lb_r, lb_i, lane_i, 7)
    rvb_r, rvb_i = _cpow_int(lb_r, lb_i, L - lane_i, 8)
    nxf_r, nxf_i = _cmul(pwf_r, pwf_i, lf_r, lf_i)

    bf_r = kfr_c * b_ref[0] - kfi_c * b_ref[1]
    bf_i = kfr_c * b_ref[1] + kfi_c * b_ref[0]
    bb_r = kbr_c * b_ref[2] - kbi_c * b_ref[3]
    bb_i = kbr_c * b_ref[3] + kbi_c * b_ref[2]
    btf_r = kfr_r * bt_ref[0] - kfi_r * bt_ref[1]
    btf_i = kfr_r * bt_ref[1] + kfi_r * bt_ref[0]
    btb_r = kbr_r * bt_ref[2] - kbi_r * bt_ref[3]
    btb_i = kbr_r * bt_ref[3] + kbi_r * bt_ref[2]

    def taps(c_r, c_i, bt_r, bt_i, pw_r, pw_i):
        cb_r = (bt_r[:, None, :] * c_r[None, :, :] - bt_i[:, None, :] * c_i[None, :, :]).reshape(256, P)
        cb_i = (bt_r[:, None, :] * c_i[None, :, :] + bt_i[:, None, :] * c_r[None, :, :]).reshape(256, P)
        k = (jnp.dot(cb_r, pw_r, preferred_element_type=F32, precision=lax.Precision.HIGHEST)
             - jnp.dot(cb_i, pw_i, preferred_element_type=F32, precision=lax.Precision.HIGHEST))
        return k, jnp.sum(cb_r, axis=1, keepdims=True)

    kf, _ = taps(c_ref[0], c_ref[1], btf_r, btf_i, pwf_r, pwf_i)
    kb, kb0 = taps(c_ref[2], c_ref[3], btb_r, btb_i, rvb_r, rvb_i)
    is0 = lane_i == 0
    vf_ref[...] = kf + jnp.where(is0, kb0, 0.0)
    vb_ref[...] = jnp.where(is0, 0.0, kb)

    row_i = lax.broadcasted_iota(I32, (L, L), 0)
    col_i = lax.broadcasted_iota(I32, (L, L), 1)
    fwd_lane = col_i + row_i < L

    def build_ci(ci, carry):
        for co in range(S5_GROUP):
            r = ci * S5_GROUP + co
            taps_rows = jnp.where(fwd_lane, jnp.broadcast_to(vf_ref[pl.ds(r, 1), :], (L, L)),
                                  jnp.broadcast_to(vb_ref[pl.ds(r, 1), :], (L, L)))
            m_ref[pl.ds(pl.multiple_of(ci * L, L), L), co * L:(co + 1) * L] = pltpu.roll(
                taps_rows, 0, 1, stride=1, stride_axis=0).astype(BF16)
        return carry

    lax.fori_loop(0, S5_GROUP, build_ci, 0)

    for ci in range(S5_GROUP):
        sl = slice(ci * L, (ci + 1) * L)
        br, bi = bf_r[:, ci:ci + 1], bf_i[:, ci:ci + 1]
        win_ref[0 * P:1 * P, sl] = (rvf_r * br - rvf_i * bi).astype(BF16)
        win_ref[1 * P:2 * P, sl] = (rvf_r * bi + rvf_i * br).astype(BF16)
        br, bi = bb_r[:, ci:ci + 1], bb_i[:, ci:ci + 1]
        win_ref[2 * P:3 * P, sl] = (pwb_r * br - pwb_i * bi).astype(BF16)
        win_ref[3 * P:4 * P, sl] = (pwb_r * bi + pwb_i * br).astype(BF16)
    for co in range(S5_GROUP):
        sl = slice(co * L, (co + 1) * L)
        cr, ci_ = ct_ref[0][:, co:co + 1], ct_ref[1][:, co:co + 1]
        wout_ref[0 * P:1 * P, sl] = (cr * nxf_r - ci_ * nxf_i).astype(BF16)
        wout_ref[1 * P:2 * P, sl] = (-(cr * nxf_i + ci_ * nxf_r)).astype(BF16)
        cr, ci_ = ct_ref[2][:, co:co + 1], ct_ref[3][:, co:co + 1]
        wout_ref[2 * P:3 * P, sl] = (cr * rvb_r - ci_ * rvb_i).astype(BF16)
        wout_ref[3 * P:4 * P, sl] = (-(cr * rvb_i + ci_ * rvb_r)).astype(BF16)

    for ci in range(S5_GROUP):
        uc_ref[ci] = u_ref[:, ci, :]
    ucat = jnp.concatenate([uc_ref[ci].astype(BF16) for ci in range(S5_GROUP)], axis=1)

    summ = _dot_nt(ucat, win_ref[...])
    sf_ref[...] = summ[:, :2 * P]
    sb_ref[...] = summ[:, 2 * P:]
    l256 = l256_ref[...]
    a_mul, a_im = _lam_bar(l256[0:1], l256[1:2], l256[2:3])
    for _ in range(7):
        a_mul, a_im = _cmul(a_mul, a_im, a_mul, a_im)
    lane256 = lax.broadcasted_iota(I32, (1, 4 * P), 1)
    b_mul = jnp.where((lane256 // P) % 2 == 0, -a_im, a_im)

    hf = jnp.zeros((nb, 2 * P), F32)
    hb = jnp.zeros((nb, 2 * P), F32)
    for c in range(nchunk):
        cr = nchunk - 1 - c
        rows_f = pl.ds(c, nb, stride=nchunk)
        rows_b = pl.ds(cr, nb, stride=nchunk)
        hf_ref[rows_f, :] = hf
        hb_ref[rows_b, :] = hb
        hf = a_mul[:, :2 * P] * hf + b_mul[:, :2 * P] * pltpu.roll(hf, P, 1) + sf_ref[rows_f, :]
        hb = a_mul[:, 2 * P:] * hb + b_mul[:, 2 * P:] * pltpu.roll(hb, P, 1) + sb_ref[rows_b, :]
    hprev = jnp.concatenate([hf_ref[...], hb_ref[...]], axis=1).astype(BF16)

    nblk = 2 * L
    for j in range(S5_GROUP * L // nblk):
        y = _dot(ucat, m_ref[:, j * nblk:(j + 1) * nblk]) + _dot(hprev, wout_ref[:, j * nblk:(j + 1) * nblk])
        for q in range(nblk // L):
            co = j * (nblk // L) + q
            yc = y[:, q * L:(q + 1) * L] + d_ref[co:co + 1, :] * uc_ref[co]
            o_ref[:, co, :] = 0.5 * yc * (1.0 + lax.erf(yc * (2.0 ** -0.5)))


def _s5_scan(u3, p, nchunk):
    n = u3.shape[0]
    g, grp, st, L = S5_GROUPS, S5_GROUP, S5_STATE, S5_CHUNK
    step_f = jnp.exp(p["log_step_f"])[:, None] * jnp.ones((1, st), F32)
    step_b = jnp.exp(p["log_step_b"])[:, None] * jnp.ones((1, st), F32)
    zeros = jnp.zeros((g, st), F32)
    lrow = jnp.stack([p["lam_re_f"], p["lam_re_b"], p["lam_im_f"], p["lam_im_b"], step_f, step_b, zeros, zeros], axis=1)
    lcol = jnp.swapaxes(lrow, 1, 2)
    cat4 = lambda f, b: jnp.concatenate([f, f, b, b], axis=1)
    z256 = jnp.zeros((g, 4 * st), F32)
    l256 = jnp.stack([cat4(p["lam_re_f"], p["lam_re_b"]), cat4(p["lam_im_f"], p["lam_im_b"]),
                      cat4(step_f, step_b)] + [z256] * 5, axis=1)
    b4 = jnp.stack([p["b_re_f"], p["b_im_f"], p["b_re_b"], p["b_im_b"]], axis=1)
    bt4 = jnp.swapaxes(b4, 2, 3)
    c4 = jnp.stack([p["c_re_f"], p["c_im_f"], p["c_re_b"], p["c_im_b"]], axis=1)
    ct4 = jnp.swapaxes(c4, 2, 3)
    dbc = jnp.broadcast_to(p["d"].reshape(g, grp, 1), (g, grp, L))
    per_g = lambda *shape: pl.BlockSpec((None,) + shape, lambda i: (i,) + (0,) * len(shape))
    return pl.pallas_call(
        functools.partial(_s5_body, nchunk=nchunk),
        grid=(g,),
        in_specs=[pl.BlockSpec((n, grp, L), lambda i: (0, i, 0)),
                  per_g(8, st), per_g(st, 8), per_g(8, 4 * st), per_g(4, st, grp), per_g(4, grp, st),
                  per_g(4, grp, st), per_g(4, st, grp), per_g(grp, L)],
        out_specs=pl.BlockSpec((n, grp, L), lambda i: (0, i, 0)),
        out_shape=jax.ShapeDtypeStruct(u3.shape, F32),
        scratch_shapes=[pltpu.VMEM((grp * grp, L), F32), pltpu.VMEM((grp * grp, L), F32),
                        pltpu.VMEM((grp * L, grp * L), BF16),
                        pltpu.VMEM((4 * st, grp * L), BF16), pltpu.VMEM((4 * st, grp * L), BF16),
                        pltpu.VMEM((n, 2 * st), F32), pltpu.VMEM((n, 2 * st), F32),
                        pltpu.VMEM((n, 2 * st), F32), pltpu.VMEM((n, 2 * st), F32),
                        pltpu.VMEM((grp, n, L), F32)],
        compiler_params=_cparams("parallel"),
        name="s5_scan",
    )(u3, lrow, lcol, l256, b4, bt4, c4, ct4, dbc)


def _s5_post_body(g_ref, wt_ref, b_ref, nw_ref, o_ref):
    for j in range(g_ref.shape[0]):
        g = g_ref[j]
        z = _dot(wt_ref[...], g.astype(BF16)) + b_ref[...]
        y = g * jax.nn.sigmoid(z)
        y = y * lax.rsqrt(jnp.mean(y * y, axis=0, keepdims=True) + RMS_EPS) * nw_ref[...]
        o_ref[j * S5_CHUNK:(j + 1) * S5_CHUNK, :] = y.T.astype(BF16)


def _s5_post(g3, w_glu_t, b_glu_col, norm_col, nck=4):
    n = g3.shape[0]
    full = lambda shape: pl.BlockSpec(shape, lambda i: (0,) * len(shape))
    return pl.pallas_call(
        _s5_post_body,
        grid=(n // nck,),
        in_specs=[pl.BlockSpec((nck, S5_WIDTH, S5_CHUNK), lambda i: (i, 0, 0)),
                  full(w_glu_t.shape), full(b_glu_col.shape), full(norm_col.shape)],
        out_specs=pl.BlockSpec((nck * S5_CHUNK, S5_WIDTH), lambda i: (i, 0)),
        out_shape=jax.ShapeDtypeStruct((n * S5_CHUNK, S5_WIDTH), BF16),
        compiler_params=_cparams("parallel"),
        name="s5_post",
    )(g3, w_glu_t, b_glu_col, norm_col)


GDN_BLOCK = 128
GDN_PAIR = 2
GDN_PREP_BATCH = 8


def _packed_tri_inverse(lps, low, upp, bd16, rings):
    def pk(xs, ys):
        outs = []
        for a, b in zip(xs, ys):
            lhs = jnp.concatenate([jnp.where(low, a, 0.0), jnp.where(upp, a, 0.0)], axis=1).astype(BF16)
            rhs = jnp.concatenate([jnp.where(low, b, 0.0), jnp.where(upp, b, 0.0)], axis=0).astype(BF16)
            outs.append(_dot(lhs, rhs))
        return outs

    d = [jnp.where(bd16, lp, 0.0) for lp in lps]
    d2 = pk(d, d)
    d4 = pk(d2, d2)
    d8 = pk(d4, d4)
    a = [y - x - p for x, y, p in zip(d, d2, pk(d, d2))]
    a = [x + y + p for x, y, p in zip(a, d4, pk(a, d4))]
    a = [x + y + p for x, y, p in zip(a, d8, pk(a, d8))]
    for ring in rings:
        n = [jnp.where(ring, lp, 0.0) for lp in lps]
        t = [x + p for x, p in zip(n, pk(a, n))]
        a = [x - y - p for x, y, p in zip(a, t, pk(t, a))]
    return a


def _gdn_body(q_ref, k_ref, v_ref, z_ref, wq_ref, wk_ref, wv_ref, g_ref, hp_ref, nw_ref, o_ref,
              qs, ks, vs, os_, sg, cf, cb, uf, ub, wqf, wqb, qkf, qkb, kdtf, kdtb, eglf, eglb, xpad):
    seq = q_ref.shape[0]
    C = GDN_BLOCK
    hd = GDN_HEAD_DIM
    nck = seq // C
    heads = range(GDN_PAIR)

    pad = 8
    half = (GDN_CONV - 1) // 2
    xpad[0:pad, :] = jnp.zeros((pad, LANES), F32)
    xpad[pad + seq:2 * pad + seq, :] = jnp.zeros((pad, LANES), F32)

    def conv_silu(x_ref, w_ref, j):
        cols = pl.ds(pl.multiple_of(j * hd, hd), hd)
        xpad[pad:pad + seq, :] = x_ref[:, cols].astype(F32)
        w = w_ref[:, cols]
        acc = xpad[pad - half:pad - half + seq, :] * w[0:1]
        for tap in range(1, GDN_CONV):
            acc = acc + xpad[pad - half + tap:pad - half + tap + seq, :] * w[tap:tap + 1]
        return acc * jax.nn.sigmoid(acc)

    def l2n(x):
        return x * lax.rsqrt(jnp.sum(x * x, axis=-1, keepdims=True) + L2_EPS)

    def softplus(x):
        return jnp.maximum(x, 0.0) + jnp.log1p(jnp.exp(-jnp.abs(x)))

    lane = lax.broadcasted_iota(I32, (1, C), 1)

    def prologue(j, carry):
        qs[j] = l2n(conv_silu(q_ref, wq_ref, j)) * (hd ** -0.5)
        ks[j] = l2n(conv_silu(k_ref, wk_ref, j))
        vs[j] = conv_silu(v_ref, wv_ref, j)
        os_[j] = jnp.zeros((seq, hd), F32)
        g = g_ref[:, pl.ds(pl.multiple_of(8 * j, 8), 8), :].reshape(nck * 8, C)
        hp = hp_ref[j]
        sg[j] = jax.nn.sigmoid(g)
        gl_f = -jnp.exp(hp[0:1]) * softplus(g + hp[1:2])
        gl_b = -jnp.exp(hp[2:3]) * softplus(g + hp[3:4])
        sh = 1
        while sh < C:
            gl_f = gl_f + jnp.where(lane >= sh, pltpu.roll(gl_f, sh, 1), 0.0)
            gl_b = gl_b + jnp.where(lane < C - sh, pltpu.roll(gl_b, C - sh, 1), 0.0)
            sh *= 2
        cf[j] = gl_f
        cb[j] = gl_b
        return carry

    lax.fori_loop(0, GDN_PAIR, prologue, 0)

    ri = lax.broadcasted_iota(I32, (C, C), 0)
    ci = lax.broadcasted_iota(I32, (C, C), 1)
    low, upp = ri > ci, ri < ci
    low_i, upp_i = ri >= ci, ri <= ci
    same = lambda w: (ri // w) == (ci // w)
    bd16 = same(16)
    rings = []
    w = 32
    while w <= C:
        rings.append(jnp.logical_and(same(w), jnp.logical_not(same(w // 2))))
        w *= 2

    nbatch = math.gcd(GDN_PREP_BATCH // GDN_PAIR, nck)

    def column(ref, r):
        rows = jnp.broadcast_to(ref[pl.ds(r, 1), :], (C, C))
        return rows.T, rows

    def prepare(it, carry):
        items = [(j, it * nbatch + i) for i in range(nbatch) for j in heads]
        sls = [pl.ds(pl.multiple_of(c * C, C), C) for _, c in items]
        g_f, g_b, bt_f, bt_b, dec_f, dec_b, kb_f, kb_b, aq = [], [], [], [], [], [], [], [], []
        for (j, c), sl in zip(items, sls):
            k = ks[j, sl, :]
            gfc, gfr = column(cf.at[j], c * 8 + 2)
            gbc, gbr = column(cb.at[j], c * 8 + 3)
            g_f.append(gfc)
            g_b.append(gbc)
            bt_f.append(column(sg.at[j], c * 8)[0])
            bt_b.append(column(sg.at[j], c * 8 + 1)[0])
            dec_f.append(jnp.where(low_i, jnp.exp(jnp.where(low_i, gfc - gfr, 0.0)), 0.0))
            dec_b.append(jnp.where(upp_i, jnp.exp(jnp.where(upp_i, gbc - gbr, 0.0)), 0.0))
            kb_f.append(k * bt_f[-1])
            kb_b.append(k * bt_b[-1])
            aq.append(_dot_nt(jnp.concatenate([kb_f[-1], kb_b[-1], qs[j, sl, :]], axis=0).astype(BF16),
                              k.astype(BF16)))
        lps = [jnp.where(low, x[:C] * df, 0.0) + jnp.where(upp, x[C:2 * C] * db, 0.0)
               for x, df, db in zip(aq, dec_f, dec_b)]
        inv = _packed_tri_inverse(lps, low, upp, bd16, rings)
        for n, ((j, c), sl) in enumerate(zip(items, sls)):
            q, k, v = qs[j, sl, :], ks[j, sl, :], vs[j, sl, :]
            for rev, g_c, kb, beta, dec, msk, u_s, wq_s, qk_s, kdt_s, egl_s in (
                    (False, g_f[n], kb_f[n], bt_f[n], dec_f[n], low, uf, wqf, qkf, kdtf, eglf),
                    (True, g_b[n], kb_b[n], bt_b[n], dec_b[n], upp, ub, wqb, qkb, kdtb, eglb)):
                eg = jnp.exp(g_c)
                rhs = jnp.concatenate([v * beta, kb * eg], axis=1)
                uw = rhs + _dot(jnp.where(msk, inv[n], 0.0).astype(BF16), rhs.astype(BF16))
                glast = g_c[0:1] if rev else g_c[C - 1:C]
                u_s[j, sl, :] = uw[:, :C]
                wq_s[j, pl.ds(pl.multiple_of(c * 2 * C, 2 * C), 2 * C), :] = jnp.concatenate(
                    [uw[:, C:], q * eg], axis=0).astype(BF16)
                qk_s[j, sl, :] = (aq[n][2 * C:] * dec).astype(BF16)
                kdt_s[j, sl, :] = (k * jnp.exp(glast - g_c)).T.astype(BF16)
                egl_s[j, pl.ds(c, 1), :] = jnp.exp(glast)
        return carry

    lax.fori_loop(0, nck // nbatch, prepare, 0)

    def body(i, carry):
        chains = []
        for j in heads:
            chains.append((j, i, uf, wqf, qkf, kdtf, eglf))
            chains.append((j, nck - 1 - i, ub, wqb, qkb, kdtb, eglb))
        sls = [pl.ds(pl.multiple_of(c * C, C), C) for _, c, *_ in chains]
        ws_qs = [_dot(wq_s[j, pl.ds(pl.multiple_of(c * 2 * C, 2 * C), 2 * C), :], st.astype(BF16))
                 for (j, c, _, wq_s, *_), st in zip(chains, carry)]
        vnb = [(u_s[j, sl, :] - x[:C]).astype(BF16) for (j, _, u_s, *_), sl, x in zip(chains, sls, ws_qs)]
        new = [st * egl_s[j, pl.ds(c, 1), :] + _dot(kdt_s[j, sl, :], v)
               for (j, c, _, _, _, kdt_s, egl_s), sl, st, v in zip(chains, sls, carry, vnb)]
        for (j, _, _, _, qk_s, _, _), sl, x, v in zip(chains, sls, ws_qs, vnb):
            os_[j, sl, :] += x[C:] + _dot(qk_s[j, sl, :], v)
        return tuple(new)

    zero = jnp.zeros((hd, hd), F32)
    lax.fori_loop(0, nck, body, (zero,) * (2 * GDN_PAIR))

    for j in heads:
        o = os_[j]
        o = o * lax.rsqrt(jnp.mean(o * o, axis=-1, keepdims=True) + RMS_EPS) * nw_ref[...]
        z = z_ref[:, j * hd:(j + 1) * hd].astype(F32)
        o_ref[:, j * hd:(j + 1) * hd] = (o * (z * jax.nn.sigmoid(z))).astype(BF16)


def _gdn(qkvz, gates3, conv_w, head_par, norm_w, bsz, seq):
    t = bsz * seq
    hd = GDN_HEAD_DIM
    nh = GDN_HEADS
    np_ = GDN_PAIR
    wd = np_ * hd
    npairs = nh // np_
    nck = seq // GDN_BLOCK
    col = lambda off: pl.BlockSpec((seq, wd), lambda b, p: (b, off * npairs + p))
    wcol = lambda off: pl.BlockSpec((GDN_CONV, wd), lambda b, p: (0, off * npairs + p))
    per_head = lambda rows, dt: pltpu.VMEM((np_, rows, hd), dt)
    return pl.pallas_call(
        _gdn_body,
        grid=(bsz, npairs),
        in_specs=[col(0), col(1), col(2), col(3), wcol(0), wcol(1), wcol(2),
                  pl.BlockSpec((nck, 8 * np_, GDN_BLOCK), lambda b, p: (b, p, 0)),
                  pl.BlockSpec((np_, 8, LANES), lambda b, p: (p, 0, 0)),
                  pl.BlockSpec((1, hd), lambda b, p: (0, 0))],
        out_specs=pl.BlockSpec((seq, wd), lambda b, p: (b, p)),
        out_shape=jax.ShapeDtypeStruct((t, nh * hd), BF16),
        scratch_shapes=([per_head(seq, F32)] * 4
                        + [per_head(nck * 8, F32)] * 3
                        + [per_head(seq, F32)] * 2
                        + [per_head(2 * seq, BF16)] * 2
                        + [per_head(seq, BF16)] * 2
                        + [per_head(seq, BF16)] * 2
                        + [per_head(nck, F32)] * 2
                        + [pltpu.VMEM((seq + 16, hd), F32)]),
        compiler_params=_cparams("parallel", "parallel"),
        name="gdn",
    )(qkvz, qkvz, qkvz, qkvz, conv_w, conv_w, conv_w, gates3, head_par, norm_w)


def _kv_body(m_ref, g_ref, wk_ref, wv_ref, k_ref, v_ref):
    mn = _rms(m_ref[...], g_ref[...]).astype(BF16)
    k_ref[...] = _dot(mn, wk_ref[...]).astype(BF16)
    v_ref[...] = _dot(mn, wv_ref[...]).astype(BF16)


def _mem_kv(mem2d, norm_w, wk, wv, tm=512):
    r, d = mem2d.shape
    full = lambda shape: pl.BlockSpec(shape, lambda i: (0,) * len(shape))
    tile = pl.BlockSpec((tm, d), lambda i: (i, 0))
    return pl.pallas_call(
        _kv_body,
        grid=(r // tm,),
        in_specs=[tile, full((1, d)), full(wk.shape), full(wv.shape)],
        out_specs=[tile, tile],
        out_shape=[jax.ShapeDtypeStruct((r, d), BF16)] * 2,
        compiler_params=_cparams("parallel"),
        name="mem_kv",
    )(mem2d, norm_w, wk, wv)


def _mix_xattn_body(x_ref, y5_ref, yg_ref, wmix_ref, g_ref, wq_ref, k_ref, v_ref, wo_ref,
                    gm_ref, whi_ref, wlo_ref, br_ref, tri_ref,
                    o_ref, mi_ref, mf_ref, cnt_ref, wmix_b, wq_b, wo_b, carry):
    @pl.when(pl.program_id(0) == 0)
    def _():
        wmix_b[...] = wmix_ref[...].astype(BF16)
        wq_b[...] = wq_ref[...].astype(BF16)
        wo_b[...] = wo_ref[...].astype(BF16)
        carry[...] = jnp.zeros_like(carry)

    x1 = (x_ref[...] + _dot(y5_ref[...], wmix_b[:S5_WIDTH, :]) + _dot(yg_ref[...], wmix_b[S5_WIDTH:, :]))
    xn = _rms(x1, g_ref[...]).astype(BF16)
    q = (_dot(xn, wq_b[...]) * (XA_HEAD_DIM ** -0.5)).astype(BF16)
    heads = []
    for h in range(XA_HEADS):
        sl = slice(h * XA_HEAD_DIM, (h + 1) * XA_HEAD_DIM)
        s = _dot_nt(q[:, sl], k_ref[:, sl])
        p = jnp.exp(s - jnp.max(s, axis=-1, keepdims=True))
        p = p / jnp.sum(p, axis=-1, keepdims=True)
        heads.append(_dot(p.astype(BF16), v_ref[:, sl]).astype(BF16))
    x2 = x1 + _dot(jnp.concatenate(heads, axis=1), wo_b[...])
    o_ref[...] = x2
    mi, mf = _route(_rms(x2, gm_ref[...]), whi_ref[...], wlo_ref[...], br_ref[...], tri_ref[...], carry)
    mi_ref[...] = mi
    mf_ref[...] = mf
    cnt_ref[...] = jnp.broadcast_to(carry[...], cnt_ref.shape)


def _mix_xattn(x2d, y5, yg, w_mix, norm_w, wq, kmem, vmem, wo, norm_moe, w_route, b_route, seq, mem_len, tm=512):
    t, d = x2d.shape
    per_b = seq // tm
    tri = jnp.tril(jnp.ones((tm, tm), BF16))
    w_hi = w_route.astype(BF16)
    w_lo = (w_route - w_hi.astype(F32)).astype(BF16)
    full = lambda shape: pl.BlockSpec(shape, lambda i: (0,) * len(shape))
    tile = lambda w: pl.BlockSpec((tm, w), lambda i: (i, 0))
    return pl.pallas_call(
        _mix_xattn_body,
        grid=(t // tm,),
        in_specs=[tile(d), tile(S5_WIDTH), tile(GDN_WIDTH),
                  full(w_mix.shape), full((1, d)), full(wq.shape),
                  pl.BlockSpec((mem_len, d), lambda i: (i // per_b, 0)),
                  pl.BlockSpec((mem_len, d), lambda i: (i // per_b, 0)),
                  full(wo.shape),
                  full((1, d)), full(w_route.shape), full(w_route.shape), full((1, LANES)), full((tm, tm))],
        out_specs=[tile(d), tile(LANES), tile(LANES), pl.BlockSpec((8, LANES), lambda i: (0, 0))],
        out_shape=[jax.ShapeDtypeStruct((t, d), F32),
                   jax.ShapeDtypeStruct((t, LANES), I32),
                   jax.ShapeDtypeStruct((t, LANES), F32),
                   jax.ShapeDtypeStruct((8, LANES), F32)],
        scratch_shapes=[pltpu.VMEM(w_mix.shape, BF16), pltpu.VMEM(wq.shape, BF16), pltpu.VMEM(wo.shape, BF16),
                        pltpu.VMEM((1, LANES), F32)],
        compiler_params=_cparams("arbitrary"),
        name="mix_xattn",
    )(x2d, y5, yg, w_mix, norm_w, wq, kmem, vmem, wo, norm_moe, w_hi, w_lo, b_route, tri)


ROUTE_EXPERT_LANE0 = 4


def _route(xn, w_hi, w_lo, bias, tri, carry):
    x_hi = xn.astype(BF16)
    x_lo = (xn - x_hi.astype(F32)).astype(BF16)
    logits = _dot(x_hi, w_hi) + _dot(x_lo, w_hi) + _dot(x_hi, w_lo) + bias
    tm = logits.shape[0]
    lane = lax.broadcasted_iota(I32, (tm, LANES), 1)
    neg = jnp.float32(-jnp.inf)
    big = jnp.int32(LANES)

    def top(vals):
        m = jnp.max(vals, axis=-1, keepdims=True)
        idx = jnp.min(jnp.where(vals == m, lane, big), axis=-1, keepdims=True)
        return m, idx

    is_g = lane < MOE_GROUPS
    gl = jnp.where(is_g, logits, neg)
    gmax, gidx = top(gl)
    p_top = 1.0 / jnp.sum(jnp.where(is_g, jnp.exp(gl - gmax), 0.0), axis=-1, keepdims=True)
    elane = lane - ROUTE_EXPERT_LANE0
    in_grp = jnp.logical_and(jnp.logical_and(elane >= 0, elane < MOE_EXPERTS), (elane // MOE_PER_GROUP) == gidx)
    es = jnp.where(in_grp, logits, neg)
    m1, i1 = top(es)
    m2, i2 = top(jnp.where(lane == i1, neg, es))
    e21 = jnp.exp(m2 - m1)
    w1 = p_top / (1.0 + e21)
    w2 = p_top * e21 / (1.0 + e21)
    e1 = i1 - ROUTE_EXPERT_LANE0
    e2 = i2 - ROUTE_EXPERT_LANE0

    a1 = (lane == e1).astype(F32)
    a2 = (lane == e2).astype(F32)
    both = a1 + a2
    before = _dot(tri, both.astype(BF16)) - both + carry[...]
    r1 = jnp.sum(a1 * before, axis=-1, keepdims=True).astype(I32)
    r2 = jnp.sum(a2 * before, axis=-1, keepdims=True).astype(I32)
    carry[...] = carry[...] + jnp.sum(both, axis=0, keepdims=True)
    mi = jnp.where(lane == 0, e1, jnp.where(lane == 1, e2, jnp.where(lane == 2, r1, jnp.where(lane == 3, r2, 0))))
    mf = jnp.where(lane == 0, w1, jnp.where(lane == 1, w2, 0.0))
    return mi, mf


def _dest_body(mi_ref, off_ref, da_ref, db_ref):
    tm = da_ref.shape[2]
    lane = lax.broadcasted_iota(I32, (tm, LANES), 1)
    off = off_ref[...]
    for s in range(da_ref.shape[0]):
        mi = mi_ref[s * tm:(s + 1) * tm, :]
        d0 = jnp.sum(jnp.where(lane == mi[:, 0:1], off, 0), axis=-1, keepdims=True) + mi[:, 2:3]
        d1 = jnp.sum(jnp.where(lane == mi[:, 1:2], off, 0), axis=-1, keepdims=True) + mi[:, 3:4]
        dt = jnp.where(lane == 0, d0, jnp.where(lane == 1, d1, 0)).T
        da_ref[s] = dt[0:1]
        db_ref[s] = dt[1:2]


def _dest_rows(mi, offsets_row, tm):
    t = mi.shape[0]
    tiles_per_step = math.gcd(8, t // tm)
    out = pl.BlockSpec((tiles_per_step, 1, tm), lambda i: (i, 0, 0))
    return pl.pallas_call(
        _dest_body,
        grid=(t // (tm * tiles_per_step),),
        in_specs=[pl.BlockSpec((tm * tiles_per_step, LANES), lambda i: (i, 0)),
                  pl.BlockSpec((1, LANES), lambda i: (0, 0))],
        out_specs=[out, out],
        out_shape=[jax.ShapeDtypeStruct((t // tm, 1, tm), I32)] * 2,
        compiler_params=_cparams("parallel"),
        name="moe_dest",
    )(mi, offsets_row)


def _dispatch_body(zs_ref, na_ref, da_ref, db_ref, xn_ref, xs_ref, zbuf, sem, zsem):
    tm = xn_ref.shape[0]
    n_tiles = xs_ref.shape[0] // tm

    @pl.when(pl.program_id(0) == 0)
    def _():
        zbuf[...] = jnp.zeros_like(zbuf)

        def fill(row0):
            return pltpu.make_async_copy(zbuf, xs_ref.at[pl.ds(pl.multiple_of(row0, tm), tm), :], zsem)

        def tail(e, c):
            @pl.when(zs_ref[e] >= 0)
            def _():
                fill(zs_ref[e]).start()
            return c

        def unused(j, c):
            fill(j * tm).start()
            return c

        def drain(j, c):
            fill(0).wait()
            return c

        lax.fori_loop(0, MOE_EXPERTS, tail, 0)
        lax.fori_loop(na_ref[0], n_tiles, unused, 0)
        lax.fori_loop(0, na_ref[1] + n_tiles - na_ref[0], drain, 0)

    def start(r, c):
        for k, d_ref in enumerate((da_ref, db_ref)):
            pltpu.make_async_copy(xn_ref.at[pl.ds(r, 1), :], xs_ref.at[pl.ds(d_ref[0, r], 1), :],
                                  sem.at[k]).start(priority=k)
        return c

    lax.fori_loop(0, tm, start, 0, unroll=8)
    for k in range(MOE_TOPK):
        pltpu.make_async_copy(xn_ref, xs_ref.at[pl.ds(0, tm), :], sem.at[k]).wait()


def _dispatch(zero_start, n_active, dest_a, dest_b, xn, n_rows, tm):
    t, dw = xn.shape
    smem_row = pl.BlockSpec((None, 1, tm), lambda i, zs, na: (i, 0, 0), memory_space=pltpu.SMEM)
    grid_spec = pltpu.PrefetchScalarGridSpec(
        num_scalar_prefetch=2,
        grid=(t // tm,),
        in_specs=[smem_row, smem_row, pl.BlockSpec((tm, dw), lambda i, zs, na: (i, 0))],
        out_specs=pl.BlockSpec(memory_space=pl.ANY),
        scratch_shapes=[pltpu.VMEM((tm, dw), xn.dtype), pltpu.SemaphoreType.DMA((MOE_TOPK,)),
                        pltpu.SemaphoreType.DMA(())],
    )
    return pl.pallas_call(
        _dispatch_body,
        grid_spec=grid_spec,
        out_shape=jax.ShapeDtypeStruct((n_rows, dw), xn.dtype),
        compiler_params=_cparams("arbitrary"),
        name="moe_dispatch",
    )(zero_start, n_active, dest_a, dest_b, xn)


def _experts_body(te_ref, na_ref, x_ref, g_ref, wg_ref, wu_ref, wd_ref, y_ref, wg_b, wu_b, wd_b):
    i = pl.program_id(0)

    @pl.when(i < na_ref[0])
    def _():
        @pl.when(jnp.logical_or(i == 0, te_ref[i] != te_ref[jnp.maximum(i - 1, 0)]))
        def _():
            wg_b[...] = wg_ref[...].astype(BF16)
            wu_b[...] = wu_ref[...].astype(BF16)
            wd_b[...] = wd_ref[...].astype(BF16)

        x = _rms(x_ref[...], g_ref[...]).astype(BF16)
        gt = _dot(x, wg_b[...])
        up = _dot(x, wu_b[...])
        hid = (gt * jax.nn.sigmoid(gt) * up).astype(BF16)
        y_ref[...] = _dot(hid, wd_b[...])

    @pl.when(i >= na_ref[0])
    def _():
        y_ref[...] = jnp.zeros_like(y_ref)


def _experts(tile_expert, n_active, xs, norm_w, w_gate, w_up, w_down, tm):
    r, d = xs.shape
    f = w_gate.shape[2]
    row_tile = lambda i, te, na: (jnp.minimum(i, na[0] - 1), 0)
    grid_spec = pltpu.PrefetchScalarGridSpec(
        num_scalar_prefetch=2,
        grid=(r // tm,),
        in_specs=[pl.BlockSpec((tm, d), row_tile),
                  pl.BlockSpec((1, d), lambda i, te, na: (0, 0)),
                  pl.BlockSpec((None, d, f), lambda i, te, na: (te[i], 0, 0)),
                  pl.BlockSpec((None, d, f), lambda i, te, na: (te[i], 0, 0)),
                  pl.BlockSpec((None, f, d), lambda i, te, na: (te[i], 0, 0))],
        out_specs=pl.BlockSpec((tm, d), lambda i, te, na: (i, 0)),
        scratch_shapes=[pltpu.VMEM((d, f), BF16), pltpu.VMEM((d, f), BF16), pltpu.VMEM((f, d), BF16)],
    )
    return pl.pallas_call(
        _experts_body,
        grid_spec=grid_spec,
        out_shape=jax.ShapeDtypeStruct((r, d), xs.dtype),
        compiler_params=_cparams("arbitrary"),
        name="moe_experts",
    )(tile_expert, n_active, xs, norm_w, w_gate, w_up, w_down)


def _combine_body(da_ref, db_ref, na_ref, nb_ref, x_ref, mf_ref, g_ref, ys_ref, o_ref, buf, sem):
    tm = x_ref.shape[0]
    i = pl.program_id(0)
    slot = lax.rem(i, 2)

    def gather(d_refs, s):
        def start(r, c):
            for k, d_ref in enumerate(d_refs):
                pltpu.make_async_copy(ys_ref.at[pl.ds(d_ref[0, r], 1), :],
                                      buf.at[s, k, pl.ds(r, 1), :], sem.at[s, k]).start(priority=k)
            return c

        lax.fori_loop(0, tm, start, 0, unroll=8)

    @pl.when(i == 0)
    def _():
        gather((da_ref, db_ref), 0)

    @pl.when(i + 1 < pl.num_programs(0))
    def _():
        gather((na_ref, nb_ref), 1 - slot)

    for k in range(MOE_TOPK):
        pltpu.make_async_copy(ys_ref.at[pl.ds(0, tm), :], buf.at[slot, k], sem.at[slot, k]).wait()
    mf = mf_ref[...]
    y = x_ref[...] + mf[:, 0:1] * buf[slot, 0] + mf[:, 1:2] * buf[slot, 1]
    o_ref[...] = _rms(y, g_ref[...])


def _combine(dest_a, dest_b, x2d, mf, norm_w, ys, tm):
    t, d = x2d.shape
    last = t // tm - 1
    cur = pl.BlockSpec((None, 1, tm), lambda i: (i, 0, 0), memory_space=pltpu.SMEM)
    nxt = pl.BlockSpec((None, 1, tm), lambda i: (jnp.minimum(i + 1, last), 0, 0), memory_space=pltpu.SMEM)
    return pl.pallas_call(
        _combine_body,
        grid=(t // tm,),
        in_specs=[cur, cur, nxt, nxt,
                  pl.BlockSpec((tm, d), lambda i: (i, 0)),
                  pl.BlockSpec((tm, LANES), lambda i: (i, 0)),
                  pl.BlockSpec((1, d), lambda i: (0, 0)),
                  pl.BlockSpec(memory_space=pl.ANY)],
        out_specs=pl.BlockSpec((tm, d), lambda i: (i, 0)),
        out_shape=jax.ShapeDtypeStruct((t, d), F32),
        scratch_shapes=[pltpu.VMEM((2, MOE_TOPK, tm, ys.shape[1]), ys.dtype), pltpu.SemaphoreType.DMA((2, MOE_TOPK))],
        compiler_params=_cparams("arbitrary"),
        name="moe_combine",
    )(dest_a, dest_b, dest_a, dest_b, x2d, mf, norm_w, ys)


MOE_ROW_TILE = 256


def _moe(x2d, mi, mf, cnt, norm_w, w_gate, w_up, w_down, norm_final):
    t, d = x2d.shape
    tm = MOE_ROW_TILE
    counts = cnt[0, :MOE_EXPERTS].astype(I32)
    padded = ((counts + tm - 1) // tm) * tm
    ends = jnp.cumsum(padded)
    offsets = ends - padded
    n_tiles = (MOE_TOPK * t + MOE_EXPERTS * (tm - 1)) // tm
    tile_start = jnp.arange(n_tiles, dtype=I32) * tm
    tile_expert = jnp.minimum(jnp.sum((ends[None, :] <= tile_start[:, None]).astype(I32), axis=1), MOE_EXPERTS - 1)
    n_info = jnp.stack([ends[-1] // tm, jnp.sum((counts > 0).astype(I32))]).astype(I32)
    zero_start = jnp.where(counts > 0, ends - tm, -1).astype(I32)
    offsets_row = jnp.pad(offsets, (0, LANES - MOE_EXPERTS))[None]
    dest_a, dest_b = _dest_rows(mi, offsets_row, tm)
    xs = _dispatch(zero_start, n_info, dest_a, dest_b, x2d, n_tiles * tm, tm)
    ys = _experts(tile_expert, n_info, xs, norm_w, w_gate, w_up, w_down, tm)
    return _combine(dest_a, dest_b, x2d, mf, norm_final, ys, tm)


def kernel(x, mem, norm_mix, w_in, w_out,
           s5_lam_re_f, s5_lam_im_f, s5_log_step_f, s5_b_re_f, s5_b_im_f, s5_c_re_f, s5_c_im_f,
           s5_lam_re_b, s5_lam_im_b, s5_log_step_b, s5_b_re_b, s5_b_im_b, s5_c_re_b, s5_c_im_b,
           s5_d, s5_w_glu, s5_b_glu, s5_norm,
           gdn_conv, gdn_a_log_f, gdn_dt_bias_f, gdn_a_log_b, gdn_dt_bias_b, gdn_norm,
           norm_xattn, norm_mem, xa_wq, xa_wk, xa_wv, xa_wo,
           norm_moe, router_group_w, router_group_b, router_expert_w, router_expert_b,
           moe_w_gate, moe_w_up, moe_w_down, norm_final):
    bsz, seq, d = x.shape
    t = bsz * seq
    l = 0
    x2d = x.reshape(t, d)
    wi = w_in[l]
    wut = wi[:, :S5_WIDTH].T
    wqkvz = wi[:, S5_WIDTH:S5_WIDTH + 4 * GDN_WIDTH]
    wg = wi[:, S5_WIDTH + 4 * GDN_WIDTH:].reshape(d, 4, GDN_HEADS)
    wg = jnp.pad(jnp.swapaxes(wg, 1, 2), ((0, 0), (0, 0), (0, 4))).reshape(d, GDN_HEADS * 8)
    wgt = wg.T
    u3, qkvz, gates = _in_proj(x2d, norm_mix[l][None], wut, wqkvz, wgt)
    s5p = dict(lam_re_f=s5_lam_re_f[l], lam_im_f=s5_lam_im_f[l], log_step_f=s5_log_step_f[l],
               b_re_f=s5_b_re_f[l], b_im_f=s5_b_im_f[l], c_re_f=s5_c_re_f[l], c_im_f=s5_c_im_f[l],
               lam_re_b=s5_lam_re_b[l], lam_im_b=s5_lam_im_b[l], log_step_b=s5_log_step_b[l],
               b_re_b=s5_b_re_b[l], b_im_b=s5_b_im_b[l], c_re_b=s5_c_re_b[l], c_im_b=s5_c_im_b[l], d=s5_d[l])
    g3 = _s5_scan(u3, s5p, seq // S5_CHUNK)
    y_s5 = _s5_post(g3, s5_w_glu[l].T.astype(BF16), s5_b_glu[l][:, None], s5_norm[l][:, None])

    head_par = jnp.stack([gdn_a_log_f[l], gdn_dt_bias_f[l], gdn_a_log_b[l], gdn_dt_bias_b[l]], axis=1)
    head_par = jnp.broadcast_to(jnp.pad(head_par, ((0, 0), (0, 4)))[:, :, None], (GDN_HEADS, 8, LANES))
    y_gdn = _gdn(qkvz, gates, gdn_conv[l], head_par, gdn_norm[l][None], bsz, seq)

    mem_len = mem.shape[1]
    kmem, vmem = _mem_kv(mem.reshape(bsz * mem_len, d), norm_mem[l][None],
                         xa_wk[l].astype(BF16), xa_wv[l].astype(BF16))
    n_pad = LANES - MOE_GROUPS - MOE_EXPERTS
    w_route = jnp.pad(jnp.concatenate([router_group_w[l], router_expert_w[l]], axis=1), ((0, 0), (0, n_pad)))
    b_route = jnp.pad(jnp.concatenate([router_group_b[l], router_expert_b[l]]), (0, n_pad))[None]
    x2, mi, mf, cnt = _mix_xattn(x2d, y_s5, y_gdn, w_out[l], norm_xattn[l][None], xa_wq[l], kmem, vmem, xa_wo[l],
                                 norm_moe[l][None], w_route, b_route, seq, mem_len)
    y = _moe(x2, mi, mf, cnt, norm_moe[l][None], moe_w_gate[l], moe_w_up[l], moe_w_down[l], norm_final[None])
    return y.reshape(bsz, seq, d)
```

```python
import functools
import math

import jax
import jax.numpy as jnp
from jax import lax
from jax.experimental import pallas as pl
from jax.experimental.pallas import tpu as pltpu

F32 = jnp.float32
BF16 = jnp.bfloat16
I32 = jnp.int32

D_MODEL = 1024
S5_WIDTH = 512
S5_GROUP = 16
S5_GROUPS = 32
S5_STATE = 64
S5_CHUNK = 128
GDN_HEADS = 4
GDN_HEAD_DIM = 128
GDN_WIDTH = 512
GDN_CONV = 5
GDN_CHUNK = 64
XA_HEADS = 4
XA_HEAD_DIM = 256
MOE_GROUPS = 4
MOE_PER_GROUP = 8
MOE_EXPERTS = 32
MOE_TOPK = 2
D_EXPERT = 256
RMS_EPS = 1e-6
L2_EPS = 1e-6
LANES = 128
VMEM_LIMIT = 56 * 1024 * 1024


def _cparams(*sem):
    return pltpu.CompilerParams(dimension_semantics=tuple(sem), vmem_limit_bytes=VMEM_LIMIT)


def _rms(x, gain):
    return x * lax.rsqrt(jnp.mean(x * x, axis=-1, keepdims=True) + RMS_EPS) * gain


def _dot(a, b):
    return jnp.dot(a, b, preferred_element_type=F32)


def _dot_nt(a, b):
    return lax.dot_general(a, b, (((1,), (1,)), ((), ())), preferred_element_type=F32)


def _dot_tn(a, b):
    return lax.dot_general(a, b, (((0,), (0,)), ((), ())), preferred_element_type=F32)


def _in_proj_body(x_ref, g_ref, w_ref, wgt_ref, u_ref, qkvz_ref, gates_ref, wut_b, wqkvz_b, wgt_b):
    @pl.when(pl.program_id(0) == 0)
    def _():
        wut_b[...] = w_ref[:, :S5_WIDTH].T.astype(BF16)
        wqkvz_b[...] = w_ref[:, S5_WIDTH:S5_WIDTH + 4 * GDN_WIDTH].astype(BF16)
        wgt_b[...] = wgt_ref[...].astype(BF16)

    h = _rms(x_ref[...], g_ref[...]).astype(BF16)
    ut = _dot_nt(wut_b[...], h)
    gt = _dot_nt(wgt_b[...], h)
    for j in range(u_ref.shape[0]):
        u_ref[j] = ut[:, j * S5_CHUNK:(j + 1) * S5_CHUNK]
        gates_ref[j] = gt[:, j * S5_CHUNK:(j + 1) * S5_CHUNK]
    qkvz_ref[...] = _dot(h, wqkvz_b[...]).astype(BF16)


def _in_proj(x2d, norm_w, w_in, wgt, tm=512):
    t = x2d.shape[0]
    nck = tm // S5_CHUNK
    nqkvz = 4 * GDN_WIDTH
    full = lambda shape: pl.BlockSpec(shape, lambda i: (0,) * len(shape))
    return pl.pallas_call(
        _in_proj_body,
        grid=(t // tm,),
        in_specs=[pl.BlockSpec((tm, D_MODEL), lambda i: (i, 0)),
                  full((1, D_MODEL)), full(w_in.shape), full(wgt.shape)],
        out_specs=[pl.BlockSpec((nck, S5_WIDTH, S5_CHUNK), lambda i: (i, 0, 0)),
                   pl.BlockSpec((tm, nqkvz), lambda i: (i, 0)),
                   pl.BlockSpec((nck, wgt.shape[0], S5_CHUNK), lambda i: (i, 0, 0))],
        out_shape=[jax.ShapeDtypeStruct((t // S5_CHUNK, S5_WIDTH, S5_CHUNK), F32),
                   jax.ShapeDtypeStruct((t, nqkvz), BF16),
                   jax.ShapeDtypeStruct((t // S5_CHUNK, wgt.shape[0], S5_CHUNK), F32)],
        scratch_shapes=[pltpu.VMEM((S5_WIDTH, D_MODEL), BF16), pltpu.VMEM((D_MODEL, nqkvz), BF16),
                        pltpu.VMEM(wgt.shape, BF16)],
        compiler_params=_cparams("arbitrary"),
        name="in_proj",
    )(x2d, norm_w, w_in, wgt)


def _cmul(ar, ai, br, bi):
    return ar * br - ai * bi, ar * bi + ai * br


def _cpow_int(lr, li, expo, nbits):
    res_r = jnp.ones(jnp.broadcast_shapes(lr.shape, expo.shape), F32)
    res_i = jnp.zeros_like(res_r)
    for b in range(nbits):
        bit = ((expo >> b) & 1) == 1
        nr, ni = _cmul(res_r, res_i, lr, li)
        res_r = jnp.where(bit, nr, res_r)
        res_i = jnp.where(bit, ni, res_i)
        if b + 1 < nbits:
            lr, li = _cmul(lr, li, lr, li)
    return res_r, res_i


def _lam_bar(re, im, step):
    er = jnp.exp(step * re)
    return er * jnp.cos(step * im), er * jnp.sin(step * im)


def _zoh_coef(re, im, lr, li):
    den = re * re + im * im
    return ((lr - 1.0) * re + li * im) / den, (li * re - (lr - 1.0) * im) / den


def _s5_body(u_ref, lrow_ref, lcol_ref, l256_ref, b_ref, bt_ref, c_ref, ct_ref, d_ref,
             o_ref, vf_ref, vb_ref, m_ref, win_ref, wout_ref, sf_ref, sb_ref, hf_ref, hb_ref, uc_ref, *, nchunk):
    L = S5_CHUNK
    P = S5_STATE
    n_rows = u_ref.shape[0]
    nb = n_rows // nchunk
    lane_i = lax.broadcasted_iota(I32, (1, L), 1)

    lcol = lcol_ref[...]
    lbc_r, lbc_i = _lam_bar(lcol[:, 0:2], lcol[:, 2:4], lcol[:, 4:6])
    kc_r, kc_i = _zoh_coef(lcol[:, 0:2], lcol[:, 2:4], lbc_r, lbc_i)
    lrow = lrow_ref[...]
    lbr_r, lbr_i = _lam_bar(lrow[0:2], lrow[2:4], lrow[4:6])
    kr_r, kr_i = _zoh_coef(lrow[0:2], lrow[2:4], lbr_r, lbr_i)
    lf_r, lf_i, lb_r, lb_i = lbc_r[:, 0:1], lbc_i[:, 0:1], lbc_r[:, 1:2], lbc_i[:, 1:2]
    kfr_c, kfi_c, kbr_c, kbi_c = kc_r[:, 0:1], kc_i[:, 0:1], kc_r[:, 1:2], kc_i[:, 1:2]
    kfr_r, kfi_r, kbr_r, kbi_r = kr_r[0:1], kr_i[0:1], kr_r[1:2], kr_i[1:2]

    pwf_r, pwf_i = _cpow_int(lf_r, lf_i, lane_i, 7)
    rvf_r, rvf_i = _cpow_int(lf_r, lf_i, (L - 1) - lane_i, 7)
    pwb_r, pwb_i = _cpow_int(lb_r, lb_i, lane_i, 7)
    rvb_r, rvb_i = _cpow_int(lb_r, lb_i, L - lane_i, 8)
    nxf_r, nxf_i = _cmul(pwf_r, pwf_i, lf_r, lf_i)

    bf_r = kfr_c * b_ref[0] - kfi_c * b_ref[1]
    bf_i = kfr_c * b_ref[1] + kfi_c * b_ref[0]
    bb_r = kbr_c * b_ref[2] - kbi_c * b_ref[3]
    bb_i = kbr_c * b_ref[3] + kbi_c * b_ref[2]
    btf_r = kfr_r * bt_ref[0] - kfi_r * bt_ref[1]
    btf_i = kfr_r * bt_ref[1] + kfi_r * bt_ref[0]
    btb_r = kbr_r * bt_ref[2] - kbi_r * bt_ref[3]
    btb_i = kbr_r * bt_ref[3] + kbi_r * bt_ref[2]

    def taps(c_r, c_i, bt_r, bt_i, pw_r, pw_i):
        cb_r = (bt_r[:, None, :] * c_r[None, :, :] - bt_i[:, None, :] * c_i[None, :, :]).reshape(256, P)
        cb_i = (bt_r[:, None, :] * c_i[None, :, :] + bt_i[:, None, :] * c_r[None, :, :]).reshape(256, P)
        k = (jnp.dot(cb_r, pw_r, preferred_element_type=F32, precision=lax.Precision.HIGHEST)
             - jnp.dot(cb_i, pw_i, preferred_element_type=F32, precision=lax.Precision.HIGHEST))
        return k, jnp.sum(cb_r, axis=1, keepdims=True)

    kf, _ = taps(c_ref[0], c_ref[1], btf_r, btf_i, pwf_r, pwf_i)
    kb, kb0 = taps(c_ref[2], c_ref[3], btb_r, btb_i, rvb_r, rvb_i)
    is0 = lane_i == 0
    vf_ref[...] = kf + jnp.where(is0, kb0, 0.0)
    vb_ref[...] = jnp.where(is0, 0.0, kb)

    row_i = lax.broadcasted_iota(I32, (L, L), 0)
    col_i = lax.broadcasted_iota(I32, (L, L), 1)
    fwd_lane = col_i + row_i < L

    def build_ci(ci, carry):
        for co in range(S5_GROUP):
            r = ci * S5_GROUP + co
            taps_rows = jnp.where(fwd_lane, jnp.broadcast_to(vf_ref[pl.ds(r, 1), :], (L, L)),
                                  jnp.broadcast_to(vb_ref[pl.ds(r, 1), :], (L, L)))
            m_ref[pl.ds(pl.multiple_of(ci * L, L), L), co * L:(co + 1) * L] = pltpu.roll(
                taps_rows, 0, 1, stride=1, stride_axis=0).astype(BF16)
        return carry

    lax.fori_loop(0, S5_GROUP, build_ci, 0)

    for ci in range(S5_GROUP):
        sl = slice(ci * L, (ci + 1) * L)
        br, bi = bf_r[:, ci:ci + 1], bf_i[:, ci:ci + 1]
        win_ref[0 * P:1 * P, sl] = (rvf_r * br - rvf_i * bi).astype(BF16)
        win_ref[1 * P:2 * P, sl] = (rvf_r * bi + rvf_i * br).astype(BF16)
        br, bi = bb_r[:, ci:ci + 1], bb_i[:, ci:ci + 1]
        win_ref[2 * P:3 * P, sl] = (pwb_r * br - pwb_i * bi).astype(BF16)
        win_ref[3 * P:4 * P, sl] = (pwb_r * bi + pwb_i * br).astype(BF16)
    for co in range(S5_GROUP):
        sl = slice(co * L, (co + 1) * L)
        cr, ci_ = ct_ref[0][:, co:co + 1], ct_ref[1][:, co:co + 1]
        wout_ref[0 * P:1 * P, sl] = (cr * nxf_r - ci_ * nxf_i).astype(BF16)
        wout_ref[1 * P:2 * P, sl] = (-(cr * nxf_i + ci_ * nxf_r)).astype(BF16)
        cr, ci_ = ct_ref[2][:, co:co + 1], ct_ref[3][:, co:co + 1]
        wout_ref[2 * P:3 * P, sl] = (cr * rvb_r - ci_ * rvb_i).astype(BF16)
        wout_ref[3 * P:4 * P, sl] = (-(cr * rvb_i + ci_ * rvb_r)).astype(BF16)

    for ci in range(S5_GROUP):
        uc_ref[ci] = u_ref[:, ci, :]
    ucat = jnp.concatenate([uc_ref[ci].astype(BF16) for ci in range(S5_GROUP)], axis=1)

    summ = _dot_nt(ucat, win_ref[...])
    sf_ref[...] = summ[:, :2 * P]
    sb_ref[...] = summ[:, 2 * P:]
    l256 = l256_ref[...]
    a_mul, a_im = _lam_bar(l256[0:1], l256[1:2], l256[2:3])
    for _ in range(7):
        a_mul, a_im = _cmul(a_mul, a_im, a_mul, a_im)
    lane256 = lax.broadcasted_iota(I32, (1, 4 * P), 1)
    b_mul = jnp.where((lane256 // P) % 2 == 0, -a_im, a_im)

    hf = jnp.zeros((nb, 2 * P), F32)
    hb = jnp.zeros((nb, 2 * P), F32)
    for c in range(nchunk):
        cr = nchunk - 1 - c
        rows_f = pl.ds(c, nb, stride=nchunk)
        rows_b = pl.ds(cr, nb, stride=nchunk)
        hf_ref[rows_f, :] = hf
        hb_ref[rows_b, :] = hb
        hf = a_mul[:, :2 * P] * hf + b_mul[:, :2 * P] * pltpu.roll(hf, P, 1) + sf_ref[rows_f, :]
        hb = a_mul[:, 2 * P:] * hb + b_mul[:, 2 * P:] * pltpu.roll(hb, P, 1) + sb_ref[rows_b, :]
    hprev = jnp.concatenate([hf_ref[...], hb_ref[...]], axis=1).astype(BF16)

    nblk = 2 * L
    for j in range(S5_GROUP * L // nblk):
        y = _dot(ucat, m_ref[:, j * nblk:(j + 1) * nblk]) + _dot(hprev, wout_ref[:, j * nblk:(j + 1) * nblk])
        for q in range(nblk // L):
            co = j * (nblk // L) + q
            yc = y[:, q * L:(q + 1) * L] + d_ref[co:co + 1, :] * uc_ref[co]
            o_ref[:, co, :] = 0.5 * yc * (1.0 + lax.erf(yc * (2.0 ** -0.5)))


def _s5_scan(u3, p, nchunk):
    n = u3.shape[0]
    g, grp, st, L = S5_GROUPS, S5_GROUP, S5_STATE, S5_CHUNK
    step_f = jnp.exp(p["log_step_f"])[:, None] * jnp.ones((1, st), F32)
    step_b = jnp.exp(p["log_step_b"])[:, None] * jnp.ones((1, st), F32)
    zeros = jnp.zeros((g, st), F32)
    lrow = jnp.stack([p["lam_re_f"], p["lam_re_b"], p["lam_im_f"], p["lam_im_b"], step_f, step_b, zeros, zeros], axis=1)
    lcol = jnp.swapaxes(lrow, 1, 2)
    cat4 = lambda f, b: jnp.concatenate([f, f, b, b], axis=1)
    z256 = jnp.zeros((g, 4 * st), F32)
    l256 = jnp.stack([cat4(p["lam_re_f"], p["lam_re_b"]), cat4(p["lam_im_f"], p["lam_im_b"]),
                      cat4(step_f, step_b)] + [z256] * 5, axis=1)
    b4 = jnp.stack([p["b_re_f"], p["b_im_f"], p["b_re_b"], p["b_im_b"]], axis=1)
    bt4 = jnp.swapaxes(b4, 2, 3)
    c4 = jnp.stack([p["c_re_f"], p["c_im_f"], p["c_re_b"], p["c_im_b"]], axis=1)
    ct4 = jnp.swapaxes(c4, 2, 3)
    dbc = jnp.broadcast_to(p["d"].reshape(g, grp, 1), (g, grp, L))
    per_g = lambda *shape: pl.BlockSpec((None,) + shape, lambda i: (i,) + (0,) * len(shape))
    return pl.pallas_call(
        functools.partial(_s5_body, nchunk=nchunk),
        grid=(g,),
        in_specs=[pl.BlockSpec((n, grp, L), lambda i: (0, i, 0)),
                  per_g(8, st), per_g(st, 8), per_g(8, 4 * st), per_g(4, st, grp), per_g(4, grp, st),
                  per_g(4, grp, st), per_g(4, st, grp), per_g(grp, L)],
        out_specs=pl.BlockSpec((n, grp, L), lambda i: (0, i, 0)),
        out_shape=jax.ShapeDtypeStruct(u3.shape, F32),
        scratch_shapes=[pltpu.VMEM((grp * grp, L), F32), pltpu.VMEM((grp * grp, L), F32),
                        pltpu.VMEM((grp * L, grp * L), BF16),
                        pltpu.VMEM((4 * st, grp * L), BF16), pltpu.VMEM((4 * st, grp * L), BF16),
                        pltpu.VMEM((n, 2 * st), F32), pltpu.VMEM((n, 2 * st), F32),
                        pltpu.VMEM((n, 2 * st), F32), pltpu.VMEM((n, 2 * st), F32),
                        pltpu.VMEM((grp, n, L), F32)],
        compiler_params=_cparams("parallel"),
        name="s5_scan",
    )(u3, lrow, lcol, l256, b4, bt4, c4, ct4, dbc)


def _s5_post_body(g_ref, wt_ref, b_ref, nw_ref, o_ref):
    for j in range(g_ref.shape[0]):
        g = g_ref[j]
        z = _dot(wt_ref[...], g.astype(BF16)) + b_ref[...]
        y = g * jax.nn.sigmoid(z)
        y = y * lax.rsqrt(jnp.mean(y * y, axis=0, keepdims=True) + RMS_EPS) * nw_ref[...]
        o_ref[j * S5_CHUNK:(j + 1) * S5_CHUNK, :] = y.T.astype(BF16)


def _s5_post(g3, w_glu_t, b_glu_col, norm_col, nck=4):
    n = g3.shape[0]
    full = lambda shape: pl.BlockSpec(shape, lambda i: (0,) * len(shape))
    return pl.pallas_call(
        _s5_post_body,
        grid=(n // nck,),
        in_specs=[pl.BlockSpec((nck, S5_WIDTH, S5_CHUNK), lambda i: (i, 0, 0)),
                  full(w_glu_t.shape), full(b_glu_col.shape), full(norm_col.shape)],
        out_specs=pl.BlockSpec((nck * S5_CHUNK, S5_WIDTH), lambda i: (i, 0)),
        out_shape=jax.ShapeDtypeStruct((n * S5_CHUNK, S5_WIDTH), BF16),
        compiler_params=_cparams("parallel"),
        name="s5_post",
    )(g3, w_glu_t, b_glu_col, norm_col)


GDN_BLOCK = 128
GDN_PAIR = 2
GDN_PREP_BATCH = 8


def _packed_tri_inverse(lps, low, upp, bd16, rings):
    def pk(xs, ys):
        outs = []
        for a, b in zip(xs, ys):
            lhs = jnp.concatenate([jnp.where(low, a, 0.0), jnp.where(upp, a, 0.0)], axis=1).astype(BF16)
            rhs = jnp.concatenate([jnp.where(low, b, 0.0), jnp.where(upp, b, 0.0)], axis=0).astype(BF16)
            outs.append(_dot(lhs, rhs))
        return outs

    d = [jnp.where(bd16, lp, 0.0) for lp in lps]
    d2 = pk(d, d)
    d4 = pk(d2, d2)
    d8 = pk(d4, d4)
    a = [y - x - p for x, y, p in zip(d, d2, pk(d, d2))]
    a = [x + y + p for x, y, p in zip(a, d4, pk(a, d4))]
    a = [x + y + p for x, y, p in zip(a, d8, pk(a, d8))]
    for ring in rings:
        n = [jnp.where(ring, lp, 0.0) for lp in lps]
        t = [x + p for x, p in zip(n, pk(a, n))]
        a = [x - y - p for x, y, p in zip(a, t, pk(t, a))]
    return a


def _gdn_body(q_ref, k_ref, v_ref, z_ref, wq_ref, wk_ref, wv_ref, g_ref, hp_ref, nw_ref, o_ref,
              qs, ks, vs, os_, sg, cf, cb, uf, ub, wqf, wqb, qkf, qkb, kdtf, kdtb, eglf, eglb, xpad):
    seq = q_ref.shape[0]
    C = GDN_BLOCK
    hd = GDN_HEAD_DIM
    nck = seq // C
    heads = range(GDN_PAIR)

    pad = 8
    half = (GDN_CONV - 1) // 2
    xpad[0:pad, :] = jnp.zeros((pad, LANES), F32)
    xpad[pad + seq:2 * pad + seq, :] = jnp.zeros((pad, LANES), F32)

    def conv_silu(x_ref, w_ref, j):
        cols = pl.ds(pl.multiple_of(j * hd, hd), hd)
        xpad[pad:pad + seq, :] = x_ref[:, cols].astype(F32)
        w = w_ref[:, cols]
        acc = xpad[pad - half:pad - half + seq, :] * w[0:1]
        for tap in range(1, GDN_CONV):
            acc = acc + xpad[pad - half + tap:pad - half + tap + seq, :] * w[tap:tap + 1]
        return acc * jax.nn.sigmoid(acc)

    def l2n(x):
        return x * lax.rsqrt(jnp.sum(x * x, axis=-1, keepdims=True) + L2_EPS)

    def softplus(x):
        return jnp.maximum(x, 0.0) + jnp.log1p(jnp.exp(-jnp.abs(x)))

    lane = lax.broadcasted_iota(I32, (1, C), 1)

    def prologue(j, carry):
        qs[j] = l2n(conv_silu(q_ref, wq_ref, j)) * (hd ** -0.5)
        ks[j] = l2n(conv_silu(k_ref, wk_ref, j))
        vs[j] = conv_silu(v_ref, wv_ref, j)
        os_[j] = jnp.zeros((seq, hd), F32)
        g = g_ref[:, pl.ds(pl.multiple_of(8 * j, 8), 8), :].reshape(nck * 8, C)
        hp = hp_ref[j]
        sg[j] = jax.nn.sigmoid(g)
        gl_f = -jnp.exp(hp[0:1]) * softplus(g + hp[1:2])
        gl_b = -jnp.exp(hp[2:3]) * softplus(g + hp[3:4])
        sh = 1
        while sh < C:
            gl_f = gl_f + jnp.where(lane >= sh, pltpu.roll(gl_f, sh, 1), 0.0)
            gl_b = gl_b + jnp.where(lane < C - sh, pltpu.roll(gl_b, C - sh, 1), 0.0)
            sh *= 2
        cf[j] = gl_f
        cb[j] = gl_b
        return carry

    lax.fori_loop(0, GDN_PAIR, prologue, 0)

    ri = lax.broadcasted_iota(I32, (C, C), 0)
    ci = lax.broadcasted_iota(I32, (C, C), 1)
    low, upp = ri > ci, ri < ci
    low_i, upp_i = ri >= ci, ri <= ci
    same = lambda w: (ri // w) == (ci // w)
    bd16 = same(16)
    rings = []
    w = 32
    while w <= C:
        rings.append(jnp.logical_and(same(w), jnp.logical_not(same(w // 2))))
        w *= 2

    nbatch = math.gcd(GDN_PREP_BATCH // GDN_PAIR, nck)

    def column(ref, r):
        rows = jnp.broadcast_to(ref[pl.ds(r, 1), :], (C, C))
        return rows.T, rows

    def prepare(it, carry):
        items = [(j, it * nbatch + i) for i in range(nbatch) for j in heads]
        sls = [pl.ds(pl.multiple_of(c * C, C), C) for _, c in items]
        g_f, g_b, bt_f, bt_b, dec_f, dec_b, kb_f, kb_b, aq = [], [], [], [], [], [], [], [], []
        for (j, c), sl in zip(items, sls):
            k = ks[j, sl, :]
            gfc, gfr = column(cf.at[j], c * 8 + 2)
            gbc, gbr = column(cb.at[j], c * 8 + 3)
            g_f.append(gfc)
            g_b.append(gbc)
            bt_f.append(column(sg.at[j], c * 8)[0])
            bt_b.append(column(sg.at[j], c * 8 + 1)[0])
            dec_f.append(jnp.where(low_i, jnp.exp(jnp.where(low_i, gfc - gfr, 0.0)), 0.0))
            dec_b.append(jnp.where(upp_i, jnp.exp(jnp.where(upp_i, gbc - gbr, 0.0)), 0.0))
            kb_f.append(k * bt_f[-1])
            kb_b.append(k * bt_b[-1])
            aq.append(_dot_nt(jnp.concatenate([kb_f[-1], kb_b[-1], qs[j, sl, :]], axis=0).astype(BF16),
                              k.astype(BF16)))
        lps = [jnp.where(low, x[:C] * df, 0.0) + jnp.where(upp, x[C:2 * C] * db, 0.0)
               for x, df, db in zip(aq, dec_f, dec_b)]
        inv = _packed_tri_inverse(lps, low, upp, bd16, rings)
        for n, ((j, c), sl) in enumerate(zip(items, sls)):
            q, k, v = qs[j, sl, :], ks[j, sl, :], vs[j, sl, :]
            for rev, g_c, kb, beta, dec, msk, u_s, wq_s, qk_s, kdt_s, egl_s in (
                    (False, g_f[n], kb_f[n], bt_f[n], dec_f[n], low, uf, wqf, qkf, kdtf, eglf),
                    (True, g_b[n], kb_b[n], bt_b[n], dec_b[n], upp, ub, wqb, qkb, kdtb, eglb)):
                eg = jnp.exp(g_c)
                rhs = jnp.concatenate([v * beta, kb * eg], axis=1)
                uw = rhs + _dot(jnp.where(msk, inv[n], 0.0).astype(BF16), rhs.astype(BF16))
                glast = g_c[0:1] if rev else g_c[C - 1:C]
                u_s[j, sl, :] = uw[:, :C]
                wq_s[j, pl.ds(pl.multiple_of(c * 2 * C, 2 * C), 2 * C), :] = jnp.concatenate(
                    [uw[:, C:], q * eg], axis=0).astype(BF16)
                qk_s[j, sl, :] = (aq[n][2 * C:] * dec).astype(BF16)
                kdt_s[j, sl, :] = (k * jnp.exp(glast - g_c)).T.astype(BF16)
                egl_s[j, pl.ds(c, 1), :] = jnp.exp(glast)
        return carry

    lax.fori_loop(0, nck // nbatch, prepare, 0)

    def body(i, carry):
        chains = []
        for j in heads:
            chains.append((j, i, uf, wqf, qkf, kdtf, eglf))
            chains.append((j, nck - 1 - i, ub, wqb, qkb, kdtb, eglb))
        sls = [pl.ds(pl.multiple_of(c * C, C), C) for _, c, *_ in chains]
        ws_qs = [_dot(wq_s[j, pl.ds(pl.multiple_of(c * 2 * C, 2 * C), 2 * C), :], st.astype(BF16))
                 for (j, c, _, wq_s, *_), st in zip(chains, carry)]
        vnb = [(u_s[j, sl, :] - x[:C]).astype(BF16) for (j, _, u_s, *_), sl, x in zip(chains, sls, ws_qs)]
        new = [st * egl_s[j, pl.ds(c, 1), :] + _dot(kdt_s[j, sl, :], v)
               for (j, c, _, _, _, kdt_s, egl_s), sl, st, v in zip(chains, sls, carry, vnb)]
        for (j, _, _, _, qk_s, _, _), sl, x, v in zip(chains, sls, ws_qs, vnb):
            os_[j, sl, :] += x[C:] + _dot(qk_s[j, sl, :], v)
        return tuple(new)

    zero = jnp.zeros((hd, hd), F32)
    lax.fori_loop(0, nck, body, (zero,) * (2 * GDN_PAIR))

    for j in heads:
        o = os_[j]
        o = o * lax.rsqrt(jnp.mean(o * o, axis=-1, keepdims=True) + RMS_EPS) * nw_ref[...]
        z = z_ref[:, j * hd:(j + 1) * hd].astype(F32)
        o_ref[:, j * hd:(j + 1) * hd] = (o * (z * jax.nn.sigmoid(z))).astype(BF16)


def _gdn(qkvz, gates3, conv_w, head_par, norm_w, bsz, seq):
    t = bsz * seq
    hd = GDN_HEAD_DIM
    nh = GDN_HEADS
    np_ = GDN_PAIR
    wd = np_ * hd
    npairs = nh // np_
    nck = seq // GDN_BLOCK
    col = lambda off: pl.BlockSpec((seq, wd), lambda b, p: (b, off * npairs + p))
    wcol = lambda off: pl.BlockSpec((GDN_CONV, wd), lambda b, p: (0, off * npairs + p))
    per_head = lambda rows, dt: pltpu.VMEM((np_, rows, hd), dt)
    return pl.pallas_call(
        _gdn_body,
        grid=(bsz, npairs),
        in_specs=[col(0), col(1), col(2), col(3), wcol(0), wcol(1), wcol(2),
                  pl.BlockSpec((nck, 8 * np_, GDN_BLOCK), lambda b, p: (b, p, 0)),
                  pl.BlockSpec((np_, 8, LANES), lambda b, p: (p, 0, 0)),
                  pl.BlockSpec((1, hd), lambda b, p: (0, 0))],
        out_specs=pl.BlockSpec((seq, wd), lambda b, p: (b, p)),
        out_shape=jax.ShapeDtypeStruct((t, nh * hd), BF16),
        scratch_shapes=([per_head(seq, F32)] * 4
                        + [per_head(nck * 8, F32)] * 3
                        + [per_head(seq, F32)] * 2
                        + [per_head(2 * seq, BF16)] * 2
                        + [per_head(seq, BF16)] * 2
                        + [per_head(seq, BF16)] * 2
                        + [per_head(nck, F32)] * 2
                        + [pltpu.VMEM((seq + 16, hd), F32)]),
        compiler_params=_cparams("parallel", "parallel"),
        name="gdn",
    )(qkvz, qkvz, qkvz, qkvz, conv_w, conv_w, conv_w, gates3, head_par, norm_w)


def _kv_body(m_ref, g_ref, wk_ref, wv_ref, k_ref, v_ref):
    mn = _rms(m_ref[...], g_ref[...]).astype(BF16)
    k_ref[...] = _dot(mn, wk_ref[...]).astype(BF16)
    v_ref[...] = _dot(mn, wv_ref[...]).astype(BF16)


def _mem_kv(mem2d, norm_w, wk, wv, tm=512):
    r, d = mem2d.shape
    full = lambda shape: pl.BlockSpec(shape, lambda i: (0,) * len(shape))
    tile = pl.BlockSpec((tm, d), lambda i: (i, 0))
    return pl.pallas_call(
        _kv_body,
        grid=(r // tm,),
        in_specs=[tile, full((1, d)), full(wk.shape), full(wv.shape)],
        out_specs=[tile, tile],
        out_shape=[jax.ShapeDtypeStruct((r, d), BF16)] * 2,
        compiler_params=_cparams("parallel"),
        name="mem_kv",
    )(mem2d, norm_w, wk, wv)


def _mix_xattn_body(x_ref, y5_ref, yg_ref, wmix_ref, g_ref, wq_ref, k_ref, v_ref, wo_ref,
                    gm_ref, whi_ref, wlo_ref, br_ref, tri_ref,
                    o_ref, mi_ref, mf_ref, cnt_ref, wmix_b, wq_b, wo_b, carry):
    @pl.when(pl.program_id(0) == 0)
    def _():
        wmix_b[...] = wmix_ref[...].astype(BF16)
        wq_b[...] = wq_ref[...].astype(BF16)
        wo_b[...] = wo_ref[...].astype(BF16)
        carry[...] = jnp.zeros_like(carry)

    x1 = (x_ref[...] + _dot(y5_ref[...], wmix_b[:S5_WIDTH, :]) + _dot(yg_ref[...], wmix_b[S5_WIDTH:, :]))
    xn = _rms(x1, g_ref[...]).astype(BF16)
    q = (_dot(xn, wq_b[...]) * (XA_HEAD_DIM ** -0.5)).astype(BF16)
    heads = []
    for h in range(XA_HEADS):
        sl = slice(h * XA_HEAD_DIM, (h + 1) * XA_HEAD_DIM)
        s = _dot_nt(q[:, sl], k_ref[:, sl])
        p = jnp.exp(s - jnp.max(s, axis=-1, keepdims=True))
        p = p / jnp.sum(p, axis=-1, keepdims=True)
        heads.append(_dot(p.astype(BF16), v_ref[:, sl]).astype(BF16))
    x2 = x1 + _dot(jnp.concatenate(heads, axis=1), wo_b[...])
    o_ref[...] = x2
    mi, mf = _route(_rms(x2, gm_ref[...]), whi_ref[...], wlo_ref[...], br_ref[...], tri_ref[...], carry)
    mi_ref[...] = mi
    mf_ref[...] = mf
    cnt_ref[...] = carry[...]


def _mix_xattn(x2d, y5, yg, w_mix, norm_w, wq, kmem, vmem, wo, norm_moe, w_route, b_route, seq, mem_len, tm=512):
    t, d = x2d.shape
    per_b = seq // tm
    tri = jnp.triu(jnp.ones((tm, tm), BF16))
    w_hi = w_route.astype(BF16)
    w_lo = (w_route - w_hi.astype(F32)).astype(BF16)
    full = lambda shape: pl.BlockSpec(shape, lambda i: (0,) * len(shape))
    tile = lambda w: pl.BlockSpec((tm, w), lambda i: (i, 0))
    return pl.pallas_call(
        _mix_xattn_body,
        grid=(t // tm,),
        in_specs=[tile(d), tile(S5_WIDTH), tile(GDN_WIDTH),
                  full(w_mix.shape), full((1, d)), full(wq.shape),
                  pl.BlockSpec((mem_len, d), lambda i: (i // per_b, 0)),
                  pl.BlockSpec((mem_len, d), lambda i: (i // per_b, 0)),
                  full(wo.shape),
                  full((1, d)), full(w_route.shape), full(w_route.shape), full(b_route.shape), full((tm, tm))],
        out_specs=[tile(d), pl.BlockSpec((8, tm), lambda i: (0, i)), tile(LANES),
                   pl.BlockSpec((ROUTE_ROWS, LANES), lambda i: (0, 0))],
        out_shape=[jax.ShapeDtypeStruct((t, d), F32),
                   jax.ShapeDtypeStruct((8, t), I32),
                   jax.ShapeDtypeStruct((t, LANES), F32),
                   jax.ShapeDtypeStruct((ROUTE_ROWS, LANES), F32)],
        scratch_shapes=[pltpu.VMEM(w_mix.shape, BF16), pltpu.VMEM(wq.shape, BF16), pltpu.VMEM(wo.shape, BF16),
                        pltpu.VMEM((ROUTE_ROWS, LANES), F32)],
        compiler_params=_cparams("arbitrary"),
        name="mix_xattn",
    )(x2d, y5, yg, w_mix, norm_w, wq, kmem, vmem, wo, norm_moe, w_hi, w_lo, b_route, tri)


ROUTE_EXPERT_LANE0 = 4


ROUTE_ROWS = 40


def _route(xn, w_hi, w_lo, bias, tri, carry):
    x_hi = xn.astype(BF16)
    x_lo = (xn - x_hi.astype(F32)).astype(BF16)
    logits = (_dot_nt(w_hi, x_hi) + _dot_nt(w_hi, x_lo) + _dot_nt(w_lo, x_hi))[:ROUTE_ROWS] + bias[:ROUTE_ROWS, 0:1]
    tm = logits.shape[1]
    row = lax.broadcasted_iota(I32, (ROUTE_ROWS, tm), 0)
    neg = jnp.float32(-jnp.inf)
    big = jnp.int32(LANES)

    def top(vals):
        m = jnp.max(vals, axis=0, keepdims=True)
        idx = jnp.min(jnp.where(vals == m, row, big), axis=0, keepdims=True)
        return m, idx

    is_g = row < MOE_GROUPS
    gl = jnp.where(is_g, logits, neg)
    gmax, gidx = top(gl)
    p_top = 1.0 / jnp.sum(jnp.where(is_g, jnp.exp(gl - gmax), 0.0), axis=0, keepdims=True)
    erow = row - ROUTE_EXPERT_LANE0
    in_grp = jnp.logical_and(jnp.logical_and(erow >= 0, erow < MOE_EXPERTS), (erow // MOE_PER_GROUP) == gidx)
    es = jnp.where(in_grp, logits, neg)
    m1, i1 = top(es)
    m2, i2 = top(jnp.where(row == i1, neg, es))
    e21 = jnp.exp(m2 - m1)
    w1 = p_top / (1.0 + e21)
    w2 = p_top * e21 / (1.0 + e21)

    a1 = (row == i1).astype(F32)
    a2 = (row == i2).astype(F32)
    both = a1 + a2
    before = _dot(both.astype(BF16), tri) - both + carry[:, 0:1]
    r1 = jnp.sum(a1 * before, axis=0, keepdims=True).astype(I32)
    r2 = jnp.sum(a2 * before, axis=0, keepdims=True).astype(I32)
    carry[...] = carry[...] + jnp.sum(both, axis=1, keepdims=True)
    row8 = lax.broadcasted_iota(I32, (8, tm), 0)
    mi = jnp.where(row8 == 0, i1 - ROUTE_EXPERT_LANE0,
                   jnp.where(row8 == 1, i2 - ROUTE_EXPERT_LANE0, jnp.where(row8 == 2, r1, jnp.where(row8 == 3, r2, 0))))
    rowl = lax.broadcasted_iota(I32, (LANES, tm), 0)
    wt = jnp.where(rowl == 0, w1, jnp.where(rowl == 1, w2, 0.0))
    mf = jnp.concatenate([wt[:, j * LANES:(j + 1) * LANES].T for j in range(tm // LANES)], axis=0)
    return mi, mf


def _dest_body(mi_ref, off_ref, da_ref, db_ref):
    tm = da_ref.shape[2]
    mi = mi_ref[...]
    n = mi.shape[1]
    row = lax.broadcasted_iota(I32, (MOE_EXPERTS, n), 0)
    off = off_ref[:, 0:1]
    d0 = jnp.sum(jnp.where(row == mi[0:1], off, 0), axis=0, keepdims=True) + mi[2:3]
    d1 = jnp.sum(jnp.where(row == mi[1:2], off, 0), axis=0, keepdims=True) + mi[3:4]
    for s in range(da_ref.shape[0]):
        da_ref[s] = d0[:, s * tm:(s + 1) * tm]
        db_ref[s] = d1[:, s * tm:(s + 1) * tm]


def _dest_rows(mi, offsets_col, tm):
    t = mi.shape[1]
    tiles_per_step = math.gcd(8, t // tm)
    out = pl.BlockSpec((tiles_per_step, 1, tm), lambda i: (i, 0, 0))
    return pl.pallas_call(
        _dest_body,
        grid=(t // (tm * tiles_per_step),),
        in_specs=[pl.BlockSpec((8, tm * tiles_per_step), lambda i: (0, i)),
                  pl.BlockSpec((MOE_EXPERTS, LANES), lambda i: (0, 0))],
        out_specs=[out, out],
        out_shape=[jax.ShapeDtypeStruct((t // tm, 1, tm), I32)] * 2,
        compiler_params=_cparams("parallel"),
        name="moe_dest",
    )(mi, offsets_col)


def _dispatch_body(zs_ref, na_ref, da_ref, db_ref, xn_ref, xs_ref, zbuf, sem, zsem):
    tm = xn_ref.shape[0]
    n_tiles = xs_ref.shape[0] // tm

    @pl.when(pl.program_id(0) == 0)
    def _():
        zbuf[...] = jnp.zeros_like(zbuf)

        def fill(row0):
            return pltpu.make_async_copy(zbuf, xs_ref.at[pl.ds(pl.multiple_of(row0, tm), tm), :], zsem)

        def tail(e, c):
            @pl.when(zs_ref[e] >= 0)
            def _():
                fill(zs_ref[e]).start()
            return c

        def unused(j, c):
            fill(j * tm).start()
            return c

        def drain(j, c):
            fill(0).wait()
            return c

        lax.fori_loop(0, MOE_EXPERTS, tail, 0)
        lax.fori_loop(na_ref[0], n_tiles, unused, 0)
        lax.fori_loop(0, na_ref[1] + n_tiles - na_ref[0], drain, 0)

    def start(r, c):
        for k, d_ref in enumerate((da_ref, db_ref)):
            pltpu.make_async_copy(xn_ref.at[pl.ds(r, 1), :], xs_ref.at[pl.ds(d_ref[0, r], 1), :],
                                  sem.at[k]).start(priority=k)
        return c

    lax.fori_loop(0, tm, start, 0, unroll=8)
    for k in range(MOE_TOPK):
        pltpu.make_async_copy(xn_ref, xs_ref.at[pl.ds(0, tm), :], sem.at[k]).wait()


def _dispatch(zero_start, n_active, dest_a, dest_b, xn, n_rows, tm):
    t, dw = xn.shape
    smem_row = pl.BlockSpec((None, 1, tm), lambda i, zs, na: (i, 0, 0), memory_space=pltpu.SMEM)
    grid_spec = pltpu.PrefetchScalarGridSpec(
        num_scalar_prefetch=2,
        grid=(t // tm,),
        in_specs=[smem_row, smem_row, pl.BlockSpec((tm, dw), lambda i, zs, na: (i, 0))],
        out_specs=pl.BlockSpec(memory_space=pl.ANY),
        scratch_shapes=[pltpu.VMEM((tm, dw), xn.dtype), pltpu.SemaphoreType.DMA((MOE_TOPK,)),
                        pltpu.SemaphoreType.DMA(())],
    )
    return pl.pallas_call(
        _dispatch_body,
        grid_spec=grid_spec,
        out_shape=jax.ShapeDtypeStruct((n_rows, dw), xn.dtype),
        compiler_params=_cparams("arbitrary"),
        name="moe_dispatch",
    )(zero_start, n_active, dest_a, dest_b, xn)


def _experts_body(te_ref, na_ref, x_ref, g_ref, wg_ref, wu_ref, wd_ref, y_ref, wg_b, wu_b, wd_b):
    i = pl.program_id(0)

    @pl.when(i < na_ref[0])
    def _():
        @pl.when(jnp.logical_or(i == 0, te_ref[i] != te_ref[jnp.maximum(i - 1, 0)]))
        def _():
            wg_b[...] = wg_ref[...].astype(BF16)
            wu_b[...] = wu_ref[...].astype(BF16)
            wd_b[...] = wd_ref[...].astype(BF16)

        x = _rms(x_ref[...], g_ref[...]).astype(BF16)
        gt = _dot(x, wg_b[...])
        up = _dot(x, wu_b[...])
        hid = (gt * jax.nn.sigmoid(gt) * up).astype(BF16)
        y_ref[...] = _dot(hid, wd_b[...])

    @pl.when(i >= na_ref[0])
    def _():
        y_ref[...] = jnp.zeros_like(y_ref)


def _experts(tile_expert, n_active, xs, norm_w, w_gate, w_up, w_down, tm):
    r, d = xs.shape
    f = w_gate.shape[2]
    row_tile = lambda i, te, na: (jnp.minimum(i, na[0] - 1), 0)
    grid_spec = pltpu.PrefetchScalarGridSpec(
        num_scalar_prefetch=2,
        grid=(r // tm,),
        in_specs=[pl.BlockSpec((tm, d), row_tile),
                  pl.BlockSpec((1, d), lambda i, te, na: (0, 0)),
                  pl.BlockSpec((None, d, f), lambda i, te, na: (te[i], 0, 0)),
                  pl.BlockSpec((None, d, f), lambda i, te, na: (te[i], 0, 0)),
                  pl.BlockSpec((None, f, d), lambda i, te, na: (te[i], 0, 0))],
        out_specs=pl.BlockSpec((tm, d), lambda i, te, na: (i, 0)),
        scratch_shapes=[pltpu.VMEM((d, f), BF16), pltpu.VMEM((d, f), BF16), pltpu.VMEM((f, d), BF16)],
    )
    return pl.pallas_call(
        _experts_body,
        grid_spec=grid_spec,
        out_shape=jax.ShapeDtypeStruct((r, d), xs.dtype),
        compiler_params=_cparams("arbitrary"),
        name="moe_experts",
    )(tile_expert, n_active, xs, norm_w, w_gate, w_up, w_down)


def _combine_body(da_ref, db_ref, na_ref, nb_ref, x_ref, mf_ref, g_ref, ys_ref, o_ref, buf, sem):
    tm = x_ref.shape[0]
    i = pl.program_id(0)
    slot = lax.rem(i, 2)

    def gather(d_refs, s):
        def start(r, c):
            for k, d_ref in enumerate(d_refs):
                pltpu.make_async_copy(ys_ref.at[pl.ds(d_ref[0, r], 1), :],
                                      buf.at[s, k, pl.ds(r, 1), :], sem.at[s, k]).start(priority=k)
            return c

        lax.fori_loop(0, tm, start, 0, unroll=8)

    @pl.when(i == 0)
    def _():
        gather((da_ref, db_ref), 0)

    @pl.when(i + 1 < pl.num_programs(0))
    def _():
        gather((na_ref, nb_ref), 1 - slot)

    for k in range(MOE_TOPK):
        pltpu.make_async_copy(ys_ref.at[pl.ds(0, tm), :], buf.at[slot, k], sem.at[slot, k]).wait()
    mf = mf_ref[...]
    y = x_ref[...] + mf[:, 0:1] * buf[slot, 0] + mf[:, 1:2] * buf[slot, 1]
    o_ref[...] = _rms(y, g_ref[...])


def _combine(dest_a, dest_b, x2d, mf, norm_w, ys, tm):
    t, d = x2d.shape
    last = t // tm - 1
    cur = pl.BlockSpec((None, 1, tm), lambda i: (i, 0, 0), memory_space=pltpu.SMEM)
    nxt = pl.BlockSpec((None, 1, tm), lambda i: (jnp.minimum(i + 1, last), 0, 0), memory_space=pltpu.SMEM)
    return pl.pallas_call(
        _combine_body,
        grid=(t // tm,),
        in_specs=[cur, cur, nxt, nxt,
                  pl.BlockSpec((tm, d), lambda i: (i, 0)),
                  pl.BlockSpec((tm, LANES), lambda i: (i, 0)),
                  pl.BlockSpec((1, d), lambda i: (0, 0)),
                  pl.BlockSpec(memory_space=pl.ANY)],
        out_specs=pl.BlockSpec((tm, d), lambda i: (i, 0)),
        out_shape=jax.ShapeDtypeStruct((t, d), F32),
        scratch_shapes=[pltpu.VMEM((2, MOE_TOPK, tm, ys.shape[1]), ys.dtype), pltpu.SemaphoreType.DMA((2, MOE_TOPK))],
        compiler_params=_cparams("arbitrary"),
        name="moe_combine",
    )(dest_a, dest_b, dest_a, dest_b, x2d, mf, norm_w, ys)


MOE_ROW_TILE = 256


def _moe(x2d, mi, mf, cnt, norm_w, w_gate, w_up, w_down, norm_final):
    t, d = x2d.shape
    tm = MOE_ROW_TILE
    counts = cnt[ROUTE_EXPERT_LANE0:ROUTE_EXPERT_LANE0 + MOE_EXPERTS, 0].astype(I32)
    padded = ((counts + tm - 1) // tm) * tm
    ends = jnp.cumsum(padded)
    offsets = ends - padded
    n_tiles = (MOE_TOPK * t + MOE_EXPERTS * (tm - 1)) // tm
    tile_start = jnp.arange(n_tiles, dtype=I32) * tm
    tile_expert = jnp.minimum(jnp.sum((ends[None, :] <= tile_start[:, None]).astype(I32), axis=1), MOE_EXPERTS - 1)
    n_info = jnp.stack([ends[-1] // tm, jnp.sum((counts > 0).astype(I32))]).astype(I32)
    zero_start = jnp.where(counts > 0, ends - tm, -1).astype(I32)
    offsets_col = jnp.broadcast_to(offsets[:, None], (MOE_EXPERTS, LANES))
    dest_a, dest_b = _dest_rows(mi, offsets_col, tm)
    xs = _dispatch(zero_start, n_info, dest_a, dest_b, x2d, n_tiles * tm, tm)
    ys = _experts(tile_expert, n_info, xs, norm_w, w_gate, w_up, w_down, tm)
    return _combine(dest_a, dest_b, x2d, mf, norm_final, ys, tm)


def kernel(x, mem, norm_mix, w_in, w_out,
           s5_lam_re_f, s5_lam_im_f, s5_log_step_f, s5_b_re_f, s5_b_im_f, s5_c_re_f, s5_c_im_f,
           s5_lam_re_b, s5_lam_im_b, s5_log_step_b, s5_b_re_b, s5_b_im_b, s5_c_re_b, s5_c_im_b,
           s5_d, s5_w_glu, s5_b_glu, s5_norm,
           gdn_conv, gdn_a_log_f, gdn_dt_bias_f, gdn_a_log_b, gdn_dt_bias_b, gdn_norm,
           norm_xattn, norm_mem, xa_wq, xa_wk, xa_wv, xa_wo,
           norm_moe, router_group_w, router_group_b, router_expert_w, router_expert_b,
           moe_w_gate, moe_w_up, moe_w_down, norm_final):
    bsz, seq, d = x.shape
    t = bsz * seq
    l = 0
    x2d = x.reshape(t, d)
    wi = w_in[l]
    wg = wi[:, S5_WIDTH + 4 * GDN_WIDTH:].reshape(d, 4, GDN_HEADS)
    wg = jnp.pad(jnp.swapaxes(wg, 1, 2), ((0, 0), (0, 0), (0, 4))).reshape(d, GDN_HEADS * 8)
    wgt = wg.T
    u3, qkvz, gates = _in_proj(x2d, norm_mix[l][None], wi, wgt)
    s5p = dict(lam_re_f=s5_lam_re_f[l], lam_im_f=s5_lam_im_f[l], log_step_f=s5_log_step_f[l],
               b_re_f=s5_b_re_f[l], b_im_f=s5_b_im_f[l], c_re_f=s5_c_re_f[l], c_im_f=s5_c_im_f[l],
               lam_re_b=s5_lam_re_b[l], lam_im_b=s5_lam_im_b[l], log_step_b=s5_log_step_b[l],
               b_re_b=s5_b_re_b[l], b_im_b=s5_b_im_b[l], c_re_b=s5_c_re_b[l], c_im_b=s5_c_im_b[l], d=s5_d[l])
    g3 = _s5_scan(u3, s5p, seq // S5_CHUNK)
    y_s5 = _s5_post(g3, s5_w_glu[l].T.astype(BF16), s5_b_glu[l][:, None], s5_norm[l][:, None])

    head_par = jnp.stack([gdn_a_log_f[l], gdn_dt_bias_f[l], gdn_a_log_b[l], gdn_dt_bias_b[l]], axis=1)
    head_par = jnp.broadcast_to(jnp.pad(head_par, ((0, 0), (0, 4)))[:, :, None], (GDN_HEADS, 8, LANES))
    y_gdn = _gdn(qkvz, gates, gdn_conv[l], head_par, gdn_norm[l][None], bsz, seq)

    mem_len = mem.shape[1]
    kmem, vmem = _mem_kv(mem.reshape(bsz * mem_len, d), norm_mem[l][None],
                         xa_wk[l].astype(BF16), xa_wv[l].astype(BF16))
    n_pad = LANES - MOE_GROUPS - MOE_EXPERTS
    w_route = jnp.pad(jnp.concatenate([router_group_w[l], router_expert_w[l]], axis=1), ((0, 0), (0, n_pad))).T
    b_route = jnp.pad(jnp.concatenate([router_group_b[l], router_expert_b[l]]), (0, n_pad))
    b_route = jnp.broadcast_to(b_route[:, None], (LANES, LANES))
    x2, mi, mf, cnt = _mix_xattn(x2d, y_s5, y_gdn, w_out[l], norm_xattn[l][None], xa_wq[l], kmem, vmem, xa_wo[l],
                                 norm_moe[l][None], w_route, b_route, seq, mem_len)
    y = _moe(x2, mi, mf, cnt, norm_moe[l][None], moe_w_gate[l], moe_w_up[l], moe_w_down[l], norm_final[None])
    return y.reshape(bsz, seq, d)
```

```python
import functools
import math

import jax
import jax.numpy as jnp
from jax import lax
from jax.experimental import pallas as pl
from jax.experimental.pallas import tpu as pltpu

F32 = jnp.float32
BF16 = jnp.bfloat16
I32 = jnp.int32

D_MODEL = 1024
S5_WIDTH = 512
S5_GROUP = 16
S5_GROUPS = 32
S5_STATE = 64
S5_CHUNK = 128
GDN_HEADS = 4
GDN_HEAD_DIM = 128
GDN_WIDTH = 512
GDN_CONV = 5
GDN_CHUNK = 64
XA_HEADS = 4
XA_HEAD_DIM = 256
MOE_GROUPS = 4
MOE_PER_GROUP = 8
MOE_EXPERTS = 32
MOE_TOPK = 2
D_EXPERT = 256
RMS_EPS = 1e-6
L2_EPS = 1e-6
LANES = 128
VMEM_LIMIT = 56 * 1024 * 1024


def _cparams(*sem):
    return pltpu.CompilerParams(dimension_semantics=tuple(sem), vmem_limit_bytes=VMEM_LIMIT)


def _rms(x, gain):
    return x * lax.rsqrt(jnp.mean(x * x, axis=-1, keepdims=True) + RMS_EPS) * gain


def _dot(a, b):
    return jnp.dot(a, b, preferred_element_type=F32)


def _dot_nt(a, b):
    return lax.dot_general(a, b, (((1,), (1,)), ((), ())), preferred_element_type=F32)


def _dot_tn(a, b):
    return lax.dot_general(a, b, (((0,), (0,)), ((), ())), preferred_element_type=F32)


def _in_proj_body(x_ref, g_ref, w_ref, wgt_ref, u_ref, qkvz_ref, gates_ref, wut_b, wqkvz_b, wgt_b):
    @pl.when(pl.program_id(0) == 0)
    def _():
        wut_b[...] = w_ref[:, :S5_WIDTH].T.astype(BF16)
        wqkvz_b[...] = w_ref[:, S5_WIDTH:S5_WIDTH + 4 * GDN_WIDTH].astype(BF16)
        wgt_b[...] = wgt_ref[...].astype(BF16)

    h = _rms(x_ref[...], g_ref[...]).astype(BF16)
    ut = _dot_nt(wut_b[...], h)
    gt = _dot_nt(wgt_b[...], h)
    for j in range(u_ref.shape[0]):
        u_ref[j] = ut[:, j * S5_CHUNK:(j + 1) * S5_CHUNK]
        gates_ref[j] = gt[:, j * S5_CHUNK:(j + 1) * S5_CHUNK]
    qkvz_ref[...] = _dot(h, wqkvz_b[...]).astype(BF16)


def _in_proj(x2d, norm_w, w_in, wgt, tm=512):
    t = x2d.shape[0]
    nck = tm // S5_CHUNK
    nqkvz = 4 * GDN_WIDTH
    full = lambda shape: pl.BlockSpec(shape, lambda i: (0,) * len(shape))
    return pl.pallas_call(
        _in_proj_body,
        grid=(t // tm,),
        in_specs=[pl.BlockSpec((tm, D_MODEL), lambda i: (i, 0)),
                  full((1, D_MODEL)), full(w_in.shape), full(wgt.shape)],
        out_specs=[pl.BlockSpec((nck, S5_WIDTH, S5_CHUNK), lambda i: (i, 0, 0)),
                   pl.BlockSpec((tm, nqkvz), lambda i: (i, 0)),
                   pl.BlockSpec((nck, wgt.shape[0], S5_CHUNK), lambda i: (i, 0, 0))],
        out_shape=[jax.ShapeDtypeStruct((t // S5_CHUNK, S5_WIDTH, S5_CHUNK), F32),
                   jax.ShapeDtypeStruct((t, nqkvz), BF16),
                   jax.ShapeDtypeStruct((t // S5_CHUNK, wgt.shape[0], S5_CHUNK), F32)],
        scratch_shapes=[pltpu.VMEM((S5_WIDTH, D_MODEL), BF16), pltpu.VMEM((D_MODEL, nqkvz), BF16),
                        pltpu.VMEM(wgt.shape, BF16)],
        compiler_params=_cparams("arbitrary"),
        name="in_proj",
    )(x2d, norm_w, w_in, wgt)


def _cmul(ar, ai, br, bi):
    return ar * br - ai * bi, ar * bi + ai * br


def _cpow_int(lr, li, expo, nbits):
    res_r = jnp.ones(jnp.broadcast_shapes(lr.shape, expo.shape), F32)
    res_i = jnp.zeros_like(res_r)
    for b in range(nbits):
        bit = ((expo >> b) & 1) == 1
        nr, ni = _cmul(res_r, res_i, lr, li)
        res_r = jnp.where(bit, nr, res_r)
        res_i = jnp.where(bit, ni, res_i)
        if b + 1 < nbits:
            lr, li = _cmul(lr, li, lr, li)
    return res_r, res_i


def _lam_bar(re, im, step):
    er = jnp.exp(step * re)
    return er * jnp.cos(step * im), er * jnp.sin(step * im)


def _zoh_coef(re, im, lr, li):
    den = re * re + im * im
    return ((lr - 1.0) * re + li * im) / den, (li * re - (lr - 1.0) * im) / den


def _s5_body(u_ref, lrow_ref, lcol_ref, l256_ref, b_ref, bt_ref, c_ref, ct_ref, d_ref,
             o_ref, vf_ref, vb_ref, m_ref, win_ref, wout_ref, sf_ref, sb_ref, hf_ref, hb_ref, uc_ref, y_ref, *, nchunk):
    L = S5_CHUNK
    P = S5_STATE
    n_rows = u_ref.shape[0]
    nb = n_rows // nchunk
    lane_i = lax.broadcasted_iota(I32, (1, L), 1)

    lcol = lcol_ref[...]
    lbc_r, lbc_i = _lam_bar(lcol[:, 0:2], lcol[:, 2:4], lcol[:, 4:6])
    kc_r, kc_i = _zoh_coef(lcol[:, 0:2], lcol[:, 2:4], lbc_r, lbc_i)
    lrow = lrow_ref[...]
    lbr_r, lbr_i = _lam_bar(lrow[0:2], lrow[2:4], lrow[4:6])
    kr_r, kr_i = _zoh_coef(lrow[0:2], lrow[2:4], lbr_r, lbr_i)
    lf_r, lf_i, lb_r, lb_i = lbc_r[:, 0:1], lbc_i[:, 0:1], lbc_r[:, 1:2], lbc_i[:, 1:2]
    kfr_c, kfi_c, kbr_c, kbi_c = kc_r[:, 0:1], kc_i[:, 0:1], kc_r[:, 1:2], kc_i[:, 1:2]
    kfr_r, kfi_r, kbr_r, kbi_r = kr_r[0:1], kr_i[0:1], kr_r[1:2], kr_i[1:2]

    pwf_r, pwf_i = _cpow_int(lf_r, lf_i, lane_i, 7)
    rvf_r, rvf_i = _cpow_int(lf_r, lf_i, (L - 1) - lane_i, 7)
    pwb_r, pwb_i = _cpow_int(lb_r, lb_i, lane_i, 7)
    rvb_r, rvb_i = _cpow_int(lb_r, lb_i, L - lane_i, 8)
    nxf_r, nxf_i = _cmul(pwf_r, pwf_i, lf_r, lf_i)

    bf_r = kfr_c * b_ref[0] - kfi_c * b_ref[1]
    bf_i = kfr_c * b_ref[1] + kfi_c * b_ref[0]
    bb_r = kbr_c * b_ref[2] - kbi_c * b_ref[3]
    bb_i = kbr_c * b_ref[3] + kbi_c * b_ref[2]
    btf_r = kfr_r * bt_ref[0] - kfi_r * bt_ref[1]
    btf_i = kfr_r * bt_ref[1] + kfi_r * bt_ref[0]
    btb_r = kbr_r * bt_ref[2] - kbi_r * bt_ref[3]
    btb_i = kbr_r * bt_ref[3] + kbi_r * bt_ref[2]

    def taps(c_r, c_i, bt_r, bt_i, pw_r, pw_i):
        cb_r = (bt_r[:, None, :] * c_r[None, :, :] - bt_i[:, None, :] * c_i[None, :, :]).reshape(256, P)
        cb_i = (bt_r[:, None, :] * c_i[None, :, :] + bt_i[:, None, :] * c_r[None, :, :]).reshape(256, P)
        k = (jnp.dot(cb_r, pw_r, preferred_element_type=F32, precision=lax.Precision.HIGHEST)
             - jnp.dot(cb_i, pw_i, preferred_element_type=F32, precision=lax.Precision.HIGHEST))
        return k, jnp.sum(cb_r, axis=1, keepdims=True)

    kf, _ = taps(c_ref[0], c_ref[1], btf_r, btf_i, pwf_r, pwf_i)
    kb, kb0 = taps(c_ref[2], c_ref[3], btb_r, btb_i, rvb_r, rvb_i)
    is0 = lane_i == 0
    vf_ref[...] = kf + jnp.where(is0, kb0, 0.0)
    vb_ref[...] = jnp.where(is0, 0.0, kb)

    row_i = lax.broadcasted_iota(I32, (L, L), 0)
    col_i = lax.broadcasted_iota(I32, (L, L), 1)
    fwd_lane = col_i + row_i < L

    def build_ci(ci, carry):
        for co in range(S5_GROUP):
            r = ci * S5_GROUP + co
            taps_rows = jnp.where(fwd_lane, jnp.broadcast_to(vf_ref[pl.ds(r, 1), :], (L, L)),
                                  jnp.broadcast_to(vb_ref[pl.ds(r, 1), :], (L, L)))
            m_ref[pl.ds(pl.multiple_of(ci * L, L), L), co * L:(co + 1) * L] = pltpu.roll(
                taps_rows, 0, 1, stride=1, stride_axis=0).astype(BF16)
        return carry

    lax.fori_loop(0, S5_GROUP, build_ci, 0)

    for ci in range(S5_GROUP):
        sl = slice(ci * L, (ci + 1) * L)
        br, bi = bf_r[:, ci:ci + 1], bf_i[:, ci:ci + 1]
        win_ref[0 * P:1 * P, sl] = (rvf_r * br - rvf_i * bi).astype(BF16)
        win_ref[1 * P:2 * P, sl] = (rvf_r * bi + rvf_i * br).astype(BF16)
        br, bi = bb_r[:, ci:ci + 1], bb_i[:, ci:ci + 1]
        win_ref[2 * P:3 * P, sl] = (pwb_r * br - pwb_i * bi).astype(BF16)
        win_ref[3 * P:4 * P, sl] = (pwb_r * bi + pwb_i * br).astype(BF16)
    for co in range(S5_GROUP):
        sl = slice(co * L, (co + 1) * L)
        cr, ci_ = ct_ref[0][:, co:co + 1], ct_ref[1][:, co:co + 1]
        wout_ref[0 * P:1 * P, sl] = (cr * nxf_r - ci_ * nxf_i).astype(BF16)
        wout_ref[1 * P:2 * P, sl] = (-(cr * nxf_i + ci_ * nxf_r)).astype(BF16)
        cr, ci_ = ct_ref[2][:, co:co + 1], ct_ref[3][:, co:co + 1]
        wout_ref[2 * P:3 * P, sl] = (cr * rvb_r - ci_ * rvb_i).astype(BF16)
        wout_ref[3 * P:4 * P, sl] = (-(cr * rvb_i + ci_ * rvb_r)).astype(BF16)

    for ci in range(S5_GROUP):
        uc_ref[ci] = u_ref[:, ci, :]
    ucat = jnp.concatenate([uc_ref[ci].astype(BF16) for ci in range(S5_GROUP)], axis=1)

    summ = _dot_nt(ucat, win_ref[...])
    sf_ref[...] = summ[:, :2 * P]
    sb_ref[...] = summ[:, 2 * P:]
    nblk = 2 * L
    for j in range(S5_GROUP * L // nblk):
        y_ref[:, j * nblk:(j + 1) * nblk] = _dot(ucat, m_ref[:, j * nblk:(j + 1) * nblk])
    l256 = l256_ref[...]
    a_mul, a_im = _lam_bar(l256[0:1], l256[1:2], l256[2:3])
    for _ in range(7):
        a_mul, a_im = _cmul(a_mul, a_im, a_mul, a_im)
    lane256 = lax.broadcasted_iota(I32, (1, 4 * P), 1)
    b_mul = jnp.where((lane256 // P) % 2 == 0, -a_im, a_im)

    hf = jnp.zeros((nb, 2 * P), F32)
    hb = jnp.zeros((nb, 2 * P), F32)
    for c in range(nchunk):
        cr = nchunk - 1 - c
        rows_f = pl.ds(c, nb, stride=nchunk)
        rows_b = pl.ds(cr, nb, stride=nchunk)
        hf_ref[rows_f, :] = hf
        hb_ref[rows_b, :] = hb
        hf = a_mul[:, :2 * P] * hf + b_mul[:, :2 * P] * pltpu.roll(hf, P, 1) + sf_ref[rows_f, :]
        hb = a_mul[:, 2 * P:] * hb + b_mul[:, 2 * P:] * pltpu.roll(hb, P, 1) + sb_ref[rows_b, :]
    hprev = jnp.concatenate([hf_ref[...], hb_ref[...]], axis=1).astype(BF16)

    for j in range(S5_GROUP * L // nblk):
        y = y_ref[:, j * nblk:(j + 1) * nblk] + _dot(hprev, wout_ref[:, j * nblk:(j + 1) * nblk])
        for q in range(nblk // L):
            co = j * (nblk // L) + q
            yc = y[:, q * L:(q + 1) * L] + d_ref[co:co + 1, :] * uc_ref[co]
            o_ref[:, co, :] = 0.5 * yc * (1.0 + lax.erf(yc * (2.0 ** -0.5)))


def _s5_scan(u3, p, nchunk):
    n = u3.shape[0]
    g, grp, st, L = S5_GROUPS, S5_GROUP, S5_STATE, S5_CHUNK
    step_f = jnp.exp(p["log_step_f"])[:, None] * jnp.ones((1, st), F32)
    step_b = jnp.exp(p["log_step_b"])[:, None] * jnp.ones((1, st), F32)
    zeros = jnp.zeros((g, st), F32)
    lrow = jnp.stack([p["lam_re_f"], p["lam_re_b"], p["lam_im_f"], p["lam_im_b"], step_f, step_b, zeros, zeros], axis=1)
    lcol = jnp.swapaxes(lrow, 1, 2)
    cat4 = lambda f, b: jnp.concatenate([f, f, b, b], axis=1)
    z256 = jnp.zeros((g, 4 * st), F32)
    l256 = jnp.stack([cat4(p["lam_re_f"], p["lam_re_b"]), cat4(p["lam_im_f"], p["lam_im_b"]),
                      cat4(step_f, step_b)] + [z256] * 5, axis=1)
    b4 = jnp.stack([p["b_re_f"], p["b_im_f"], p["b_re_b"], p["b_im_b"]], axis=1)
    bt4 = jnp.swapaxes(b4, 2, 3)
    c4 = jnp.stack([p["c_re_f"], p["c_im_f"], p["c_re_b"], p["c_im_b"]], axis=1)
    ct4 = jnp.swapaxes(c4, 2, 3)
    dbc = jnp.broadcast_to(p["d"].reshape(g, grp, 1), (g, grp, L))
    per_g = lambda *shape: pl.BlockSpec((None,) + shape, lambda i: (i,) + (0,) * len(shape))
    return pl.pallas_call(
        functools.partial(_s5_body, nchunk=nchunk),
        grid=(g,),
        in_specs=[pl.BlockSpec((n, grp, L), lambda i: (0, i, 0)),
                  per_g(8, st), per_g(st, 8), per_g(8, 4 * st), per_g(4, st, grp), per_g(4, grp, st),
                  per_g(4, grp, st), per_g(4, st, grp), per_g(grp, L)],
        out_specs=pl.BlockSpec((n, grp, L), lambda i: (0, i, 0)),
        out_shape=jax.ShapeDtypeStruct(u3.shape, F32),
        scratch_shapes=[pltpu.VMEM((grp * grp, L), F32), pltpu.VMEM((grp * grp, L), F32),
                        pltpu.VMEM((grp * L, grp * L), BF16),
                        pltpu.VMEM((4 * st, grp * L), BF16), pltpu.VMEM((4 * st, grp * L), BF16),
                        pltpu.VMEM((n, 2 * st), F32), pltpu.VMEM((n, 2 * st), F32),
                        pltpu.VMEM((n, 2 * st), F32), pltpu.VMEM((n, 2 * st), F32),
                        pltpu.VMEM((grp, n, L), F32), pltpu.VMEM((n, grp * L), F32)],
        compiler_params=_cparams("parallel"),
        name="s5_scan",
    )(u3, lrow, lcol, l256, b4, bt4, c4, ct4, dbc)


def _s5_post_body(g_ref, wt_ref, b_ref, nw_ref, o_ref):
    for j in range(g_ref.shape[0]):
        g = g_ref[j]
        z = _dot(wt_ref[...], g.astype(BF16)) + b_ref[...]
        y = g * jax.nn.sigmoid(z)
        y = y * lax.rsqrt(jnp.mean(y * y, axis=0, keepdims=True) + RMS_EPS) * nw_ref[...]
        o_ref[j * S5_CHUNK:(j + 1) * S5_CHUNK, :] = y.T.astype(BF16)


def _s5_post(g3, w_glu_t, b_glu_col, norm_col, nck=4):
    n = g3.shape[0]
    full = lambda shape: pl.BlockSpec(shape, lambda i: (0,) * len(shape))
    return pl.pallas_call(
        _s5_post_body,
        grid=(n // nck,),
        in_specs=[pl.BlockSpec((nck, S5_WIDTH, S5_CHUNK), lambda i: (i, 0, 0)),
                  full(w_glu_t.shape), full(b_glu_col.shape), full(norm_col.shape)],
        out_specs=pl.BlockSpec((nck * S5_CHUNK, S5_WIDTH), lambda i: (i, 0)),
        out_shape=jax.ShapeDtypeStruct((n * S5_CHUNK, S5_WIDTH), BF16),
        compiler_params=_cparams("parallel"),
        name="s5_post",
    )(g3, w_glu_t, b_glu_col, norm_col)


GDN_BLOCK = 128
GDN_PAIR = 2
GDN_PREP_BATCH = 8


def _packed_tri_inverse(lps, low, upp, bd16, rings):
    def pk(xs, ys):
        outs = []
        for a, b in zip(xs, ys):
            lhs = jnp.concatenate([jnp.where(low, a, 0.0), jnp.where(upp, a, 0.0)], axis=1).astype(BF16)
            rhs = jnp.concatenate([jnp.where(low, b, 0.0), jnp.where(upp, b, 0.0)], axis=0).astype(BF16)
            outs.append(_dot(lhs, rhs))
        return outs

    d = [jnp.where(bd16, lp, 0.0) for lp in lps]
    d2 = pk(d, d)
    d4 = pk(d2, d2)
    d8 = pk(d4, d4)
    a = [y - x - p for x, y, p in zip(d, d2, pk(d, d2))]
    a = [x + y + p for x, y, p in zip(a, d4, pk(a, d4))]
    a = [x + y + p for x, y, p in zip(a, d8, pk(a, d8))]
    for ring in rings:
        n = [jnp.where(ring, lp, 0.0) for lp in lps]
        t = [x + p for x, p in zip(n, pk(a, n))]
        a = [x - y - p for x, y, p in zip(a, t, pk(t, a))]
    return a


def _gdn_body(q_ref, k_ref, v_ref, z_ref, wq_ref, wk_ref, wv_ref, g_ref, hp_ref, nw_ref, o_ref,
              qs, ks, vs, os_, sg, cf, cb, uf, ub, wqf, wqb, qkf, qkb, kdtf, kdtb, eglf, eglb, xpad):
    seq = q_ref.shape[0]
    C = GDN_BLOCK
    hd = GDN_HEAD_DIM
    nck = seq // C
    heads = range(GDN_PAIR)

    pad = 8
    half = (GDN_CONV - 1) // 2
    xpad[0:pad, :] = jnp.zeros((pad, LANES), F32)
    xpad[pad + seq:2 * pad + seq, :] = jnp.zeros((pad, LANES), F32)

    def conv_silu(x_ref, w_ref, j):
        cols = pl.ds(pl.multiple_of(j * hd, hd), hd)
        xpad[pad:pad + seq, :] = x_ref[:, cols].astype(F32)
        w = w_ref[:, cols]
        acc = xpad[pad - half:pad - half + seq, :] * w[0:1]
        for tap in range(1, GDN_CONV):
            acc = acc + xpad[pad - half + tap:pad - half + tap + seq, :] * w[tap:tap + 1]
        return acc * jax.nn.sigmoid(acc)

    def l2n(x):
        return x * lax.rsqrt(jnp.sum(x * x, axis=-1, keepdims=True) + L2_EPS)

    def softplus(x):
        return jnp.maximum(x, 0.0) + jnp.log1p(jnp.exp(-jnp.abs(x)))

    lane = lax.broadcasted_iota(I32, (1, C), 1)

    def prologue(j, carry):
        qs[j] = l2n(conv_silu(q_ref, wq_ref, j)) * (hd ** -0.5)
        ks[j] = l2n(conv_silu(k_ref, wk_ref, j))
        vs[j] = conv_silu(v_ref, wv_ref, j)
        os_[j] = jnp.zeros((seq, hd), F32)
        g = g_ref[:, pl.ds(pl.multiple_of(8 * j, 8), 8), :].reshape(nck * 8, C)
        hp = hp_ref[j]
        sg[j] = jax.nn.sigmoid(g)
        gl_f = -jnp.exp(hp[0:1]) * softplus(g + hp[1:2])
        gl_b = -jnp.exp(hp[2:3]) * softplus(g + hp[3:4])
        sh = 1
        while sh < C:
            gl_f = gl_f + jnp.where(lane >= sh, pltpu.roll(gl_f, sh, 1), 0.0)
            gl_b = gl_b + jnp.where(lane < C - sh, pltpu.roll(gl_b, C - sh, 1), 0.0)
            sh *= 2
        cf[j] = gl_f
        cb[j] = gl_b
        return carry

    lax.fori_loop(0, GDN_PAIR, prologue, 0)

    ri = lax.broadcasted_iota(I32, (C, C), 0)
    ci = lax.broadcasted_iota(I32, (C, C), 1)
    low, upp = ri > ci, ri < ci
    low_i, upp_i = ri >= ci, ri <= ci
    same = lambda w: (ri // w) == (ci // w)
    bd16 = same(16)
    rings = []
    w = 32
    while w <= C:
        rings.append(jnp.logical_and(same(w), jnp.logical_not(same(w // 2))))
        w *= 2

    nbatch = math.gcd(GDN_PREP_BATCH // GDN_PAIR, nck)

    def column(ref, r):
        rows = jnp.broadcast_to(ref[pl.ds(r, 1), :], (C, C))
        return rows.T, rows

    def prepare(it, carry):
        items = [(j, it * nbatch + i) for i in range(nbatch) for j in heads]
        sls = [pl.ds(pl.multiple_of(c * C, C), C) for _, c in items]
        g_f, g_b, bt_f, bt_b, dec_f, dec_b, kb_f, kb_b, aq = [], [], [], [], [], [], [], [], []
        for (j, c), sl in zip(items, sls):
            k = ks[j, sl, :]
            gfc, gfr = column(cf.at[j], c * 8 + 2)
            gbc, gbr = column(cb.at[j], c * 8 + 3)
            g_f.append(gfc)
            g_b.append(gbc)
            bt_f.append(column(sg.at[j], c * 8)[0])
            bt_b.append(column(sg.at[j], c * 8 + 1)[0])
            dec_f.append(jnp.where(low_i, jnp.exp(jnp.where(low_i, gfc - gfr, 0.0)), 0.0))
            dec_b.append(jnp.where(upp_i, jnp.exp(jnp.where(upp_i, gbc - gbr, 0.0)), 0.0))
            kb_f.append(k * bt_f[-1])
            kb_b.append(k * bt_b[-1])
            aq.append(_dot_nt(jnp.concatenate([kb_f[-1], kb_b[-1], qs[j, sl, :]], axis=0).astype(BF16),
                              k.astype(BF16)))
        lps = [jnp.where(low, x[:C] * df, 0.0) + jnp.where(upp, x[C:2 * C] * db, 0.0)
               for x, df, db in zip(aq, dec_f, dec_b)]
        inv = _packed_tri_inverse(lps, low, upp, bd16, rings)
        for n, ((j, c), sl) in enumerate(zip(items, sls)):
            q, k, v = qs[j, sl, :], ks[j, sl, :], vs[j, sl, :]
            for rev, g_c, kb, beta, dec, msk, u_s, wq_s, qk_s, kdt_s, egl_s in (
                    (False, g_f[n], kb_f[n], bt_f[n], dec_f[n], low, uf, wqf, qkf, kdtf, eglf),
                    (True, g_b[n], kb_b[n], bt_b[n], dec_b[n], upp, ub, wqb, qkb, kdtb, eglb)):
                eg = jnp.exp(g_c)
                rhs = jnp.concatenate([v * beta, kb * eg], axis=1)
                uw = rhs + _dot(jnp.where(msk, inv[n], 0.0).astype(BF16), rhs.astype(BF16))
                glast = g_c[0:1] if rev else g_c[C - 1:C]
                u_s[j, sl, :] = uw[:, :C]
                wq_s[j, pl.ds(pl.multiple_of(c * 2 * C, 2 * C), 2 * C), :] = jnp.concatenate(
                    [uw[:, C:], q * eg], axis=0).astype(BF16)
                qk_s[j, sl, :] = (aq[n][2 * C:] * dec).astype(BF16)
                kdt_s[j, sl, :] = (k * jnp.exp(glast - g_c)).T.astype(BF16)
                egl_s[j, pl.ds(c, 1), :] = jnp.exp(glast)
        return carry

    lax.fori_loop(0, nck // nbatch, prepare, 0)

    def body(i, carry):
        chains = []
        for j in heads:
            chains.append((j, i, uf, wqf, qkf, kdtf, eglf))
            chains.append((j, nck - 1 - i, ub, wqb, qkb, kdtb, eglb))
        sls = [pl.ds(pl.multiple_of(c * C, C), C) for _, c, *_ in chains]
        ws_qs = [_dot(wq_s[j, pl.ds(pl.multiple_of(c * 2 * C, 2 * C), 2 * C), :], st.astype(BF16))
                 for (j, c, _, wq_s, *_), st in zip(chains, carry)]
        vnb = [(u_s[j, sl, :] - x[:C]).astype(BF16) for (j, _, u_s, *_), sl, x in zip(chains, sls, ws_qs)]
        new = [st * egl_s[j, pl.ds(c, 1), :] + _dot(kdt_s[j, sl, :], v)
               for (j, c, _, _, _, kdt_s, egl_s), sl, st, v in zip(chains, sls, carry, vnb)]
        for (j, _, _, _, qk_s, _, _), sl, x, v in zip(chains, sls, ws_qs, vnb):
            os_[j, sl, :] += x[C:] + _dot(qk_s[j, sl, :], v)
        return tuple(new)

    zero = jnp.zeros((hd, hd), F32)
    lax.fori_loop(0, nck, body, (zero,) * (2 * GDN_PAIR))

    for j in heads:
        o = os_[j]
        o = o * lax.rsqrt(jnp.mean(o * o, axis=-1, keepdims=True) + RMS_EPS) * nw_ref[...]
        z = z_ref[:, j * hd:(j + 1) * hd].astype(F32)
        o_ref[:, j * hd:(j + 1) * hd] = (o * (z * jax.nn.sigmoid(z))).astype(BF16)


def _gdn(qkvz, gates3, conv_w, head_par, norm_w, bsz, seq):
    t = bsz * seq
    hd = GDN_HEAD_DIM
    nh = GDN_HEADS
    np_ = GDN_PAIR
    wd = np_ * hd
    npairs = nh // np_
    nck = seq // GDN_BLOCK
    col = lambda off: pl.BlockSpec((seq, wd), lambda b, p: (b, off * npairs + p))
    wcol = lambda off: pl.BlockSpec((GDN_CONV, wd), lambda b, p: (0, off * npairs + p))
    per_head = lambda rows, dt: pltpu.VMEM((np_, rows, hd), dt)
    return pl.pallas_call(
        _gdn_body,
        grid=(bsz, npairs),
        in_specs=[col(0), col(1), col(2), col(3), wcol(0), wcol(1), wcol(2),
                  pl.BlockSpec((nck, 8 * np_, GDN_BLOCK), lambda b, p: (b, p, 0)),
                  pl.BlockSpec((np_, 8, LANES), lambda b, p: (p, 0, 0)),
                  pl.BlockSpec((1, hd), lambda b, p: (0, 0))],
        out_specs=pl.BlockSpec((seq, wd), lambda b, p: (b, p)),
        out_shape=jax.ShapeDtypeStruct((t, nh * hd), BF16),
        scratch_shapes=([per_head(seq, F32)] * 4
                        + [per_head(nck * 8, F32)] * 3
                        + [per_head(seq, F32)] * 2
                        + [per_head(2 * seq, BF16)] * 2
                        + [per_head(seq, BF16)] * 2
                        + [per_head(seq, BF16)] * 2
                        + [per_head(nck, F32)] * 2
                        + [pltpu.VMEM((seq + 16, hd), F32)]),
        compiler_params=_cparams("parallel", "parallel"),
        name="gdn",
    )(qkvz, qkvz, qkvz, qkvz, conv_w, conv_w, conv_w, gates3, head_par, norm_w)


def _kv_body(m_ref, g_ref, wk_ref, wv_ref, k_ref, v_ref):
    mn = _rms(m_ref[...], g_ref[...]).astype(BF16)
    k_ref[...] = _dot(mn, wk_ref[...]).astype(BF16)
    v_ref[...] = _dot(mn, wv_ref[...]).astype(BF16)


def _mem_kv(mem2d, norm_w, wk, wv, tm=512):
    r, d = mem2d.shape
    full = lambda shape: pl.BlockSpec(shape, lambda i: (0,) * len(shape))
    tile = pl.BlockSpec((tm, d), lambda i: (i, 0))
    return pl.pallas_call(
        _kv_body,
        grid=(r // tm,),
        in_specs=[tile, full((1, d)), full(wk.shape), full(wv.shape)],
        out_specs=[tile, tile],
        out_shape=[jax.ShapeDtypeStruct((r, d), BF16)] * 2,
        compiler_params=_cparams("parallel"),
        name="mem_kv",
    )(mem2d, norm_w, wk, wv)


def _mix_xattn_body(x_ref, y5_ref, yg_ref, wmix_ref, g_ref, wq_ref, k_ref, v_ref, wo_ref,
                    gm_ref, whi_ref, wlo_ref, br_ref, tri_ref,
                    o_ref, mi_ref, mf_ref, cnt_ref, wmix_b, wq_b, wo_b, carry):
    @pl.when(pl.program_id(0) == 0)
    def _():
        wmix_b[...] = wmix_ref[...].astype(BF16)
        wq_b[...] = wq_ref[...].astype(BF16)
        wo_b[...] = wo_ref[...].astype(BF16)
        carry[...] = jnp.zeros_like(carry)

    x1 = (x_ref[...] + _dot(y5_ref[...], wmix_b[:S5_WIDTH, :]) + _dot(yg_ref[...], wmix_b[S5_WIDTH:, :]))
    xn = _rms(x1, g_ref[...]).astype(BF16)
    q = (_dot(xn, wq_b[...]) * (XA_HEAD_DIM ** -0.5)).astype(BF16)
    heads = []
    for h in range(XA_HEADS):
        sl = slice(h * XA_HEAD_DIM, (h + 1) * XA_HEAD_DIM)
        s = _dot_nt(q[:, sl], k_ref[:, sl])
        p = jnp.exp(s - jnp.max(s, axis=-1, keepdims=True))
        p = p / jnp.sum(p, axis=-1, keepdims=True)
        heads.append(_dot(p.astype(BF16), v_ref[:, sl]).astype(BF16))
    x2 = x1 + _dot(jnp.concatenate(heads, axis=1), wo_b[...])
    o_ref[...] = x2
    mi, mf = _route(_rms(x2, gm_ref[...]), whi_ref[...], wlo_ref[...], br_ref[...], tri_ref[...], carry)
    mi_ref[...] = mi
    mf_ref[...] = mf
    cnt_ref[...] = carry[...]


def _mix_xattn(x2d, y5, yg, w_mix, norm_w, wq, kmem, vmem, wo, norm_moe, w_route, b_route, seq, mem_len, tm=512):
    t, d = x2d.shape
    per_b = seq // tm
    tri = jnp.triu(jnp.ones((tm, tm), BF16))
    w_hi = w_route.astype(BF16)
    w_lo = (w_route - w_hi.astype(F32)).astype(BF16)
    full = lambda shape: pl.BlockSpec(shape, lambda i: (0,) * len(shape))
    tile = lambda w: pl.BlockSpec((tm, w), lambda i: (i, 0))
    return pl.pallas_call(
        _mix_xattn_body,
        grid=(t // tm,),
        in_specs=[tile(d), tile(S5_WIDTH), tile(GDN_WIDTH),
                  full(w_mix.shape), full((1, d)), full(wq.shape),
                  pl.BlockSpec((mem_len, d), lambda i: (i // per_b, 0)),
                  pl.BlockSpec((mem_len, d), lambda i: (i // per_b, 0)),
                  full(wo.shape),
                  full((1, d)), full(w_route.shape), full(w_route.shape), full(b_route.shape), full((tm, tm))],
        out_specs=[tile(d), pl.BlockSpec((8, tm), lambda i: (0, i)), tile(LANES),
                   pl.BlockSpec((ROUTE_ROWS, LANES), lambda i: (0, 0))],
        out_shape=[jax.ShapeDtypeStruct((t, d), F32),
                   jax.ShapeDtypeStruct((8, t), I32),
                   jax.ShapeDtypeStruct((t, LANES), F32),
                   jax.ShapeDtypeStruct((ROUTE_ROWS, LANES), F32)],
        scratch_shapes=[pltpu.VMEM(w_mix.shape, BF16), pltpu.VMEM(wq.shape, BF16), pltpu.VMEM(wo.shape, BF16),
                        pltpu.VMEM((ROUTE_ROWS, LANES), F32)],
        compiler_params=_cparams("arbitrary"),
        name="mix_xattn",
    )(x2d, y5, yg, w_mix, norm_w, wq, kmem, vmem, wo, norm_moe, w_hi, w_lo, b_route, tri)


ROUTE_EXPERT_LANE0 = 4


ROUTE_ROWS = 40


def _route(xn, w_hi, w_lo, bias, tri, carry):
    x_hi = xn.astype(BF16)
    x_lo = (xn - x_hi.astype(F32)).astype(BF16)
    logits = (_dot_nt(w_hi, x_hi) + _dot_nt(w_hi, x_lo) + _dot_nt(w_lo, x_hi))[:ROUTE_ROWS] + bias[:ROUTE_ROWS, 0:1]
    tm = logits.shape[1]
    row = lax.broadcasted_iota(I32, (ROUTE_ROWS, tm), 0)
    neg = jnp.float32(-jnp.inf)
    big = jnp.int32(LANES)

    def top(vals):
        m = jnp.max(vals, axis=0, keepdims=True)
        idx = jnp.min(jnp.where(vals == m, row, big), axis=0, keepdims=True)
        return m, idx

    is_g = row < MOE_GROUPS
    gl = jnp.where(is_g, logits, neg)
    gmax, gidx = top(gl)
    p_top = 1.0 / jnp.sum(jnp.where(is_g, jnp.exp(gl - gmax), 0.0), axis=0, keepdims=True)
    erow = row - ROUTE_EXPERT_LANE0
    in_grp = jnp.logical_and(jnp.logical_and(erow >= 0, erow < MOE_EXPERTS), (erow // MOE_PER_GROUP) == gidx)
    es = jnp.where(in_grp, logits, neg)
    m1, i1 = top(es)
    m2, i2 = top(jnp.where(row == i1, neg, es))
    e21 = jnp.exp(m2 - m1)
    w1 = p_top / (1.0 + e21)
    w2 = p_top * e21 / (1.0 + e21)

    a1 = (row == i1).astype(F32)
    a2 = (row == i2).astype(F32)
    both = a1 + a2
    before = _dot(both.astype(BF16), tri) - both + carry[:, 0:1]
    r1 = jnp.sum(a1 * before, axis=0, keepdims=True).astype(I32)
    r2 = jnp.sum(a2 * before, axis=0, keepdims=True).astype(I32)
    carry[...] = carry[...] + jnp.sum(both, axis=1, keepdims=True)
    row8 = lax.broadcasted_iota(I32, (8, tm), 0)
    mi = jnp.where(row8 == 0, i1 - ROUTE_EXPERT_LANE0,
                   jnp.where(row8 == 1, i2 - ROUTE_EXPERT_LANE0, jnp.where(row8 == 2, r1, jnp.where(row8 == 3, r2, 0))))
    rowl = lax.broadcasted_iota(I32, (LANES, tm), 0)
    wt = jnp.where(rowl == 0, w1, jnp.where(rowl == 1, w2, 0.0))
    mf = jnp.concatenate([wt[:, j * LANES:(j + 1) * LANES].T for j in range(tm // LANES)], axis=0)
    return mi, mf


def _dest_body(mi_ref, off_ref, da_ref, db_ref):
    tm = da_ref.shape[2]
    mi = mi_ref[...]
    n = mi.shape[1]
    row = lax.broadcasted_iota(I32, (MOE_EXPERTS, n), 0)
    off = off_ref[:, 0:1]
    d0 = jnp.sum(jnp.where(row == mi[0:1], off, 0), axis=0, keepdims=True) + mi[2:3]
    d1 = jnp.sum(jnp.where(row == mi[1:2], off, 0), axis=0, keepdims=True) + mi[3:4]
    for s in range(da_ref.shape[0]):
        da_ref[s] = d0[:, s * tm:(s + 1) * tm]
        db_ref[s] = d1[:, s * tm:(s + 1) * tm]


def _dest_rows(mi, offsets_col, tm):
    t = mi.shape[1]
    tiles_per_step = math.gcd(8, t // tm)
    out = pl.BlockSpec((tiles_per_step, 1, tm), lambda i: (i, 0, 0))
    return pl.pallas_call(
        _dest_body,
        grid=(t // (tm * tiles_per_step),),
        in_specs=[pl.BlockSpec((8, tm * tiles_per_step), lambda i: (0, i)),
                  pl.BlockSpec((MOE_EXPERTS, LANES), lambda i: (0, 0))],
        out_specs=[out, out],
        out_shape=[jax.ShapeDtypeStruct((t // tm, 1, tm), I32)] * 2,
        compiler_params=_cparams("parallel"),
        name="moe_dest",
    )(mi, offsets_col)


def _dispatch_body(zs_ref, na_ref, da_ref, db_ref, xn_ref, xs_ref, zbuf, sem, zsem):
    tm = xn_ref.shape[0]
    n_tiles = xs_ref.shape[0] // tm

    @pl.when(pl.program_id(0) == 0)
    def _():
        zbuf[...] = jnp.zeros_like(zbuf)

        def fill(row0):
            return pltpu.make_async_copy(zbuf, xs_ref.at[pl.ds(pl.multiple_of(row0, tm), tm), :], zsem)

        def tail(e, c):
            @pl.when(zs_ref[e] >= 0)
            def _():
                fill(zs_ref[e]).start()
            return c

        def unused(j, c):
            fill(j * tm).start()
            return c

        def drain(j, c):
            fill(0).wait()
            return c

        lax.fori_loop(0, MOE_EXPERTS, tail, 0)
        lax.fori_loop(na_ref[0], n_tiles, unused, 0)
        lax.fori_loop(0, na_ref[1] + n_tiles - na_ref[0], drain, 0)

    def start(r, c):
        for k, d_ref in enumerate((da_ref, db_ref)):
            pltpu.make_async_copy(xn_ref.at[pl.ds(r, 1), :], xs_ref.at[pl.ds(d_ref[0, r], 1), :],
                                  sem.at[k]).start(priority=k)
        return c

    lax.fori_loop(0, tm, start, 0, unroll=8)
    for k in range(MOE_TOPK):
        pltpu.make_async_copy(xn_ref, xs_ref.at[pl.ds(0, tm), :], sem.at[k]).wait()


def _dispatch(zero_start, n_active, dest_a, dest_b, xn, n_rows, tm):
    t, dw = xn.shape
    smem_row = pl.BlockSpec((None, 1, tm), lambda i, zs, na: (i, 0, 0), memory_space=pltpu.SMEM)
    grid_spec = pltpu.PrefetchScalarGridSpec(
        num_scalar_prefetch=2,
        grid=(t // tm,),
        in_specs=[smem_row, smem_row, pl.BlockSpec((tm, dw), lambda i, zs, na: (i, 0))],
        out_specs=pl.BlockSpec(memory_space=pl.ANY),
        scratch_shapes=[pltpu.VMEM((tm, dw), xn.dtype), pltpu.SemaphoreType.DMA((MOE_TOPK,)),
                        pltpu.SemaphoreType.DMA(())],
    )
    return pl.pallas_call(
        _dispatch_body,
        grid_spec=grid_spec,
        out_shape=jax.ShapeDtypeStruct((n_rows, dw), xn.dtype),
        compiler_params=_cparams("arbitrary"),
        name="moe_dispatch",
    )(zero_start, n_active, dest_a, dest_b, xn)


def _experts_body(te_ref, na_ref, x_ref, g_ref, wg_ref, wu_ref, wd_ref, y_ref, wg_b, wu_b, wd_b):
    i = pl.program_id(0)

    @pl.when(i < na_ref[0])
    def _():
        @pl.when(jnp.logical_or(i == 0, te_ref[i] != te_ref[jnp.maximum(i - 1, 0)]))
        def _():
            wg_b[...] = wg_ref[...].astype(BF16)
            wu_b[...] = wu_ref[...].astype(BF16)
            wd_b[...] = wd_ref[...].astype(BF16)

        x = _rms(x_ref[...], g_ref[...]).astype(BF16)
        gt = _dot(x, wg_b[...])
        up = _dot(x, wu_b[...])
        hid = (gt * jax.nn.sigmoid(gt) * up).astype(BF16)
        y_ref[...] = _dot(hid, wd_b[...])

    @pl.when(i >= na_ref[0])
    def _():
        y_ref[...] = jnp.zeros_like(y_ref)


def _experts(tile_expert, n_active, xs, norm_w, w_gate, w_up, w_down, tm):
    r, d = xs.shape
    f = w_gate.shape[2]
    row_tile = lambda i, te, na: (jnp.minimum(i, na[0] - 1), 0)
    grid_spec = pltpu.PrefetchScalarGridSpec(
        num_scalar_prefetch=2,
        grid=(r // tm,),
        in_specs=[pl.BlockSpec((tm, d), row_tile),
                  pl.BlockSpec((1, d), lambda i, te, na: (0, 0)),
                  pl.BlockSpec((None, d, f), lambda i, te, na: (te[i], 0, 0)),
                  pl.BlockSpec((None, d, f), lambda i, te, na: (te[i], 0, 0)),
                  pl.BlockSpec((None, f, d), lambda i, te, na: (te[i], 0, 0))],
        out_specs=pl.BlockSpec((tm, d), lambda i, te, na: (i, 0)),
        scratch_shapes=[pltpu.VMEM((d, f), BF16), pltpu.VMEM((d, f), BF16), pltpu.VMEM((f, d), BF16)],
    )
    return pl.pallas_call(
        _experts_body,
        grid_spec=grid_spec,
        out_shape=jax.ShapeDtypeStruct((r, d), xs.dtype),
        compiler_params=_cparams("arbitrary"),
        name="moe_experts",
    )(tile_expert, n_active, xs, norm_w, w_gate, w_up, w_down)


def _combine_body(da_ref, db_ref, na_ref, nb_ref, x_ref, mf_ref, g_ref, ys_ref, o_ref, buf, sem):
    tm = x_ref.shape[0]
    i = pl.program_id(0)
    slot = lax.rem(i, 2)

    def gather(d_refs, s):
        def start(r, c):
            for k, d_ref in enumerate(d_refs):
                pltpu.make_async_copy(ys_ref.at[pl.ds(d_ref[0, r], 1), :],
                                      buf.at[s, k, pl.ds(r, 1), :], sem.at[s, k]).start(priority=k)
            return c

        lax.fori_loop(0, tm, start, 0, unroll=8)

    @pl.when(i == 0)
    def _():
        gather((da_ref, db_ref), 0)

    @pl.when(i + 1 < pl.num_programs(0))
    def _():
        gather((na_ref, nb_ref), 1 - slot)

    for k in range(MOE_TOPK):
        pltpu.make_async_copy(ys_ref.at[pl.ds(0, tm), :], buf.at[slot, k], sem.at[slot, k]).wait()
    mf = mf_ref[...]
    y = x_ref[...] + mf[:, 0:1] * buf[slot, 0] + mf[:, 1:2] * buf[slot, 1]
    o_ref[...] = _rms(y, g_ref[...])


def _combine(dest_a, dest_b, x2d, mf, norm_w, ys, tm):
    t, d = x2d.shape
    last = t // tm - 1
    cur = pl.BlockSpec((None, 1, tm), lambda i: (i, 0, 0), memory_space=pltpu.SMEM)
    nxt = pl.BlockSpec((None, 1, tm), lambda i: (jnp.minimum(i + 1, last), 0, 0), memory_space=pltpu.SMEM)
    return pl.pallas_call(
        _combine_body,
        grid=(t // tm,),
        in_specs=[cur, cur, nxt, nxt,
                  pl.BlockSpec((tm, d), lambda i: (i, 0)),
                  pl.BlockSpec((tm, LANES), lambda i: (i, 0)),
                  pl.BlockSpec((1, d), lambda i: (0, 0)),
                  pl.BlockSpec(memory_space=pl.ANY)],
        out_specs=pl.BlockSpec((tm, d), lambda i: (i, 0)),
        out_shape=jax.ShapeDtypeStruct((t, d), F32),
        scratch_shapes=[pltpu.VMEM((2, MOE_TOPK, tm, ys.shape[1]), ys.dtype), pltpu.SemaphoreType.DMA((2, MOE_TOPK))],
        compiler_params=_cparams("arbitrary"),
        name="moe_combine",
    )(dest_a, dest_b, dest_a, dest_b, x2d, mf, norm_w, ys)


MOE_ROW_TILE = 256


def _moe(x2d, mi, mf, cnt, norm_w, w_gate, w_up, w_down, norm_final):
    t, d = x2d.shape
    tm = MOE_ROW_TILE
    counts = cnt[ROUTE_EXPERT_LANE0:ROUTE_EXPERT_LANE0 + MOE_EXPERTS, 0].astype(I32)
    padded = ((counts + tm - 1) // tm) * tm
    ends = jnp.cumsum(padded)
    offsets = ends - padded
    n_tiles = (MOE_TOPK * t + MOE_EXPERTS * (tm - 1)) // tm
    tile_start = jnp.arange(n_tiles, dtype=I32) * tm
    tile_expert = jnp.minimum(jnp.sum((ends[None, :] <= tile_start[:, None]).astype(I32), axis=1), MOE_EXPERTS - 1)
    n_info = jnp.stack([ends[-1] // tm, jnp.sum((counts > 0).astype(I32))]).astype(I32)
    zero_start = jnp.where(counts > 0, ends - tm, -1).astype(I32)
    offsets_col = jnp.broadcast_to(offsets[:, None], (MOE_EXPERTS, LANES))
    dest_a, dest_b = _dest_rows(mi, offsets_col, tm)
    xs = _dispatch(zero_start, n_info, dest_a, dest_b, x2d, n_tiles * tm, tm)
    ys = _experts(tile_expert, n_info, xs, norm_w, w_gate, w_up, w_down, tm)
    return _combine(dest_a, dest_b, x2d, mf, norm_final, ys, tm)


def kernel(x, mem, norm_mix, w_in, w_out,
           s5_lam_re_f, s5_lam_im_f, s5_log_step_f, s5_b_re_f, s5_b_im_f, s5_c_re_f, s5_c_im_f,
           s5_lam_re_b, s5_lam_im_b, s5_log_step_b, s5_b_re_b, s5_b_im_b, s5_c_re_b, s5_c_im_b,
           s5_d, s5_w_glu, s5_b_glu, s5_norm,
           gdn_conv, gdn_a_log_f, gdn_dt_bias_f, gdn_a_log_b, gdn_dt_bias_b, gdn_norm,
           norm_xattn, norm_mem, xa_wq, xa_wk, xa_wv, xa_wo,
           norm_moe, router_group_w, router_group_b, router_expert_w, router_expert_b,
           moe_w_gate, moe_w_up, moe_w_down, norm_final):
    bsz, seq, d = x.shape
    t = bsz * seq
    l = 0
    x2d = x.reshape(t, d)
    wi = w_in[l]
    wg = wi[:, S5_WIDTH + 4 * GDN_WIDTH:].reshape(d, 4, GDN_HEADS)
    wg = jnp.pad(jnp.swapaxes(wg, 1, 2), ((0, 0), (0, 0), (0, 4))).reshape(d, GDN_HEADS * 8)
    wgt = wg.T
    u3, qkvz, gates = _in_proj(x2d, norm_mix[l][None], wi, wgt)
    s5p = dict(lam_re_f=s5_lam_re_f[l], lam_im_f=s5_lam_im_f[l], log_step_f=s5_log_step_f[l],
               b_re_f=s5_b_re_f[l], b_im_f=s5_b_im_f[l], c_re_f=s5_c_re_f[l], c_im_f=s5_c_im_f[l],
               lam_re_b=s5_lam_re_b[l], lam_im_b=s5_lam_im_b[l], log_step_b=s5_log_step_b[l],
               b_re_b=s5_b_re_b[l], b_im_b=s5_b_im_b[l], c_re_b=s5_c_re_b[l], c_im_b=s5_c_im_b[l], d=s5_d[l])
    g3 = _s5_scan(u3, s5p, seq // S5_CHUNK)
    y_s5 = _s5_post(g3, s5_w_glu[l].T.astype(BF16), s5_b_glu[l][:, None], s5_norm[l][:, None])

    head_par = jnp.stack([gdn_a_log_f[l], gdn_dt_bias_f[l], gdn_a_log_b[l], gdn_dt_bias_b[l]], axis=1)
    head_par = jnp.broadcast_to(jnp.pad(head_par, ((0, 0), (0, 4)))[:, :, None], (GDN_HEADS, 8, LANES))
    y_gdn = _gdn(qkvz, gates, gdn_conv[l], head_par, gdn_norm[l][None], bsz, seq)

    mem_len = mem.shape[1]
    kmem, vmem = _mem_kv(mem.reshape(bsz * mem_len, d), norm_mem[l][None],
                         xa_wk[l].astype(BF16), xa_wv[l].astype(BF16))
    n_pad = LANES - MOE_GROUPS - MOE_EXPERTS
    w_route = jnp.pad(jnp.concatenate([router_group_w[l], router_expert_w[l]], axis=1), ((0, 0), (0, n_pad))).T
    b_route = jnp.pad(jnp.concatenate([router_group_b[l], router_expert_b[l]]), (0, n_pad))
    b_route = jnp.broadcast_to(b_route[:, None], (LANES, LANES))
    x2, mi, mf, cnt = _mix_xattn(x2d, y_s5, y_gdn, w_out[l], norm_xattn[l][None], xa_wq[l], kmem, vmem, xa_wo[l],
                                 norm_moe[l][None], w_route, b_route, seq, mem_len)
    y = _moe(x2, mi, mf, cnt, norm_moe[l][None], moe_w_gate[l], moe_w_up[l], moe_w_down[l], norm_final[None])
    return y.reshape(bsz, seq, d)
```

```python
import functools
import math

import jax
import jax.numpy as jnp
from jax import lax
from jax.experimental import pallas as pl
from jax.experimental.pallas import tpu as pltpu

F32 = jnp.float32
BF16 = jnp.bfloat16
I32 = jnp.int32

D_MODEL = 1024
S5_WIDTH = 512
S5_GROUP = 16
S5_GROUPS = 32
S5_STATE = 64
S5_CHUNK = 128
GDN_HEADS = 4
GDN_HEAD_DIM = 128
GDN_WIDTH = 512
GDN_CONV = 5
GDN_CHUNK = 64
XA_HEADS = 4
XA_HEAD_DIM = 256
MOE_GROUPS = 4
MOE_PER_GROUP = 8
MOE_EXPERTS = 32
MOE_TOPK = 2
D_EXPERT = 256
RMS_EPS = 1e-6
L2_EPS = 1e-6
LANES = 128
VMEM_LIMIT = 56 * 1024 * 1024


def _cparams(*sem):
    return pltpu.CompilerParams(dimension_semantics=tuple(sem), vmem_limit_bytes=VMEM_LIMIT)


def _rms(x, gain):
    return x * lax.rsqrt(jnp.mean(x * x, axis=-1, keepdims=True) + RMS_EPS) * gain


def _dot(a, b):
    return jnp.dot(a, b, preferred_element_type=F32)


def _dot_nt(a, b):
    return lax.dot_general(a, b, (((1,), (1,)), ((), ())), preferred_element_type=F32)


def _dot_tn(a, b):
    return lax.dot_general(a, b, (((0,), (0,)), ((), ())), preferred_element_type=F32)


def _in_proj_body(x_ref, g_ref, w_ref, wgt_ref, u_ref, qkvz_ref, gates_ref, wut_b, wqkvz_b, wgt_b):
    @pl.when(pl.program_id(0) == 0)
    def _():
        wut_b[...] = w_ref[:, :S5_WIDTH].T.astype(BF16)
        wqkvz_b[...] = w_ref[:, S5_WIDTH:S5_WIDTH + 4 * GDN_WIDTH].astype(BF16)
        wgt_b[...] = wgt_ref[...].astype(BF16)

    h = _rms(x_ref[...], g_ref[...]).astype(BF16)
    ut = _dot_nt(wut_b[...], h)
    gt = _dot_nt(wgt_b[...], h)
    for j in range(u_ref.shape[0]):
        u_ref[j] = ut[:, j * S5_CHUNK:(j + 1) * S5_CHUNK]
        gates_ref[j] = gt[:, j * S5_CHUNK:(j + 1) * S5_CHUNK]
    qkvz_ref[...] = _dot(h, wqkvz_b[...]).astype(BF16)


def _in_proj(x2d, norm_w, w_in, wgt, tm=512):
    t = x2d.shape[0]
    nck = tm // S5_CHUNK
    nqkvz = 4 * GDN_WIDTH
    full = lambda shape: pl.BlockSpec(shape, lambda i: (0,) * len(shape))
    return pl.pallas_call(
        _in_proj_body,
        grid=(t // tm,),
        in_specs=[pl.BlockSpec((tm, D_MODEL), lambda i: (i, 0)),
                  full((1, D_MODEL)), full(w_in.shape), full(wgt.shape)],
        out_specs=[pl.BlockSpec((nck, S5_WIDTH, S5_CHUNK), lambda i: (i, 0, 0)),
                   pl.BlockSpec((tm, nqkvz), lambda i: (i, 0)),
                   pl.BlockSpec((nck, wgt.shape[0], S5_CHUNK), lambda i: (i, 0, 0))],
        out_shape=[jax.ShapeDtypeStruct((t // S5_CHUNK, S5_WIDTH, S5_CHUNK), F32),
                   jax.ShapeDtypeStruct((t, nqkvz), BF16),
                   jax.ShapeDtypeStruct((t // S5_CHUNK, wgt.shape[0], S5_CHUNK), F32)],
        scratch_shapes=[pltpu.VMEM((S5_WIDTH, D_MODEL), BF16), pltpu.VMEM((D_MODEL, nqkvz), BF16),
                        pltpu.VMEM(wgt.shape, BF16)],
        compiler_params=_cparams("arbitrary"),
        name="in_proj",
    )(x2d, norm_w, w_in, wgt)


def _cmul(ar, ai, br, bi):
    return ar * br - ai * bi, ar * bi + ai * br


def _cpow_int(lr, li, expo, nbits):
    res_r = jnp.ones(jnp.broadcast_shapes(lr.shape, expo.shape), F32)
    res_i = jnp.zeros_like(res_r)
    for b in range(nbits):
        bit = ((expo >> b) & 1) == 1
        nr, ni = _cmul(res_r, res_i, lr, li)
        res_r = jnp.where(bit, nr, res_r)
        res_i = jnp.where(bit, ni, res_i)
        if b + 1 < nbits:
            lr, li = _cmul(lr, li, lr, li)
    return res_r, res_i


def _lam_bar(re, im, step):
    er = jnp.exp(step * re)
    return er * jnp.cos(step * im), er * jnp.sin(step * im)


def _zoh_coef(re, im, lr, li):
    den = re * re + im * im
    return ((lr - 1.0) * re + li * im) / den, (li * re - (lr - 1.0) * im) / den


def _s5_body(u_ref, lrow_ref, lcol_ref, l256_ref, b_ref, bt_ref, c_ref, ct_ref, d_ref,
             o_ref, vf_ref, vb_ref, m_ref, win_ref, wout_ref, sf_ref, sb_ref, hf_ref, hb_ref, uc_ref, y_ref, *, nchunk):
    L = S5_CHUNK
    P = S5_STATE
    n_rows = u_ref.shape[0]
    nb = n_rows // nchunk
    lane_i = lax.broadcasted_iota(I32, (1, L), 1)

    lcol = lcol_ref[...]
    lbc_r, lbc_i = _lam_bar(lcol[:, 0:2], lcol[:, 2:4], lcol[:, 4:6])
    kc_r, kc_i = _zoh_coef(lcol[:, 0:2], lcol[:, 2:4], lbc_r, lbc_i)
    lrow = lrow_ref[...]
    lbr_r, lbr_i = _lam_bar(lrow[0:2], lrow[2:4], lrow[4:6])
    kr_r, kr_i = _zoh_coef(lrow[0:2], lrow[2:4], lbr_r, lbr_i)
    lf_r, lf_i, lb_r, lb_i = lbc_r[:, 0:1], lbc_i[:, 0:1], lbc_r[:, 1:2], lbc_i[:, 1:2]
    kfr_c, kfi_c, kbr_c, kbi_c = kc_r[:, 0:1], kc_i[:, 0:1], kc_r[:, 1:2], kc_i[:, 1:2]
    kfr_r, kfi_r, kbr_r, kbi_r = kr_r[0:1], kr_i[0:1], kr_r[1:2], kr_i[1:2]

    pwf_r, pwf_i = _cpow_int(lf_r, lf_i, lane_i, 7)
    rvf_r, rvf_i = _cpow_int(lf_r, lf_i, (L - 1) - lane_i, 7)
    pwb_r, pwb_i = _cpow_int(lb_r, lb_i, lane_i, 7)
    rvb_r, rvb_i = _cpow_int(lb_r, lb_i, L - lane_i, 8)
    nxf_r, nxf_i = _cmul(pwf_r, pwf_i, lf_r, lf_i)

    bf_r = kfr_c * b_ref[0] - kfi_c * b_ref[1]
    bf_i = kfr_c * b_ref[1] + kfi_c * b_ref[0]
    bb_r = kbr_c * b_ref[2] - kbi_c * b_ref[3]
    bb_i = kbr_c * b_ref[3] + kbi_c * b_ref[2]
    btf_r = kfr_r * bt_ref[0] - kfi_r * bt_ref[1]
    btf_i = kfr_r * bt_ref[1] + kfi_r * bt_ref[0]
    btb_r = kbr_r * bt_ref[2] - kbi_r * bt_ref[3]
    btb_i = kbr_r * bt_ref[3] + kbi_r * bt_ref[2]

    def taps(c_r, c_i, bt_r, bt_i, pw_r, pw_i):
        cb_r = (bt_r[:, None, :] * c_r[None, :, :] - bt_i[:, None, :] * c_i[None, :, :]).reshape(256, P)
        cb_i = (bt_r[:, None, :] * c_i[None, :, :] + bt_i[:, None, :] * c_r[None, :, :]).reshape(256, P)
        k = (jnp.dot(cb_r, pw_r, preferred_element_type=F32, precision=lax.Precision.HIGHEST)
             - jnp.dot(cb_i, pw_i, preferred_element_type=F32, precision=lax.Precision.HIGHEST))
        return k, jnp.sum(cb_r, axis=1, keepdims=True)

    kf, _ = taps(c_ref[0], c_ref[1], btf_r, btf_i, pwf_r, pwf_i)
    kb, kb0 = taps(c_ref[2], c_ref[3], btb_r, btb_i, rvb_r, rvb_i)
    is0 = lane_i == 0
    vf_ref[...] = kf + jnp.where(is0, kb0, 0.0)
    vb_ref[...] = jnp.where(is0, 0.0, kb)

    row_i = lax.broadcasted_iota(I32, (L, L), 0)
    col_i = lax.broadcasted_iota(I32, (L, L), 1)
    fwd_lane = col_i + row_i < L

    def build_ci(ci, carry):
        for co in range(S5_GROUP):
            r = ci * S5_GROUP + co
            taps_rows = jnp.where(fwd_lane, jnp.broadcast_to(vf_ref[pl.ds(r, 1), :], (L, L)),
                                  jnp.broadcast_to(vb_ref[pl.ds(r, 1), :], (L, L)))
            m_ref[pl.ds(pl.multiple_of(ci * L, L), L), co * L:(co + 1) * L] = pltpu.roll(
                taps_rows, 0, 1, stride=1, stride_axis=0).astype(BF16)
        return carry

    lax.fori_loop(0, S5_GROUP, build_ci, 0)

    for ci in range(S5_GROUP):
        sl = slice(ci * L, (ci + 1) * L)
        br, bi = bf_r[:, ci:ci + 1], bf_i[:, ci:ci + 1]
        win_ref[0 * P:1 * P, sl] = (rvf_r * br - rvf_i * bi).astype(BF16)
        win_ref[1 * P:2 * P, sl] = (rvf_r * bi + rvf_i * br).astype(BF16)
        br, bi = bb_r[:, ci:ci + 1], bb_i[:, ci:ci + 1]
        win_ref[2 * P:3 * P, sl] = (pwb_r * br - pwb_i * bi).astype(BF16)
        win_ref[3 * P:4 * P, sl] = (pwb_r * bi + pwb_i * br).astype(BF16)
    for co in range(S5_GROUP):
        sl = slice(co * L, (co + 1) * L)
        cr, ci_ = ct_ref[0][:, co:co + 1], ct_ref[1][:, co:co + 1]
        wout_ref[0 * P:1 * P, sl] = (cr * nxf_r - ci_ * nxf_i).astype(BF16)
        wout_ref[1 * P:2 * P, sl] = (-(cr * nxf_i + ci_ * nxf_r)).astype(BF16)
        cr, ci_ = ct_ref[2][:, co:co + 1], ct_ref[3][:, co:co + 1]
        wout_ref[2 * P:3 * P, sl] = (cr * rvb_r - ci_ * rvb_i).astype(BF16)
        wout_ref[3 * P:4 * P, sl] = (-(cr * rvb_i + ci_ * rvb_r)).astype(BF16)

    for ci in range(S5_GROUP):
        uc_ref[ci] = u_ref[:, ci, :]
    ucat = jnp.concatenate([uc_ref[ci].astype(BF16) for ci in range(S5_GROUP)], axis=1)

    summ = _dot_nt(ucat, win_ref[...])
    sf_ref[...] = summ[:, :2 * P]
    sb_ref[...] = summ[:, 2 * P:]
    nblk = 2 * L
    for j in range(S5_GROUP * L // nblk):
        y_ref[:, j * nblk:(j + 1) * nblk] = _dot(ucat, m_ref[:, j * nblk:(j + 1) * nblk])
    l256 = l256_ref[...]
    a_mul, a_im = _lam_bar(l256[0:1], l256[1:2], l256[2:3])
    for _ in range(7):
        a_mul, a_im = _cmul(a_mul, a_im, a_mul, a_im)
    lane256 = lax.broadcasted_iota(I32, (1, 4 * P), 1)
    b_mul = jnp.where((lane256 // P) % 2 == 0, -a_im, a_im)

    hf = jnp.zeros((nb, 2 * P), F32)
    hb = jnp.zeros((nb, 2 * P), F32)
    for c in range(nchunk):
        cr = nchunk - 1 - c
        rows_f = pl.ds(c, nb, stride=nchunk)
        rows_b = pl.ds(cr, nb, stride=nchunk)
        hf_ref[rows_f, :] = hf
        hb_ref[rows_b, :] = hb
        hf = a_mul[:, :2 * P] * hf + b_mul[:, :2 * P] * pltpu.roll(hf, P, 1) + sf_ref[rows_f, :]
        hb = a_mul[:, 2 * P:] * hb + b_mul[:, 2 * P:] * pltpu.roll(hb, P, 1) + sb_ref[rows_b, :]
    hprev = jnp.concatenate([hf_ref[...], hb_ref[...]], axis=1).astype(BF16)

    for j in range(S5_GROUP * L // nblk):
        y = y_ref[:, j * nblk:(j + 1) * nblk] + _dot(hprev, wout_ref[:, j * nblk:(j + 1) * nblk])
        for q in range(nblk // L):
            co = j * (nblk // L) + q
            yc = y[:, q * L:(q + 1) * L] + d_ref[co:co + 1, :] * uc_ref[co]
            o_ref[:, co, :] = 0.5 * yc * (1.0 + lax.erf(yc * (2.0 ** -0.5)))


def _s5_scan(u3, p, nchunk):
    n = u3.shape[0]
    g, grp, st, L = S5_GROUPS, S5_GROUP, S5_STATE, S5_CHUNK
    step_f = jnp.exp(p["log_step_f"])[:, None] * jnp.ones((1, st), F32)
    step_b = jnp.exp(p["log_step_b"])[:, None] * jnp.ones((1, st), F32)
    zeros = jnp.zeros((g, st), F32)
    lrow = jnp.stack([p["lam_re_f"], p["lam_re_b"], p["lam_im_f"], p["lam_im_b"], step_f, step_b, zeros, zeros], axis=1)
    lcol = jnp.swapaxes(lrow, 1, 2)
    cat4 = lambda f, b: jnp.concatenate([f, f, b, b], axis=1)
    z256 = jnp.zeros((g, 4 * st), F32)
    l256 = jnp.stack([cat4(p["lam_re_f"], p["lam_re_b"]), cat4(p["lam_im_f"], p["lam_im_b"]),
                      cat4(step_f, step_b)] + [z256] * 5, axis=1)
    b4 = jnp.stack([p["b_re_f"], p["b_im_f"], p["b_re_b"], p["b_im_b"]], axis=1)
    bt4 = jnp.swapaxes(b4, 2, 3)
    c4 = jnp.stack([p["c_re_f"], p["c_im_f"], p["c_re_b"], p["c_im_b"]], axis=1)
    ct4 = jnp.swapaxes(c4, 2, 3)
    dbc = jnp.broadcast_to(p["d"].reshape(g, grp, 1), (g, grp, L))
    per_g = lambda *shape: pl.BlockSpec((None,) + shape, lambda i: (i,) + (0,) * len(shape))
    return pl.pallas_call(
        functools.partial(_s5_body, nchunk=nchunk),
        grid=(g,),
        in_specs=[pl.BlockSpec((n, grp, L), lambda i: (0, i, 0)),
                  per_g(8, st), per_g(st, 8), per_g(8, 4 * st), per_g(4, st, grp), per_g(4, grp, st),
                  per_g(4, grp, st), per_g(4, st, grp), per_g(grp, L)],
        out_specs=pl.BlockSpec((n, grp, L), lambda i: (0, i, 0)),
        out_shape=jax.ShapeDtypeStruct(u3.shape, F32),
        scratch_shapes=[pltpu.VMEM((grp * grp, L), F32), pltpu.VMEM((grp * grp, L), F32),
                        pltpu.VMEM((grp * L, grp * L), BF16),
                        pltpu.VMEM((4 * st, grp * L), BF16), pltpu.VMEM((4 * st, grp * L), BF16),
                        pltpu.VMEM((n, 2 * st), F32), pltpu.VMEM((n, 2 * st), F32),
                        pltpu.VMEM((n, 2 * st), F32), pltpu.VMEM((n, 2 * st), F32),
                        pltpu.VMEM((grp, n, L), F32), pltpu.VMEM((n, grp * L), F32)],
        compiler_params=_cparams("parallel"),
        name="s5_scan",
    )(u3, lrow, lcol, l256, b4, bt4, c4, ct4, dbc)


def _s5_post_body(g_ref, wt_ref, b_ref, nw_ref, o_ref):
    for j in range(g_ref.shape[0]):
        g = g_ref[j]
        z = _dot(wt_ref[...], g.astype(BF16)) + b_ref[...]
        y = g * jax.nn.sigmoid(z)
        y = y * lax.rsqrt(jnp.mean(y * y, axis=0, keepdims=True) + RMS_EPS) * nw_ref[...]
        o_ref[j * S5_CHUNK:(j + 1) * S5_CHUNK, :] = y.T.astype(BF16)


def _s5_post(g3, w_glu_t, b_glu_col, norm_col, nck=4):
    n = g3.shape[0]
    full = lambda shape: pl.BlockSpec(shape, lambda i: (0,) * len(shape))
    return pl.pallas_call(
        _s5_post_body,
        grid=(n // nck,),
        in_specs=[pl.BlockSpec((nck, S5_WIDTH, S5_CHUNK), lambda i: (i, 0, 0)),
                  full(w_glu_t.shape), full(b_glu_col.shape), full(norm_col.shape)],
        out_specs=pl.BlockSpec((nck * S5_CHUNK, S5_WIDTH), lambda i: (i, 0)),
        out_shape=jax.ShapeDtypeStruct((n * S5_CHUNK, S5_WIDTH), BF16),
        compiler_params=_cparams("parallel"),
        name="s5_post",
    )(g3, w_glu_t, b_glu_col, norm_col)


GDN_BLOCK = 128
GDN_PAIR = 2
GDN_PREP_BATCH = 8


def _packed_tri_inverse(lps, low, upp, bd16, rings):
    def pk(xs, ys):
        outs = []
        for a, b in zip(xs, ys):
            lhs = jnp.concatenate([jnp.where(low, a, 0.0), jnp.where(upp, a, 0.0)], axis=1).astype(BF16)
            rhs = jnp.concatenate([jnp.where(low, b, 0.0), jnp.where(upp, b, 0.0)], axis=0).astype(BF16)
            outs.append(_dot(lhs, rhs))
        return outs

    d = [jnp.where(bd16, lp, 0.0) for lp in lps]
    d2 = pk(d, d)
    d4 = pk(d2, d2)
    d8 = pk(d4, d4)
    a = [y - x - p for x, y, p in zip(d, d2, pk(d, d2))]
    a = [x + y + p for x, y, p in zip(a, d4, pk(a, d4))]
    a = [x + y + p for x, y, p in zip(a, d8, pk(a, d8))]
    for ring in rings:
        n = [jnp.where(ring, lp, 0.0) for lp in lps]
        t = [x + p for x, p in zip(n, pk(a, n))]
        a = [x - y - p for x, y, p in zip(a, t, pk(t, a))]
    return a


def _gdn_body(q_ref, k_ref, v_ref, z_ref, wq_ref, wk_ref, wv_ref, g_ref, hp_ref, nw_ref, o_ref,
              qs, ks, vs, os_, sg, cf, cb, uf, ub, wqf, wqb, qkf, qkb, kdtf, kdtb, eglf, eglb, xpad):
    seq = q_ref.shape[0]
    C = GDN_BLOCK
    hd = GDN_HEAD_DIM
    nck = seq // C
    heads = range(GDN_PAIR)

    pad = 8
    half = (GDN_CONV - 1) // 2
    xpad[0:pad, :] = jnp.zeros((pad, LANES), F32)
    xpad[pad + seq:2 * pad + seq, :] = jnp.zeros((pad, LANES), F32)

    def conv_silu(x_ref, w_ref, j):
        cols = pl.ds(pl.multiple_of(j * hd, hd), hd)
        xpad[pad:pad + seq, :] = x_ref[:, cols].astype(F32)
        w = w_ref[:, cols]
        acc = xpad[pad - half:pad - half + seq, :] * w[0:1]
        for tap in range(1, GDN_CONV):
            acc = acc + xpad[pad - half + tap:pad - half + tap + seq, :] * w[tap:tap + 1]
        return acc * jax.nn.sigmoid(acc)

    def l2n(x):
        return x * lax.rsqrt(jnp.sum(x * x, axis=-1, keepdims=True) + L2_EPS)

    def softplus(x):
        return jnp.maximum(x, 0.0) + jnp.log1p(jnp.exp(-jnp.abs(x)))

    lane = lax.broadcasted_iota(I32, (1, C), 1)

    def prologue(j, carry):
        qs[j] = l2n(conv_silu(q_ref, wq_ref, j)) * (hd ** -0.5)
        ks[j] = l2n(conv_silu(k_ref, wk_ref, j))
        vs[j] = conv_silu(v_ref, wv_ref, j)
        os_[j] = jnp.zeros((seq, hd), F32)
        g = g_ref[:, pl.ds(pl.multiple_of(8 * j, 8), 8), :].reshape(nck * 8, C)
        hp = hp_ref[j]
        sg[j] = jax.nn.sigmoid(g)
        gl_f = -jnp.exp(hp[0:1]) * softplus(g + hp[1:2])
        gl_b = -jnp.exp(hp[2:3]) * softplus(g + hp[3:4])
        sh = 1
        while sh < C:
            gl_f = gl_f + jnp.where(lane >= sh, pltpu.roll(gl_f, sh, 1), 0.0)
            gl_b = gl_b + jnp.where(lane < C - sh, pltpu.roll(gl_b, C - sh, 1), 0.0)
            sh *= 2
        cf[j] = gl_f
        cb[j] = gl_b
        return carry

    lax.fori_loop(0, GDN_PAIR, prologue, 0)

    ri = lax.broadcasted_iota(I32, (C, C), 0)
    ci = lax.broadcasted_iota(I32, (C, C), 1)
    low, upp = ri > ci, ri < ci
    low_i, upp_i = ri >= ci, ri <= ci
    same = lambda w: (ri // w) == (ci // w)
    bd16 = same(16)
    rings = []
    w = 32
    while w <= C:
        rings.append(jnp.logical_and(same(w), jnp.logical_not(same(w // 2))))
        w *= 2

    nbatch = math.gcd(GDN_PREP_BATCH // GDN_PAIR, nck)

    def column(ref, r):
        rows = jnp.broadcast_to(ref[pl.ds(r, 1), :], (C, C))
        return rows.T, rows

    def prepare(it, carry):
        items = [(j, it * nbatch + i) for i in range(nbatch) for j in heads]
        sls = [pl.ds(pl.multiple_of(c * C, C), C) for _, c in items]
        g_f, g_b, bt_f, bt_b, dec_f, dec_b, kb_f, kb_b, aq = [], [], [], [], [], [], [], [], []
        for (j, c), sl in zip(items, sls):
            k = ks[j, sl, :]
            gfc, gfr = column(cf.at[j], c * 8 + 2)
            gbc, gbr = column(cb.at[j], c * 8 + 3)
            g_f.append(gfc)
            g_b.append(gbc)
            bt_f.append(column(sg.at[j], c * 8)[0])
            bt_b.append(column(sg.at[j], c * 8 + 1)[0])
            dec_f.append(jnp.where(low_i, jnp.exp(jnp.where(low_i, gfc - gfr, 0.0)), 0.0))
            dec_b.append(jnp.where(upp_i, jnp.exp(jnp.where(upp_i, gbc - gbr, 0.0)), 0.0))
            kb_f.append(k * bt_f[-1])
            kb_b.append(k * bt_b[-1])
            aq.append(_dot_nt(jnp.concatenate([kb_f[-1], kb_b[-1], qs[j, sl, :]], axis=0).astype(BF16),
                              k.astype(BF16)))
        lps = [jnp.where(low, x[:C] * df, 0.0) + jnp.where(upp, x[C:2 * C] * db, 0.0)
               for x, df, db in zip(aq, dec_f, dec_b)]
        inv = _packed_tri_inverse(lps, low, upp, bd16, rings)
        for n, ((j, c), sl) in enumerate(zip(items, sls)):
            q, k, v = qs[j, sl, :], ks[j, sl, :], vs[j, sl, :]
            for rev, g_c, kb, beta, dec, msk, u_s, wq_s, qk_s, kdt_s, egl_s in (
                    (False, g_f[n], kb_f[n], bt_f[n], dec_f[n], low, uf, wqf, qkf, kdtf, eglf),
                    (True, g_b[n], kb_b[n], bt_b[n], dec_b[n], upp, ub, wqb, qkb, kdtb, eglb)):
                eg = jnp.exp(g_c)
                rhs = jnp.concatenate([v * beta, kb * eg], axis=1)
                uw = rhs + _dot(jnp.where(msk, inv[n], 0.0).astype(BF16), rhs.astype(BF16))
                glast = g_c[0:1] if rev else g_c[C - 1:C]
                u_s[j, sl, :] = uw[:, :C]
                wq_s[j, pl.ds(pl.multiple_of(c * 2 * C, 2 * C), 2 * C), :] = jnp.concatenate(
                    [uw[:, C:], q * eg], axis=0).astype(BF16)
                qk_s[j, sl, :] = (aq[n][2 * C:] * dec).astype(BF16)
                kdt_s[j, sl, :] = (k * jnp.exp(glast - g_c)).T.astype(BF16)
                egl_s[j, pl.ds(c, 1), :] = jnp.exp(glast)
        return carry

    lax.fori_loop(0, nck // nbatch, prepare, 0)

    def body(i, carry):
        chains = []
        for j in heads:
            chains.append((j, i, uf, wqf, qkf, kdtf, eglf))
            chains.append((j, nck - 1 - i, ub, wqb, qkb, kdtb, eglb))
        sls = [pl.ds(pl.multiple_of(c * C, C), C) for _, c, *_ in chains]
        ws_qs = [_dot(wq_s[j, pl.ds(pl.multiple_of(c * 2 * C, 2 * C), 2 * C), :], st.astype(BF16))
                 for (j, c, _, wq_s, *_), st in zip(chains, carry)]
        vnb = [(u_s[j, sl, :] - x[:C]).astype(BF16) for (j, _, u_s, *_), sl, x in zip(chains, sls, ws_qs)]
        new = [st * egl_s[j, pl.ds(c, 1), :] + _dot(kdt_s[j, sl, :], v)
               for (j, c, _, _, _, kdt_s, egl_s), sl, st, v in zip(chains, sls, carry, vnb)]
        for (j, _, _, _, qk_s, _, _), sl, x, v in zip(chains, sls, ws_qs, vnb):
            os_[j, sl, :] += x[C:] + _dot(qk_s[j, sl, :], v)
        return tuple(new)

    zero = jnp.zeros((hd, hd), F32)
    lax.fori_loop(0, nck, body, (zero,) * (2 * GDN_PAIR))

    for j in heads:
        o = os_[j]
        o = o * lax.rsqrt(jnp.mean(o * o, axis=-1, keepdims=True) + RMS_EPS) * nw_ref[...]
        z = z_ref[:, j * hd:(j + 1) * hd].astype(F32)
        o_ref[:, j * hd:(j + 1) * hd] = (o * (z * jax.nn.sigmoid(z))).astype(BF16)


def _gdn(qkvz, gates3, conv_w, head_par, norm_w, bsz, seq):
    t = bsz * seq
    hd = GDN_HEAD_DIM
    nh = GDN_HEADS
    np_ = GDN_PAIR
    wd = np_ * hd
    npairs = nh // np_
    nck = seq // GDN_BLOCK
    col = lambda off: pl.BlockSpec((seq, wd), lambda b, p: (b, off * npairs + p))
    wcol = lambda off: pl.BlockSpec((GDN_CONV, wd), lambda b, p: (0, off * npairs + p))
    per_head = lambda rows, dt: pltpu.VMEM((np_, rows, hd), dt)
    return pl.pallas_call(
        _gdn_body,
        grid=(bsz, npairs),
        in_specs=[col(0), col(1), col(2), col(3), wcol(0), wcol(1), wcol(2),
                  pl.BlockSpec((nck, 8 * np_, GDN_BLOCK), lambda b, p: (b, p, 0)),
                  pl.BlockSpec((np_, 8, LANES), lambda b, p: (p, 0, 0)),
                  pl.BlockSpec((1, hd), lambda b, p: (0, 0))],
        out_specs=pl.BlockSpec((seq, wd), lambda b, p: (b, p)),
        out_shape=jax.ShapeDtypeStruct((t, nh * hd), BF16),
        scratch_shapes=([per_head(seq, F32)] * 4
                        + [per_head(nck * 8, F32)] * 3
                        + [per_head(seq, F32)] * 2
                        + [per_head(2 * seq, BF16)] * 2
                        + [per_head(seq, BF16)] * 2
                        + [per_head(seq, BF16)] * 2
                        + [per_head(nck, F32)] * 2
                        + [pltpu.VMEM((seq + 16, hd), F32)]),
        compiler_params=_cparams("parallel", "parallel"),
        name="gdn",
    )(qkvz, qkvz, qkvz, qkvz, conv_w, conv_w, conv_w, gates3, head_par, norm_w)


def _kv_body(m_ref, g_ref, wk_ref, wv_ref, k_ref, v_ref):
    mn = _rms(m_ref[...], g_ref[...]).astype(BF16)
    k_ref[...] = _dot(mn, wk_ref[...]).astype(BF16)
    v_ref[...] = _dot(mn, wv_ref[...]).astype(BF16)


def _mem_kv(mem2d, norm_w, wk, wv, tm=512):
    r, d = mem2d.shape
    full = lambda shape: pl.BlockSpec(shape, lambda i: (0,) * len(shape))
    tile = pl.BlockSpec((tm, d), lambda i: (i, 0))
    return pl.pallas_call(
        _kv_body,
        grid=(r // tm,),
        in_specs=[tile, full((1, d)), full(wk.shape), full(wv.shape)],
        out_specs=[tile, tile],
        out_shape=[jax.ShapeDtypeStruct((r, d), BF16)] * 2,
        compiler_params=_cparams("parallel"),
        name="mem_kv",
    )(mem2d, norm_w, wk, wv)


def _mix_xattn_body(x_ref, y5_ref, yg_ref, wmix_ref, g_ref, wq_ref, k_ref, v_ref, wo_ref,
                    gm_ref, whi_ref, wlo_ref, br_ref, tri_ref,
                    o_ref, mi_ref, mf_ref, cnt_ref, wmix_b, wq_b, wo_b, carry):
    @pl.when(pl.program_id(0) == 0)
    def _():
        wmix_b[...] = wmix_ref[...].astype(BF16)
        wq_b[...] = wq_ref[...].astype(BF16)
        wo_b[...] = wo_ref[...].astype(BF16)
        carry[...] = jnp.zeros_like(carry)

    x1 = (x_ref[...] + _dot(y5_ref[...], wmix_b[:S5_WIDTH, :]) + _dot(yg_ref[...], wmix_b[S5_WIDTH:, :]))
    xn = _rms(x1, g_ref[...]).astype(BF16)
    q = (_dot(xn, wq_b[...]) * (XA_HEAD_DIM ** -0.5)).astype(BF16)
    hsl = [slice(h * XA_HEAD_DIM, (h + 1) * XA_HEAD_DIM) for h in range(XA_HEADS)]
    sc = [_dot_nt(q[:, sl], k_ref[:, sl]) for sl in hsl]
    pr = [jnp.exp(s - jnp.max(s, axis=-1, keepdims=True)) for s in sc]
    pr = [p / jnp.sum(p, axis=-1, keepdims=True) for p in pr]
    heads = [_dot(p.astype(BF16), v_ref[:, sl]).astype(BF16) for p, sl in zip(pr, hsl)]
    x2 = x1 + _dot(jnp.concatenate(heads, axis=1), wo_b[...])
    o_ref[...] = x2
    mi, mf = _route(_rms(x2, gm_ref[...]), whi_ref[...], wlo_ref[...], br_ref[...], tri_ref[...], carry)
    mi_ref[...] = mi
    mf_ref[...] = mf
    cnt_ref[...] = carry[...]


def _mix_xattn(x2d, y5, yg, w_mix, norm_w, wq, kmem, vmem, wo, norm_moe, w_route, b_route, seq, mem_len, tm=512):
    t, d = x2d.shape
    per_b = seq // tm
    tri = jnp.triu(jnp.ones((tm, tm), BF16))
    w_hi = w_route.astype(BF16)
    w_lo = (w_route - w_hi.astype(F32)).astype(BF16)
    full = lambda shape: pl.BlockSpec(shape, lambda i: (0,) * len(shape))
    tile = lambda w: pl.BlockSpec((tm, w), lambda i: (i, 0))
    return pl.pallas_call(
        _mix_xattn_body,
        grid=(t // tm,),
        in_specs=[tile(d), tile(S5_WIDTH), tile(GDN_WIDTH),
                  full(w_mix.shape), full((1, d)), full(wq.shape),
                  pl.BlockSpec((mem_len, d), lambda i: (i // per_b, 0)),
                  pl.BlockSpec((mem_len, d), lambda i: (i // per_b, 0)),
                  full(wo.shape),
                  full((1, d)), full(w_route.shape), full(w_route.shape), full(b_route.shape), full((tm, tm))],
        out_specs=[tile(d), pl.BlockSpec((8, tm), lambda i: (0, i)), tile(LANES),
                   pl.BlockSpec((ROUTE_ROWS, LANES), lambda i: (0, 0))],
        out_shape=[jax.ShapeDtypeStruct((t, d), F32),
                   jax.ShapeDtypeStruct((8, t), I32),
                   jax.ShapeDtypeStruct((t, LANES), F32),
                   jax.ShapeDtypeStruct((ROUTE_ROWS, LANES), F32)],
        scratch_shapes=[pltpu.VMEM(w_mix.shape, BF16), pltpu.VMEM(wq.shape, BF16), pltpu.VMEM(wo.shape, BF16),
                        pltpu.VMEM((ROUTE_ROWS, LANES), F32)],
        compiler_params=_cparams("arbitrary"),
        name="mix_xattn",
    )(x2d, y5, yg, w_mix, norm_w, wq, kmem, vmem, wo, norm_moe, w_hi, w_lo, b_route, tri)


ROUTE_EXPERT_LANE0 = 4


ROUTE_ROWS = 40


def _route(xn, w_hi, w_lo, bias, tri, carry):
    x_hi = xn.astype(BF16)
    x_lo = (xn - x_hi.astype(F32)).astype(BF16)
    logits = (_dot_nt(w_hi, x_hi) + _dot_nt(w_hi, x_lo) + _dot_nt(w_lo, x_hi))[:ROUTE_ROWS] + bias[:ROUTE_ROWS, 0:1]
    tm = logits.shape[1]
    row = lax.broadcasted_iota(I32, (ROUTE_ROWS, tm), 0)
    neg = jnp.float32(-jnp.inf)
    big = jnp.int32(LANES)

    def top(vals):
        m = jnp.max(vals, axis=0, keepdims=True)
        idx = jnp.min(jnp.where(vals == m, row, big), axis=0, keepdims=True)
        return m, idx

    is_g = row < MOE_GROUPS
    gl = jnp.where(is_g, logits, neg)
    gmax, gidx = top(gl)
    p_top = 1.0 / jnp.sum(jnp.where(is_g, jnp.exp(gl - gmax), 0.0), axis=0, keepdims=True)
    erow = row - ROUTE_EXPERT_LANE0
    in_grp = jnp.logical_and(jnp.logical_and(erow >= 0, erow < MOE_EXPERTS), (erow // MOE_PER_GROUP) == gidx)
    es = jnp.where(in_grp, logits, neg)
    m1, i1 = top(es)
    m2, i2 = top(jnp.where(row == i1, neg, es))
    e21 = jnp.exp(m2 - m1)
    w1 = p_top / (1.0 + e21)
    w2 = p_top * e21 / (1.0 + e21)

    a1 = (row == i1).astype(F32)
    a2 = (row == i2).astype(F32)
    both = a1 + a2
    before = _dot(both.astype(BF16), tri) - both + carry[:, 0:1]
    r1 = jnp.sum(a1 * before, axis=0, keepdims=True).astype(I32)
    r2 = jnp.sum(a2 * before, axis=0, keepdims=True).astype(I32)
    carry[...] = carry[...] + jnp.sum(both, axis=1, keepdims=True)
    row8 = lax.broadcasted_iota(I32, (8, tm), 0)
    mi = jnp.where(row8 == 0, i1 - ROUTE_EXPERT_LANE0,
                   jnp.where(row8 == 1, i2 - ROUTE_EXPERT_LANE0, jnp.where(row8 == 2, r1, jnp.where(row8 == 3, r2, 0))))
    rowl = lax.broadcasted_iota(I32, (LANES, tm), 0)
    wt = jnp.where(rowl == 0, w1, jnp.where(rowl == 1, w2, 0.0))
    mf = jnp.concatenate([wt[:, j * LANES:(j + 1) * LANES].T for j in range(tm // LANES)], axis=0)
    return mi, mf


def _dest_body(mi_ref, off_ref, da_ref, db_ref):
    tm = da_ref.shape[2]
    mi = mi_ref[...]
    n = mi.shape[1]
    row = lax.broadcasted_iota(I32, (MOE_EXPERTS, n), 0)
    off = off_ref[:, 0:1]
    d0 = jnp.sum(jnp.where(row == mi[0:1], off, 0), axis=0, keepdims=True) + mi[2:3]
    d1 = jnp.sum(jnp.where(row == mi[1:2], off, 0), axis=0, keepdims=True) + mi[3:4]
    for s in range(da_ref.shape[0]):
        da_ref[s] = d0[:, s * tm:(s + 1) * tm]
        db_ref[s] = d1[:, s * tm:(s + 1) * tm]


def _dest_rows(mi, offsets_col, tm):
    t = mi.shape[1]
    tiles_per_step = math.gcd(8, t // tm)
    out = pl.BlockSpec((tiles_per_step, 1, tm), lambda i: (i, 0, 0))
    return pl.pallas_call(
        _dest_body,
        grid=(t // (tm * tiles_per_step),),
        in_specs=[pl.BlockSpec((8, tm * tiles_per_step), lambda i: (0, i)),
                  pl.BlockSpec((MOE_EXPERTS, LANES), lambda i: (0, 0))],
        out_specs=[out, out],
        out_shape=[jax.ShapeDtypeStruct((t // tm, 1, tm), I32)] * 2,
        compiler_params=_cparams("parallel"),
        name="moe_dest",
    )(mi, offsets_col)


def _dispatch_body(zs_ref, na_ref, da_ref, db_ref, xn_ref, xs_ref, zbuf, sem, zsem):
    tm = xn_ref.shape[0]
    n_tiles = xs_ref.shape[0] // tm

    @pl.when(pl.program_id(0) == 0)
    def _():
        zbuf[...] = jnp.zeros_like(zbuf)

        def fill(row0):
            return pltpu.make_async_copy(zbuf, xs_ref.at[pl.ds(pl.multiple_of(row0, tm), tm), :], zsem)

        def tail(e, c):
            @pl.when(zs_ref[e] >= 0)
            def _():
                fill(zs_ref[e]).start()
            return c

        def unused(j, c):
            fill(j * tm).start()
            return c

        def drain(j, c):
            fill(0).wait()
            return c

        lax.fori_loop(0, MOE_EXPERTS, tail, 0)
        lax.fori_loop(na_ref[0], n_tiles, unused, 0)
        lax.fori_loop(0, na_ref[1] + n_tiles - na_ref[0], drain, 0)

    def start(r, c):
        for k, d_ref in enumerate((da_ref, db_ref)):
            pltpu.make_async_copy(xn_ref.at[pl.ds(r, 1), :], xs_ref.at[pl.ds(d_ref[0, r], 1), :],
                                  sem.at[k]).start(priority=k)
        return c

    lax.fori_loop(0, tm, start, 0, unroll=8)
    for k in range(MOE_TOPK):
        pltpu.make_async_copy(xn_ref, xs_ref.at[pl.ds(0, tm), :], sem.at[k]).wait()


def _dispatch(zero_start, n_active, dest_a, dest_b, xn, n_rows, tm):
    t, dw = xn.shape
    smem_row = pl.BlockSpec((None, 1, tm), lambda i, zs, na: (i, 0, 0), memory_space=pltpu.SMEM)
    grid_spec = pltpu.PrefetchScalarGridSpec(
        num_scalar_prefetch=2,
        grid=(t // tm,),
        in_specs=[smem_row, smem_row, pl.BlockSpec((tm, dw), lambda i, zs, na: (i, 0))],
        out_specs=pl.BlockSpec(memory_space=pl.ANY),
        scratch_shapes=[pltpu.VMEM((tm, dw), xn.dtype), pltpu.SemaphoreType.DMA((MOE_TOPK,)),
                        pltpu.SemaphoreType.DMA(())],
    )
    return pl.pallas_call(
        _dispatch_body,
        grid_spec=grid_spec,
        out_shape=jax.ShapeDtypeStruct((n_rows, dw), xn.dtype),
        compiler_params=_cparams("arbitrary"),
        name="moe_dispatch",
    )(zero_start, n_active, dest_a, dest_b, xn)


def _experts_body(te_ref, na_ref, x_ref, g_ref, wg_ref, wu_ref, wd_ref, y_ref, wg_b, wu_b, wd_b):
    i = pl.program_id(0)

    @pl.when(i < na_ref[0])
    def _():
        @pl.when(jnp.logical_or(i == 0, te_ref[i] != te_ref[jnp.maximum(i - 1, 0)]))
        def _():
            wg_b[...] = wg_ref[...].astype(BF16)
            wu_b[...] = wu_ref[...].astype(BF16)
            wd_b[...] = wd_ref[...].astype(BF16)

        x = _rms(x_ref[...], g_ref[...]).astype(BF16)
        gt = _dot(x, wg_b[...])
        up = _dot(x, wu_b[...])
        hid = (gt * jax.nn.sigmoid(gt) * up).astype(BF16)
        y_ref[...] = _dot(hid, wd_b[...])

    @pl.when(i >= na_ref[0])
    def _():
        y_ref[...] = jnp.zeros_like(y_ref)


def _experts(tile_expert, n_active, xs, norm_w, w_gate, w_up, w_down, tm):
    r, d = xs.shape
    f = w_gate.shape[2]
    row_tile = lambda i, te, na: (jnp.minimum(i, na[0] - 1), 0)
    grid_spec = pltpu.PrefetchScalarGridSpec(
        num_scalar_prefetch=2,
        grid=(r // tm,),
        in_specs=[pl.BlockSpec((tm, d), row_tile),
                  pl.BlockSpec((1, d), lambda i, te, na: (0, 0)),
                  pl.BlockSpec((None, d, f), lambda i, te, na: (te[i], 0, 0)),
                  pl.BlockSpec((None, d, f), lambda i, te, na: (te[i], 0, 0)),
                  pl.BlockSpec((None, f, d), lambda i, te, na: (te[i], 0, 0))],
        out_specs=pl.BlockSpec((tm, d), lambda i, te, na: (i, 0)),
        scratch_shapes=[pltpu.VMEM((d, f), BF16), pltpu.VMEM((d, f), BF16), pltpu.VMEM((f, d), BF16)],
    )
    return pl.pallas_call(
        _experts_body,
        grid_spec=grid_spec,
        out_shape=jax.ShapeDtypeStruct((r, d), xs.dtype),
        compiler_params=_cparams("arbitrary"),
        name="moe_experts",
    )(tile_expert, n_active, xs, norm_w, w_gate, w_up, w_down)


def _combine_body(da_ref, db_ref, na_ref, nb_ref, x_ref, mf_ref, g_ref, ys_ref, o_ref, buf, sem):
    tm = x_ref.shape[0]
    i = pl.program_id(0)
    slot = lax.rem(i, 2)

    def gather(d_refs, s):
        def start(r, c):
            for k, d_ref in enumerate(d_refs):
                pltpu.make_async_copy(ys_ref.at[pl.ds(d_ref[0, r], 1), :],
                                      buf.at[s, k, pl.ds(r, 1), :], sem.at[s, k]).start(priority=k)
            return c

        lax.fori_loop(0, tm, start, 0, unroll=8)

    @pl.when(i == 0)
    def _():
        gather((da_ref, db_ref), 0)

    @pl.when(i + 1 < pl.num_programs(0))
    def _():
        gather((na_ref, nb_ref), 1 - slot)

    for k in range(MOE_TOPK):
        pltpu.make_async_copy(ys_ref.at[pl.ds(0, tm), :], buf.at[slot, k], sem.at[slot, k]).wait()
    mf = mf_ref[...]
    y = x_ref[...] + mf[:, 0:1] * buf[slot, 0] + mf[:, 1:2] * buf[slot, 1]
    o_ref[...] = _rms(y, g_ref[...])


def _combine(dest_a, dest_b, x2d, mf, norm_w, ys, tm):
    t, d = x2d.shape
    last = t // tm - 1
    cur = pl.BlockSpec((None, 1, tm), lambda i: (i, 0, 0), memory_space=pltpu.SMEM)
    nxt = pl.BlockSpec((None, 1, tm), lambda i: (jnp.minimum(i + 1, last), 0, 0), memory_space=pltpu.SMEM)
    return pl.pallas_call(
        _combine_body,
        grid=(t // tm,),
        in_specs=[cur, cur, nxt, nxt,
                  pl.BlockSpec((tm, d), lambda i: (i, 0)),
                  pl.BlockSpec((tm, LANES), lambda i: (i, 0)),
                  pl.BlockSpec((1, d), lambda i: (0, 0)),
                  pl.BlockSpec(memory_space=pl.ANY)],
        out_specs=pl.BlockSpec((tm, d), lambda i: (i, 0)),
        out_shape=jax.ShapeDtypeStruct((t, d), F32),
        scratch_shapes=[pltpu.VMEM((2, MOE_TOPK, tm, ys.shape[1]), ys.dtype), pltpu.SemaphoreType.DMA((2, MOE_TOPK))],
        compiler_params=_cparams("arbitrary"),
        name="moe_combine",
    )(dest_a, dest_b, dest_a, dest_b, x2d, mf, norm_w, ys)


MOE_ROW_TILE = 256


def _moe(x2d, mi, mf, cnt, norm_w, w_gate, w_up, w_down, norm_final):
    t, d = x2d.shape
    tm = MOE_ROW_TILE
    counts = cnt[ROUTE_EXPERT_LANE0:ROUTE_EXPERT_LANE0 + MOE_EXPERTS, 0].astype(I32)
    padded = ((counts + tm - 1) // tm) * tm
    ends = jnp.cumsum(padded)
    offsets = ends - padded
    n_tiles = (MOE_TOPK * t + MOE_EXPERTS * (tm - 1)) // tm
    tile_start = jnp.arange(n_tiles, dtype=I32) * tm
    tile_expert = jnp.minimum(jnp.sum((ends[None, :] <= tile_start[:, None]).astype(I32), axis=1), MOE_EXPERTS - 1)
    n_info = jnp.stack([ends[-1] // tm, jnp.sum((counts > 0).astype(I32))]).astype(I32)
    zero_start = jnp.where(counts > 0, ends - tm, -1).astype(I32)
    offsets_col = jnp.broadcast_to(offsets[:, None], (MOE_EXPERTS, LANES))
    dest_a, dest_b = _dest_rows(mi, offsets_col, tm)
    xs = _dispatch(zero_start, n_info, dest_a, dest_b, x2d, n_tiles * tm, tm)
    ys = _experts(tile_expert, n_info, xs, norm_w, w_gate, w_up, w_down, tm)
    return _combine(dest_a, dest_b, x2d, mf, norm_final, ys, tm)


def kernel(x, mem, norm_mix, w_in, w_out,
           s5_lam_re_f, s5_lam_im_f, s5_log_step_f, s5_b_re_f, s5_b_im_f, s5_c_re_f, s5_c_im_f,
           s5_lam_re_b, s5_lam_im_b, s5_log_step_b, s5_b_re_b, s5_b_im_b, s5_c_re_b, s5_c_im_b,
           s5_d, s5_w_glu, s5_b_glu, s5_norm,
           gdn_conv, gdn_a_log_f, gdn_dt_bias_f, gdn_a_log_b, gdn_dt_bias_b, gdn_norm,
           norm_xattn, norm_mem, xa_wq, xa_wk, xa_wv, xa_wo,
           norm_moe, router_group_w, router_group_b, router_expert_w, router_expert_b,
           moe_w_gate, moe_w_up, moe_w_down, norm_final):
    bsz, seq, d = x.shape
    t = bsz * seq
    l = 0
    x2d = x.reshape(t, d)
    wi = w_in[l]
    wg = wi[:, S5_WIDTH + 4 * GDN_WIDTH:].reshape(d, 4, GDN_HEADS)
    wg = jnp.pad(jnp.swapaxes(wg, 1, 2), ((0, 0), (0, 0), (0, 4))).reshape(d, GDN_HEADS * 8)
    wgt = wg.T
    u3, qkvz, gates = _in_proj(x2d, norm_mix[l][None], wi, wgt)
    s5p = dict(lam_re_f=s5_lam_re_f[l], lam_im_f=s5_lam_im_f[l], log_step_f=s5_log_step_f[l],
               b_re_f=s5_b_re_f[l], b_im_f=s5_b_im_f[l], c_re_f=s5_c_re_f[l], c_im_f=s5_c_im_f[l],
               lam_re_b=s5_lam_re_b[l], lam_im_b=s5_lam_im_b[l], log_step_b=s5_log_step_b[l],
               b_re_b=s5_b_re_b[l], b_im_b=s5_b_im_b[l], c_re_b=s5_c_re_b[l], c_im_b=s5_c_im_b[l], d=s5_d[l])
    g3 = _s5_scan(u3, s5p, seq // S5_CHUNK)
    y_s5 = _s5_post(g3, s5_w_glu[l].T.astype(BF16), s5_b_glu[l][:, None], s5_norm[l][:, None])

    head_par = jnp.stack([gdn_a_log_f[l], gdn_dt_bias_f[l], gdn_a_log_b[l], gdn_dt_bias_b[l]], axis=1)
    head_par = jnp.broadcast_to(jnp.pad(head_par, ((0, 0), (0, 4)))[:, :, None], (GDN_HEADS, 8, LANES))
    y_gdn = _gdn(qkvz, gates, gdn_conv[l], head_par, gdn_norm[l][None], bsz, seq)

    mem_len = mem.shape[1]
    kmem, vmem = _mem_kv(mem.reshape(bsz * mem_len, d), norm_mem[l][None],
                         xa_wk[l].astype(BF16), xa_wv[l].astype(BF16))
    n_pad = LANES - MOE_GROUPS - MOE_EXPERTS
    w_route = jnp.pad(jnp.concatenate([router_group_w[l], router_expert_w[l]], axis=1), ((0, 0), (0, n_pad))).T
    b_route = jnp.pad(jnp.concatenate([router_group_b[l], router_expert_b[l]]), (0, n_pad))
    b_route = jnp.broadcast_to(b_route[:, None], (LANES, LANES))
    x2, mi, mf, cnt = _mix_xattn(x2d, y_s5, y_gdn, w_out[l], norm_xattn[l][None], xa_wq[l], kmem, vmem, xa_wo[l],
                                 norm_moe[l][None], w_route, b_route, seq, mem_len)
    y = _moe(x2, mi, mf, cnt, norm_moe[l][None], moe_w_gate[l], moe_w_up[l], moe_w_down[l], norm_final[None])
    return y.reshape(bsz, seq, d)
```

```python
import functools
import math

import jax
import jax.numpy as jnp
from jax import lax
from jax.experimental import pallas as pl
from jax.experimental.pallas import tpu as pltpu

F32 = jnp.float32
BF16 = jnp.bfloat16
I32 = jnp.int32

D_MODEL = 1024
S5_WIDTH = 512
S5_GROUP = 16
S5_GROUPS = 32
S5_STATE = 64
S5_CHUNK = 128
GDN_HEADS = 4
GDN_HEAD_DIM = 128
GDN_WIDTH = 512
GDN_CONV = 5
GDN_CHUNK = 64
XA_HEADS = 4
XA_HEAD_DIM = 256
MOE_GROUPS = 4
MOE_PER_GROUP = 8
MOE_EXPERTS = 32
MOE_TOPK = 2
D_EXPERT = 256
RMS_EPS = 1e-6
L2_EPS = 1e-6
LANES = 128
VMEM_LIMIT = 56 * 1024 * 1024


def _cparams(*sem):
    return pltpu.CompilerParams(dimension_semantics=tuple(sem), vmem_limit_bytes=VMEM_LIMIT)


def _rms(x, gain):
    return x * lax.rsqrt(jnp.mean(x * x, axis=-1, keepdims=True) + RMS_EPS) * gain


def _dot(a, b):
    return jnp.dot(a, b, preferred_element_type=F32)


def _dot_nt(a, b):
    return lax.dot_general(a, b, (((1,), (1,)), ((), ())), preferred_element_type=F32)


def _dot_tn(a, b):
    return lax.dot_general(a, b, (((0,), (0,)), ((), ())), preferred_element_type=F32)


def _in_proj_body(x_ref, g_ref, w_ref, wgt_ref, u_ref, qkvz_ref, gates_ref, wut_b, wqkvz_b, wgt_b):
    @pl.when(pl.program_id(0) == 0)
    def _():
        wut_b[...] = w_ref[:, :S5_WIDTH].T.astype(BF16)
        wqkvz_b[...] = w_ref[:, S5_WIDTH:S5_WIDTH + 4 * GDN_WIDTH].astype(BF16)
        wgt_b[...] = wgt_ref[...].astype(BF16)

    h = _rms(x_ref[...], g_ref[...]).astype(BF16)
    ut = _dot_nt(wut_b[...], h)
    gt = _dot_nt(wgt_b[...], h)
    for j in range(u_ref.shape[0]):
        u_ref[j] = ut[:, j * S5_CHUNK:(j + 1) * S5_CHUNK]
        gates_ref[j] = gt[:, j * S5_CHUNK:(j + 1) * S5_CHUNK]
    qkvz_ref[...] = _dot(h, wqkvz_b[...]).astype(BF16)


def _in_proj(x2d, norm_w, w_in, wgt, tm=512):
    t = x2d.shape[0]
    nck = tm // S5_CHUNK
    nqkvz = 4 * GDN_WIDTH
    full = lambda shape: pl.BlockSpec(shape, lambda i: (0,) * len(shape))
    return pl.pallas_call(
        _in_proj_body,
        grid=(t // tm,),
        in_specs=[pl.BlockSpec((tm, D_MODEL), lambda i: (i, 0)),
                  full((1, D_MODEL)), full(w_in.shape), full(wgt.shape)],
        out_specs=[pl.BlockSpec((nck, S5_WIDTH, S5_CHUNK), lambda i: (i, 0, 0)),
                   pl.BlockSpec((tm, nqkvz), lambda i: (i, 0)),
                   pl.BlockSpec((nck, wgt.shape[0], S5_CHUNK), lambda i: (i, 0, 0))],
        out_shape=[jax.ShapeDtypeStruct((t // S5_CHUNK, S5_WIDTH, S5_CHUNK), F32),
                   jax.ShapeDtypeStruct((t, nqkvz), BF16),
                   jax.ShapeDtypeStruct((t // S5_CHUNK, wgt.shape[0], S5_CHUNK), F32)],
        scratch_shapes=[pltpu.VMEM((S5_WIDTH, D_MODEL), BF16), pltpu.VMEM((D_MODEL, nqkvz), BF16),
                        pltpu.VMEM(wgt.shape, BF16)],
        compiler_params=_cparams("arbitrary"),
        name="in_proj",
    )(x2d, norm_w, w_in, wgt)


def _cmul(ar, ai, br, bi):
    return ar * br - ai * bi, ar * bi + ai * br


def _cpow_int(lr, li, expo, nbits):
    res_r = jnp.ones(jnp.broadcast_shapes(lr.shape, expo.shape), F32)
    res_i = jnp.zeros_like(res_r)
    for b in range(nbits):
        bit = ((expo >> b) & 1) == 1
        nr, ni = _cmul(res_r, res_i, lr, li)
        res_r = jnp.where(bit, nr, res_r)
        res_i = jnp.where(bit, ni, res_i)
        if b + 1 < nbits:
            lr, li = _cmul(lr, li, lr, li)
    return res_r, res_i


def _lam_bar(re, im, step):
    er = jnp.exp(step * re)
    return er * jnp.cos(step * im), er * jnp.sin(step * im)


def _zoh_coef(re, im, lr, li):
    den = re * re + im * im
    return ((lr - 1.0) * re + li * im) / den, (li * re - (lr - 1.0) * im) / den


def _s5_body(u_ref, lrow_ref, lcol_ref, l256_ref, b_ref, bt_ref, c_ref, ct_ref, d_ref,
             o_ref, vf_ref, vb_ref, m_ref, win_ref, wout_ref, sf_ref, sb_ref, hf_ref, hb_ref, uc_ref, y_ref, *, nchunk):
    L = S5_CHUNK
    P = S5_STATE
    n_rows = u_ref.shape[0]
    nb = n_rows // nchunk
    lane_i = lax.broadcasted_iota(I32, (1, L), 1)

    lcol = lcol_ref[...]
    lbc_r, lbc_i = _lam_bar(lcol[:, 0:2], lcol[:, 2:4], lcol[:, 4:6])
    kc_r, kc_i = _zoh_coef(lcol[:, 0:2], lcol[:, 2:4], lbc_r, lbc_i)
    lrow = lrow_ref[...]
    lbr_r, lbr_i = _lam_bar(lrow[0:2], lrow[2:4], lrow[4:6])
    kr_r, kr_i = _zoh_coef(lrow[0:2], lrow[2:4], lbr_r, lbr_i)
    lf_r, lf_i, lb_r, lb_i = lbc_r[:, 0:1], lbc_i[:, 0:1], lbc_r[:, 1:2], lbc_i[:, 1:2]
    kfr_c, kfi_c, kbr_c, kbi_c = kc_r[:, 0:1], kc_i[:, 0:1], kc_r[:, 1:2], kc_i[:, 1:2]
    kfr_r, kfi_r, kbr_r, kbi_r = kr_r[0:1], kr_i[0:1], kr_r[1:2], kr_i[1:2]

    pwf_r, pwf_i = _cpow_int(lf_r, lf_i, lane_i, 7)
    rvf_r, rvf_i = _cpow_int(lf_r, lf_i, (L - 1) - lane_i, 7)
    pwb_r, pwb_i = _cpow_int(lb_r, lb_i, lane_i, 7)
    rvb_r, rvb_i = _cpow_int(lb_r, lb_i, L - lane_i, 8)
    nxf_r, nxf_i = _cmul(pwf_r, pwf_i, lf_r, lf_i)

    bf_r = kfr_c * b_ref[0] - kfi_c * b_ref[1]
    bf_i = kfr_c * b_ref[1] + kfi_c * b_ref[0]
    bb_r = kbr_c * b_ref[2] - kbi_c * b_ref[3]
    bb_i = kbr_c * b_ref[3] + kbi_c * b_ref[2]
    btf_r = kfr_r * bt_ref[0] - kfi_r * bt_ref[1]
    btf_i = kfr_r * bt_ref[1] + kfi_r * bt_ref[0]
    btb_r = kbr_r * bt_ref[2] - kbi_r * bt_ref[3]
    btb_i = kbr_r * bt_ref[3] + kbi_r * bt_ref[2]

    def taps(c_r, c_i, bt_r, bt_i, pw_r, pw_i):
        cb_r = (bt_r[:, None, :] * c_r[None, :, :] - bt_i[:, None, :] * c_i[None, :, :]).reshape(256, P)
        cb_i = (bt_r[:, None, :] * c_i[None, :, :] + bt_i[:, None, :] * c_r[None, :, :]).reshape(256, P)
        k = (jnp.dot(cb_r, pw_r, preferred_element_type=F32, precision=lax.Precision.HIGHEST)
             - jnp.dot(cb_i, pw_i, preferred_element_type=F32, precision=lax.Precision.HIGHEST))
        return k, jnp.sum(cb_r, axis=1, keepdims=True)

    kf, _ = taps(c_ref[0], c_ref[1], btf_r, btf_i, pwf_r, pwf_i)
    kb, kb0 = taps(c_ref[2], c_ref[3], btb_r, btb_i, rvb_r, rvb_i)
    is0 = lane_i == 0
    vf_ref[...] = kf + jnp.where(is0, kb0, 0.0)
    vb_ref[...] = jnp.where(is0, 0.0, kb)

    row_i = lax.broadcasted_iota(I32, (L, L), 0)
    col_i = lax.broadcasted_iota(I32, (L, L), 1)
    fwd_lane = col_i + row_i < L

    def build_ci(ci, carry):
        for co in range(S5_GROUP):
            r = ci * S5_GROUP + co
            taps_rows = jnp.where(fwd_lane, jnp.broadcast_to(vf_ref[pl.ds(r, 1), :], (L, L)),
                                  jnp.broadcast_to(vb_ref[pl.ds(r, 1), :], (L, L)))
            m_ref[pl.ds(pl.multiple_of(ci * L, L), L), co * L:(co + 1) * L] = pltpu.roll(
                taps_rows, 0, 1, stride=1, stride_axis=0).astype(BF16)
        return carry

    lax.fori_loop(0, S5_GROUP, build_ci, 0)

    for ci in range(S5_GROUP):
        sl = slice(ci * L, (ci + 1) * L)
        br, bi = bf_r[:, ci:ci + 1], bf_i[:, ci:ci + 1]
        win_ref[0 * P:1 * P, sl] = (rvf_r * br - rvf_i * bi).astype(BF16)
        win_ref[1 * P:2 * P, sl] = (rvf_r * bi + rvf_i * br).astype(BF16)
        br, bi = bb_r[:, ci:ci + 1], bb_i[:, ci:ci + 1]
        win_ref[2 * P:3 * P, sl] = (pwb_r * br - pwb_i * bi).astype(BF16)
        win_ref[3 * P:4 * P, sl] = (pwb_r * bi + pwb_i * br).astype(BF16)
    for co in range(S5_GROUP):
        sl = slice(co * L, (co + 1) * L)
        cr, ci_ = ct_ref[0][:, co:co + 1], ct_ref[1][:, co:co + 1]
        wout_ref[0 * P:1 * P, sl] = (cr * nxf_r - ci_ * nxf_i).astype(BF16)
        wout_ref[1 * P:2 * P, sl] = (-(cr * nxf_i + ci_ * nxf_r)).astype(BF16)
        cr, ci_ = ct_ref[2][:, co:co + 1], ct_ref[3][:, co:co + 1]
        wout_ref[2 * P:3 * P, sl] = (cr * rvb_r - ci_ * rvb_i).astype(BF16)
        wout_ref[3 * P:4 * P, sl] = (-(cr * rvb_i + ci_ * rvb_r)).astype(BF16)

    for ci in range(S5_GROUP):
        uc_ref[ci] = u_ref[:, ci, :]
    ucat = jnp.concatenate([uc_ref[ci].astype(BF16) for ci in range(S5_GROUP)], axis=1)

    summ = _dot_nt(ucat, win_ref[...])
    sf_ref[...] = summ[:, :2 * P]
    sb_ref[...] = summ[:, 2 * P:]
    nblk = 2 * L
    for j in range(S5_GROUP * L // nblk):
        y_ref[:, j * nblk:(j + 1) * nblk] = _dot(ucat, m_ref[:, j * nblk:(j + 1) * nblk])
    l256 = l256_ref[...]
    a_mul, a_im = _lam_bar(l256[0:1], l256[1:2], l256[2:3])
    for _ in range(7):
        a_mul, a_im = _cmul(a_mul, a_im, a_mul, a_im)
    lane256 = lax.broadcasted_iota(I32, (1, 4 * P), 1)
    b_mul = jnp.where((lane256 // P) % 2 == 0, -a_im, a_im)

    hf = jnp.zeros((nb, 2 * P), F32)
    hb = jnp.zeros((nb, 2 * P), F32)
    for c in range(nchunk):
        cr = nchunk - 1 - c
        rows_f = pl.ds(c, nb, stride=nchunk)
        rows_b = pl.ds(cr, nb, stride=nchunk)
        hf_ref[rows_f, :] = hf
        hb_ref[rows_b, :] = hb
        hf = a_mul[:, :2 * P] * hf + b_mul[:, :2 * P] * pltpu.roll(hf, P, 1) + sf_ref[rows_f, :]
        hb = a_mul[:, 2 * P:] * hb + b_mul[:, 2 * P:] * pltpu.roll(hb, P, 1) + sb_ref[rows_b, :]
    hprev = jnp.concatenate([hf_ref[...], hb_ref[...]], axis=1).astype(BF16)

    for j in range(S5_GROUP * L // nblk):
        y = y_ref[:, j * nblk:(j + 1) * nblk] + _dot(hprev, wout_ref[:, j * nblk:(j + 1) * nblk])
        for q in range(nblk // L):
            co = j * (nblk // L) + q
            yc = y[:, q * L:(q + 1) * L] + d_ref[co:co + 1, :] * uc_ref[co]
            o_ref[:, co, :] = 0.5 * yc * (1.0 + lax.erf(yc * (2.0 ** -0.5)))


def _s5_scan(u3, p, nchunk):
    n = u3.shape[0]
    g, grp, st, L = S5_GROUPS, S5_GROUP, S5_STATE, S5_CHUNK
    step_f = jnp.exp(p["log_step_f"])[:, None] * jnp.ones((1, st), F32)
    step_b = jnp.exp(p["log_step_b"])[:, None] * jnp.ones((1, st), F32)
    zeros = jnp.zeros((g, st), F32)
    lrow = jnp.stack([p["lam_re_f"], p["lam_re_b"], p["lam_im_f"], p["lam_im_b"], step_f, step_b, zeros, zeros], axis=1)
    lcol = jnp.swapaxes(lrow, 1, 2)
    cat4 = lambda f, b: jnp.concatenate([f, f, b, b], axis=1)
    z256 = jnp.zeros((g, 4 * st), F32)
    l256 = jnp.stack([cat4(p["lam_re_f"], p["lam_re_b"]), cat4(p["lam_im_f"], p["lam_im_b"]),
                      cat4(step_f, step_b)] + [z256] * 5, axis=1)
    b4 = jnp.stack([p["b_re_f"], p["b_im_f"], p["b_re_b"], p["b_im_b"]], axis=1)
    bt4 = jnp.swapaxes(b4, 2, 3)
    c4 = jnp.stack([p["c_re_f"], p["c_im_f"], p["c_re_b"], p["c_im_b"]], axis=1)
    ct4 = jnp.swapaxes(c4, 2, 3)
    dbc = jnp.broadcast_to(p["d"].reshape(g, grp, 1), (g, grp, L))
    per_g = lambda *shape: pl.BlockSpec((None,) + shape, lambda i: (i,) + (0,) * len(shape))
    return pl.pallas_call(
        functools.partial(_s5_body, nchunk=nchunk),
        grid=(g,),
        in_specs=[pl.BlockSpec((n, grp, L), lambda i: (0, i, 0)),
                  per_g(8, st), per_g(st, 8), per_g(8, 4 * st), per_g(4, st, grp), per_g(4, grp, st),
                  per_g(4, grp, st), per_g(4, st, grp), per_g(grp, L)],
        out_specs=pl.BlockSpec((n, grp, L), lambda i: (0, i, 0)),
        out_shape=jax.ShapeDtypeStruct(u3.shape, F32),
        scratch_shapes=[pltpu.VMEM((grp * grp, L), F32), pltpu.VMEM((grp * grp, L), F32),
                        pltpu.VMEM((grp * L, grp * L), BF16),
                        pltpu.VMEM((4 * st, grp * L), BF16), pltpu.VMEM((4 * st, grp * L), BF16),
                        pltpu.VMEM((n, 2 * st), F32), pltpu.VMEM((n, 2 * st), F32),
                        pltpu.VMEM((n, 2 * st), F32), pltpu.VMEM((n, 2 * st), F32),
                        pltpu.VMEM((grp, n, L), F32), pltpu.VMEM((n, grp * L), F32)],
        compiler_params=_cparams("parallel"),
        name="s5_scan",
    )(u3, lrow, lcol, l256, b4, bt4, c4, ct4, dbc)


def _s5_post_body(g_ref, wt_ref, b_ref, nw_ref, o_ref):
    for j in range(g_ref.shape[0]):
        g = g_ref[j]
        z = _dot(wt_ref[...], g.astype(BF16)) + b_ref[...]
        y = g * jax.nn.sigmoid(z)
        y = y * lax.rsqrt(jnp.mean(y * y, axis=0, keepdims=True) + RMS_EPS) * nw_ref[...]
        o_ref[j * S5_CHUNK:(j + 1) * S5_CHUNK, :] = y.T.astype(BF16)


def _s5_post(g3, w_glu_t, b_glu_col, norm_col, nck=4):
    n = g3.shape[0]
    full = lambda shape: pl.BlockSpec(shape, lambda i: (0,) * len(shape))
    return pl.pallas_call(
        _s5_post_body,
        grid=(n // nck,),
        in_specs=[pl.BlockSpec((nck, S5_WIDTH, S5_CHUNK), lambda i: (i, 0, 0)),
                  full(w_glu_t.shape), full(b_glu_col.shape), full(norm_col.shape)],
        out_specs=pl.BlockSpec((nck * S5_CHUNK, S5_WIDTH), lambda i: (i, 0)),
        out_shape=jax.ShapeDtypeStruct((n * S5_CHUNK, S5_WIDTH), BF16),
        compiler_params=_cparams("parallel"),
        name="s5_post",
    )(g3, w_glu_t, b_glu_col, norm_col)


GDN_BLOCK = 128
GDN_PAIR = 2
GDN_PREP_BATCH = 8


def _packed_tri_inverse(lps, low, upp, bd16, rings):
    def pk(xs, ys):
        outs = []
        for a, b in zip(xs, ys):
            lhs = jnp.concatenate([jnp.where(low, a, 0.0), jnp.where(upp, a, 0.0)], axis=1).astype(BF16)
            rhs = jnp.concatenate([jnp.where(low, b, 0.0), jnp.where(upp, b, 0.0)], axis=0).astype(BF16)
            outs.append(_dot(lhs, rhs))
        return outs

    d = [jnp.where(bd16, lp, 0.0) for lp in lps]
    d2 = pk(d, d)
    d4 = pk(d2, d2)
    d8 = pk(d4, d4)
    a = [y - x - p for x, y, p in zip(d, d2, pk(d, d2))]
    a = [x + y + p for x, y, p in zip(a, d4, pk(a, d4))]
    a = [x + y + p for x, y, p in zip(a, d8, pk(a, d8))]
    for ring in rings:
        n = [jnp.where(ring, lp, 0.0) for lp in lps]
        t = [x + p for x, p in zip(n, pk(a, n))]
        a = [x - y - p for x, y, p in zip(a, t, pk(t, a))]
    return a


def _gdn_body(q_ref, k_ref, v_ref, z_ref, wq_ref, wk_ref, wv_ref, g_ref, hp_ref, nw_ref, o_ref,
              qs, ks, vs, os_, sg, cf, cb, uf, ub, wqf, wqb, qkf, qkb, kdtf, kdtb, eglf, eglb, xpad):
    seq = q_ref.shape[0]
    C = GDN_BLOCK
    hd = GDN_HEAD_DIM
    nck = seq // C
    heads = range(GDN_PAIR)

    pad = 8
    half = (GDN_CONV - 1) // 2
    xpad[0:pad, :] = jnp.zeros((pad, LANES), F32)
    xpad[pad + seq:2 * pad + seq, :] = jnp.zeros((pad, LANES), F32)

    def conv_silu(x_ref, w_ref, j):
        cols = pl.ds(pl.multiple_of(j * hd, hd), hd)
        xpad[pad:pad + seq, :] = x_ref[:, cols].astype(F32)
        w = w_ref[:, cols]
        acc = xpad[pad - half:pad - half + seq, :] * w[0:1]
        for tap in range(1, GDN_CONV):
            acc = acc + xpad[pad - half + tap:pad - half + tap + seq, :] * w[tap:tap + 1]
        return acc * jax.nn.sigmoid(acc)

    def l2n(x):
        return x * lax.rsqrt(jnp.sum(x * x, axis=-1, keepdims=True) + L2_EPS)

    def softplus(x):
        return jnp.maximum(x, 0.0) + jnp.log1p(jnp.exp(-jnp.abs(x)))

    lane = lax.broadcasted_iota(I32, (1, C), 1)

    def prologue(j, carry):
        qs[j] = l2n(conv_silu(q_ref, wq_ref, j)) * (hd ** -0.5)
        ks[j] = l2n(conv_silu(k_ref, wk_ref, j))
        vs[j] = conv_silu(v_ref, wv_ref, j)
        os_[j] = jnp.zeros((seq, hd), F32)
        g = g_ref[:, pl.ds(pl.multiple_of(8 * j, 8), 8), :].reshape(nck * 8, C)
        hp = hp_ref[j]
        sg[j] = jax.nn.sigmoid(g)
        gl_f = -jnp.exp(hp[0:1]) * softplus(g + hp[1:2])
        gl_b = -jnp.exp(hp[2:3]) * softplus(g + hp[3:4])
        sh = 1
        while sh < C:
            gl_f = gl_f + jnp.where(lane >= sh, pltpu.roll(gl_f, sh, 1), 0.0)
            gl_b = gl_b + jnp.where(lane < C - sh, pltpu.roll(gl_b, C - sh, 1), 0.0)
            sh *= 2
        cf[j] = gl_f
        cb[j] = gl_b
        return carry

    lax.fori_loop(0, GDN_PAIR, prologue, 0)

    ri = lax.broadcasted_iota(I32, (C, C), 0)
    ci = lax.broadcasted_iota(I32, (C, C), 1)
    low, upp = ri > ci, ri < ci
    low_i, upp_i = ri >= ci, ri <= ci
    same = lambda w: (ri // w) == (ci // w)
    bd16 = same(16)
    rings = []
    w = 32
    while w <= C:
        rings.append(jnp.logical_and(same(w), jnp.logical_not(same(w // 2))))
        w *= 2

    nbatch = math.gcd(GDN_PREP_BATCH // GDN_PAIR, nck)

    def column(ref, r):
        rows = jnp.broadcast_to(ref[pl.ds(r, 1), :], (C, C))
        return rows.T, rows

    def prepare(it, carry):
        items = [(j, it * nbatch + i) for i in range(nbatch) for j in heads]
        sls = [pl.ds(pl.multiple_of(c * C, C), C) for _, c in items]
        g_f, g_b, bt_f, bt_b, dec_f, dec_b, kb_f, kb_b, aq = [], [], [], [], [], [], [], [], []
        for (j, c), sl in zip(items, sls):
            k = ks[j, sl, :]
            gfc, gfr = column(cf.at[j], c * 8 + 2)
            gbc, gbr = column(cb.at[j], c * 8 + 3)
            g_f.append(gfc)
            g_b.append(gbc)
            bt_f.append(column(sg.at[j], c * 8)[0])
            bt_b.append(column(sg.at[j], c * 8 + 1)[0])
            dec_f.append(jnp.where(low_i, jnp.exp(jnp.where(low_i, gfc - gfr, 0.0)), 0.0))
            dec_b.append(jnp.where(upp_i, jnp.exp(jnp.where(upp_i, gbc - gbr, 0.0)), 0.0))
            kb_f.append(k * bt_f[-1])
            kb_b.append(k * bt_b[-1])
            aq.append(_dot_nt(jnp.concatenate([kb_f[-1], kb_b[-1], qs[j, sl, :]], axis=0).astype(BF16),
                              k.astype(BF16)))
        lps = [jnp.where(low, x[:C] * df, 0.0) + jnp.where(upp, x[C:2 * C] * db, 0.0)
               for x, df, db in zip(aq, dec_f, dec_b)]
        inv = _packed_tri_inverse(lps, low, upp, bd16, rings)
        for n, ((j, c), sl) in enumerate(zip(items, sls)):
            q, k, v = qs[j, sl, :], ks[j, sl, :], vs[j, sl, :]
            for rev, g_c, kb, beta, dec, msk, u_s, wq_s, qk_s, kdt_s, egl_s in (
                    (False, g_f[n], kb_f[n], bt_f[n], dec_f[n], low, uf, wqf, qkf, kdtf, eglf),
                    (True, g_b[n], kb_b[n], bt_b[n], dec_b[n], upp, ub, wqb, qkb, kdtb, eglb)):
                eg = jnp.exp(g_c)
                rhs = jnp.concatenate([v * beta, kb * eg], axis=1)
                uw = rhs + _dot(jnp.where(msk, inv[n], 0.0).astype(BF16), rhs.astype(BF16))
                glast = g_c[0:1] if rev else g_c[C - 1:C]
                u_s[j, sl, :] = uw[:, :C]
                wq_s[j, pl.ds(pl.multiple_of(c * 2 * C, 2 * C), 2 * C), :] = jnp.concatenate(
                    [uw[:, C:], q * eg], axis=0).astype(BF16)
                qk_s[j, sl, :] = (aq[n][2 * C:] * dec).astype(BF16)
                kdt_s[j, sl, :] = (k * jnp.exp(glast - g_c)).T.astype(BF16)
                egl_s[j, pl.ds(c, 1), :] = jnp.exp(glast)
        return carry

    lax.fori_loop(0, nck // nbatch, prepare, 0)

    def body(i, carry):
        chains = []
        for j in heads:
            chains.append((j, i, uf, wqf, qkf, kdtf, eglf))
            chains.append((j, nck - 1 - i, ub, wqb, qkb, kdtb, eglb))
        sls = [pl.ds(pl.multiple_of(c * C, C), C) for _, c, *_ in chains]
        ws_qs = [_dot(wq_s[j, pl.ds(pl.multiple_of(c * 2 * C, 2 * C), 2 * C), :], st.astype(BF16))
                 for (j, c, _, wq_s, *_), st in zip(chains, carry)]
        vnb = [(u_s[j, sl, :] - x[:C]).astype(BF16) for (j, _, u_s, *_), sl, x in zip(chains, sls, ws_qs)]
        new = [st * egl_s[j, pl.ds(c, 1), :] + _dot(kdt_s[j, sl, :], v)
               for (j, c, _, _, _, kdt_s, egl_s), sl, st, v in zip(chains, sls, carry, vnb)]
        for (j, _, _, _, qk_s, _, _), sl, x, v in zip(chains, sls, ws_qs, vnb):
            os_[j, sl, :] += x[C:] + _dot(qk_s[j, sl, :], v)
        return tuple(new)

    zero = jnp.zeros((hd, hd), F32)
    lax.fori_loop(0, nck, body, (zero,) * (2 * GDN_PAIR))

    for j in heads:
        o = os_[j]
        o = o * lax.rsqrt(jnp.mean(o * o, axis=-1, keepdims=True) + RMS_EPS) * nw_ref[...]
        z = z_ref[:, j * hd:(j + 1) * hd].astype(F32)
        o_ref[:, j * hd:(j + 1) * hd] = (o * (z * jax.nn.sigmoid(z))).astype(BF16)


def _gdn(qkvz, gates3, conv_w, head_par, norm_w, bsz, seq):
    t = bsz * seq
    hd = GDN_HEAD_DIM
    nh = GDN_HEADS
    np_ = GDN_PAIR
    wd = np_ * hd
    npairs = nh // np_
    nck = seq // GDN_BLOCK
    col = lambda off: pl.BlockSpec((seq, wd), lambda b, p: (b, off * npairs + p))
    wcol = lambda off: pl.BlockSpec((GDN_CONV, wd), lambda b, p: (0, off * npairs + p))
    per_head = lambda rows, dt: pltpu.VMEM((np_, rows, hd), dt)
    return pl.pallas_call(
        _gdn_body,
        grid=(bsz, npairs),
        in_specs=[col(0), col(1), col(2), col(3), wcol(0), wcol(1), wcol(2),
                  pl.BlockSpec((nck, 8 * np_, GDN_BLOCK), lambda b, p: (b, p, 0)),
                  pl.BlockSpec((np_, 8, LANES), lambda b, p: (p, 0, 0)),
                  pl.BlockSpec((1, hd), lambda b, p: (0, 0))],
        out_specs=pl.BlockSpec((seq, wd), lambda b, p: (b, p)),
        out_shape=jax.ShapeDtypeStruct((t, nh * hd), BF16),
        scratch_shapes=([per_head(seq, F32)] * 4
                        + [per_head(nck * 8, F32)] * 3
                        + [per_head(seq, F32)] * 2
                        + [per_head(2 * seq, BF16)] * 2
                        + [per_head(seq, BF16)] * 2
                        + [per_head(seq, BF16)] * 2
                        + [per_head(nck, F32)] * 2
                        + [pltpu.VMEM((seq + 16, hd), F32)]),
        compiler_params=_cparams("parallel", "parallel"),
        name="gdn",
    )(qkvz, qkvz, qkvz, qkvz, conv_w, conv_w, conv_w, gates3, head_par, norm_w)


def _kv_body(m_ref, g_ref, wk_ref, wv_ref, k_ref, v_ref):
    mn = _rms(m_ref[...], g_ref[...]).astype(BF16)
    k_ref[...] = _dot(mn, wk_ref[...]).astype(BF16)
    v_ref[...] = _dot(mn, wv_ref[...]).astype(BF16)


def _mem_kv(mem2d, norm_w, wk, wv, tm=512):
    r, d = mem2d.shape
    full = lambda shape: pl.BlockSpec(shape, lambda i: (0,) * len(shape))
    tile = pl.BlockSpec((tm, d), lambda i: (i, 0))
    return pl.pallas_call(
        _kv_body,
        grid=(r // tm,),
        in_specs=[tile, full((1, d)), full(wk.shape), full(wv.shape)],
        out_specs=[tile, tile],
        out_shape=[jax.ShapeDtypeStruct((r, d), BF16)] * 2,
        compiler_params=_cparams("parallel"),
        name="mem_kv",
    )(mem2d, norm_w, wk, wv)


def _mix_xattn_body(x_ref, y5_ref, yg_ref, wmix_ref, g_ref, wq_ref, k_ref, v_ref, wo_ref,
                    gm_ref, whi_ref, wlo_ref, br_ref, tri_ref,
                    o_ref, mi_ref, mf_ref, cnt_ref, wmix_b, wq_b, wo_b, carry):
    @pl.when(pl.program_id(0) == 0)
    def _():
        wmix_b[...] = wmix_ref[...].astype(BF16)
        wq_b[...] = wq_ref[...].astype(BF16)
        wo_b[...] = wo_ref[...].astype(BF16)
        carry[...] = jnp.zeros_like(carry)

    x1 = (x_ref[...] + _dot(y5_ref[...], wmix_b[:S5_WIDTH, :]) + _dot(yg_ref[...], wmix_b[S5_WIDTH:, :]))
    xn = _rms(x1, g_ref[...]).astype(BF16)
    q = (_dot(xn, wq_b[...]) * (XA_HEAD_DIM ** -0.5)).astype(BF16)
    hsl = [slice(h * XA_HEAD_DIM, (h + 1) * XA_HEAD_DIM) for h in range(XA_HEADS)]
    sc = [_dot_nt(q[:, sl], k_ref[:, sl]) for sl in hsl]
    pr = [jnp.exp(s - jnp.max(s, axis=-1, keepdims=True)) for s in sc]
    pr = [p / jnp.sum(p, axis=-1, keepdims=True) for p in pr]
    heads = [_dot(p.astype(BF16), v_ref[:, sl]).astype(BF16) for p, sl in zip(pr, hsl)]
    x2 = x1 + _dot(jnp.concatenate(heads, axis=1), wo_b[...])
    o_ref[...] = x2
    mi, mf = _route(_rms(x2, gm_ref[...]), whi_ref[...], wlo_ref[...], br_ref[...], tri_ref[...], carry)
    mi_ref[...] = mi
    mf_ref[...] = mf
    cnt_ref[...] = carry[...]


def _mix_xattn(x2d, y5, yg, w_mix, norm_w, wq, kmem, vmem, wo, norm_moe, w_route, b_route, seq, mem_len, tm=512):
    t, d = x2d.shape
    per_b = seq // tm
    tri = jnp.triu(jnp.ones((tm, tm), BF16))
    w_hi = w_route.astype(BF16)
    w_lo = (w_route - w_hi.astype(F32)).astype(BF16)
    full = lambda shape: pl.BlockSpec(shape, lambda i: (0,) * len(shape))
    tile = lambda w: pl.BlockSpec((tm, w), lambda i: (i, 0))
    return pl.pallas_call(
        _mix_xattn_body,
        grid=(t // tm,),
        in_specs=[tile(d), tile(S5_WIDTH), tile(GDN_WIDTH),
                  full(w_mix.shape), full((1, d)), full(wq.shape),
                  pl.BlockSpec((mem_len, d), lambda i: (i // per_b, 0)),
                  pl.BlockSpec((mem_len, d), lambda i: (i // per_b, 0)),
                  full(wo.shape),
                  full((1, d)), full(w_route.shape), full(w_route.shape), full(b_route.shape), full((tm, tm))],
        out_specs=[tile(d), pl.BlockSpec((8, tm), lambda i: (0, i)), tile(LANES),
                   pl.BlockSpec((ROUTE_ROWS, LANES), lambda i: (0, 0))],
        out_shape=[jax.ShapeDtypeStruct((t, d), F32),
                   jax.ShapeDtypeStruct((8, t), I32),
                   jax.ShapeDtypeStruct((t, LANES), F32),
                   jax.ShapeDtypeStruct((ROUTE_ROWS, LANES), F32)],
        scratch_shapes=[pltpu.VMEM(w_mix.shape, BF16), pltpu.VMEM(wq.shape, BF16), pltpu.VMEM(wo.shape, BF16),
                        pltpu.VMEM((ROUTE_ROWS, LANES), F32)],
        compiler_params=_cparams("arbitrary"),
        name="mix_xattn",
    )(x2d, y5, yg, w_mix, norm_w, wq, kmem, vmem, wo, norm_moe, w_hi, w_lo, b_route, tri)


ROUTE_EXPERT_LANE0 = 4


ROUTE_ROWS = 40


def _route(xn, w_hi, w_lo, bias, tri, carry):
    x_hi = xn.astype(BF16)
    x_lo = (xn - x_hi.astype(F32)).astype(BF16)
    logits = (_dot_nt(w_hi, x_hi) + _dot_nt(w_hi, x_lo) + _dot_nt(w_lo, x_hi))[:ROUTE_ROWS] + bias[:ROUTE_ROWS, 0:1]
    tm = logits.shape[1]
    row = lax.broadcasted_iota(I32, (ROUTE_ROWS, tm), 0)
    neg = jnp.float32(-jnp.inf)
    big = jnp.int32(LANES)

    def top(vals):
        m = jnp.max(vals, axis=0, keepdims=True)
        idx = jnp.min(jnp.where(vals == m, row, big), axis=0, keepdims=True)
        return m, idx

    is_g = row < MOE_GROUPS
    gl = jnp.where(is_g, logits, neg)
    gmax, gidx = top(gl)
    p_top = 1.0 / jnp.sum(jnp.where(is_g, jnp.exp(gl - gmax), 0.0), axis=0, keepdims=True)
    erow = row - ROUTE_EXPERT_LANE0
    in_grp = jnp.logical_and(jnp.logical_and(erow >= 0, erow < MOE_EXPERTS), (erow // MOE_PER_GROUP) == gidx)
    es = jnp.where(in_grp, logits, neg)
    m1, i1 = top(es)
    m2, i2 = top(jnp.where(row == i1, neg, es))
    e21 = jnp.exp(m2 - m1)
    w1 = p_top / (1.0 + e21)
    w2 = p_top * e21 / (1.0 + e21)

    a1 = (row == i1).astype(F32)
    a2 = (row == i2).astype(F32)
    both = a1 + a2
    before = _dot(both.astype(BF16), tri) - both + carry[:, 0:1]
    r1 = jnp.sum(a1 * before, axis=0, keepdims=True).astype(I32)
    r2 = jnp.sum(a2 * before, axis=0, keepdims=True).astype(I32)
    carry[...] = carry[...] + jnp.sum(both, axis=1, keepdims=True)
    row8 = lax.broadcasted_iota(I32, (8, tm), 0)
    mi = jnp.where(row8 == 0, i1 - ROUTE_EXPERT_LANE0,
                   jnp.where(row8 == 1, i2 - ROUTE_EXPERT_LANE0, jnp.where(row8 == 2, r1, jnp.where(row8 == 3, r2, 0))))
    rowl = lax.broadcasted_iota(I32, (LANES, tm), 0)
    wt = jnp.where(rowl == 0, w1, jnp.where(rowl == 1, w2, 0.0))
    mf = jnp.concatenate([wt[:, j * LANES:(j + 1) * LANES].T for j in range(tm // LANES)], axis=0)
    return mi, mf


def _dest_body(mi_ref, off_ref, da_ref, db_ref):
    tm = da_ref.shape[2]
    mi = mi_ref[...]
    n = mi.shape[1]
    row = lax.broadcasted_iota(I32, (MOE_EXPERTS, n), 0)
    off = off_ref[:, 0:1]
    d0 = jnp.sum(jnp.where(row == mi[0:1], off, 0), axis=0, keepdims=True) + mi[2:3]
    d1 = jnp.sum(jnp.where(row == mi[1:2], off, 0), axis=0, keepdims=True) + mi[3:4]
    for s in range(da_ref.shape[0]):
        da_ref[s] = d0[:, s * tm:(s + 1) * tm]
        db_ref[s] = d1[:, s * tm:(s + 1) * tm]


def _dest_rows(mi, offsets_col, tm):
    t = mi.shape[1]
    tiles_per_step = math.gcd(8, t // tm)
    out = pl.BlockSpec((tiles_per_step, 1, tm), lambda i: (i, 0, 0))
    return pl.pallas_call(
        _dest_body,
        grid=(t // (tm * tiles_per_step),),
        in_specs=[pl.BlockSpec((8, tm * tiles_per_step), lambda i: (0, i)),
                  pl.BlockSpec((MOE_EXPERTS, LANES), lambda i: (0, 0))],
        out_specs=[out, out],
        out_shape=[jax.ShapeDtypeStruct((t // tm, 1, tm), I32)] * 2,
        compiler_params=_cparams("parallel"),
        name="moe_dest",
    )(mi, offsets_col)


def _dispatch_body(zs_ref, na_ref, da_ref, db_ref, xn_ref, xs_ref, zbuf, sem, zsem):
    tm = xn_ref.shape[0]
    te = zbuf.shape[0]
    n_tiles = xs_ref.shape[0] // te

    @pl.when(pl.program_id(0) == 0)
    def _():
        zbuf[...] = jnp.zeros_like(zbuf)

        def fill(row0):
            return pltpu.make_async_copy(zbuf, xs_ref.at[pl.ds(pl.multiple_of(row0, te), te), :], zsem)

        def tail(e, c):
            @pl.when(zs_ref[e] >= 0)
            def _():
                fill(zs_ref[e]).start()
            return c

        def unused(j, c):
            fill(j * te).start()
            return c

        def drain(j, c):
            fill(0).wait()
            return c

        lax.fori_loop(0, MOE_EXPERTS, tail, 0)
        lax.fori_loop(na_ref[0], n_tiles, unused, 0)
        lax.fori_loop(0, na_ref[1] + n_tiles - na_ref[0], drain, 0)

    def start(r, c):
        for k, d_ref in enumerate((da_ref, db_ref)):
            pltpu.make_async_copy(xn_ref.at[pl.ds(r, 1), :], xs_ref.at[pl.ds(d_ref[0, r], 1), :],
                                  sem.at[k]).start(priority=k)
        return c

    lax.fori_loop(0, tm, start, 0, unroll=8)
    for k in range(MOE_TOPK):
        pltpu.make_async_copy(xn_ref, xs_ref.at[pl.ds(0, tm), :], sem.at[k]).wait()


def _dispatch(zero_start, n_active, dest_a, dest_b, xn, n_rows, tm, te):
    t, dw = xn.shape
    smem_row = pl.BlockSpec((None, 1, tm), lambda i, zs, na: (i, 0, 0), memory_space=pltpu.SMEM)
    grid_spec = pltpu.PrefetchScalarGridSpec(
        num_scalar_prefetch=2,
        grid=(t // tm,),
        in_specs=[smem_row, smem_row, pl.BlockSpec((tm, dw), lambda i, zs, na: (i, 0))],
        out_specs=pl.BlockSpec(memory_space=pl.ANY),
        scratch_shapes=[pltpu.VMEM((te, dw), xn.dtype), pltpu.SemaphoreType.DMA((MOE_TOPK,)),
                        pltpu.SemaphoreType.DMA(())],
    )
    return pl.pallas_call(
        _dispatch_body,
        grid_spec=grid_spec,
        out_shape=jax.ShapeDtypeStruct((n_rows, dw), xn.dtype),
        compiler_params=_cparams("arbitrary"),
        name="moe_dispatch",
    )(zero_start, n_active, dest_a, dest_b, xn)


def _experts_body(te_ref, na_ref, x_ref, g_ref, wg_ref, wu_ref, wd_ref, y_ref, wg_b, wu_b, wd_b):
    i = pl.program_id(0)

    @pl.when(i < na_ref[0])
    def _():
        @pl.when(jnp.logical_or(i == 0, te_ref[i] != te_ref[jnp.maximum(i - 1, 0)]))
        def _():
            wg_b[...] = wg_ref[...].astype(BF16)
            wu_b[...] = wu_ref[...].astype(BF16)
            wd_b[...] = wd_ref[...].astype(BF16)

        x = _rms(x_ref[...], g_ref[...]).astype(BF16)
        gt = _dot(x, wg_b[...])
        up = _dot(x, wu_b[...])
        hid = (gt * jax.nn.sigmoid(gt) * up).astype(BF16)
        y_ref[...] = _dot(hid, wd_b[...])

    @pl.when(i >= na_ref[0])
    def _():
        y_ref[...] = jnp.zeros_like(y_ref)


def _experts(tile_expert, n_active, xs, norm_w, w_gate, w_up, w_down, tm):
    r, d = xs.shape
    f = w_gate.shape[2]
    row_tile = lambda i, te, na: (jnp.minimum(i, na[0] - 1), 0)
    grid_spec = pltpu.PrefetchScalarGridSpec(
        num_scalar_prefetch=2,
        grid=(r // tm,),
        in_specs=[pl.BlockSpec((tm, d), row_tile),
                  pl.BlockSpec((1, d), lambda i, te, na: (0, 0)),
                  pl.BlockSpec((None, d, f), lambda i, te, na: (te[i], 0, 0)),
                  pl.BlockSpec((None, d, f), lambda i, te, na: (te[i], 0, 0)),
                  pl.BlockSpec((None, f, d), lambda i, te, na: (te[i], 0, 0))],
        out_specs=pl.BlockSpec((tm, d), lambda i, te, na: (i, 0)),
        scratch_shapes=[pltpu.VMEM((d, f), BF16), pltpu.VMEM((d, f), BF16), pltpu.VMEM((f, d), BF16)],
    )
    return pl.pallas_call(
        _experts_body,
        grid_spec=grid_spec,
        out_shape=jax.ShapeDtypeStruct((r, d), xs.dtype),
        compiler_params=_cparams("arbitrary"),
        name="moe_experts",
    )(tile_expert, n_active, xs, norm_w, w_gate, w_up, w_down)


def _combine_body(da_ref, db_ref, na_ref, nb_ref, x_ref, mf_ref, g_ref, ys_ref, o_ref, buf, sem):
    tm = x_ref.shape[0]
    i = pl.program_id(0)
    slot = lax.rem(i, 2)

    def gather(d_refs, s):
        def start(r, c):
            for k, d_ref in enumerate(d_refs):
                pltpu.make_async_copy(ys_ref.at[pl.ds(d_ref[0, r], 1), :],
                                      buf.at[s, k, pl.ds(r, 1), :], sem.at[s, k]).start(priority=k)
            return c

        lax.fori_loop(0, tm, start, 0, unroll=8)

    @pl.when(i == 0)
    def _():
        gather((da_ref, db_ref), 0)

    @pl.when(i + 1 < pl.num_programs(0))
    def _():
        gather((na_ref, nb_ref), 1 - slot)

    for k in range(MOE_TOPK):
        pltpu.make_async_copy(ys_ref.at[pl.ds(0, tm), :], buf.at[slot, k], sem.at[slot, k]).wait()
    mf = mf_ref[...]
    y = x_ref[...] + mf[:, 0:1] * buf[slot, 0] + mf[:, 1:2] * buf[slot, 1]
    o_ref[...] = _rms(y, g_ref[...])


def _combine(dest_a, dest_b, x2d, mf, norm_w, ys, tm):
    t, d = x2d.shape
    last = t // tm - 1
    cur = pl.BlockSpec((None, 1, tm), lambda i: (i, 0, 0), memory_space=pltpu.SMEM)
    nxt = pl.BlockSpec((None, 1, tm), lambda i: (jnp.minimum(i + 1, last), 0, 0), memory_space=pltpu.SMEM)
    return pl.pallas_call(
        _combine_body,
        grid=(t // tm,),
        in_specs=[cur, cur, nxt, nxt,
                  pl.BlockSpec((tm, d), lambda i: (i, 0)),
                  pl.BlockSpec((tm, LANES), lambda i: (i, 0)),
                  pl.BlockSpec((1, d), lambda i: (0, 0)),
                  pl.BlockSpec(memory_space=pl.ANY)],
        out_specs=pl.BlockSpec((tm, d), lambda i: (i, 0)),
        out_shape=jax.ShapeDtypeStruct((t, d), F32),
        scratch_shapes=[pltpu.VMEM((2, MOE_TOPK, tm, ys.shape[1]), ys.dtype), pltpu.SemaphoreType.DMA((2, MOE_TOPK))],
        compiler_params=_cparams("arbitrary"),
        name="moe_combine",
    )(dest_a, dest_b, dest_a, dest_b, x2d, mf, norm_w, ys)


MOE_ROW_TILE = 512
MOE_TOKEN_TILE = 256


def _moe(x2d, mi, mf, cnt, norm_w, w_gate, w_up, w_down, norm_final):
    t, d = x2d.shape
    tm = MOE_ROW_TILE
    tok = MOE_TOKEN_TILE
    counts = cnt[ROUTE_EXPERT_LANE0:ROUTE_EXPERT_LANE0 + MOE_EXPERTS, 0].astype(I32)
    padded = ((counts + tm - 1) // tm) * tm
    ends = jnp.cumsum(padded)
    offsets = ends - padded
    n_tiles = (MOE_TOPK * t + MOE_EXPERTS * (tm - 1)) // tm
    tile_start = jnp.arange(n_tiles, dtype=I32) * tm
    tile_expert = jnp.minimum(jnp.sum((ends[None, :] <= tile_start[:, None]).astype(I32), axis=1), MOE_EXPERTS - 1)
    n_info = jnp.stack([ends[-1] // tm, jnp.sum((counts > 0).astype(I32))]).astype(I32)
    zero_start = jnp.where(counts > 0, ends - tm, -1).astype(I32)
    offsets_col = jnp.broadcast_to(offsets[:, None], (MOE_EXPERTS, LANES))
    dest_a, dest_b = _dest_rows(mi, offsets_col, tok)
    xs = _dispatch(zero_start, n_info, dest_a, dest_b, x2d, n_tiles * tm, tok, tm)
    ys = _experts(tile_expert, n_info, xs, norm_w, w_gate, w_up, w_down, tm)
    return _combine(dest_a, dest_b, x2d, mf, norm_final, ys, tok)


def kernel(x, mem, norm_mix, w_in, w_out,
           s5_lam_re_f, s5_lam_im_f, s5_log_step_f, s5_b_re_f, s5_b_im_f, s5_c_re_f, s5_c_im_f,
           s5_lam_re_b, s5_lam_im_b, s5_log_step_b, s5_b_re_b, s5_b_im_b, s5_c_re_b, s5_c_im_b,
           s5_d, s5_w_glu, s5_b_glu, s5_norm,
           gdn_conv, gdn_a_log_f, gdn_dt_bias_f, gdn_a_log_b, gdn_dt_bias_b, gdn_norm,
           norm_xattn, norm_mem, xa_wq, xa_wk, xa_wv, xa_wo,
           norm_moe, router_group_w, router_group_b, router_expert_w, router_expert_b,
           moe_w_gate, moe_w_up, moe_w_down, norm_final):
    bsz, seq, d = x.shape
    t = bsz * seq
    l = 0
    x2d = x.reshape(t, d)
    wi = w_in[l]
    wg = wi[:, S5_WIDTH + 4 * GDN_WIDTH:].reshape(d, 4, GDN_HEADS)
    wg = jnp.pad(jnp.swapaxes(wg, 1, 2), ((0, 0), (0, 0), (0, 4))).reshape(d, GDN_HEADS * 8)
    wgt = wg.T
    u3, qkvz, gates = _in_proj(x2d, norm_mix[l][None], wi, wgt)
    s5p = dict(lam_re_f=s5_lam_re_f[l], lam_im_f=s5_lam_im_f[l], log_step_f=s5_log_step_f[l],
               b_re_f=s5_b_re_f[l], b_im_f=s5_b_im_f[l], c_re_f=s5_c_re_f[l], c_im_f=s5_c_im_f[l],
               lam_re_b=s5_lam_re_b[l], lam_im_b=s5_lam_im_b[l], log_step_b=s5_log_step_b[l],
               b_re_b=s5_b_re_b[l], b_im_b=s5_b_im_b[l], c_re_b=s5_c_re_b[l], c_im_b=s5_c_im_b[l], d=s5_d[l])
    g3 = _s5_scan(u3, s5p, seq // S5_CHUNK)
    y_s5 = _s5_post(g3, s5_w_glu[l].T.astype(BF16), s5_b_glu[l][:, None], s5_norm[l][:, None])

    head_par = jnp.stack([gdn_a_log_f[l], gdn_dt_bias_f[l], gdn_a_log_b[l], gdn_dt_bias_b[l]], axis=1)
    head_par = jnp.broadcast_to(jnp.pad(head_par, ((0, 0), (0, 4)))[:, :, None], (GDN_HEADS, 8, LANES))
    y_gdn = _gdn(qkvz, gates, gdn_conv[l], head_par, gdn_norm[l][None], bsz, seq)

    mem_len = mem.shape[1]
    kmem, vmem = _mem_kv(mem.reshape(bsz * mem_len, d), norm_mem[l][None],
                         xa_wk[l].astype(BF16), xa_wv[l].astype(BF16))
    n_pad = LANES - MOE_GROUPS - MOE_EXPERTS
    w_route = jnp.pad(jnp.concatenate([router_group_w[l], router_expert_w[l]], axis=1), ((0, 0), (0, n_pad))).T
    b_route = jnp.pad(jnp.concatenate([router_group_b[l], router_expert_b[l]]), (0, n_pad))
    b_route = jnp.broadcast_to(b_route[:, None], (LANES, LANES))
    x2, mi, mf, cnt = _mix_xattn(x2d, y_s5, y_gdn, w_out[l], norm_xattn[l][None], xa_wq[l], kmem, vmem, xa_wo[l],
                                 norm_moe[l][None], w_route, b_route, seq, mem_len)
    y = _moe(x2, mi, mf, cnt, norm_moe[l][None], moe_w_gate[l], moe_w_up[l], moe_w_down[l], norm_final[None])
    return y.reshape(bsz, seq, d)
```

```python
import functools
import math

import jax
import jax.numpy as jnp
from jax import lax
from jax.experimental import pallas as pl
from jax.experimental.pallas import tpu as pltpu

F32 = jnp.float32
BF16 = jnp.bfloat16
I32 = jnp.int32

D_MODEL = 1024
S5_WIDTH = 512
S5_GROUP = 16
S5_GROUPS = 32
S5_STATE = 64
S5_CHUNK = 128
GDN_HEADS = 4
GDN_HEAD_DIM = 128
GDN_WIDTH = 512
GDN_CONV = 5
GDN_CHUNK = 64
XA_HEADS = 4
XA_HEAD_DIM = 256
MOE_GROUPS = 4
MOE_PER_GROUP = 8
MOE_EXPERTS = 32
MOE_TOPK = 2
D_EXPERT = 256
RMS_EPS = 1e-6
L2_EPS = 1e-6
LANES = 128
VMEM_LIMIT = 56 * 1024 * 1024


def _cparams(*sem):
    return pltpu.CompilerParams(dimension_semantics=tuple(sem), vmem_limit_bytes=VMEM_LIMIT)


def _rms(x, gain):
    return x * lax.rsqrt(jnp.mean(x * x, axis=-1, keepdims=True) + RMS_EPS) * gain


def _dot(a, b):
    return jnp.dot(a, b, preferred_element_type=F32)


def _dot_nt(a, b):
    return lax.dot_general(a, b, (((1,), (1,)), ((), ())), preferred_element_type=F32)


def _dot_tn(a, b):
    return lax.dot_general(a, b, (((0,), (0,)), ((), ())), preferred_element_type=F32)


def _in_proj_body(x_ref, g_ref, w_ref, wgt_ref, u_ref, qkvz_ref, gates_ref, wut_b, wqkvz_b, wgt_b):
    @pl.when(pl.program_id(0) == 0)
    def _():
        wut_b[...] = w_ref[:, :S5_WIDTH].T.astype(BF16)
        wqkvz_b[...] = w_ref[:, S5_WIDTH:S5_WIDTH + 4 * GDN_WIDTH].astype(BF16)
        wgt_b[...] = wgt_ref[...].astype(BF16)

    h = _rms(x_ref[...], g_ref[...]).astype(BF16)
    ut = _dot_nt(wut_b[...], h)
    gt = _dot_nt(wgt_b[...], h)
    for j in range(u_ref.shape[0]):
        u_ref[j] = ut[:, j * S5_CHUNK:(j + 1) * S5_CHUNK]
        gates_ref[j] = gt[:, j * S5_CHUNK:(j + 1) * S5_CHUNK]
    qkvz_ref[...] = _dot(h, wqkvz_b[...]).astype(BF16)


def _in_proj(x2d, norm_w, w_in, wgt, tm=512):
    t = x2d.shape[0]
    nck = tm // S5_CHUNK
    nqkvz = 4 * GDN_WIDTH
    full = lambda shape: pl.BlockSpec(shape, lambda i: (0,) * len(shape))
    return pl.pallas_call(
        _in_proj_body,
        grid=(t // tm,),
        in_specs=[pl.BlockSpec((tm, D_MODEL), lambda i: (i, 0)),
                  full((1, D_MODEL)), full(w_in.shape), full(wgt.shape)],
        out_specs=[pl.BlockSpec((nck, S5_WIDTH, S5_CHUNK), lambda i: (i, 0, 0)),
                   pl.BlockSpec((tm, nqkvz), lambda i: (i, 0)),
                   pl.BlockSpec((nck, wgt.shape[0], S5_CHUNK), lambda i: (i, 0, 0))],
        out_shape=[jax.ShapeDtypeStruct((t // S5_CHUNK, S5_WIDTH, S5_CHUNK), F32),
                   jax.ShapeDtypeStruct((t, nqkvz), BF16),
                   jax.ShapeDtypeStruct((t // S5_CHUNK, wgt.shape[0], S5_CHUNK), F32)],
        scratch_shapes=[pltpu.VMEM((S5_WIDTH, D_MODEL), BF16), pltpu.VMEM((D_MODEL, nqkvz), BF16),
                        pltpu.VMEM(wgt.shape, BF16)],
        compiler_params=_cparams("arbitrary"),
        name="in_proj",
    )(x2d, norm_w, w_in, wgt)


def _cmul(ar, ai, br, bi):
    return ar * br - ai * bi, ar * bi + ai * br


def _cpow_int(lr, li, expo, nbits):
    res_r = jnp.ones(jnp.broadcast_shapes(lr.shape, expo.shape), F32)
    res_i = jnp.zeros_like(res_r)
    for b in range(nbits):
        bit = ((expo >> b) & 1) == 1
        nr, ni = _cmul(res_r, res_i, lr, li)
        res_r = jnp.where(bit, nr, res_r)
        res_i = jnp.where(bit, ni, res_i)
        if b + 1 < nbits:
            lr, li = _cmul(lr, li, lr, li)
    return res_r, res_i


def _lam_bar(re, im, step):
    er = jnp.exp(step * re)
    return er * jnp.cos(step * im), er * jnp.sin(step * im)


def _zoh_coef(re, im, lr, li):
    den = re * re + im * im
    return ((lr - 1.0) * re + li * im) / den, (li * re - (lr - 1.0) * im) / den


def _s5_body(u_ref, lrow_ref, lcol_ref, l256_ref, b_ref, bt_ref, c_ref, ct_ref, d_ref,
             o_ref, vf_ref, vb_ref, m_ref, win_ref, wout_ref, sf_ref, sb_ref, hf_ref, hb_ref, uc_ref, y_ref, *, nchunk):
    L = S5_CHUNK
    P = S5_STATE
    n_rows = u_ref.shape[0]
    nb = n_rows // nchunk
    lane_i = lax.broadcasted_iota(I32, (1, L), 1)

    lcol = lcol_ref[...]
    lbc_r, lbc_i = _lam_bar(lcol[:, 0:2], lcol[:, 2:4], lcol[:, 4:6])
    kc_r, kc_i = _zoh_coef(lcol[:, 0:2], lcol[:, 2:4], lbc_r, lbc_i)
    lrow = lrow_ref[...]
    lbr_r, lbr_i = _lam_bar(lrow[0:2], lrow[2:4], lrow[4:6])
    kr_r, kr_i = _zoh_coef(lrow[0:2], lrow[2:4], lbr_r, lbr_i)
    lf_r, lf_i, lb_r, lb_i = lbc_r[:, 0:1], lbc_i[:, 0:1], lbc_r[:, 1:2], lbc_i[:, 1:2]
    kfr_c, kfi_c, kbr_c, kbi_c = kc_r[:, 0:1], kc_i[:, 0:1], kc_r[:, 1:2], kc_i[:, 1:2]
    kfr_r, kfi_r, kbr_r, kbi_r = kr_r[0:1], kr_i[0:1], kr_r[1:2], kr_i[1:2]

    pwf_r, pwf_i = _cpow_int(lf_r, lf_i, lane_i, 7)
    rvf_r, rvf_i = _cpow_int(lf_r, lf_i, (L - 1) - lane_i, 7)
    pwb_r, pwb_i = _cpow_int(lb_r, lb_i, lane_i, 7)
    rvb_r, rvb_i = _cpow_int(lb_r, lb_i, L - lane_i, 8)
    nxf_r, nxf_i = _cmul(pwf_r, pwf_i, lf_r, lf_i)

    bf_r = kfr_c * b_ref[0] - kfi_c * b_ref[1]
    bf_i = kfr_c * b_ref[1] + kfi_c * b_ref[0]
    bb_r = kbr_c * b_ref[2] - kbi_c * b_ref[3]
    bb_i = kbr_c * b_ref[3] + kbi_c * b_ref[2]
    btf_r = kfr_r * bt_ref[0] - kfi_r * bt_ref[1]
    btf_i = kfr_r * bt_ref[1] + kfi_r * bt_ref[0]
    btb_r = kbr_r * bt_ref[2] - kbi_r * bt_ref[3]
    btb_i = kbr_r * bt_ref[3] + kbi_r * bt_ref[2]

    def taps(c_r, c_i, bt_r, bt_i, pw_r, pw_i):
        cb_r = (bt_r[:, None, :] * c_r[None, :, :] - bt_i[:, None, :] * c_i[None, :, :]).reshape(256, P)
        cb_i = (bt_r[:, None, :] * c_i[None, :, :] + bt_i[:, None, :] * c_r[None, :, :]).reshape(256, P)
        k = (jnp.dot(cb_r, pw_r, preferred_element_type=F32, precision=lax.Precision.HIGHEST)
             - jnp.dot(cb_i, pw_i, preferred_element_type=F32, precision=lax.Precision.HIGHEST))
        return k, jnp.sum(cb_r, axis=1, keepdims=True)

    kf, _ = taps(c_ref[0], c_ref[1], btf_r, btf_i, pwf_r, pwf_i)
    kb, kb0 = taps(c_ref[2], c_ref[3], btb_r, btb_i, rvb_r, rvb_i)
    is0 = lane_i == 0
    vf_ref[...] = kf + jnp.where(is0, kb0, 0.0)
    vb_ref[...] = jnp.where(is0, 0.0, kb)

    row_i = lax.broadcasted_iota(I32, (L, L), 0)
    col_i = lax.broadcasted_iota(I32, (L, L), 1)
    fwd_lane = col_i + row_i < L

    def build_ci(ci, carry):
        for co in range(S5_GROUP):
            r = ci * S5_GROUP + co
            taps_rows = jnp.where(fwd_lane, jnp.broadcast_to(vf_ref[pl.ds(r, 1), :], (L, L)),
                                  jnp.broadcast_to(vb_ref[pl.ds(r, 1), :], (L, L)))
            m_ref[pl.ds(pl.multiple_of(ci * L, L), L), co * L:(co + 1) * L] = pltpu.roll(
                taps_rows, 0, 1, stride=1, stride_axis=0).astype(BF16)
        return carry

    lax.fori_loop(0, S5_GROUP, build_ci, 0)

    for ci in range(S5_GROUP):
        sl = slice(ci * L, (ci + 1) * L)
        br, bi = bf_r[:, ci:ci + 1], bf_i[:, ci:ci + 1]
        win_ref[0 * P:1 * P, sl] = (rvf_r * br - rvf_i * bi).astype(BF16)
        win_ref[1 * P:2 * P, sl] = (rvf_r * bi + rvf_i * br).astype(BF16)
        br, bi = bb_r[:, ci:ci + 1], bb_i[:, ci:ci + 1]
        win_ref[2 * P:3 * P, sl] = (pwb_r * br - pwb_i * bi).astype(BF16)
        win_ref[3 * P:4 * P, sl] = (pwb_r * bi + pwb_i * br).astype(BF16)
    for co in range(S5_GROUP):
        sl = slice(co * L, (co + 1) * L)
        cr, ci_ = ct_ref[0][:, co:co + 1], ct_ref[1][:, co:co + 1]
        wout_ref[0 * P:1 * P, sl] = (cr * nxf_r - ci_ * nxf_i).astype(BF16)
        wout_ref[1 * P:2 * P, sl] = (-(cr * nxf_i + ci_ * nxf_r)).astype(BF16)
        cr, ci_ = ct_ref[2][:, co:co + 1], ct_ref[3][:, co:co + 1]
        wout_ref[2 * P:3 * P, sl] = (cr * rvb_r - ci_ * rvb_i).astype(BF16)
        wout_ref[3 * P:4 * P, sl] = (-(cr * rvb_i + ci_ * rvb_r)).astype(BF16)

    for ci in range(S5_GROUP):
        uc_ref[ci] = u_ref[:, ci, :]
    ucat = jnp.concatenate([uc_ref[ci].astype(BF16) for ci in range(S5_GROUP)], axis=1)

    summ = _dot_nt(ucat, win_ref[...])
    sf_ref[...] = summ[:, :2 * P]
    sb_ref[...] = summ[:, 2 * P:]
    nblk = 2 * L
    for j in range(S5_GROUP * L // nblk):
        y_ref[:, j * nblk:(j + 1) * nblk] = _dot(ucat, m_ref[:, j * nblk:(j + 1) * nblk])
    l256 = l256_ref[...]
    a_mul, a_im = _lam_bar(l256[0:1], l256[1:2], l256[2:3])
    for _ in range(7):
        a_mul, a_im = _cmul(a_mul, a_im, a_mul, a_im)
    lane256 = lax.broadcasted_iota(I32, (1, 4 * P), 1)
    b_mul = jnp.where((lane256 // P) % 2 == 0, -a_im, a_im)

    hf = jnp.zeros((nb, 2 * P), F32)
    hb = jnp.zeros((nb, 2 * P), F32)
    for c in range(nchunk):
        cr = nchunk - 1 - c
        rows_f = pl.ds(c, nb, stride=nchunk)
        rows_b = pl.ds(cr, nb, stride=nchunk)
        hf_ref[rows_f, :] = hf
        hb_ref[rows_b, :] = hb
        hf = a_mul[:, :2 * P] * hf + b_mul[:, :2 * P] * pltpu.roll(hf, P, 1) + sf_ref[rows_f, :]
        hb = a_mul[:, 2 * P:] * hb + b_mul[:, 2 * P:] * pltpu.roll(hb, P, 1) + sb_ref[rows_b, :]
    hprev = jnp.concatenate([hf_ref[...], hb_ref[...]], axis=1).astype(BF16)

    for j in range(S5_GROUP * L // nblk):
        y = y_ref[:, j * nblk:(j + 1) * nblk] + _dot(hprev, wout_ref[:, j * nblk:(j + 1) * nblk])
        for q in range(nblk // L):
            co = j * (nblk // L) + q
            yc = y[:, q * L:(q + 1) * L] + d_ref[co:co + 1, :] * uc_ref[co]
            o_ref[:, co, :] = 0.5 * yc * (1.0 + lax.erf(yc * (2.0 ** -0.5)))


def _s5_scan(u3, p, nchunk):
    n = u3.shape[0]
    g, grp, st, L = S5_GROUPS, S5_GROUP, S5_STATE, S5_CHUNK
    step_f = jnp.exp(p["log_step_f"])[:, None] * jnp.ones((1, st), F32)
    step_b = jnp.exp(p["log_step_b"])[:, None] * jnp.ones((1, st), F32)
    zeros = jnp.zeros((g, st), F32)
    lrow = jnp.stack([p["lam_re_f"], p["lam_re_b"], p["lam_im_f"], p["lam_im_b"], step_f, step_b, zeros, zeros], axis=1)
    lcol = jnp.swapaxes(lrow, 1, 2)
    cat4 = lambda f, b: jnp.concatenate([f, f, b, b], axis=1)
    z256 = jnp.zeros((g, 4 * st), F32)
    l256 = jnp.stack([cat4(p["lam_re_f"], p["lam_re_b"]), cat4(p["lam_im_f"], p["lam_im_b"]),
                      cat4(step_f, step_b)] + [z256] * 5, axis=1)
    b4 = jnp.stack([p["b_re_f"], p["b_im_f"], p["b_re_b"], p["b_im_b"]], axis=1)
    bt4 = jnp.swapaxes(b4, 2, 3)
    c4 = jnp.stack([p["c_re_f"], p["c_im_f"], p["c_re_b"], p["c_im_b"]], axis=1)
    ct4 = jnp.swapaxes(c4, 2, 3)
    dbc = jnp.broadcast_to(p["d"].reshape(g, grp, 1), (g, grp, L))
    per_g = lambda *shape: pl.BlockSpec((None,) + shape, lambda i: (i,) + (0,) * len(shape))
    return pl.pallas_call(
        functools.partial(_s5_body, nchunk=nchunk),
        grid=(g,),
        in_specs=[pl.BlockSpec((n, grp, L), lambda i: (0, i, 0)),
                  per_g(8, st), per_g(st, 8), per_g(8, 4 * st), per_g(4, st, grp), per_g(4, grp, st),
                  per_g(4, grp, st), per_g(4, st, grp), per_g(grp, L)],
        out_specs=pl.BlockSpec((n, grp, L), lambda i: (0, i, 0)),
        out_shape=jax.ShapeDtypeStruct(u3.shape, F32),
        scratch_shapes=[pltpu.VMEM((grp * grp, L), F32), pltpu.VMEM((grp * grp, L), F32),
                        pltpu.VMEM((grp * L, grp * L), BF16),
                        pltpu.VMEM((4 * st, grp * L), BF16), pltpu.VMEM((4 * st, grp * L), BF16),
                        pltpu.VMEM((n, 2 * st), F32), pltpu.VMEM((n, 2 * st), F32),
                        pltpu.VMEM((n, 2 * st), F32), pltpu.VMEM((n, 2 * st), F32),
                        pltpu.VMEM((grp, n, L), F32), pltpu.VMEM((n, grp * L), F32)],
        compiler_params=_cparams("parallel"),
        name="s5_scan",
    )(u3, lrow, lcol, l256, b4, bt4, c4, ct4, dbc)


def _s5_post_body(g_ref, wt_ref, b_ref, nw_ref, o_ref):
    for j in range(g_ref.shape[0]):
        g = g_ref[j]
        z = _dot(wt_ref[...], g.astype(BF16)) + b_ref[...]
        y = g * jax.nn.sigmoid(z)
        y = y * lax.rsqrt(jnp.mean(y * y, axis=0, keepdims=True) + RMS_EPS) * nw_ref[...]
        o_ref[j * S5_CHUNK:(j + 1) * S5_CHUNK, :] = y.T.astype(BF16)


def _s5_post(g3, w_glu_t, b_glu_col, norm_col, nck=8):
    n = g3.shape[0]
    full = lambda shape: pl.BlockSpec(shape, lambda i: (0,) * len(shape))
    return pl.pallas_call(
        _s5_post_body,
        grid=(n // nck,),
        in_specs=[pl.BlockSpec((nck, S5_WIDTH, S5_CHUNK), lambda i: (i, 0, 0)),
                  full(w_glu_t.shape), full(b_glu_col.shape), full(norm_col.shape)],
        out_specs=pl.BlockSpec((nck * S5_CHUNK, S5_WIDTH), lambda i: (i, 0)),
        out_shape=jax.ShapeDtypeStruct((n * S5_CHUNK, S5_WIDTH), BF16),
        compiler_params=_cparams("parallel"),
        name="s5_post",
    )(g3, w_glu_t, b_glu_col, norm_col)


GDN_BLOCK = 128
GDN_PAIR = 2
GDN_PREP_BATCH = 8


def _packed_tri_inverse(lps, low, upp, bd16, rings):
    def pk(xs, ys):
        outs = []
        for a, b in zip(xs, ys):
            lhs = jnp.concatenate([jnp.where(low, a, 0.0), jnp.where(upp, a, 0.0)], axis=1).astype(BF16)
            rhs = jnp.concatenate([jnp.where(low, b, 0.0), jnp.where(upp, b, 0.0)], axis=0).astype(BF16)
            outs.append(_dot(lhs, rhs))
        return outs

    d = [jnp.where(bd16, lp, 0.0) for lp in lps]
    d2 = pk(d, d)
    d4 = pk(d2, d2)
    d8 = pk(d4, d4)
    a = [y - x - p for x, y, p in zip(d, d2, pk(d, d2))]
    a = [x + y + p for x, y, p in zip(a, d4, pk(a, d4))]
    a = [x + y + p for x, y, p in zip(a, d8, pk(a, d8))]
    for ring in rings:
        n = [jnp.where(ring, lp, 0.0) for lp in lps]
        t = [x + p for x, p in zip(n, pk(a, n))]
        a = [x - y - p for x, y, p in zip(a, t, pk(t, a))]
    return a


def _gdn_body(q_ref, k_ref, v_ref, z_ref, wq_ref, wk_ref, wv_ref, g_ref, hp_ref, nw_ref, o_ref,
              qs, ks, vs, os_, sg, cf, cb, uf, ub, wqf, wqb, qkf, qkb, kdtf, kdtb, eglf, eglb, xpad):
    seq = q_ref.shape[0]
    C = GDN_BLOCK
    hd = GDN_HEAD_DIM
    nck = seq // C
    heads = range(GDN_PAIR)

    pad = 8
    half = (GDN_CONV - 1) // 2
    xpad[0:pad, :] = jnp.zeros((pad, LANES), F32)
    xpad[pad + seq:2 * pad + seq, :] = jnp.zeros((pad, LANES), F32)

    def conv_silu(x_ref, w_ref, j):
        cols = pl.ds(pl.multiple_of(j * hd, hd), hd)
        xpad[pad:pad + seq, :] = x_ref[:, cols].astype(F32)
        w = w_ref[:, cols]
        acc = xpad[pad - half:pad - half + seq, :] * w[0:1]
        for tap in range(1, GDN_CONV):
            acc = acc + xpad[pad - half + tap:pad - half + tap + seq, :] * w[tap:tap + 1]
        return acc * jax.nn.sigmoid(acc)

    def l2n(x):
        return x * lax.rsqrt(jnp.sum(x * x, axis=-1, keepdims=True) + L2_EPS)

    def softplus(x):
        return jnp.maximum(x, 0.0) + jnp.log1p(jnp.exp(-jnp.abs(x)))

    lane = lax.broadcasted_iota(I32, (1, C), 1)

    def prologue(j, carry):
        qs[j] = l2n(conv_silu(q_ref, wq_ref, j)) * (hd ** -0.5)
        ks[j] = l2n(conv_silu(k_ref, wk_ref, j))
        vs[j] = conv_silu(v_ref, wv_ref, j)
        os_[j] = jnp.zeros((seq, hd), F32)
        g = g_ref[:, pl.ds(pl.multiple_of(8 * j, 8), 8), :].reshape(nck * 8, C)
        hp = hp_ref[j]
        sg[j] = jax.nn.sigmoid(g)
        gl_f = -jnp.exp(hp[0:1]) * softplus(g + hp[1:2])
        gl_b = -jnp.exp(hp[2:3]) * softplus(g + hp[3:4])
        sh = 1
        while sh < C:
            gl_f = gl_f + jnp.where(lane >= sh, pltpu.roll(gl_f, sh, 1), 0.0)
            gl_b = gl_b + jnp.where(lane < C - sh, pltpu.roll(gl_b, C - sh, 1), 0.0)
            sh *= 2
        cf[j] = gl_f
        cb[j] = gl_b
        return carry

    lax.fori_loop(0, GDN_PAIR, prologue, 0)

    ri = lax.broadcasted_iota(I32, (C, C), 0)
    ci = lax.broadcasted_iota(I32, (C, C), 1)
    low, upp = ri > ci, ri < ci
    low_i, upp_i = ri >= ci, ri <= ci
    same = lambda w: (ri // w) == (ci // w)
    bd16 = same(16)
    rings = []
    w = 32
    while w <= C:
        rings.append(jnp.logical_and(same(w), jnp.logical_not(same(w // 2))))
        w *= 2

    nbatch = math.gcd(GDN_PREP_BATCH // GDN_PAIR, nck)

    def column(ref, r):
        rows = jnp.broadcast_to(ref[pl.ds(r, 1), :], (C, C))
        return rows.T, rows

    def prepare(it, carry):
        items = [(j, it * nbatch + i) for i in range(nbatch) for j in heads]
        sls = [pl.ds(pl.multiple_of(c * C, C), C) for _, c in items]
        g_f, g_b, bt_f, bt_b, dec_f, dec_b, kb_f, kb_b, aq = [], [], [], [], [], [], [], [], []
        for (j, c), sl in zip(items, sls):
            k = ks[j, sl, :]
            gfc, gfr = column(cf.at[j], c * 8 + 2)
            gbc, gbr = column(cb.at[j], c * 8 + 3)
            g_f.append(gfc)
            g_b.append(gbc)
            bt_f.append(column(sg.at[j], c * 8)[0])
            bt_b.append(column(sg.at[j], c * 8 + 1)[0])
            dec_f.append(jnp.where(low_i, jnp.exp(jnp.where(low_i, gfc - gfr, 0.0)), 0.0))
            dec_b.append(jnp.where(upp_i, jnp.exp(jnp.where(upp_i, gbc - gbr, 0.0)), 0.0))
            kb_f.append(k * bt_f[-1])
            kb_b.append(k * bt_b[-1])
            aq.append(_dot_nt(jnp.concatenate([kb_f[-1], kb_b[-1], qs[j, sl, :]], axis=0).astype(BF16),
                              k.astype(BF16)))
        lps = [jnp.where(low, x[:C] * df, 0.0) + jnp.where(upp, x[C:2 * C] * db, 0.0)
               for x, df, db in zip(aq, dec_f, dec_b)]
        inv = _packed_tri_inverse(lps, low, upp, bd16, rings)
        for n, ((j, c), sl) in enumerate(zip(items, sls)):
            q, k, v = qs[j, sl, :], ks[j, sl, :], vs[j, sl, :]
            for rev, g_c, kb, beta, dec, msk, u_s, wq_s, qk_s, kdt_s, egl_s in (
                    (False, g_f[n], kb_f[n], bt_f[n], dec_f[n], low, uf, wqf, qkf, kdtf, eglf),
                    (True, g_b[n], kb_b[n], bt_b[n], dec_b[n], upp, ub, wqb, qkb, kdtb, eglb)):
                eg = jnp.exp(g_c)
                rhs = jnp.concatenate([v * beta, kb * eg], axis=1)
                uw = rhs + _dot(jnp.where(msk, inv[n], 0.0).astype(BF16), rhs.astype(BF16))
                glast = g_c[0:1] if rev else g_c[C - 1:C]
                u_s[j, sl, :] = uw[:, :C]
                wq_s[j, pl.ds(pl.multiple_of(c * 2 * C, 2 * C), 2 * C), :] = jnp.concatenate(
                    [uw[:, C:], q * eg], axis=0).astype(BF16)
                qk_s[j, sl, :] = (aq[n][2 * C:] * dec).astype(BF16)
                kdt_s[j, sl, :] = (k * jnp.exp(glast - g_c)).T.astype(BF16)
                egl_s[j, pl.ds(c, 1), :] = jnp.exp(glast)
        return carry

    lax.fori_loop(0, nck // nbatch, prepare, 0)

    def body(i, carry):
        chains = []
        for j in heads:
            chains.append((j, i, uf, wqf, qkf, kdtf, eglf))
            chains.append((j, nck - 1 - i, ub, wqb, qkb, kdtb, eglb))
        sls = [pl.ds(pl.multiple_of(c * C, C), C) for _, c, *_ in chains]
        ws_qs = [_dot(wq_s[j, pl.ds(pl.multiple_of(c * 2 * C, 2 * C), 2 * C), :], st.astype(BF16))
                 for (j, c, _, wq_s, *_), st in zip(chains, carry)]
        vnb = [(u_s[j, sl, :] - x[:C]).astype(BF16) for (j, _, u_s, *_), sl, x in zip(chains, sls, ws_qs)]
        new = [st * egl_s[j, pl.ds(c, 1), :] + _dot(kdt_s[j, sl, :], v)
               for (j, c, _, _, _, kdt_s, egl_s), sl, st, v in zip(chains, sls, carry, vnb)]
        for (j, _, _, _, qk_s, _, _), sl, x, v in zip(chains, sls, ws_qs, vnb):
            os_[j, sl, :] += x[C:] + _dot(qk_s[j, sl, :], v)
        return tuple(new)

    zero = jnp.zeros((hd, hd), F32)
    lax.fori_loop(0, nck, body, (zero,) * (2 * GDN_PAIR))

    for j in heads:
        o = os_[j]
        o = o * lax.rsqrt(jnp.mean(o * o, axis=-1, keepdims=True) + RMS_EPS) * nw_ref[...]
        z = z_ref[:, j * hd:(j + 1) * hd].astype(F32)
        o_ref[:, j * hd:(j + 1) * hd] = (o * (z * jax.nn.sigmoid(z))).astype(BF16)


def _gdn(qkvz, gates3, conv_w, head_par, norm_w, bsz, seq):
    t = bsz * seq
    hd = GDN_HEAD_DIM
    nh = GDN_HEADS
    np_ = GDN_PAIR
    wd = np_ * hd
    npairs = nh // np_
    nck = seq // GDN_BLOCK
    col = lambda off: pl.BlockSpec((seq, wd), lambda b, p: (b, off * npairs + p))
    wcol = lambda off: pl.BlockSpec((GDN_CONV, wd), lambda b, p: (0, off * npairs + p))
    per_head = lambda rows, dt: pltpu.VMEM((np_, rows, hd), dt)
    return pl.pallas_call(
        _gdn_body,
        grid=(bsz, npairs),
        in_specs=[col(0), col(1), col(2), col(3), wcol(0), wcol(1), wcol(2),
                  pl.BlockSpec((nck, 8 * np_, GDN_BLOCK), lambda b, p: (b, p, 0)),
                  pl.BlockSpec((np_, 8, LANES), lambda b, p: (p, 0, 0)),
                  pl.BlockSpec((1, hd), lambda b, p: (0, 0))],
        out_specs=pl.BlockSpec((seq, wd), lambda b, p: (b, p)),
        out_shape=jax.ShapeDtypeStruct((t, nh * hd), BF16),
        scratch_shapes=([per_head(seq, F32)] * 4
                        + [per_head(nck * 8, F32)] * 3
                        + [per_head(seq, F32)] * 2
                        + [per_head(2 * seq, BF16)] * 2
                        + [per_head(seq, BF16)] * 2
                        + [per_head(seq, BF16)] * 2
                        + [per_head(nck, F32)] * 2
                        + [pltpu.VMEM((seq + 16, hd), F32)]),
        compiler_params=_cparams("parallel", "parallel"),
        name="gdn",
    )(qkvz, qkvz, qkvz, qkvz, conv_w, conv_w, conv_w, gates3, head_par, norm_w)


def _kv_body(m_ref, g_ref, wk_ref, wv_ref, k_ref, v_ref):
    mn = _rms(m_ref[...], g_ref[...]).astype(BF16)
    k_ref[...] = _dot(mn, wk_ref[...]).astype(BF16)
    v_ref[...] = _dot(mn, wv_ref[...]).astype(BF16)


def _mem_kv(mem2d, norm_w, wk, wv, tm=512):
    r, d = mem2d.shape
    full = lambda shape: pl.BlockSpec(shape, lambda i: (0,) * len(shape))
    tile = pl.BlockSpec((tm, d), lambda i: (i, 0))
    return pl.pallas_call(
        _kv_body,
        grid=(r // tm,),
        in_specs=[tile, full((1, d)), full(wk.shape), full(wv.shape)],
        out_specs=[tile, tile],
        out_shape=[jax.ShapeDtypeStruct((r, d), BF16)] * 2,
        compiler_params=_cparams("parallel"),
        name="mem_kv",
    )(mem2d, norm_w, wk, wv)


def _mix_xattn_body(x_ref, y5_ref, yg_ref, wmix_ref, g_ref, wq_ref, k_ref, v_ref, wo_ref,
                    gm_ref, whi_ref, wlo_ref, br_ref, tri_ref,
                    o_ref, mi_ref, mf_ref, cnt_ref, wmix_b, wq_b, wo_b, carry):
    @pl.when(pl.program_id(0) == 0)
    def _():
        wmix_b[...] = wmix_ref[...].astype(BF16)
        wq_b[...] = wq_ref[...].astype(BF16)
        wo_b[...] = wo_ref[...].astype(BF16)
        carry[...] = jnp.zeros_like(carry)

    x1 = (x_ref[...] + _dot(y5_ref[...], wmix_b[:S5_WIDTH, :]) + _dot(yg_ref[...], wmix_b[S5_WIDTH:, :]))
    xn = _rms(x1, g_ref[...]).astype(BF16)
    q = (_dot(xn, wq_b[...]) * (XA_HEAD_DIM ** -0.5)).astype(BF16)
    hsl = [slice(h * XA_HEAD_DIM, (h + 1) * XA_HEAD_DIM) for h in range(XA_HEADS)]
    sc = [_dot_nt(q[:, sl], k_ref[:, sl]) for sl in hsl]
    pr = [jnp.exp(s - jnp.max(s, axis=-1, keepdims=True)) for s in sc]
    pr = [p / jnp.sum(p, axis=-1, keepdims=True) for p in pr]
    heads = [_dot(p.astype(BF16), v_ref[:, sl]).astype(BF16) for p, sl in zip(pr, hsl)]
    x2 = x1 + _dot(jnp.concatenate(heads, axis=1), wo_b[...])
    o_ref[...] = x2
    mi, mf = _route(_rms(x2, gm_ref[...]), whi_ref[...], wlo_ref[...], br_ref[...], tri_ref[...], carry)
    mi_ref[...] = mi
    mf_ref[...] = mf
    cnt_ref[...] = carry[...]


def _mix_xattn(x2d, y5, yg, w_mix, norm_w, wq, kmem, vmem, wo, norm_moe, w_route, b_route, seq, mem_len, tm=512):
    t, d = x2d.shape
    per_b = seq // tm
    tri = jnp.triu(jnp.ones((tm, tm), BF16))
    w_hi = w_route.astype(BF16)
    w_lo = (w_route - w_hi.astype(F32)).astype(BF16)
    full = lambda shape: pl.BlockSpec(shape, lambda i: (0,) * len(shape))
    tile = lambda w: pl.BlockSpec((tm, w), lambda i: (i, 0))
    return pl.pallas_call(
        _mix_xattn_body,
        grid=(t // tm,),
        in_specs=[tile(d), tile(S5_WIDTH), tile(GDN_WIDTH),
                  full(w_mix.shape), full((1, d)), full(wq.shape),
                  pl.BlockSpec((mem_len, d), lambda i: (i // per_b, 0)),
                  pl.BlockSpec((mem_len, d), lambda i: (i // per_b, 0)),
                  full(wo.shape),
                  full((1, d)), full(w_route.shape), full(w_route.shape), full(b_route.shape), full((tm, tm))],
        out_specs=[tile(d), pl.BlockSpec((8, tm), lambda i: (0, i)), tile(LANES),
                   pl.BlockSpec((ROUTE_ROWS, LANES), lambda i: (0, 0))],
        out_shape=[jax.ShapeDtypeStruct((t, d), F32),
                   jax.ShapeDtypeStruct((8, t), I32),
                   jax.ShapeDtypeStruct((t, LANES), F32),
                   jax.ShapeDtypeStruct((ROUTE_ROWS, LANES), F32)],
        scratch_shapes=[pltpu.VMEM(w_mix.shape, BF16), pltpu.VMEM(wq.shape, BF16), pltpu.VMEM(wo.shape, BF16),
                        pltpu.VMEM((ROUTE_ROWS, LANES), F32)],
        compiler_params=_cparams("arbitrary"),
        name="mix_xattn",
    )(x2d, y5, yg, w_mix, norm_w, wq, kmem, vmem, wo, norm_moe, w_hi, w_lo, b_route, tri)


ROUTE_EXPERT_LANE0 = 4


ROUTE_ROWS = 40


def _route(xn, w_hi, w_lo, bias, tri, carry):
    x_hi = xn.astype(BF16)
    x_lo = (xn - x_hi.astype(F32)).astype(BF16)
    logits = (_dot_nt(w_hi, x_hi) + _dot_nt(w_hi, x_lo) + _dot_nt(w_lo, x_hi))[:ROUTE_ROWS] + bias[:ROUTE_ROWS, 0:1]
    tm = logits.shape[1]
    row = lax.broadcasted_iota(I32, (ROUTE_ROWS, tm), 0)
    neg = jnp.float32(-jnp.inf)
    big = jnp.int32(LANES)

    def top(vals):
        m = jnp.max(vals, axis=0, keepdims=True)
        idx = jnp.min(jnp.where(vals == m, row, big), axis=0, keepdims=True)
        return m, idx

    is_g = row < MOE_GROUPS
    gl = jnp.where(is_g, logits, neg)
    gmax, gidx = top(gl)
    p_top = 1.0 / jnp.sum(jnp.where(is_g, jnp.exp(gl - gmax), 0.0), axis=0, keepdims=True)
    erow = row - ROUTE_EXPERT_LANE0
    in_grp = jnp.logical_and(jnp.logical_and(erow >= 0, erow < MOE_EXPERTS), (erow // MOE_PER_GROUP) == gidx)
    es = jnp.where(in_grp, logits, neg)
    m1, i1 = top(es)
    m2, i2 = top(jnp.where(row == i1, neg, es))
    e21 = jnp.exp(m2 - m1)
    w1 = p_top / (1.0 + e21)
    w2 = p_top * e21 / (1.0 + e21)

    a1 = (row == i1).astype(F32)
    a2 = (row == i2).astype(F32)
    both = a1 + a2
    before = _dot(both.astype(BF16), tri) - both + carry[:, 0:1]
    r1 = jnp.sum(a1 * before, axis=0, keepdims=True).astype(I32)
    r2 = jnp.sum(a2 * before, axis=0, keepdims=True).astype(I32)
    carry[...] = carry[...] + jnp.sum(both, axis=1, keepdims=True)
    row8 = lax.broadcasted_iota(I32, (8, tm), 0)
    mi = jnp.where(row8 == 0, i1 - ROUTE_EXPERT_LANE0,
                   jnp.where(row8 == 1, i2 - ROUTE_EXPERT_LANE0, jnp.where(row8 == 2, r1, jnp.where(row8 == 3, r2, 0))))
    rowl = lax.broadcasted_iota(I32, (LANES, tm), 0)
    wt = jnp.where(rowl == 0, w1, jnp.where(rowl == 1, w2, 0.0))
    mf = jnp.concatenate([wt[:, j * LANES:(j + 1) * LANES].T for j in range(tm // LANES)], axis=0)
    return mi, mf


def _dest_body(mi_ref, off_ref, da_ref, db_ref):
    tm = da_ref.shape[2]
    mi = mi_ref[...]
    n = mi.shape[1]
    row = lax.broadcasted_iota(I32, (MOE_EXPERTS, n), 0)
    off = off_ref[:, 0:1]
    d0 = jnp.sum(jnp.where(row == mi[0:1], off, 0), axis=0, keepdims=True) + mi[2:3]
    d1 = jnp.sum(jnp.where(row == mi[1:2], off, 0), axis=0, keepdims=True) + mi[3:4]
    for s in range(da_ref.shape[0]):
        da_ref[s] = d0[:, s * tm:(s + 1) * tm]
        db_ref[s] = d1[:, s * tm:(s + 1) * tm]


def _dest_rows(mi, offsets_col, tm):
    t = mi.shape[1]
    tiles_per_step = math.gcd(8, t // tm)
    out = pl.BlockSpec((tiles_per_step, 1, tm), lambda i: (i, 0, 0))
    return pl.pallas_call(
        _dest_body,
        grid=(t // (tm * tiles_per_step),),
        in_specs=[pl.BlockSpec((8, tm * tiles_per_step), lambda i: (0, i)),
                  pl.BlockSpec((MOE_EXPERTS, LANES), lambda i: (0, 0))],
        out_specs=[out, out],
        out_shape=[jax.ShapeDtypeStruct((t // tm, 1, tm), I32)] * 2,
        compiler_params=_cparams("parallel"),
        name="moe_dest",
    )(mi, offsets_col)


def _dispatch_body(zs_ref, na_ref, da_ref, db_ref, xn_ref, xs_ref, zbuf, sem, zsem):
    tm = xn_ref.shape[0]
    te = zbuf.shape[0]
    n_tiles = xs_ref.shape[0] // te

    @pl.when(pl.program_id(0) == 0)
    def _():
        zbuf[...] = jnp.zeros_like(zbuf)

        def fill(row0):
            return pltpu.make_async_copy(zbuf, xs_ref.at[pl.ds(pl.multiple_of(row0, te), te), :], zsem)

        def tail(e, c):
            @pl.when(zs_ref[e] >= 0)
            def _():
                fill(zs_ref[e]).start()
            return c

        def unused(j, c):
            fill(j * te).start()
            return c

        def drain(j, c):
            fill(0).wait()
            return c

        lax.fori_loop(0, MOE_EXPERTS, tail, 0)
        lax.fori_loop(na_ref[0], n_tiles, unused, 0)
        lax.fori_loop(0, na_ref[1] + n_tiles - na_ref[0], drain, 0)

    def start(r, c):
        for k, d_ref in enumerate((da_ref, db_ref)):
            pltpu.make_async_copy(xn_ref.at[pl.ds(r, 1), :], xs_ref.at[pl.ds(d_ref[0, r], 1), :],
                                  sem.at[k]).start(priority=k)
        return c

    lax.fori_loop(0, tm, start, 0, unroll=8)
    for k in range(MOE_TOPK):
        pltpu.make_async_copy(xn_ref, xs_ref.at[pl.ds(0, tm), :], sem.at[k]).wait()


def _dispatch(zero_start, n_active, dest_a, dest_b, xn, n_rows, tm, te):
    t, dw = xn.shape
    smem_row = pl.BlockSpec((None, 1, tm), lambda i, zs, na: (i, 0, 0), memory_space=pltpu.SMEM)
    grid_spec = pltpu.PrefetchScalarGridSpec(
        num_scalar_prefetch=2,
        grid=(t // tm,),
        in_specs=[smem_row, smem_row, pl.BlockSpec((tm, dw), lambda i, zs, na: (i, 0))],
        out_specs=pl.BlockSpec(memory_space=pl.ANY),
        scratch_shapes=[pltpu.VMEM((te, dw), xn.dtype), pltpu.SemaphoreType.DMA((MOE_TOPK,)),
                        pltpu.SemaphoreType.DMA(())],
    )
    return pl.pallas_call(
        _dispatch_body,
        grid_spec=grid_spec,
        out_shape=jax.ShapeDtypeStruct((n_rows, dw), xn.dtype),
        compiler_params=_cparams("arbitrary"),
        name="moe_dispatch",
    )(zero_start, n_active, dest_a, dest_b, xn)


def _experts_body(te_ref, na_ref, x_ref, g_ref, wg_ref, wu_ref, wd_ref, y_ref, wg_b, wu_b, wd_b):
    i = pl.program_id(0)

    @pl.when(i < na_ref[0])
    def _():
        @pl.when(jnp.logical_or(i == 0, te_ref[i] != te_ref[jnp.maximum(i - 1, 0)]))
        def _():
            wg_b[...] = wg_ref[...].astype(BF16)
            wu_b[...] = wu_ref[...].astype(BF16)
            wd_b[...] = wd_ref[...].astype(BF16)

        x = _rms(x_ref[...], g_ref[...]).astype(BF16)
        gt = _dot(x, wg_b[...])
        up = _dot(x, wu_b[...])
        hid = (gt * jax.nn.sigmoid(gt) * up).astype(BF16)
        y_ref[...] = _dot(hid, wd_b[...])

    @pl.when(i >= na_ref[0])
    def _():
        y_ref[...] = jnp.zeros_like(y_ref)


def _experts(tile_expert, n_active, xs, norm_w, w_gate, w_up, w_down, tm):
    r, d = xs.shape
    f = w_gate.shape[2]
    row_tile = lambda i, te, na: (jnp.minimum(i, na[0] - 1), 0)
    grid_spec = pltpu.PrefetchScalarGridSpec(
        num_scalar_prefetch=2,
        grid=(r // tm,),
        in_specs=[pl.BlockSpec((tm, d), row_tile),
                  pl.BlockSpec((1, d), lambda i, te, na: (0, 0)),
                  pl.BlockSpec((None, d, f), lambda i, te, na: (te[i], 0, 0)),
                  pl.BlockSpec((None, d, f), lambda i, te, na: (te[i], 0, 0)),
                  pl.BlockSpec((None, f, d), lambda i, te, na: (te[i], 0, 0))],
        out_specs=pl.BlockSpec((tm, d), lambda i, te, na: (i, 0)),
        scratch_shapes=[pltpu.VMEM((d, f), BF16), pltpu.VMEM((d, f), BF16), pltpu.VMEM((f, d), BF16)],
    )
    return pl.pallas_call(
        _experts_body,
        grid_spec=grid_spec,
        out_shape=jax.ShapeDtypeStruct((r, d), xs.dtype),
        compiler_params=_cparams("arbitrary"),
        name="moe_experts",
    )(tile_expert, n_active, xs, norm_w, w_gate, w_up, w_down)


def _combine_body(da_ref, db_ref, na_ref, nb_ref, x_ref, mf_ref, g_ref, ys_ref, o_ref, buf, sem):
    tm = x_ref.shape[0]
    i = pl.program_id(0)
    slot = lax.rem(i, 2)

    def gather(d_refs, s):
        def start(r, c):
            for k, d_ref in enumerate(d_refs):
                pltpu.make_async_copy(ys_ref.at[pl.ds(d_ref[0, r], 1), :],
                                      buf.at[s, k, pl.ds(r, 1), :], sem.at[s, k]).start(priority=k)
            return c

        lax.fori_loop(0, tm, start, 0, unroll=8)

    @pl.when(i == 0)
    def _():
        gather((da_ref, db_ref), 0)

    @pl.when(i + 1 < pl.num_programs(0))
    def _():
        gather((na_ref, nb_ref), 1 - slot)

    for k in range(MOE_TOPK):
        pltpu.make_async_copy(ys_ref.at[pl.ds(0, tm), :], buf.at[slot, k], sem.at[slot, k]).wait()
    mf = mf_ref[...]
    y = x_ref[...] + mf[:, 0:1] * buf[slot, 0] + mf[:, 1:2] * buf[slot, 1]
    o_ref[...] = _rms(y, g_ref[...])


def _combine(dest_a, dest_b, x2d, mf, norm_w, ys, tm):
    t, d = x2d.shape
    last = t // tm - 1
    cur = pl.BlockSpec((None, 1, tm), lambda i: (i, 0, 0), memory_space=pltpu.SMEM)
    nxt = pl.BlockSpec((None, 1, tm), lambda i: (jnp.minimum(i + 1, last), 0, 0), memory_space=pltpu.SMEM)
    return pl.pallas_call(
        _combine_body,
        grid=(t // tm,),
        in_specs=[cur, cur, nxt, nxt,
                  pl.BlockSpec((tm, d), lambda i: (i, 0)),
                  pl.BlockSpec((tm, LANES), lambda i: (i, 0)),
                  pl.BlockSpec((1, d), lambda i: (0, 0)),
                  pl.BlockSpec(memory_space=pl.ANY)],
        out_specs=pl.BlockSpec((tm, d), lambda i: (i, 0)),
        out_shape=jax.ShapeDtypeStruct((t, d), F32),
        scratch_shapes=[pltpu.VMEM((2, MOE_TOPK, tm, ys.shape[1]), ys.dtype), pltpu.SemaphoreType.DMA((2, MOE_TOPK))],
        compiler_params=_cparams("arbitrary"),
        name="moe_combine",
    )(dest_a, dest_b, dest_a, dest_b, x2d, mf, norm_w, ys)


MOE_ROW_TILE = 512
MOE_TOKEN_TILE = 512


def _moe(x2d, mi, mf, cnt, norm_w, w_gate, w_up, w_down, norm_final):
    t, d = x2d.shape
    tm = MOE_ROW_TILE
    tok = MOE_TOKEN_TILE
    counts = cnt[ROUTE_EXPERT_LANE0:ROUTE_EXPERT_LANE0 + MOE_EXPERTS, 0].astype(I32)
    padded = ((counts + tm - 1) // tm) * tm
    ends = jnp.cumsum(padded)
    offsets = ends - padded
    n_tiles = (MOE_TOPK * t + MOE_EXPERTS * (tm - 1)) // tm
    tile_start = jnp.arange(n_tiles, dtype=I32) * tm
    tile_expert = jnp.minimum(jnp.sum((ends[None, :] <= tile_start[:, None]).astype(I32), axis=1), MOE_EXPERTS - 1)
    n_info = jnp.stack([ends[-1] // tm, jnp.sum((counts > 0).astype(I32))]).astype(I32)
    zero_start = jnp.where(counts > 0, ends - tm, -1).astype(I32)
    offsets_col = jnp.broadcast_to(offsets[:, None], (MOE_EXPERTS, LANES))
    dest_a, dest_b = _dest_rows(mi, offsets_col, tok)
    xs = _dispatch(zero_start, n_info, dest_a, dest_b, x2d, n_tiles * tm, tok, tm)
    ys = _experts(tile_expert, n_info, xs, norm_w, w_gate, w_up, w_down, tm)
    return _combine(dest_a, dest_b, x2d, mf, norm_final, ys, tok)


def kernel(x, mem, norm_mix, w_in, w_out,
           s5_lam_re_f, s5_lam_im_f, s5_log_step_f, s5_b_re_f, s5_b_im_f, s5_c_re_f, s5_c_im_f,
           s5_lam_re_b, s5_lam_im_b, s5_log_step_b, s5_b_re_b, s5_b_im_b, s5_c_re_b, s5_c_im_b,
           s5_d, s5_w_glu, s5_b_glu, s5_norm,
           gdn_conv, gdn_a_log_f, gdn_dt_bias_f, gdn_a_log_b, gdn_dt_bias_b, gdn_norm,
           norm_xattn, norm_mem, xa_wq, xa_wk, xa_wv, xa_wo,
           norm_moe, router_group_w, router_group_b, router_expert_w, router_expert_b,
           moe_w_gate, moe_w_up, moe_w_down, norm_final):
    bsz, seq, d = x.shape
    t = bsz * seq
    l = 0
    x2d = x.reshape(t, d)
    wi = w_in[l]
    wg = wi[:, S5_WIDTH + 4 * GDN_WIDTH:].reshape(d, 4, GDN_HEADS)
    wg = jnp.pad(jnp.swapaxes(wg, 1, 2), ((0, 0), (0, 0), (0, 4))).reshape(d, GDN_HEADS * 8)
    wgt = wg.T
    u3, qkvz, gates = _in_proj(x2d, norm_mix[l][None], wi, wgt)
    s5p = dict(lam_re_f=s5_lam_re_f[l], lam_im_f=s5_lam_im_f[l], log_step_f=s5_log_step_f[l],
               b_re_f=s5_b_re_f[l], b_im_f=s5_b_im_f[l], c_re_f=s5_c_re_f[l], c_im_f=s5_c_im_f[l],
               lam_re_b=s5_lam_re_b[l], lam_im_b=s5_lam_im_b[l], log_step_b=s5_log_step_b[l],
               b_re_b=s5_b_re_b[l], b_im_b=s5_b_im_b[l], c_re_b=s5_c_re_b[l], c_im_b=s5_c_im_b[l], d=s5_d[l])
    g3 = _s5_scan(u3, s5p, seq // S5_CHUNK)
    y_s5 = _s5_post(g3, s5_w_glu[l].T.astype(BF16), s5_b_glu[l][:, None], s5_norm[l][:, None])

    head_par = jnp.stack([gdn_a_log_f[l], gdn_dt_bias_f[l], gdn_a_log_b[l], gdn_dt_bias_b[l]], axis=1)
    head_par = jnp.broadcast_to(jnp.pad(head_par, ((0, 0), (0, 4)))[:, :, None], (GDN_HEADS, 8, LANES))
    y_gdn = _gdn(qkvz, gates, gdn_conv[l], head_par, gdn_norm[l][None], bsz, seq)

    mem_len = mem.shape[1]
    kmem, vmem = _mem_kv(mem.reshape(bsz * mem_len, d), norm_mem[l][None],
                         xa_wk[l].astype(BF16), xa_wv[l].astype(BF16))
    n_pad = LANES - MOE_GROUPS - MOE_EXPERTS
    w_route = jnp.pad(jnp.concatenate([router_group_w[l], router_expert_w[l]], axis=1), ((0, 0), (0, n_pad))).T
    b_route = jnp.pad(jnp.concatenate([router_group_b[l], router_expert_b[l]]), (0, n_pad))
    b_route = jnp.broadcast_to(b_route[:, None], (LANES, LANES))
    x2, mi, mf, cnt = _mix_xattn(x2d, y_s5, y_gdn, w_out[l], norm_xattn[l][None], xa_wq[l], kmem, vmem, xa_wo[l],
                                 norm_moe[l][None], w_route, b_route, seq, mem_len)
    y = _moe(x2, mi, mf, cnt, norm_moe[l][None], moe_w_gate[l], moe_w_up[l], moe_w_down[l], norm_final[None])
    return y.reshape(bsz, seq, d)
```

```python
import functools
import math

import jax
import jax.numpy as jnp
from jax import lax
from jax.experimental import pallas as pl
from jax.experimental.pallas import tpu as pltpu

F32 = jnp.float32
BF16 = jnp.bfloat16
I32 = jnp.int32

D_MODEL = 1024
S5_WIDTH = 512
S5_GROUP = 16
S5_GROUPS = 32
S5_STATE = 64
S5_CHUNK = 128
GDN_HEADS = 4
GDN_HEAD_DIM = 128
GDN_WIDTH = 512
GDN_CONV = 5
GDN_CHUNK = 64
XA_HEADS = 4
XA_HEAD_DIM = 256
MOE_GROUPS = 4
MOE_PER_GROUP = 8
MOE_EXPERTS = 32
MOE_TOPK = 2
D_EXPERT = 256
RMS_EPS = 1e-6
L2_EPS = 1e-6
LANES = 128
VMEM_LIMIT = 56 * 1024 * 1024


def _cparams(*sem):
    return pltpu.CompilerParams(dimension_semantics=tuple(sem), vmem_limit_bytes=VMEM_LIMIT)


def _rms(x, gain):
    return x * lax.rsqrt(jnp.mean(x * x, axis=-1, keepdims=True) + RMS_EPS) * gain


def _dot(a, b):
    return jnp.dot(a, b, preferred_element_type=F32)


def _dot_nt(a, b):
    return lax.dot_general(a, b, (((1,), (1,)), ((), ())), preferred_element_type=F32)


def _dot_tn(a, b):
    return lax.dot_general(a, b, (((0,), (0,)), ((), ())), preferred_element_type=F32)


def _in_proj_body(x_ref, g_ref, w_ref, wgt_ref, u_ref, qkvz_ref, gates_ref, wut_b, wqkvz_b, wgt_b):
    @pl.when(pl.program_id(0) == 0)
    def _():
        wut_b[...] = w_ref[:, :S5_WIDTH].T.astype(BF16)
        wqkvz_b[...] = w_ref[:, S5_WIDTH:S5_WIDTH + 4 * GDN_WIDTH].astype(BF16)
        wgt_b[...] = wgt_ref[...].astype(BF16)

    h = _rms(x_ref[...], g_ref[...]).astype(BF16)
    ut = _dot_nt(wut_b[...], h)
    gt = _dot_nt(wgt_b[...], h)
    for j in range(u_ref.shape[0]):
        u_ref[j] = ut[:, j * S5_CHUNK:(j + 1) * S5_CHUNK]
        gates_ref[j] = gt[:, j * S5_CHUNK:(j + 1) * S5_CHUNK]
    qkvz_ref[...] = _dot(h, wqkvz_b[...]).astype(BF16)


def _in_proj(x2d, norm_w, w_in, wgt, tm=512):
    t = x2d.shape[0]
    nck = tm // S5_CHUNK
    nqkvz = 4 * GDN_WIDTH
    full = lambda shape: pl.BlockSpec(shape, lambda i: (0,) * len(shape))
    return pl.pallas_call(
        _in_proj_body,
        grid=(t // tm,),
        in_specs=[pl.BlockSpec((tm, D_MODEL), lambda i: (i, 0)),
                  full((1, D_MODEL)), full(w_in.shape), full(wgt.shape)],
        out_specs=[pl.BlockSpec((nck, S5_WIDTH, S5_CHUNK), lambda i: (i, 0, 0)),
                   pl.BlockSpec((tm, nqkvz), lambda i: (i, 0)),
                   pl.BlockSpec((nck, wgt.shape[0], S5_CHUNK), lambda i: (i, 0, 0))],
        out_shape=[jax.ShapeDtypeStruct((t // S5_CHUNK, S5_WIDTH, S5_CHUNK), F32),
                   jax.ShapeDtypeStruct((t, nqkvz), BF16),
                   jax.ShapeDtypeStruct((t // S5_CHUNK, wgt.shape[0], S5_CHUNK), F32)],
        scratch_shapes=[pltpu.VMEM((S5_WIDTH, D_MODEL), BF16), pltpu.VMEM((D_MODEL, nqkvz), BF16),
                        pltpu.VMEM(wgt.shape, BF16)],
        compiler_params=_cparams("arbitrary"),
        name="in_proj",
    )(x2d, norm_w, w_in, wgt)


def _cmul(ar, ai, br, bi):
    return ar * br - ai * bi, ar * bi + ai * br


def _cpow_int(lr, li, expo, nbits):
    res_r = jnp.ones(jnp.broadcast_shapes(lr.shape, expo.shape), F32)
    res_i = jnp.zeros_like(res_r)
    for b in range(nbits):
        bit = ((expo >> b) & 1) == 1
        nr, ni = _cmul(res_r, res_i, lr, li)
        res_r = jnp.where(bit, nr, res_r)
        res_i = jnp.where(bit, ni, res_i)
        if b + 1 < nbits:
            lr, li = _cmul(lr, li, lr, li)
    return res_r, res_i


def _lam_bar(re, im, step):
    er = jnp.exp(step * re)
    return er * jnp.cos(step * im), er * jnp.sin(step * im)


def _zoh_coef(re, im, lr, li):
    den = re * re + im * im
    return ((lr - 1.0) * re + li * im) / den, (li * re - (lr - 1.0) * im) / den


def _s5_body(u_ref, lrow_ref, lcol_ref, l256_ref, b_ref, bt_ref, c_ref, ct_ref, d_ref,
             o_ref, vf_ref, vb_ref, m_ref, win_ref, wout_ref, sf_ref, sb_ref, hf_ref, hb_ref, uc_ref, y_ref, *, nchunk):
    L = S5_CHUNK
    P = S5_STATE
    n_rows = u_ref.shape[0]
    nb = n_rows // nchunk
    lane_i = lax.broadcasted_iota(I32, (1, L), 1)

    lcol = lcol_ref[...]
    lbc_r, lbc_i = _lam_bar(lcol[:, 0:2], lcol[:, 2:4], lcol[:, 4:6])
    kc_r, kc_i = _zoh_coef(lcol[:, 0:2], lcol[:, 2:4], lbc_r, lbc_i)
    lrow = lrow_ref[...]
    lbr_r, lbr_i = _lam_bar(lrow[0:2], lrow[2:4], lrow[4:6])
    kr_r, kr_i = _zoh_coef(lrow[0:2], lrow[2:4], lbr_r, lbr_i)
    lf_r, lf_i, lb_r, lb_i = lbc_r[:, 0:1], lbc_i[:, 0:1], lbc_r[:, 1:2], lbc_i[:, 1:2]
    kfr_c, kfi_c, kbr_c, kbi_c = kc_r[:, 0:1], kc_i[:, 0:1], kc_r[:, 1:2], kc_i[:, 1:2]
    kfr_r, kfi_r, kbr_r, kbi_r = kr_r[0:1], kr_i[0:1], kr_r[1:2], kr_i[1:2]

    pwf_r, pwf_i = _cpow_int(lf_r, lf_i, lane_i, 7)
    rvf_r, rvf_i = _cpow_int(lf_r, lf_i, (L - 1) - lane_i, 7)
    pwb_r, pwb_i = _cpow_int(lb_r, lb_i, lane_i, 7)
    rvb_r, rvb_i = _cpow_int(lb_r, lb_i, L - lane_i, 8)
    nxf_r, nxf_i = _cmul(pwf_r, pwf_i, lf_r, lf_i)

    bf_r = kfr_c * b_ref[0] - kfi_c * b_ref[1]
    bf_i = kfr_c * b_ref[1] + kfi_c * b_ref[0]
    bb_r = kbr_c * b_ref[2] - kbi_c * b_ref[3]
    bb_i = kbr_c * b_ref[3] + kbi_c * b_ref[2]
    btf_r = kfr_r * bt_ref[0] - kfi_r * bt_ref[1]
    btf_i = kfr_r * bt_ref[1] + kfi_r * bt_ref[0]
    btb_r = kbr_r * bt_ref[2] - kbi_r * bt_ref[3]
    btb_i = kbr_r * bt_ref[3] + kbi_r * bt_ref[2]

    def taps(c_r, c_i, bt_r, bt_i, pw_r, pw_i):
        cb_r = (bt_r[:, None, :] * c_r[None, :, :] - bt_i[:, None, :] * c_i[None, :, :]).reshape(256, P)
        cb_i = (bt_r[:, None, :] * c_i[None, :, :] + bt_i[:, None, :] * c_r[None, :, :]).reshape(256, P)
        k = (jnp.dot(cb_r, pw_r, preferred_element_type=F32, precision=lax.Precision.HIGHEST)
             - jnp.dot(cb_i, pw_i, preferred_element_type=F32, precision=lax.Precision.HIGHEST))
        return k, jnp.sum(cb_r, axis=1, keepdims=True)

    kf, _ = taps(c_ref[0], c_ref[1], btf_r, btf_i, pwf_r, pwf_i)
    kb, kb0 = taps(c_ref[2], c_ref[3], btb_r, btb_i, rvb_r, rvb_i)
    is0 = lane_i == 0
    vf_ref[...] = kf + jnp.where(is0, kb0, 0.0)
    vb_ref[...] = jnp.where(is0, 0.0, kb)

    row_i = lax.broadcasted_iota(I32, (L, L), 0)
    col_i = lax.broadcasted_iota(I32, (L, L), 1)
    fwd_lane = col_i + row_i < L

    def build_ci(ci, carry):
        for co in range(S5_GROUP):
            r = ci * S5_GROUP + co
            taps_rows = jnp.where(fwd_lane, jnp.broadcast_to(vf_ref[pl.ds(r, 1), :], (L, L)),
                                  jnp.broadcast_to(vb_ref[pl.ds(r, 1), :], (L, L)))
            m_ref[pl.ds(pl.multiple_of(ci * L, L), L), co * L:(co + 1) * L] = pltpu.roll(
                taps_rows, 0, 1, stride=1, stride_axis=0).astype(BF16)
        return carry

    lax.fori_loop(0, S5_GROUP, build_ci, 0)

    for ci in range(S5_GROUP):
        sl = slice(ci * L, (ci + 1) * L)
        br, bi = bf_r[:, ci:ci + 1], bf_i[:, ci:ci + 1]
        win_ref[0 * P:1 * P, sl] = (rvf_r * br - rvf_i * bi).astype(BF16)
        win_ref[1 * P:2 * P, sl] = (rvf_r * bi + rvf_i * br).astype(BF16)
        br, bi = bb_r[:, ci:ci + 1], bb_i[:, ci:ci + 1]
        win_ref[2 * P:3 * P, sl] = (pwb_r * br - pwb_i * bi).astype(BF16)
        win_ref[3 * P:4 * P, sl] = (pwb_r * bi + pwb_i * br).astype(BF16)
    for co in range(S5_GROUP):
        sl = slice(co * L, (co + 1) * L)
        cr, ci_ = ct_ref[0][:, co:co + 1], ct_ref[1][:, co:co + 1]
        wout_ref[0 * P:1 * P, sl] = (cr * nxf_r - ci_ * nxf_i).astype(BF16)
        wout_ref[1 * P:2 * P, sl] = (-(cr * nxf_i + ci_ * nxf_r)).astype(BF16)
        cr, ci_ = ct_ref[2][:, co:co + 1], ct_ref[3][:, co:co + 1]
        wout_ref[2 * P:3 * P, sl] = (cr * rvb_r - ci_ * rvb_i).astype(BF16)
        wout_ref[3 * P:4 * P, sl] = (-(cr * rvb_i + ci_ * rvb_r)).astype(BF16)

    for ci in range(S5_GROUP):
        uc_ref[ci] = u_ref[:, ci, :]
    ucat = jnp.concatenate([uc_ref[ci].astype(BF16) for ci in range(S5_GROUP)], axis=1)

    summ = _dot_nt(ucat, win_ref[...])
    sf_ref[...] = summ[:, :2 * P]
    sb_ref[...] = summ[:, 2 * P:]
    nblk = 2 * L
    for j in range(S5_GROUP * L // nblk):
        y_ref[:, j * nblk:(j + 1) * nblk] = _dot(ucat, m_ref[:, j * nblk:(j + 1) * nblk])
    l256 = l256_ref[...]
    a_mul, a_im = _lam_bar(l256[0:1], l256[1:2], l256[2:3])
    for _ in range(7):
        a_mul, a_im = _cmul(a_mul, a_im, a_mul, a_im)
    lane256 = lax.broadcasted_iota(I32, (1, 4 * P), 1)
    b_mul = jnp.where((lane256 // P) % 2 == 0, -a_im, a_im)

    hf = jnp.zeros((nb, 2 * P), F32)
    hb = jnp.zeros((nb, 2 * P), F32)
    for c in range(nchunk):
        cr = nchunk - 1 - c
        rows_f = pl.ds(c, nb, stride=nchunk)
        rows_b = pl.ds(cr, nb, stride=nchunk)
        hf_ref[rows_f, :] = hf
        hb_ref[rows_b, :] = hb
        hf = a_mul[:, :2 * P] * hf + b_mul[:, :2 * P] * pltpu.roll(hf, P, 1) + sf_ref[rows_f, :]
        hb = a_mul[:, 2 * P:] * hb + b_mul[:, 2 * P:] * pltpu.roll(hb, P, 1) + sb_ref[rows_b, :]
    hprev = jnp.concatenate([hf_ref[...], hb_ref[...]], axis=1).astype(BF16)

    for j in range(S5_GROUP * L // nblk):
        y = y_ref[:, j * nblk:(j + 1) * nblk] + _dot(hprev, wout_ref[:, j * nblk:(j + 1) * nblk])
        for q in range(nblk // L):
            co = j * (nblk // L) + q
            yc = y[:, q * L:(q + 1) * L] + d_ref[co:co + 1, :] * uc_ref[co]
            o_ref[:, co, :] = 0.5 * yc * (1.0 + lax.erf(yc * (2.0 ** -0.5)))


def _s5_scan(u3, p, nchunk):
    n = u3.shape[0]
    g, grp, st, L = S5_GROUPS, S5_GROUP, S5_STATE, S5_CHUNK
    step_f = jnp.exp(p["log_step_f"])[:, None] * jnp.ones((1, st), F32)
    step_b = jnp.exp(p["log_step_b"])[:, None] * jnp.ones((1, st), F32)
    zeros = jnp.zeros((g, st), F32)
    lrow = jnp.stack([p["lam_re_f"], p["lam_re_b"], p["lam_im_f"], p["lam_im_b"], step_f, step_b, zeros, zeros], axis=1)
    lcol = jnp.swapaxes(lrow, 1, 2)
    cat4 = lambda f, b: jnp.concatenate([f, f, b, b], axis=1)
    z256 = jnp.zeros((g, 4 * st), F32)
    l256 = jnp.stack([cat4(p["lam_re_f"], p["lam_re_b"]), cat4(p["lam_im_f"], p["lam_im_b"]),
                      cat4(step_f, step_b)] + [z256] * 5, axis=1)
    b4 = jnp.stack([p["b_re_f"], p["b_im_f"], p["b_re_b"], p["b_im_b"]], axis=1)
    bt4 = jnp.swapaxes(b4, 2, 3)
    c4 = jnp.stack([p["c_re_f"], p["c_im_f"], p["c_re_b"], p["c_im_b"]], axis=1)
    ct4 = jnp.swapaxes(c4, 2, 3)
    dbc = jnp.broadcast_to(p["d"].reshape(g, grp, 1), (g, grp, L))
    per_g = lambda *shape: pl.BlockSpec((None,) + shape, lambda i: (i,) + (0,) * len(shape))
    return pl.pallas_call(
        functools.partial(_s5_body, nchunk=nchunk),
        grid=(g,),
        in_specs=[pl.BlockSpec((n, grp, L), lambda i: (0, i, 0)),
                  per_g(8, st), per_g(st, 8), per_g(8, 4 * st), per_g(4, st, grp), per_g(4, grp, st),
                  per_g(4, grp, st), per_g(4, st, grp), per_g(grp, L)],
        out_specs=pl.BlockSpec((n, grp, L), lambda i: (0, i, 0)),
        out_shape=jax.ShapeDtypeStruct(u3.shape, F32),
        scratch_shapes=[pltpu.VMEM((grp * grp, L), F32), pltpu.VMEM((grp * grp, L), F32),
                        pltpu.VMEM((grp * L, grp * L), BF16),
                        pltpu.VMEM((4 * st, grp * L), BF16), pltpu.VMEM((4 * st, grp * L), BF16),
                        pltpu.VMEM((n, 2 * st), F32), pltpu.VMEM((n, 2 * st), F32),
                        pltpu.VMEM((n, 2 * st), F32), pltpu.VMEM((n, 2 * st), F32),
                        pltpu.VMEM((grp, n, L), F32), pltpu.VMEM((n, grp * L), F32)],
        compiler_params=_cparams("parallel"),
        name="s5_scan",
    )(u3, lrow, lcol, l256, b4, bt4, c4, ct4, dbc)


def _s5_post_body(g_ref, wt_ref, b_ref, nw_ref, o_ref):
    for j in range(g_ref.shape[0]):
        g = g_ref[j]
        z = _dot(wt_ref[...], g.astype(BF16)) + b_ref[...]
        y = g * jax.nn.sigmoid(z)
        y = y * lax.rsqrt(jnp.mean(y * y, axis=0, keepdims=True) + RMS_EPS) * nw_ref[...]
        o_ref[j * S5_CHUNK:(j + 1) * S5_CHUNK, :] = y.T.astype(BF16)


def _s5_post(g3, w_glu_t, b_glu_col, norm_col):
    n = g3.shape[0]
    nck = math.gcd(16, n)
    full = lambda shape: pl.BlockSpec(shape, lambda i: (0,) * len(shape))
    return pl.pallas_call(
        _s5_post_body,
        grid=(n // nck,),
        in_specs=[pl.BlockSpec((nck, S5_WIDTH, S5_CHUNK), lambda i: (i, 0, 0)),
                  full(w_glu_t.shape), full(b_glu_col.shape), full(norm_col.shape)],
        out_specs=pl.BlockSpec((nck * S5_CHUNK, S5_WIDTH), lambda i: (i, 0)),
        out_shape=jax.ShapeDtypeStruct((n * S5_CHUNK, S5_WIDTH), BF16),
        compiler_params=_cparams("parallel"),
        name="s5_post",
    )(g3, w_glu_t, b_glu_col, norm_col)


GDN_BLOCK = 128
GDN_PAIR = 2
GDN_PREP_BATCH = 8


def _packed_tri_inverse(lps, low, upp, bd16, rings):
    def pk(xs, ys):
        outs = []
        for a, b in zip(xs, ys):
            lhs = jnp.concatenate([jnp.where(low, a, 0.0), jnp.where(upp, a, 0.0)], axis=1).astype(BF16)
            rhs = jnp.concatenate([jnp.where(low, b, 0.0), jnp.where(upp, b, 0.0)], axis=0).astype(BF16)
            outs.append(_dot(lhs, rhs))
        return outs

    d = [jnp.where(bd16, lp, 0.0) for lp in lps]
    d2 = pk(d, d)
    d4 = pk(d2, d2)
    d8 = pk(d4, d4)
    a = [y - x - p for x, y, p in zip(d, d2, pk(d, d2))]
    a = [x + y + p for x, y, p in zip(a, d4, pk(a, d4))]
    a = [x + y + p for x, y, p in zip(a, d8, pk(a, d8))]
    for ring in rings:
        n = [jnp.where(ring, lp, 0.0) for lp in lps]
        t = [x + p for x, p in zip(n, pk(a, n))]
        a = [x - y - p for x, y, p in zip(a, t, pk(t, a))]
    return a


def _gdn_body(q_ref, k_ref, v_ref, z_ref, wq_ref, wk_ref, wv_ref, g_ref, hp_ref, nw_ref, o_ref,
              qs, ks, vs, os_, sg, cf, cb, uf, ub, wqf, wqb, qkf, qkb, kdtf, kdtb, eglf, eglb, xpad):
    seq = q_ref.shape[0]
    C = GDN_BLOCK
    hd = GDN_HEAD_DIM
    nck = seq // C
    heads = range(GDN_PAIR)

    pad = 8
    half = (GDN_CONV - 1) // 2
    xpad[0:pad, :] = jnp.zeros((pad, LANES), F32)
    xpad[pad + seq:2 * pad + seq, :] = jnp.zeros((pad, LANES), F32)

    def conv_silu(x_ref, w_ref, j):
        cols = pl.ds(pl.multiple_of(j * hd, hd), hd)
        xpad[pad:pad + seq, :] = x_ref[:, cols].astype(F32)
        w = w_ref[:, cols]
        acc = xpad[pad - half:pad - half + seq, :] * w[0:1]
        for tap in range(1, GDN_CONV):
            acc = acc + xpad[pad - half + tap:pad - half + tap + seq, :] * w[tap:tap + 1]
        return acc * jax.nn.sigmoid(acc)

    def l2n(x):
        return x * lax.rsqrt(jnp.sum(x * x, axis=-1, keepdims=True) + L2_EPS)

    def softplus(x):
        return jnp.maximum(x, 0.0) + jnp.log1p(jnp.exp(-jnp.abs(x)))

    lane = lax.broadcasted_iota(I32, (1, C), 1)

    def prologue(j, carry):
        qs[j] = l2n(conv_silu(q_ref, wq_ref, j)) * (hd ** -0.5)
        ks[j] = l2n(conv_silu(k_ref, wk_ref, j))
        vs[j] = conv_silu(v_ref, wv_ref, j)
        os_[j] = jnp.zeros((seq, hd), F32)
        g = g_ref[:, pl.ds(pl.multiple_of(8 * j, 8), 8), :].reshape(nck * 8, C)
        hp = hp_ref[j]
        sg[j] = jax.nn.sigmoid(g)
        gl_f = -jnp.exp(hp[0:1]) * softplus(g + hp[1:2])
        gl_b = -jnp.exp(hp[2:3]) * softplus(g + hp[3:4])
        sh = 1
        while sh < C:
            gl_f = gl_f + jnp.where(lane >= sh, pltpu.roll(gl_f, sh, 1), 0.0)
            gl_b = gl_b + jnp.where(lane < C - sh, pltpu.roll(gl_b, C - sh, 1), 0.0)
            sh *= 2
        cf[j] = gl_f
        cb[j] = gl_b
        return carry

    lax.fori_loop(0, GDN_PAIR, prologue, 0)

    ri = lax.broadcasted_iota(I32, (C, C), 0)
    ci = lax.broadcasted_iota(I32, (C, C), 1)
    low, upp = ri > ci, ri < ci
    low_i, upp_i = ri >= ci, ri <= ci
    same = lambda w: (ri // w) == (ci // w)
    bd16 = same(16)
    rings = []
    w = 32
    while w <= C:
        rings.append(jnp.logical_and(same(w), jnp.logical_not(same(w // 2))))
        w *= 2

    nbatch = math.gcd(GDN_PREP_BATCH // GDN_PAIR, nck)

    def column(ref, r):
        rows = jnp.broadcast_to(ref[pl.ds(r, 1), :], (C, C))
        return rows.T, rows

    def prepare(it, carry):
        items = [(j, it * nbatch + i) for i in range(nbatch) for j in heads]
        sls = [pl.ds(pl.multiple_of(c * C, C), C) for _, c in items]
        g_f, g_b, bt_f, bt_b, dec_f, dec_b, kb_f, kb_b, aq = [], [], [], [], [], [], [], [], []
        for (j, c), sl in zip(items, sls):
            k = ks[j, sl, :]
            gfc, gfr = column(cf.at[j], c * 8 + 2)
            gbc, gbr = column(cb.at[j], c * 8 + 3)
            g_f.append(gfc)
            g_b.append(gbc)
            bt_f.append(column(sg.at[j], c * 8)[0])
            bt_b.append(column(sg.at[j], c * 8 + 1)[0])
            dec_f.append(jnp.where(low_i, jnp.exp(jnp.where(low_i, gfc - gfr, 0.0)), 0.0))
            dec_b.append(jnp.where(upp_i, jnp.exp(jnp.where(upp_i, gbc - gbr, 0.0)), 0.0))
            kb_f.append(k * bt_f[-1])
            kb_b.append(k * bt_b[-1])
            aq.append(_dot_nt(jnp.concatenate([kb_f[-1], kb_b[-1], qs[j, sl, :]], axis=0).astype(BF16),
                              k.astype(BF16)))
        lps = [jnp.where(low, x[:C] * df, 0.0) + jnp.where(upp, x[C:2 * C] * db, 0.0)
               for x, df, db in zip(aq, dec_f, dec_b)]
        inv = _packed_tri_inverse(lps, low, upp, bd16, rings)
        for n, ((j, c), sl) in enumerate(zip(items, sls)):
            q, k, v = qs[j, sl, :], ks[j, sl, :], vs[j, sl, :]
            for rev, g_c, kb, beta, dec, msk, u_s, wq_s, qk_s, kdt_s, egl_s in (
                    (False, g_f[n], kb_f[n], bt_f[n], dec_f[n], low, uf, wqf, qkf, kdtf, eglf),
                    (True, g_b[n], kb_b[n], bt_b[n], dec_b[n], upp, ub, wqb, qkb, kdtb, eglb)):
                eg = jnp.exp(g_c)
                rhs = jnp.concatenate([v * beta, kb * eg], axis=1)
                uw = rhs + _dot(jnp.where(msk, inv[n], 0.0).astype(BF16), rhs.astype(BF16))
                glast = g_c[0:1] if rev else g_c[C - 1:C]
                u_s[j, sl, :] = uw[:, :C]
                wq_s[j, pl.ds(pl.multiple_of(c * 2 * C, 2 * C), 2 * C), :] = jnp.concatenate(
                    [uw[:, C:], q * eg], axis=0).astype(BF16)
                qk_s[j, sl, :] = (aq[n][2 * C:] * dec).astype(BF16)
                kdt_s[j, sl, :] = (k * jnp.exp(glast - g_c)).T.astype(BF16)
                egl_s[j, pl.ds(c, 1), :] = jnp.exp(glast)
        return carry

    lax.fori_loop(0, nck // nbatch, prepare, 0)

    def body(i, carry):
        chains = []
        for j in heads:
            chains.append((j, i, uf, wqf, qkf, kdtf, eglf))
            chains.append((j, nck - 1 - i, ub, wqb, qkb, kdtb, eglb))
        sls = [pl.ds(pl.multiple_of(c * C, C), C) for _, c, *_ in chains]
        ws_qs = [_dot(wq_s[j, pl.ds(pl.multiple_of(c * 2 * C, 2 * C), 2 * C), :], st.astype(BF16))
                 for (j, c, _, wq_s, *_), st in zip(chains, carry)]
        vnb = [(u_s[j, sl, :] - x[:C]).astype(BF16) for (j, _, u_s, *_), sl, x in zip(chains, sls, ws_qs)]
        new = [st * egl_s[j, pl.ds(c, 1), :] + _dot(kdt_s[j, sl, :], v)
               for (j, c, _, _, _, kdt_s, egl_s), sl, st, v in zip(chains, sls, carry, vnb)]
        for (j, _, _, _, qk_s, _, _), sl, x, v in zip(chains, sls, ws_qs, vnb):
            os_[j, sl, :] += x[C:] + _dot(qk_s[j, sl, :], v)
        return tuple(new)

    zero = jnp.zeros((hd, hd), F32)
    lax.fori_loop(0, nck, body, (zero,) * (2 * GDN_PAIR))

    for j in heads:
        o = os_[j]
        o = o * lax.rsqrt(jnp.mean(o * o, axis=-1, keepdims=True) + RMS_EPS) * nw_ref[...]
        z = z_ref[:, j * hd:(j + 1) * hd].astype(F32)
        o_ref[:, j * hd:(j + 1) * hd] = (o * (z * jax.nn.sigmoid(z))).astype(BF16)


def _gdn(qkvz, gates3, conv_w, head_par, norm_w, bsz, seq):
    t = bsz * seq
    hd = GDN_HEAD_DIM
    nh = GDN_HEADS
    np_ = GDN_PAIR
    wd = np_ * hd
    npairs = nh // np_
    nck = seq // GDN_BLOCK
    col = lambda off: pl.BlockSpec((seq, wd), lambda b, p: (b, off * npairs + p))
    wcol = lambda off: pl.BlockSpec((GDN_CONV, wd), lambda b, p: (0, off * npairs + p))
    per_head = lambda rows, dt: pltpu.VMEM((np_, rows, hd), dt)
    return pl.pallas_call(
        _gdn_body,
        grid=(bsz, npairs),
        in_specs=[col(0), col(1), col(2), col(3), wcol(0), wcol(1), wcol(2),
                  pl.BlockSpec((nck, 8 * np_, GDN_BLOCK), lambda b, p: (b, p, 0)),
                  pl.BlockSpec((np_, 8, LANES), lambda b, p: (p, 0, 0)),
                  pl.BlockSpec((1, hd), lambda b, p: (0, 0))],
        out_specs=pl.BlockSpec((seq, wd), lambda b, p: (b, p)),
        out_shape=jax.ShapeDtypeStruct((t, nh * hd), BF16),
        scratch_shapes=([per_head(seq, F32)] * 4
                        + [per_head(nck * 8, F32)] * 3
                        + [per_head(seq, F32)] * 2
                        + [per_head(2 * seq, BF16)] * 2
                        + [per_head(seq, BF16)] * 2
                        + [per_head(seq, BF16)] * 2
                        + [per_head(nck, F32)] * 2
                        + [pltpu.VMEM((seq + 16, hd), F32)]),
        compiler_params=_cparams("parallel", "parallel"),
        name="gdn",
    )(qkvz, qkvz, qkvz, qkvz, conv_w, conv_w, conv_w, gates3, head_par, norm_w)


def _kv_body(m_ref, g_ref, wk_ref, wv_ref, k_ref, v_ref):
    mn = _rms(m_ref[...], g_ref[...]).astype(BF16)
    k_ref[...] = _dot(mn, wk_ref[...]).astype(BF16)
    v_ref[...] = _dot(mn, wv_ref[...]).astype(BF16)


def _mem_kv(mem2d, norm_w, wk, wv, tm=512):
    r, d = mem2d.shape
    full = lambda shape: pl.BlockSpec(shape, lambda i: (0,) * len(shape))
    tile = pl.BlockSpec((tm, d), lambda i: (i, 0))
    return pl.pallas_call(
        _kv_body,
        grid=(r // tm,),
        in_specs=[tile, full((1, d)), full(wk.shape), full(wv.shape)],
        out_specs=[tile, tile],
        out_shape=[jax.ShapeDtypeStruct((r, d), BF16)] * 2,
        compiler_params=_cparams("parallel"),
        name="mem_kv",
    )(mem2d, norm_w, wk, wv)


def _mix_xattn_body(x_ref, y5_ref, yg_ref, wmix_ref, g_ref, wq_ref, k_ref, v_ref, wo_ref,
                    gm_ref, whi_ref, wlo_ref, br_ref, tri_ref,
                    o_ref, mi_ref, mf_ref, cnt_ref, wmix_b, wq_b, wo_b, carry):
    @pl.when(pl.program_id(0) == 0)
    def _():
        wmix_b[...] = wmix_ref[...].astype(BF16)
        wq_b[...] = wq_ref[...].astype(BF16)
        wo_b[...] = wo_ref[...].astype(BF16)
        carry[...] = jnp.zeros_like(carry)

    x1 = (x_ref[...] + _dot(y5_ref[...], wmix_b[:S5_WIDTH, :]) + _dot(yg_ref[...], wmix_b[S5_WIDTH:, :]))
    xn = _rms(x1, g_ref[...]).astype(BF16)
    q = (_dot(xn, wq_b[...]) * (XA_HEAD_DIM ** -0.5)).astype(BF16)
    hsl = [slice(h * XA_HEAD_DIM, (h + 1) * XA_HEAD_DIM) for h in range(XA_HEADS)]
    sc = [_dot_nt(q[:, sl], k_ref[:, sl]) for sl in hsl]
    pr = [jnp.exp(s - jnp.max(s, axis=-1, keepdims=True)) for s in sc]
    pr = [p / jnp.sum(p, axis=-1, keepdims=True) for p in pr]
    heads = [_dot(p.astype(BF16), v_ref[:, sl]).astype(BF16) for p, sl in zip(pr, hsl)]
    x2 = x1 + _dot(jnp.concatenate(heads, axis=1), wo_b[...])
    o_ref[...] = x2
    mi, mf = _route(_rms(x2, gm_ref[...]), whi_ref[...], wlo_ref[...], br_ref[...], tri_ref[...], carry)
    mi_ref[...] = mi
    mf_ref[...] = mf
    cnt_ref[...] = carry[...]


def _mix_xattn(x2d, y5, yg, w_mix, norm_w, wq, kmem, vmem, wo, norm_moe, w_route, b_route, seq, mem_len, tm=512):
    t, d = x2d.shape
    per_b = seq // tm
    tri = jnp.triu(jnp.ones((tm, tm), BF16))
    w_hi = w_route.astype(BF16)
    w_lo = (w_route - w_hi.astype(F32)).astype(BF16)
    full = lambda shape: pl.BlockSpec(shape, lambda i: (0,) * len(shape))
    tile = lambda w: pl.BlockSpec((tm, w), lambda i: (i, 0))
    return pl.pallas_call(
        _mix_xattn_body,
        grid=(t // tm,),
        in_specs=[tile(d), tile(S5_WIDTH), tile(GDN_WIDTH),
                  full(w_mix.shape), full((1, d)), full(wq.shape),
                  pl.BlockSpec((mem_len, d), lambda i: (i // per_b, 0)),
                  pl.BlockSpec((mem_len, d), lambda i: (i // per_b, 0)),
                  full(wo.shape),
                  full((1, d)), full(w_route.shape), full(w_route.shape), full(b_route.shape), full((tm, tm))],
        out_specs=[tile(d), pl.BlockSpec((8, tm), lambda i: (0, i)), tile(LANES),
                   pl.BlockSpec((ROUTE_ROWS, LANES), lambda i: (0, 0))],
        out_shape=[jax.ShapeDtypeStruct((t, d), F32),
                   jax.ShapeDtypeStruct((8, t), I32),
                   jax.ShapeDtypeStruct((t, LANES), F32),
                   jax.ShapeDtypeStruct((ROUTE_ROWS, LANES), F32)],
        scratch_shapes=[pltpu.VMEM(w_mix.shape, BF16), pltpu.VMEM(wq.shape, BF16), pltpu.VMEM(wo.shape, BF16),
                        pltpu.VMEM((ROUTE_ROWS, LANES), F32)],
        compiler_params=_cparams("arbitrary"),
        name="mix_xattn",
    )(x2d, y5, yg, w_mix, norm_w, wq, kmem, vmem, wo, norm_moe, w_hi, w_lo, b_route, tri)


ROUTE_EXPERT_LANE0 = 4


ROUTE_ROWS = 40


def _route(xn, w_hi, w_lo, bias, tri, carry):
    x_hi = xn.astype(BF16)
    x_lo = (xn - x_hi.astype(F32)).astype(BF16)
    logits = (_dot_nt(w_hi, x_hi) + _dot_nt(w_hi, x_lo) + _dot_nt(w_lo, x_hi))[:ROUTE_ROWS] + bias[:ROUTE_ROWS, 0:1]
    tm = logits.shape[1]
    row = lax.broadcasted_iota(I32, (ROUTE_ROWS, tm), 0)
    neg = jnp.float32(-jnp.inf)
    big = jnp.int32(LANES)

    def top(vals):
        m = jnp.max(vals, axis=0, keepdims=True)
        idx = jnp.min(jnp.where(vals == m, row, big), axis=0, keepdims=True)
        return m, idx

    is_g = row < MOE_GROUPS
    gl = jnp.where(is_g, logits, neg)
    gmax, gidx = top(gl)
    p_top = 1.0 / jnp.sum(jnp.where(is_g, jnp.exp(gl - gmax), 0.0), axis=0, keepdims=True)
    erow = row - ROUTE_EXPERT_LANE0
    in_grp = jnp.logical_and(jnp.logical_and(erow >= 0, erow < MOE_EXPERTS), (erow // MOE_PER_GROUP) == gidx)
    es = jnp.where(in_grp, logits, neg)
    m1, i1 = top(es)
    m2, i2 = top(jnp.where(row == i1, neg, es))
    e21 = jnp.exp(m2 - m1)
    w1 = p_top / (1.0 + e21)
    w2 = p_top * e21 / (1.0 + e21)

    a1 = (row == i1).astype(F32)
    a2 = (row == i2).astype(F32)
    both = a1 + a2
    before = _dot(both.astype(BF16), tri) - both + carry[:, 0:1]
    r1 = jnp.sum(a1 * before, axis=0, keepdims=True).astype(I32)
    r2 = jnp.sum(a2 * before, axis=0, keepdims=True).astype(I32)
    carry[...] = carry[...] + jnp.sum(both, axis=1, keepdims=True)
    row8 = lax.broadcasted_iota(I32, (8, tm), 0)
    mi = jnp.where(row8 == 0, i1 - ROUTE_EXPERT_LANE0,
                   jnp.where(row8 == 1, i2 - ROUTE_EXPERT_LANE0, jnp.where(row8 == 2, r1, jnp.where(row8 == 3, r2, 0))))
    rowl = lax.broadcasted_iota(I32, (LANES, tm), 0)
    wt = jnp.where(rowl == 0, w1, jnp.where(rowl == 1, w2, 0.0))
    mf = jnp.concatenate([wt[:, j * LANES:(j + 1) * LANES].T for j in range(tm // LANES)], axis=0)
    return mi, mf


def _dest_body(mi_ref, off_ref, da_ref, db_ref):
    tm = da_ref.shape[2]
    mi = mi_ref[...]
    n = mi.shape[1]
    row = lax.broadcasted_iota(I32, (MOE_EXPERTS, n), 0)
    off = off_ref[:, 0:1]
    d0 = jnp.sum(jnp.where(row == mi[0:1], off, 0), axis=0, keepdims=True) + mi[2:3]
    d1 = jnp.sum(jnp.where(row == mi[1:2], off, 0), axis=0, keepdims=True) + mi[3:4]
    for s in range(da_ref.shape[0]):
        da_ref[s] = d0[:, s * tm:(s + 1) * tm]
        db_ref[s] = d1[:, s * tm:(s + 1) * tm]


def _dest_rows(mi, offsets_col, tm):
    t = mi.shape[1]
    tiles_per_step = math.gcd(8, t // tm)
    out = pl.BlockSpec((tiles_per_step, 1, tm), lambda i: (i, 0, 0))
    return pl.pallas_call(
        _dest_body,
        grid=(t // (tm * tiles_per_step),),
        in_specs=[pl.BlockSpec((8, tm * tiles_per_step), lambda i: (0, i)),
                  pl.BlockSpec((MOE_EXPERTS, LANES), lambda i: (0, 0))],
        out_specs=[out, out],
        out_shape=[jax.ShapeDtypeStruct((t // tm, 1, tm), I32)] * 2,
        compiler_params=_cparams("parallel"),
        name="moe_dest",
    )(mi, offsets_col)


def _dispatch_body(zs_ref, na_ref, da_ref, db_ref, xn_ref, xs_ref, zbuf, sem, zsem):
    tm = xn_ref.shape[0]
    tz = zbuf.shape[0]
    n_blocks = xs_ref.shape[0] // tz

    @pl.when(pl.program_id(0) == 0)
    def _():
        zbuf[...] = jnp.zeros_like(zbuf)

        def fill(row0):
            return pltpu.make_async_copy(zbuf, xs_ref.at[pl.ds(pl.multiple_of(row0, tz), tz), :], zsem)

        def tail(e, c):
            @pl.when(zs_ref[e] >= 0)
            def _():
                fill(zs_ref[e]).start()
            return c

        def unused(j, c):
            fill(j * tz).start()
            return c

        def drain(j, c):
            fill(0).wait()
            return c

        lax.fori_loop(0, zs_ref.shape[0], tail, 0)
        lax.fori_loop(na_ref[0], n_blocks, unused, 0)
        lax.fori_loop(0, na_ref[1] + n_blocks - na_ref[0], drain, 0)

    def start(r, c):
        for k, d_ref in enumerate((da_ref, db_ref)):
            pltpu.make_async_copy(xn_ref.at[pl.ds(r, 1), :], xs_ref.at[pl.ds(d_ref[0, r], 1), :],
                                  sem.at[k]).start(priority=k)
        return c

    lax.fori_loop(0, tm, start, 0, unroll=8)
    for k in range(MOE_TOPK):
        pltpu.make_async_copy(xn_ref, xs_ref.at[pl.ds(0, tm), :], sem.at[k]).wait()


def _dispatch(zero_start, n_active, dest_a, dest_b, xn, n_rows, tm, te):
    t, dw = xn.shape
    smem_row = pl.BlockSpec((None, 1, tm), lambda i, zs, na: (i, 0, 0), memory_space=pltpu.SMEM)
    grid_spec = pltpu.PrefetchScalarGridSpec(
        num_scalar_prefetch=2,
        grid=(t // tm,),
        in_specs=[smem_row, smem_row, pl.BlockSpec((tm, dw), lambda i, zs, na: (i, 0))],
        out_specs=pl.BlockSpec(memory_space=pl.ANY),
        scratch_shapes=[pltpu.VMEM((te, dw), xn.dtype), pltpu.SemaphoreType.DMA((MOE_TOPK,)),
                        pltpu.SemaphoreType.DMA(())],
    )
    return pl.pallas_call(
        _dispatch_body,
        grid_spec=grid_spec,
        out_shape=jax.ShapeDtypeStruct((n_rows, dw), xn.dtype),
        compiler_params=_cparams("arbitrary"),
        name="moe_dispatch",
    )(zero_start, n_active, dest_a, dest_b, xn)


def _experts_body(te_ref, na_ref, x_ref, g_ref, wg_ref, wu_ref, wd_ref, y_ref, wg_b, wu_b, wd_b):
    i = pl.program_id(0)

    @pl.when(i < na_ref[0])
    def _():
        @pl.when(jnp.logical_or(i == 0, te_ref[i] != te_ref[jnp.maximum(i - 1, 0)]))
        def _():
            wg_b[...] = wg_ref[...].astype(BF16)
            wu_b[...] = wu_ref[...].astype(BF16)
            wd_b[...] = wd_ref[...].astype(BF16)

        x = _rms(x_ref[...], g_ref[...]).astype(BF16)
        gt = _dot(x, wg_b[...])
        up = _dot(x, wu_b[...])
        hid = (gt * jax.nn.sigmoid(gt) * up).astype(BF16)
        y_ref[...] = _dot(hid, wd_b[...])

    @pl.when(i >= na_ref[0])
    def _():
        y_ref[...] = jnp.zeros_like(y_ref)


def _experts(tile_expert, n_active, xs, norm_w, w_gate, w_up, w_down, tm):
    r, d = xs.shape
    f = w_gate.shape[2]
    row_tile = lambda i, te, na: (jnp.minimum(i, na[0] - 1), 0)
    grid_spec = pltpu.PrefetchScalarGridSpec(
        num_scalar_prefetch=2,
        grid=(r // tm,),
        in_specs=[pl.BlockSpec((tm, d), row_tile),
                  pl.BlockSpec((1, d), lambda i, te, na: (0, 0)),
                  pl.BlockSpec((None, d, f), lambda i, te, na: (te[i], 0, 0)),
                  pl.BlockSpec((None, d, f), lambda i, te, na: (te[i], 0, 0)),
                  pl.BlockSpec((None, f, d), lambda i, te, na: (te[i], 0, 0))],
        out_specs=pl.BlockSpec((tm, d), lambda i, te, na: (i, 0)),
        scratch_shapes=[pltpu.VMEM((d, f), BF16), pltpu.VMEM((d, f), BF16), pltpu.VMEM((f, d), BF16)],
    )
    return pl.pallas_call(
        _experts_body,
        grid_spec=grid_spec,
        out_shape=jax.ShapeDtypeStruct((r, d), xs.dtype),
        compiler_params=_cparams("arbitrary"),
        name="moe_experts",
    )(tile_expert, n_active, xs, norm_w, w_gate, w_up, w_down)


def _combine_body(da_ref, db_ref, na_ref, nb_ref, x_ref, mf_ref, g_ref, ys_ref, o_ref, buf, sem):
    tm = x_ref.shape[0]
    i = pl.program_id(0)
    slot = lax.rem(i, 2)

    def gather(d_refs, s):
        def start(r, c):
            for k, d_ref in enumerate(d_refs):
                pltpu.make_async_copy(ys_ref.at[pl.ds(d_ref[0, r], 1), :],
                                      buf.at[s, k, pl.ds(r, 1), :], sem.at[s, k]).start(priority=k)
            return c

        lax.fori_loop(0, tm, start, 0, unroll=8)

    @pl.when(i == 0)
    def _():
        gather((da_ref, db_ref), 0)

    @pl.when(i + 1 < pl.num_programs(0))
    def _():
        gather((na_ref, nb_ref), 1 - slot)

    for k in range(MOE_TOPK):
        pltpu.make_async_copy(ys_ref.at[pl.ds(0, tm), :], buf.at[slot, k], sem.at[slot, k]).wait()
    mf = mf_ref[...]
    y = x_ref[...] + mf[:, 0:1] * buf[slot, 0] + mf[:, 1:2] * buf[slot, 1]
    o_ref[...] = _rms(y, g_ref[...])


def _combine(dest_a, dest_b, x2d, mf, norm_w, ys, tm):
    t, d = x2d.shape
    last = t // tm - 1
    cur = pl.BlockSpec((None, 1, tm), lambda i: (i, 0, 0), memory_space=pltpu.SMEM)
    nxt = pl.BlockSpec((None, 1, tm), lambda i: (jnp.minimum(i + 1, last), 0, 0), memory_space=pltpu.SMEM)
    return pl.pallas_call(
        _combine_body,
        grid=(t // tm,),
        in_specs=[cur, cur, nxt, nxt,
                  pl.BlockSpec((tm, d), lambda i: (i, 0)),
                  pl.BlockSpec((tm, LANES), lambda i: (i, 0)),
                  pl.BlockSpec((1, d), lambda i: (0, 0)),
                  pl.BlockSpec(memory_space=pl.ANY)],
        out_specs=pl.BlockSpec((tm, d), lambda i: (i, 0)),
        out_shape=jax.ShapeDtypeStruct((t, d), F32),
        scratch_shapes=[pltpu.VMEM((2, MOE_TOPK, tm, ys.shape[1]), ys.dtype), pltpu.SemaphoreType.DMA((2, MOE_TOPK))],
        compiler_params=_cparams("arbitrary"),
        name="moe_combine",
    )(dest_a, dest_b, dest_a, dest_b, x2d, mf, norm_w, ys)


MOE_ROW_TILE = 512
MOE_TOKEN_TILE = 1024
MOE_ZERO_BLOCK = 256


def _moe(x2d, mi, mf, cnt, norm_w, w_gate, w_up, w_down, norm_final):
    t, d = x2d.shape
    tm = MOE_ROW_TILE
    tok = MOE_TOKEN_TILE
    counts = cnt[ROUTE_EXPERT_LANE0:ROUTE_EXPERT_LANE0 + MOE_EXPERTS, 0].astype(I32)
    padded = ((counts + tm - 1) // tm) * tm
    ends = jnp.cumsum(padded)
    offsets = ends - padded
    n_tiles = (MOE_TOPK * t + MOE_EXPERTS * (tm - 1)) // tm
    tile_start = jnp.arange(n_tiles, dtype=I32) * tm
    tile_expert = jnp.minimum(jnp.sum((ends[None, :] <= tile_start[:, None]).astype(I32), axis=1), MOE_EXPERTS - 1)
    n_info = jnp.stack([ends[-1] // tm, jnp.sum((counts > 0).astype(I32))]).astype(I32)
    tz = MOE_ZERO_BLOCK
    pad = padded - counts
    zero_start = jnp.concatenate([jnp.where(pad > j * tz, ends - (j + 1) * tz, -1) for j in range(tm // tz)])
    zero_start = jnp.where(jnp.tile(counts, tm // tz) > 0, zero_start, -1).astype(I32)
    z_info = jnp.stack([ends[-1] // tz, jnp.sum((zero_start >= 0).astype(I32))]).astype(I32)
    offsets_col = jnp.broadcast_to(offsets[:, None], (MOE_EXPERTS, LANES))
    dest_a, dest_b = _dest_rows(mi, offsets_col, tok)
    xs = _dispatch(zero_start, z_info, dest_a, dest_b, x2d, n_tiles * tm, tok, tz)
    ys = _experts(tile_expert, n_info, xs, norm_w, w_gate, w_up, w_down, tm)
    return _combine(dest_a, dest_b, x2d, mf, norm_final, ys, tok)


def kernel(x, mem, norm_mix, w_in, w_out,
           s5_lam_re_f, s5_lam_im_f, s5_log_step_f, s5_b_re_f, s5_b_im_f, s5_c_re_f, s5_c_im_f,
           s5_lam_re_b, s5_lam_im_b, s5_log_step_b, s5_b_re_b, s5_b_im_b, s5_c_re_b, s5_c_im_b,
           s5_d, s5_w_glu, s5_b_glu, s5_norm,
           gdn_conv, gdn_a_log_f, gdn_dt_bias_f, gdn_a_log_b, gdn_dt_bias_b, gdn_norm,
           norm_xattn, norm_mem, xa_wq, xa_wk, xa_wv, xa_wo,
           norm_moe, router_group_w, router_group_b, router_expert_w, router_expert_b,
           moe_w_gate, moe_w_up, moe_w_down, norm_final):
    bsz, seq, d = x.shape
    t = bsz * seq
    l = 0
    x2d = x.reshape(t, d)
    wi = w_in[l]
    wg = wi[:, S5_WIDTH + 4 * GDN_WIDTH:].reshape(d, 4, GDN_HEADS)
    wg = jnp.pad(jnp.swapaxes(wg, 1, 2), ((0, 0), (0, 0), (0, 4))).reshape(d, GDN_HEADS * 8)
    wgt = wg.T
    u3, qkvz, gates = _in_proj(x2d, norm_mix[l][None], wi, wgt)
    s5p = dict(lam_re_f=s5_lam_re_f[l], lam_im_f=s5_lam_im_f[l], log_step_f=s5_log_step_f[l],
               b_re_f=s5_b_re_f[l], b_im_f=s5_b_im_f[l], c_re_f=s5_c_re_f[l], c_im_f=s5_c_im_f[l],
               lam_re_b=s5_lam_re_b[l], lam_im_b=s5_lam_im_b[l], log_step_b=s5_log_step_b[l],
               b_re_b=s5_b_re_b[l], b_im_b=s5_b_im_b[l], c_re_b=s5_c_re_b[l], c_im_b=s5_c_im_b[l], d=s5_d[l])
    g3 = _s5_scan(u3, s5p, seq // S5_CHUNK)
    y_s5 = _s5_post(g3, s5_w_glu[l].T.astype(BF16), s5_b_glu[l][:, None], s5_norm[l][:, None])

    head_par = jnp.stack([gdn_a_log_f[l], gdn_dt_bias_f[l], gdn_a_log_b[l], gdn_dt_bias_b[l]], axis=1)
    head_par = jnp.broadcast_to(jnp.pad(head_par, ((0, 0), (0, 4)))[:, :, None], (GDN_HEADS, 8, LANES))
    y_gdn = _gdn(qkvz, gates, gdn_conv[l], head_par, gdn_norm[l][None], bsz, seq)

    mem_len = mem.shape[1]
    kmem, vmem = _mem_kv(mem.reshape(bsz * mem_len, d), norm_mem[l][None],
                         xa_wk[l].astype(BF16), xa_wv[l].astype(BF16))
    n_pad = LANES - MOE_GROUPS - MOE_EXPERTS
    w_route = jnp.pad(jnp.concatenate([router_group_w[l], router_expert_w[l]], axis=1), ((0, 0), (0, n_pad))).T
    b_route = jnp.pad(jnp.concatenate([router_group_b[l], router_expert_b[l]]), (0, n_pad))
    b_route = jnp.broadcast_to(b_route[:, None], (LANES, LANES))
    x2, mi, mf, cnt = _mix_xattn(x2d, y_s5, y_gdn, w_out[l], norm_xattn[l][None], xa_wq[l], kmem, vmem, xa_wo[l],
                                 norm_moe[l][None], w_route, b_route, seq, mem_len)
    y = _moe(x2, mi, mf, cnt, norm_moe[l][None], moe_w_gate[l], moe_w_up[l], moe_w_down[l], norm_final[None])
    return y.reshape(bsz, seq, d)
```

```python
import functools
import math

import jax
import jax.numpy as jnp
from jax import lax
from jax.experimental import pallas as pl
from jax.experimental.pallas import tpu as pltpu

F32 = jnp.float32
BF16 = jnp.bfloat16
I32 = jnp.int32

D_MODEL = 1024
S5_WIDTH = 512
S5_GROUP = 16
S5_GROUPS = 32
S5_STATE = 64
S5_CHUNK = 128
GDN_HEADS = 4
GDN_HEAD_DIM = 128
GDN_WIDTH = 512
GDN_CONV = 5
GDN_CHUNK = 64
XA_HEADS = 4
XA_HEAD_DIM = 256
MOE_GROUPS = 4
MOE_PER_GROUP = 8
MOE_EXPERTS = 32
MOE_TOPK = 2
D_EXPERT = 256
RMS_EPS = 1e-6
L2_EPS = 1e-6
LANES = 128
VMEM_LIMIT = 56 * 1024 * 1024


def _cparams(*sem):
    return pltpu.CompilerParams(dimension_semantics=tuple(sem), vmem_limit_bytes=VMEM_LIMIT)


def _rms(x, gain):
    return x * lax.rsqrt(jnp.mean(x * x, axis=-1, keepdims=True) + RMS_EPS) * gain


def _dot(a, b):
    return jnp.dot(a, b, preferred_element_type=F32)


def _dot_nt(a, b):
    return lax.dot_general(a, b, (((1,), (1,)), ((), ())), preferred_element_type=F32)


def _dot_tn(a, b):
    return lax.dot_general(a, b, (((0,), (0,)), ((), ())), preferred_element_type=F32)


def _in_proj_body(x_ref, g_ref, w_ref, wgt_ref, u_ref, qkvz_ref, gates_ref, wut_b, wqkvz_b, wgt_b):
    @pl.when(pl.program_id(0) == 0)
    def _():
        wut_b[...] = w_ref[:, :S5_WIDTH].T.astype(BF16)
        wqkvz_b[...] = w_ref[:, S5_WIDTH:S5_WIDTH + 4 * GDN_WIDTH].astype(BF16)
        wgt_b[...] = wgt_ref[...].astype(BF16)

    h = _rms(x_ref[...], g_ref[...]).astype(BF16)
    ut = _dot_nt(wut_b[...], h)
    gt = _dot_nt(wgt_b[...], h)
    for j in range(u_ref.shape[0]):
        u_ref[j] = ut[:, j * S5_CHUNK:(j + 1) * S5_CHUNK]
        gates_ref[j] = gt[:, j * S5_CHUNK:(j + 1) * S5_CHUNK]
    qkvz_ref[...] = _dot(h, wqkvz_b[...]).astype(BF16)


def _in_proj(x2d, norm_w, w_in, wgt, tm=512):
    t = x2d.shape[0]
    nck = tm // S5_CHUNK
    nqkvz = 4 * GDN_WIDTH
    full = lambda shape: pl.BlockSpec(shape, lambda i: (0,) * len(shape))
    return pl.pallas_call(
        _in_proj_body,
        grid=(t // tm,),
        in_specs=[pl.BlockSpec((tm, D_MODEL), lambda i: (i, 0)),
                  full((1, D_MODEL)), full(w_in.shape), full(wgt.shape)],
        out_specs=[pl.BlockSpec((nck, S5_WIDTH, S5_CHUNK), lambda i: (i, 0, 0)),
                   pl.BlockSpec((tm, nqkvz), lambda i: (i, 0)),
                   pl.BlockSpec((nck, wgt.shape[0], S5_CHUNK), lambda i: (i, 0, 0))],
        out_shape=[jax.ShapeDtypeStruct((t // S5_CHUNK, S5_WIDTH, S5_CHUNK), F32),
                   jax.ShapeDtypeStruct((t, nqkvz), BF16),
                   jax.ShapeDtypeStruct((t // S5_CHUNK, wgt.shape[0], S5_CHUNK), F32)],
        scratch_shapes=[pltpu.VMEM((S5_WIDTH, D_MODEL), BF16), pltpu.VMEM((D_MODEL, nqkvz), BF16),
                        pltpu.VMEM(wgt.shape, BF16)],
        compiler_params=_cparams("arbitrary"),
        name="in_proj",
    )(x2d, norm_w, w_in, wgt)


def _cmul(ar, ai, br, bi):
    return ar * br - ai * bi, ar * bi + ai * br


def _cpow_int(lr, li, expo, nbits):
    res_r = jnp.ones(jnp.broadcast_shapes(lr.shape, expo.shape), F32)
    res_i = jnp.zeros_like(res_r)
    for b in range(nbits):
        bit = ((expo >> b) & 1) == 1
        nr, ni = _cmul(res_r, res_i, lr, li)
        res_r = jnp.where(bit, nr, res_r)
        res_i = jnp.where(bit, ni, res_i)
        if b + 1 < nbits:
            lr, li = _cmul(lr, li, lr, li)
    return res_r, res_i


def _lam_bar(re, im, step):
    er = jnp.exp(step * re)
    return er * jnp.cos(step * im), er * jnp.sin(step * im)


def _zoh_coef(re, im, lr, li):
    den = re * re + im * im
    return ((lr - 1.0) * re + li * im) / den, (li * re - (lr - 1.0) * im) / den


def _s5_body(u_ref, lrow_ref, lcol_ref, l256_ref, b_ref, bt_ref, c_ref, ct_ref, d_ref,
             o_ref, vf_ref, vb_ref, m_ref, win_ref, wout_ref, sf_ref, sb_ref, hf_ref, hb_ref, uc_ref, y_ref, *, nchunk):
    L = S5_CHUNK
    P = S5_STATE
    n_rows = u_ref.shape[0]
    nb = n_rows // nchunk
    lane_i = lax.broadcasted_iota(I32, (1, L), 1)

    lcol = lcol_ref[...]
    lbc_r, lbc_i = _lam_bar(lcol[:, 0:2], lcol[:, 2:4], lcol[:, 4:6])
    kc_r, kc_i = _zoh_coef(lcol[:, 0:2], lcol[:, 2:4], lbc_r, lbc_i)
    lrow = lrow_ref[...]
    lbr_r, lbr_i = _lam_bar(lrow[0:2], lrow[2:4], lrow[4:6])
    kr_r, kr_i = _zoh_coef(lrow[0:2], lrow[2:4], lbr_r, lbr_i)
    lf_r, lf_i, lb_r, lb_i = lbc_r[:, 0:1], lbc_i[:, 0:1], lbc_r[:, 1:2], lbc_i[:, 1:2]
    kfr_c, kfi_c, kbr_c, kbi_c = kc_r[:, 0:1], kc_i[:, 0:1], kc_r[:, 1:2], kc_i[:, 1:2]
    kfr_r, kfi_r, kbr_r, kbi_r = kr_r[0:1], kr_i[0:1], kr_r[1:2], kr_i[1:2]

    pwf_r, pwf_i = _cpow_int(lf_r, lf_i, lane_i, 7)
    rvf_r, rvf_i = _cpow_int(lf_r, lf_i, (L - 1) - lane_i, 7)
    pwb_r, pwb_i = _cpow_int(lb_r, lb_i, lane_i, 7)
    rvb_r, rvb_i = _cpow_int(lb_r, lb_i, L - lane_i, 8)
    nxf_r, nxf_i = _cmul(pwf_r, pwf_i, lf_r, lf_i)

    bf_r = kfr_c * b_ref[0] - kfi_c * b_ref[1]
    bf_i = kfr_c * b_ref[1] + kfi_c * b_ref[0]
    bb_r = kbr_c * b_ref[2] - kbi_c * b_ref[3]
    bb_i = kbr_c * b_ref[3] + kbi_c * b_ref[2]
    btf_r = kfr_r * bt_ref[0] - kfi_r * bt_ref[1]
    btf_i = kfr_r * bt_ref[1] + kfi_r * bt_ref[0]
    btb_r = kbr_r * bt_ref[2] - kbi_r * bt_ref[3]
    btb_i = kbr_r * bt_ref[3] + kbi_r * bt_ref[2]

    def taps(c_r, c_i, bt_r, bt_i, pw_r, pw_i):
        cb_r = (bt_r[:, None, :] * c_r[None, :, :] - bt_i[:, None, :] * c_i[None, :, :]).reshape(256, P)
        cb_i = (bt_r[:, None, :] * c_i[None, :, :] + bt_i[:, None, :] * c_r[None, :, :]).reshape(256, P)
        k = (jnp.dot(cb_r, pw_r, preferred_element_type=F32, precision=lax.Precision.HIGHEST)
             - jnp.dot(cb_i, pw_i, preferred_element_type=F32, precision=lax.Precision.HIGHEST))
        return k, jnp.sum(cb_r, axis=1, keepdims=True)

    kf, _ = taps(c_ref[0], c_ref[1], btf_r, btf_i, pwf_r, pwf_i)
    kb, kb0 = taps(c_ref[2], c_ref[3], btb_r, btb_i, rvb_r, rvb_i)
    is0 = lane_i == 0
    vf_ref[...] = kf + jnp.where(is0, kb0, 0.0)
    vb_ref[...] = jnp.where(is0, 0.0, kb)

    row_i = lax.broadcasted_iota(I32, (L, L), 0)
    col_i = lax.broadcasted_iota(I32, (L, L), 1)
    fwd_lane = col_i + row_i < L

    def build_ci(ci, carry):
        for co in range(S5_GROUP):
            r = ci * S5_GROUP + co
            taps_rows = jnp.where(fwd_lane, jnp.broadcast_to(vf_ref[pl.ds(r, 1), :], (L, L)),
                                  jnp.broadcast_to(vb_ref[pl.ds(r, 1), :], (L, L)))
            m_ref[pl.ds(pl.multiple_of(ci * L, L), L), co * L:(co + 1) * L] = pltpu.roll(
                taps_rows, 0, 1, stride=1, stride_axis=0).astype(BF16)
        return carry

    lax.fori_loop(0, S5_GROUP, build_ci, 0, unroll=4)

    for ci in range(S5_GROUP):
        sl = slice(ci * L, (ci + 1) * L)
        br, bi = bf_r[:, ci:ci + 1], bf_i[:, ci:ci + 1]
        win_ref[0 * P:1 * P, sl] = (rvf_r * br - rvf_i * bi).astype(BF16)
        win_ref[1 * P:2 * P, sl] = (rvf_r * bi + rvf_i * br).astype(BF16)
        br, bi = bb_r[:, ci:ci + 1], bb_i[:, ci:ci + 1]
        win_ref[2 * P:3 * P, sl] = (pwb_r * br - pwb_i * bi).astype(BF16)
        win_ref[3 * P:4 * P, sl] = (pwb_r * bi + pwb_i * br).astype(BF16)
    for co in range(S5_GROUP):
        sl = slice(co * L, (co + 1) * L)
        cr, ci_ = ct_ref[0][:, co:co + 1], ct_ref[1][:, co:co + 1]
        wout_ref[0 * P:1 * P, sl] = (cr * nxf_r - ci_ * nxf_i).astype(BF16)
        wout_ref[1 * P:2 * P, sl] = (-(cr * nxf_i + ci_ * nxf_r)).astype(BF16)
        cr, ci_ = ct_ref[2][:, co:co + 1], ct_ref[3][:, co:co + 1]
        wout_ref[2 * P:3 * P, sl] = (cr * rvb_r - ci_ * rvb_i).astype(BF16)
        wout_ref[3 * P:4 * P, sl] = (-(cr * rvb_i + ci_ * rvb_r)).astype(BF16)

    for ci in range(S5_GROUP):
        uc_ref[ci] = u_ref[:, ci, :]
    ucat = jnp.concatenate([uc_ref[ci].astype(BF16) for ci in range(S5_GROUP)], axis=1)

    summ = _dot_nt(ucat, win_ref[...])
    sf_ref[...] = summ[:, :2 * P]
    sb_ref[...] = summ[:, 2 * P:]
    nblk = 2 * L
    for j in range(S5_GROUP * L // nblk):
        y_ref[:, j * nblk:(j + 1) * nblk] = _dot(ucat, m_ref[:, j * nblk:(j + 1) * nblk])
    l256 = l256_ref[...]
    a_mul, a_im = _lam_bar(l256[0:1], l256[1:2], l256[2:3])
    for _ in range(7):
        a_mul, a_im = _cmul(a_mul, a_im, a_mul, a_im)
    lane256 = lax.broadcasted_iota(I32, (1, 4 * P), 1)
    b_mul = jnp.where((lane256 // P) % 2 == 0, -a_im, a_im)

    hf = jnp.zeros((nb, 2 * P), F32)
    hb = jnp.zeros((nb, 2 * P), F32)
    for c in range(nchunk):
        cr = nchunk - 1 - c
        rows_f = pl.ds(c, nb, stride=nchunk)
        rows_b = pl.ds(cr, nb, stride=nchunk)
        hf_ref[rows_f, :] = hf
        hb_ref[rows_b, :] = hb
        hf = a_mul[:, :2 * P] * hf + b_mul[:, :2 * P] * pltpu.roll(hf, P, 1) + sf_ref[rows_f, :]
        hb = a_mul[:, 2 * P:] * hb + b_mul[:, 2 * P:] * pltpu.roll(hb, P, 1) + sb_ref[rows_b, :]
    hprev = jnp.concatenate([hf_ref[...], hb_ref[...]], axis=1).astype(BF16)

    for j in range(S5_GROUP * L // nblk):
        y = y_ref[:, j * nblk:(j + 1) * nblk] + _dot(hprev, wout_ref[:, j * nblk:(j + 1) * nblk])
        for q in range(nblk // L):
            co = j * (nblk // L) + q
            yc = y[:, q * L:(q + 1) * L] + d_ref[co:co + 1, :] * uc_ref[co]
            o_ref[:, co, :] = 0.5 * yc * (1.0 + lax.erf(yc * (2.0 ** -0.5)))


def _s5_scan(u3, p, nchunk):
    n = u3.shape[0]
    g, grp, st, L = S5_GROUPS, S5_GROUP, S5_STATE, S5_CHUNK
    step_f = jnp.exp(p["log_step_f"])[:, None] * jnp.ones((1, st), F32)
    step_b = jnp.exp(p["log_step_b"])[:, None] * jnp.ones((1, st), F32)
    zeros = jnp.zeros((g, st), F32)
    lrow = jnp.stack([p["lam_re_f"], p["lam_re_b"], p["lam_im_f"], p["lam_im_b"], step_f, step_b, zeros, zeros], axis=1)
    lcol = jnp.swapaxes(lrow, 1, 2)
    cat4 = lambda f, b: jnp.concatenate([f, f, b, b], axis=1)
    z256 = jnp.zeros((g, 4 * st), F32)
    l256 = jnp.stack([cat4(p["lam_re_f"], p["lam_re_b"]), cat4(p["lam_im_f"], p["lam_im_b"]),
                      cat4(step_f, step_b)] + [z256] * 5, axis=1)
    b4 = jnp.stack([p["b_re_f"], p["b_im_f"], p["b_re_b"], p["b_im_b"]], axis=1)
    bt4 = jnp.swapaxes(b4, 2, 3)
    c4 = jnp.stack([p["c_re_f"], p["c_im_f"], p["c_re_b"], p["c_im_b"]], axis=1)
    ct4 = jnp.swapaxes(c4, 2, 3)
    dbc = jnp.broadcast_to(p["d"].reshape(g, grp, 1), (g, grp, L))
    per_g = lambda *shape: pl.BlockSpec((None,) + shape, lambda i: (i,) + (0,) * len(shape))
    return pl.pallas_call(
        functools.partial(_s5_body, nchunk=nchunk),
        grid=(g,),
        in_specs=[pl.BlockSpec((n, grp, L), lambda i: (0, i, 0)),
                  per_g(8, st), per_g(st, 8), per_g(8, 4 * st), per_g(4, st, grp), per_g(4, grp, st),
                  per_g(4, grp, st), per_g(4, st, grp), per_g(grp, L)],
        out_specs=pl.BlockSpec((n, grp, L), lambda i: (0, i, 0)),
        out_shape=jax.ShapeDtypeStruct(u3.shape, F32),
        scratch_shapes=[pltpu.VMEM((grp * grp, L), F32), pltpu.VMEM((grp * grp, L), F32),
                        pltpu.VMEM((grp * L, grp * L), BF16),
                        pltpu.VMEM((4 * st, grp * L), BF16), pltpu.VMEM((4 * st, grp * L), BF16),
                        pltpu.VMEM((n, 2 * st), F32), pltpu.VMEM((n, 2 * st), F32),
                        pltpu.VMEM((n, 2 * st), F32), pltpu.VMEM((n, 2 * st), F32),
                        pltpu.VMEM((grp, n, L), F32), pltpu.VMEM((n, grp * L), F32)],
        compiler_params=_cparams("parallel"),
        name="s5_scan",
    )(u3, lrow, lcol, l256, b4, bt4, c4, ct4, dbc)


def _s5_post_body(g_ref, wt_ref, b_ref, nw_ref, o_ref):
    for j in range(g_ref.shape[0]):
        g = g_ref[j]
        z = _dot(wt_ref[...], g.astype(BF16)) + b_ref[...]
        y = g * jax.nn.sigmoid(z)
        y = y * lax.rsqrt(jnp.mean(y * y, axis=0, keepdims=True) + RMS_EPS) * nw_ref[...]
        o_ref[j * S5_CHUNK:(j + 1) * S5_CHUNK, :] = y.T.astype(BF16)


def _s5_post(g3, w_glu_t, b_glu_col, norm_col):
    n = g3.shape[0]
    nck = math.gcd(16, n)
    full = lambda shape: pl.BlockSpec(shape, lambda i: (0,) * len(shape))
    return pl.pallas_call(
        _s5_post_body,
        grid=(n // nck,),
        in_specs=[pl.BlockSpec((nck, S5_WIDTH, S5_CHUNK), lambda i: (i, 0, 0)),
                  full(w_glu_t.shape), full(b_glu_col.shape), full(norm_col.shape)],
        out_specs=pl.BlockSpec((nck * S5_CHUNK, S5_WIDTH), lambda i: (i, 0)),
        out_shape=jax.ShapeDtypeStruct((n * S5_CHUNK, S5_WIDTH), BF16),
        compiler_params=_cparams("parallel"),
        name="s5_post",
    )(g3, w_glu_t, b_glu_col, norm_col)


GDN_BLOCK = 128
GDN_PAIR = 2
GDN_PREP_BATCH = 8


def _packed_tri_inverse(lps, low, upp, bd16, rings):
    def pk(xs, ys):
        outs = []
        for a, b in zip(xs, ys):
            lhs = jnp.concatenate([jnp.where(low, a, 0.0), jnp.where(upp, a, 0.0)], axis=1).astype(BF16)
            rhs = jnp.concatenate([jnp.where(low, b, 0.0), jnp.where(upp, b, 0.0)], axis=0).astype(BF16)
            outs.append(_dot(lhs, rhs))
        return outs

    d = [jnp.where(bd16, lp, 0.0) for lp in lps]
    d2 = pk(d, d)
    d4 = pk(d2, d2)
    d8 = pk(d4, d4)
    a = [y - x - p for x, y, p in zip(d, d2, pk(d, d2))]
    a = [x + y + p for x, y, p in zip(a, d4, pk(a, d4))]
    a = [x + y + p for x, y, p in zip(a, d8, pk(a, d8))]
    for ring in rings:
        n = [jnp.where(ring, lp, 0.0) for lp in lps]
        t = [x + p for x, p in zip(n, pk(a, n))]
        a = [x - y - p for x, y, p in zip(a, t, pk(t, a))]
    return a


def _gdn_body(q_ref, k_ref, v_ref, z_ref, wq_ref, wk_ref, wv_ref, g_ref, hp_ref, nw_ref, o_ref,
              qs, ks, vs, os_, sg, cf, cb, uf, ub, wqf, wqb, qkf, qkb, kdtf, kdtb, eglf, eglb, xpad):
    seq = q_ref.shape[0]
    C = GDN_BLOCK
    hd = GDN_HEAD_DIM
    nck = seq // C
    heads = range(GDN_PAIR)

    pad = 8
    half = (GDN_CONV - 1) // 2
    xpad[0:pad, :] = jnp.zeros((pad, LANES), F32)
    xpad[pad + seq:2 * pad + seq, :] = jnp.zeros((pad, LANES), F32)

    def conv_silu(x_ref, w_ref, j):
        cols = pl.ds(pl.multiple_of(j * hd, hd), hd)
        xpad[pad:pad + seq, :] = x_ref[:, cols].astype(F32)
        w = w_ref[:, cols]
        acc = xpad[pad - half:pad - half + seq, :] * w[0:1]
        for tap in range(1, GDN_CONV):
            acc = acc + xpad[pad - half + tap:pad - half + tap + seq, :] * w[tap:tap + 1]
        return acc * jax.nn.sigmoid(acc)

    def l2n(x):
        return x * lax.rsqrt(jnp.sum(x * x, axis=-1, keepdims=True) + L2_EPS)

    def softplus(x):
        return jnp.maximum(x, 0.0) + jnp.log1p(jnp.exp(-jnp.abs(x)))

    lane = lax.broadcasted_iota(I32, (1, C), 1)

    def prologue(j, carry):
        qs[j] = l2n(conv_silu(q_ref, wq_ref, j)) * (hd ** -0.5)
        ks[j] = l2n(conv_silu(k_ref, wk_ref, j))
        vs[j] = conv_silu(v_ref, wv_ref, j)
        os_[j] = jnp.zeros((seq, hd), F32)
        g = g_ref[:, pl.ds(pl.multiple_of(8 * j, 8), 8), :].reshape(nck * 8, C)
        hp = hp_ref[j]
        sg[j] = jax.nn.sigmoid(g)
        gl_f = -jnp.exp(hp[0:1]) * softplus(g + hp[1:2])
        gl_b = -jnp.exp(hp[2:3]) * softplus(g + hp[3:4])
        sh = 1
        while sh < C:
            gl_f = gl_f + jnp.where(lane >= sh, pltpu.roll(gl_f, sh, 1), 0.0)
            gl_b = gl_b + jnp.where(lane < C - sh, pltpu.roll(gl_b, C - sh, 1), 0.0)
            sh *= 2
        cf[j] = gl_f
        cb[j] = gl_b
        return carry

    lax.fori_loop(0, GDN_PAIR, prologue, 0)

    ri = lax.broadcasted_iota(I32, (C, C), 0)
    ci = lax.broadcasted_iota(I32, (C, C), 1)
    low, upp = ri > ci, ri < ci
    low_i, upp_i = ri >= ci, ri <= ci
    same = lambda w: (ri // w) == (ci // w)
    bd16 = same(16)
    rings = []
    w = 32
    while w <= C:
        rings.append(jnp.logical_and(same(w), jnp.logical_not(same(w // 2))))
        w *= 2

    nbatch = math.gcd(GDN_PREP_BATCH // GDN_PAIR, nck)

    def column(ref, r):
        rows = jnp.broadcast_to(ref[pl.ds(r, 1), :], (C, C))
        return rows.T, rows

    def prepare(it, carry):
        items = [(j, it * nbatch + i) for i in range(nbatch) for j in heads]
        sls = [pl.ds(pl.multiple_of(c * C, C), C) for _, c in items]
        g_f, g_b, bt_f, bt_b, dec_f, dec_b, kb_f, kb_b, aq = [], [], [], [], [], [], [], [], []
        for (j, c), sl in zip(items, sls):
            k = ks[j, sl, :]
            gfc, gfr = column(cf.at[j], c * 8 + 2)
            gbc, gbr = column(cb.at[j], c * 8 + 3)
            g_f.append(gfc)
            g_b.append(gbc)
            bt_f.append(column(sg.at[j], c * 8)[0])
            bt_b.append(column(sg.at[j], c * 8 + 1)[0])
            dec_f.append(jnp.where(low_i, jnp.exp(jnp.where(low_i, gfc - gfr, 0.0)), 0.0))
            dec_b.append(jnp.where(upp_i, jnp.exp(jnp.where(upp_i, gbc - gbr, 0.0)), 0.0))
            kb_f.append(k * bt_f[-1])
            kb_b.append(k * bt_b[-1])
            aq.append(_dot_nt(jnp.concatenate([kb_f[-1], kb_b[-1], qs[j, sl, :]], axis=0).astype(BF16),
                              k.astype(BF16)))
        lps = [jnp.where(low, x[:C] * df, 0.0) + jnp.where(upp, x[C:2 * C] * db, 0.0)
               for x, df, db in zip(aq, dec_f, dec_b)]
        inv = _packed_tri_inverse(lps, low, upp, bd16, rings)
        for n, ((j, c), sl) in enumerate(zip(items, sls)):
            q, k, v = qs[j, sl, :], ks[j, sl, :], vs[j, sl, :]
            for rev, g_c, kb, beta, dec, msk, u_s, wq_s, qk_s, kdt_s, egl_s in (
                    (False, g_f[n], kb_f[n], bt_f[n], dec_f[n], low, uf, wqf, qkf, kdtf, eglf),
                    (True, g_b[n], kb_b[n], bt_b[n], dec_b[n], upp, ub, wqb, qkb, kdtb, eglb)):
                eg = jnp.exp(g_c)
                rhs = jnp.concatenate([v * beta, kb * eg], axis=1)
                uw = rhs + _dot(jnp.where(msk, inv[n], 0.0).astype(BF16), rhs.astype(BF16))
                glast = g_c[0:1] if rev else g_c[C - 1:C]
                u_s[j, sl, :] = uw[:, :C]
                wq_s[j, pl.ds(pl.multiple_of(c * 2 * C, 2 * C), 2 * C), :] = jnp.concatenate(
                    [uw[:, C:], q * eg], axis=0).astype(BF16)
                qk_s[j, sl, :] = (aq[n][2 * C:] * dec).astype(BF16)
                kdt_s[j, sl, :] = (k * jnp.exp(glast - g_c)).T.astype(BF16)
                egl_s[j, pl.ds(c, 1), :] = jnp.exp(glast)
        return carry

    lax.fori_loop(0, nck // nbatch, prepare, 0)

    def body(i, carry):
        chains = []
        for j in heads:
            chains.append((j, i, uf, wqf, qkf, kdtf, eglf))
            chains.append((j, nck - 1 - i, ub, wqb, qkb, kdtb, eglb))
        sls = [pl.ds(pl.multiple_of(c * C, C), C) for _, c, *_ in chains]
        ws_qs = [_dot(wq_s[j, pl.ds(pl.multiple_of(c * 2 * C, 2 * C), 2 * C), :], st.astype(BF16))
                 for (j, c, _, wq_s, *_), st in zip(chains, carry)]
        vnb = [(u_s[j, sl, :] - x[:C]).astype(BF16) for (j, _, u_s, *_), sl, x in zip(chains, sls, ws_qs)]
        new = [st * egl_s[j, pl.ds(c, 1), :] + _dot(kdt_s[j, sl, :], v)
               for (j, c, _, _, _, kdt_s, egl_s), sl, st, v in zip(chains, sls, carry, vnb)]
        for (j, _, _, _, qk_s, _, _), sl, x, v in zip(chains, sls, ws_qs, vnb):
            os_[j, sl, :] += x[C:] + _dot(qk_s[j, sl, :], v)
        return tuple(new)

    zero = jnp.zeros((hd, hd), F32)
    lax.fori_loop(0, nck, body, (zero,) * (2 * GDN_PAIR))

    for j in heads:
        o = os_[j]
        o = o * lax.rsqrt(jnp.mean(o * o, axis=-1, keepdims=True) + RMS_EPS) * nw_ref[...]
        z = z_ref[:, j * hd:(j + 1) * hd].astype(F32)
        o_ref[:, j * hd:(j + 1) * hd] = (o * (z * jax.nn.sigmoid(z))).astype(BF16)


def _gdn(qkvz, gates3, conv_w, head_par, norm_w, bsz, seq):
    t = bsz * seq
    hd = GDN_HEAD_DIM
    nh = GDN_HEADS
    np_ = GDN_PAIR
    wd = np_ * hd
    npairs = nh // np_
    nck = seq // GDN_BLOCK
    col = lambda off: pl.BlockSpec((seq, wd), lambda b, p: (b, off * npairs + p))
    wcol = lambda off: pl.BlockSpec((GDN_CONV, wd), lambda b, p: (0, off * npairs + p))
    per_head = lambda rows, dt: pltpu.VMEM((np_, rows, hd), dt)
    return pl.pallas_call(
        _gdn_body,
        grid=(bsz, npairs),
        in_specs=[col(0), col(1), col(2), col(3), wcol(0), wcol(1), wcol(2),
                  pl.BlockSpec((nck, 8 * np_, GDN_BLOCK), lambda b, p: (b, p, 0)),
                  pl.BlockSpec((np_, 8, LANES), lambda b, p: (p, 0, 0)),
                  pl.BlockSpec((1, hd), lambda b, p: (0, 0))],
        out_specs=pl.BlockSpec((seq, wd), lambda b, p: (b, p)),
        out_shape=jax.ShapeDtypeStruct((t, nh * hd), BF16),
        scratch_shapes=([per_head(seq, F32)] * 4
                        + [per_head(nck * 8, F32)] * 3
                        + [per_head(seq, F32)] * 2
                        + [per_head(2 * seq, BF16)] * 2
                        + [per_head(seq, BF16)] * 2
                        + [per_head(seq, BF16)] * 2
                        + [per_head(nck, F32)] * 2
                        + [pltpu.VMEM((seq + 16, hd), F32)]),
        compiler_params=_cparams("parallel", "parallel"),
        name="gdn",
    )(qkvz, qkvz, qkvz, qkvz, conv_w, conv_w, conv_w, gates3, head_par, norm_w)


def _kv_body(m_ref, g_ref, wk_ref, wv_ref, k_ref, v_ref):
    mn = _rms(m_ref[...], g_ref[...]).astype(BF16)
    k_ref[...] = _dot(mn, wk_ref[...]).astype(BF16)
    v_ref[...] = _dot(mn, wv_ref[...]).astype(BF16)


def _mem_kv(mem2d, norm_w, wk, wv, tm=512):
    r, d = mem2d.shape
    full = lambda shape: pl.BlockSpec(shape, lambda i: (0,) * len(shape))
    tile = pl.BlockSpec((tm, d), lambda i: (i, 0))
    return pl.pallas_call(
        _kv_body,
        grid=(r // tm,),
        in_specs=[tile, full((1, d)), full(wk.shape), full(wv.shape)],
        out_specs=[tile, tile],
        out_shape=[jax.ShapeDtypeStruct((r, d), BF16)] * 2,
        compiler_params=_cparams("parallel"),
        name="mem_kv",
    )(mem2d, norm_w, wk, wv)


def _mix_xattn_body(x_ref, y5_ref, yg_ref, wmix_ref, g_ref, wq_ref, k_ref, v_ref, wo_ref,
                    gm_ref, whi_ref, wlo_ref, br_ref, tri_ref,
                    o_ref, mi_ref, mf_ref, cnt_ref, wmix_b, wq_b, wo_b, carry):
    @pl.when(pl.program_id(0) == 0)
    def _():
        wmix_b[...] = wmix_ref[...].astype(BF16)
        wq_b[...] = wq_ref[...].astype(BF16)
        wo_b[...] = wo_ref[...].astype(BF16)
        carry[...] = jnp.zeros_like(carry)

    x1 = (x_ref[...] + _dot(y5_ref[...], wmix_b[:S5_WIDTH, :]) + _dot(yg_ref[...], wmix_b[S5_WIDTH:, :]))
    xn = _rms(x1, g_ref[...]).astype(BF16)
    q = (_dot(xn, wq_b[...]) * (XA_HEAD_DIM ** -0.5)).astype(BF16)
    hsl = [slice(h * XA_HEAD_DIM, (h + 1) * XA_HEAD_DIM) for h in range(XA_HEADS)]
    sc = [_dot_nt(q[:, sl], k_ref[:, sl]) for sl in hsl]
    pr = [jnp.exp(s - jnp.max(s, axis=-1, keepdims=True)) for s in sc]
    pr = [p / jnp.sum(p, axis=-1, keepdims=True) for p in pr]
    heads = [_dot(p.astype(BF16), v_ref[:, sl]).astype(BF16) for p, sl in zip(pr, hsl)]
    x2 = x1 + _dot(jnp.concatenate(heads, axis=1), wo_b[...])
    o_ref[...] = x2
    mi, mf = _route(_rms(x2, gm_ref[...]), whi_ref[...], wlo_ref[...], br_ref[...], tri_ref[...], carry)
    mi_ref[...] = mi
    mf_ref[...] = mf
    cnt_ref[...] = carry[...]


def _mix_xattn(x2d, y5, yg, w_mix, norm_w, wq, kmem, vmem, wo, norm_moe, w_route, b_route, seq, mem_len, tm=512):
    t, d = x2d.shape
    per_b = seq // tm
    tri = jnp.triu(jnp.ones((tm, tm), BF16))
    w_hi = w_route.astype(BF16)
    w_lo = (w_route - w_hi.astype(F32)).astype(BF16)
    full = lambda shape: pl.BlockSpec(shape, lambda i: (0,) * len(shape))
    tile = lambda w: pl.BlockSpec((tm, w), lambda i: (i, 0))
    return pl.pallas_call(
        _mix_xattn_body,
        grid=(t // tm,),
        in_specs=[tile(d), tile(S5_WIDTH), tile(GDN_WIDTH),
                  full(w_mix.shape), full((1, d)), full(wq.shape),
                  pl.BlockSpec((mem_len, d), lambda i: (i // per_b, 0)),
                  pl.BlockSpec((mem_len, d), lambda i: (i // per_b, 0)),
                  full(wo.shape),
                  full((1, d)), full(w_route.shape), full(w_route.shape), full(b_route.shape), full((tm, tm))],
        out_specs=[tile(d), pl.BlockSpec((8, tm), lambda i: (0, i)), tile(LANES),
                   pl.BlockSpec((ROUTE_ROWS, LANES), lambda i: (0, 0))],
        out_shape=[jax.ShapeDtypeStruct((t, d), F32),
                   jax.ShapeDtypeStruct((8, t), I32),
                   jax.ShapeDtypeStruct((t, LANES), F32),
                   jax.ShapeDtypeStruct((ROUTE_ROWS, LANES), F32)],
        scratch_shapes=[pltpu.VMEM(w_mix.shape, BF16), pltpu.VMEM(wq.shape, BF16), pltpu.VMEM(wo.shape, BF16),
                        pltpu.VMEM((ROUTE_ROWS, LANES), F32)],
        compiler_params=_cparams("arbitrary"),
        name="mix_xattn",
    )(x2d, y5, yg, w_mix, norm_w, wq, kmem, vmem, wo, norm_moe, w_hi, w_lo, b_route, tri)


ROUTE_EXPERT_LANE0 = 4


ROUTE_ROWS = 40


def _route(xn, w_hi, w_lo, bias, tri, carry):
    x_hi = xn.astype(BF16)
    x_lo = (xn - x_hi.astype(F32)).astype(BF16)
    logits = (_dot_nt(w_hi, x_hi) + _dot_nt(w_hi, x_lo) + _dot_nt(w_lo, x_hi))[:ROUTE_ROWS] + bias[:ROUTE_ROWS, 0:1]
    tm = logits.shape[1]
    row = lax.broadcasted_iota(I32, (ROUTE_ROWS, tm), 0)
    neg = jnp.float32(-jnp.inf)
    big = jnp.int32(LANES)

    def top(vals):
        m = jnp.max(vals, axis=0, keepdims=True)
        idx = jnp.min(jnp.where(vals == m, row, big), axis=0, keepdims=True)
        return m, idx

    is_g = row < MOE_GROUPS
    gl = jnp.where(is_g, logits, neg)
    gmax, gidx = top(gl)
    p_top = 1.0 / jnp.sum(jnp.where(is_g, jnp.exp(gl - gmax), 0.0), axis=0, keepdims=True)
    erow = row - ROUTE_EXPERT_LANE0
    in_grp = jnp.logical_and(jnp.logical_and(erow >= 0, erow < MOE_EXPERTS), (erow // MOE_PER_GROUP) == gidx)
    es = jnp.where(in_grp, logits, neg)
    m1, i1 = top(es)
    m2, i2 = top(jnp.where(row == i1, neg, es))
    e21 = jnp.exp(m2 - m1)
    w1 = p_top / (1.0 + e21)
    w2 = p_top * e21 / (1.0 + e21)

    a1 = (row == i1).astype(F32)
    a2 = (row == i2).astype(F32)
    both = a1 + a2
    before = _dot(both.astype(BF16), tri) - both + carry[:, 0:1]
    r1 = jnp.sum(a1 * before, axis=0, keepdims=True).astype(I32)
    r2 = jnp.sum(a2 * before, axis=0, keepdims=True).astype(I32)
    carry[...] = carry[...] + jnp.sum(both, axis=1, keepdims=True)
    row8 = lax.broadcasted_iota(I32, (8, tm), 0)
    mi = jnp.where(row8 == 0, i1 - ROUTE_EXPERT_LANE0,
                   jnp.where(row8 == 1, i2 - ROUTE_EXPERT_LANE0, jnp.where(row8 == 2, r1, jnp.where(row8 == 3, r2, 0))))
    rowl = lax.broadcasted_iota(I32, (LANES, tm), 0)
    wt = jnp.where(rowl == 0, w1, jnp.where(rowl == 1, w2, 0.0))
    mf = jnp.concatenate([wt[:, j * LANES:(j + 1) * LANES].T for j in range(tm // LANES)], axis=0)
    return mi, mf


def _dest_body(mi_ref, off_ref, da_ref, db_ref):
    tm = da_ref.shape[2]
    mi = mi_ref[...]
    n = mi.shape[1]
    row = lax.broadcasted_iota(I32, (MOE_EXPERTS, n), 0)
    off = off_ref[:, 0:1]
    d0 = jnp.sum(jnp.where(row == mi[0:1], off, 0), axis=0, keepdims=True) + mi[2:3]
    d1 = jnp.sum(jnp.where(row == mi[1:2], off, 0), axis=0, keepdims=True) + mi[3:4]
    for s in range(da_ref.shape[0]):
        da_ref[s] = d0[:, s * tm:(s + 1) * tm]
        db_ref[s] = d1[:, s * tm:(s + 1) * tm]


def _dest_rows(mi, offsets_col, tm):
    t = mi.shape[1]
    tiles_per_step = math.gcd(8, t // tm)
    out = pl.BlockSpec((tiles_per_step, 1, tm), lambda i: (i, 0, 0))
    return pl.pallas_call(
        _dest_body,
        grid=(t // (tm * tiles_per_step),),
        in_specs=[pl.BlockSpec((8, tm * tiles_per_step), lambda i: (0, i)),
                  pl.BlockSpec((MOE_EXPERTS, LANES), lambda i: (0, 0))],
        out_specs=[out, out],
        out_shape=[jax.ShapeDtypeStruct((t // tm, 1, tm), I32)] * 2,
        compiler_params=_cparams("parallel"),
        name="moe_dest",
    )(mi, offsets_col)


def _dispatch_body(zs_ref, na_ref, da_ref, db_ref, xn_ref, xs_ref, zbuf, sem, zsem):
    tm = xn_ref.shape[0]
    tz = zbuf.shape[0]
    n_blocks = xs_ref.shape[0] // tz

    @pl.when(pl.program_id(0) == 0)
    def _():
        zbuf[...] = jnp.zeros_like(zbuf)

        def fill(row0):
            return pltpu.make_async_copy(zbuf, xs_ref.at[pl.ds(pl.multiple_of(row0, tz), tz), :], zsem)

        def tail(e, c):
            @pl.when(zs_ref[e] >= 0)
            def _():
                fill(zs_ref[e]).start()
            return c

        def unused(j, c):
            fill(j * tz).start()
            return c

        def drain(j, c):
            fill(0).wait()
            return c

        lax.fori_loop(0, zs_ref.shape[0], tail, 0)
        lax.fori_loop(na_ref[0], n_blocks, unused, 0)
        lax.fori_loop(0, na_ref[1] + n_blocks - na_ref[0], drain, 0)

    def start(r, c):
        for k, d_ref in enumerate((da_ref, db_ref)):
            pltpu.make_async_copy(xn_ref.at[pl.ds(r, 1), :], xs_ref.at[pl.ds(d_ref[0, r], 1), :],
                                  sem.at[k]).start(priority=k)
        return c

    lax.fori_loop(0, tm, start, 0, unroll=8)
    for k in range(MOE_TOPK):
        pltpu.make_async_copy(xn_ref, xs_ref.at[pl.ds(0, tm), :], sem.at[k]).wait()


def _dispatch(zero_start, n_active, dest_a, dest_b, xn, n_rows, tm, te):
    t, dw = xn.shape
    smem_row = pl.BlockSpec((None, 1, tm), lambda i, zs, na: (i, 0, 0), memory_space=pltpu.SMEM)
    grid_spec = pltpu.PrefetchScalarGridSpec(
        num_scalar_prefetch=2,
        grid=(t // tm,),
        in_specs=[smem_row, smem_row, pl.BlockSpec((tm, dw), lambda i, zs, na: (i, 0))],
        out_specs=pl.BlockSpec(memory_space=pl.ANY),
        scratch_shapes=[pltpu.VMEM((te, dw), xn.dtype), pltpu.SemaphoreType.DMA((MOE_TOPK,)),
                        pltpu.SemaphoreType.DMA(())],
    )
    return pl.pallas_call(
        _dispatch_body,
        grid_spec=grid_spec,
        out_shape=jax.ShapeDtypeStruct((n_rows, dw), xn.dtype),
        compiler_params=_cparams("arbitrary"),
        name="moe_dispatch",
    )(zero_start, n_active, dest_a, dest_b, xn)


def _experts_body(te_ref, na_ref, x_ref, g_ref, wg_ref, wu_ref, wd_ref, y_ref, wg_b, wu_b, wd_b):
    i = pl.program_id(0)

    @pl.when(i < na_ref[0])
    def _():
        @pl.when(jnp.logical_or(i == 0, te_ref[i] != te_ref[jnp.maximum(i - 1, 0)]))
        def _():
            wg_b[...] = wg_ref[...].astype(BF16)
            wu_b[...] = wu_ref[...].astype(BF16)
            wd_b[...] = wd_ref[...].astype(BF16)

        x = _rms(x_ref[...], g_ref[...]).astype(BF16)
        gt = _dot(x, wg_b[...])
        up = _dot(x, wu_b[...])
        hid = (gt * jax.nn.sigmoid(gt) * up).astype(BF16)
        y_ref[...] = _dot(hid, wd_b[...])

    @pl.when(i >= na_ref[0])
    def _():
        y_ref[...] = jnp.zeros_like(y_ref)


def _experts(tile_expert, n_active, xs, norm_w, w_gate, w_up, w_down, tm):
    r, d = xs.shape
    f = w_gate.shape[2]
    row_tile = lambda i, te, na: (jnp.minimum(i, na[0] - 1), 0)
    grid_spec = pltpu.PrefetchScalarGridSpec(
        num_scalar_prefetch=2,
        grid=(r // tm,),
        in_specs=[pl.BlockSpec((tm, d), row_tile),
                  pl.BlockSpec((1, d), lambda i, te, na: (0, 0)),
                  pl.BlockSpec((None, d, f), lambda i, te, na: (te[i], 0, 0)),
                  pl.BlockSpec((None, d, f), lambda i, te, na: (te[i], 0, 0)),
                  pl.BlockSpec((None, f, d), lambda i, te, na: (te[i], 0, 0))],
        out_specs=pl.BlockSpec((tm, d), lambda i, te, na: (i, 0)),
        scratch_shapes=[pltpu.VMEM((d, f), BF16), pltpu.VMEM((d, f), BF16), pltpu.VMEM((f, d), BF16)],
    )
    return pl.pallas_call(
        _experts_body,
        grid_spec=grid_spec,
        out_shape=jax.ShapeDtypeStruct((r, d), xs.dtype),
        compiler_params=_cparams("arbitrary"),
        name="moe_experts",
    )(tile_expert, n_active, xs, norm_w, w_gate, w_up, w_down)


def _combine_body(da_ref, db_ref, na_ref, nb_ref, x_ref, mf_ref, g_ref, ys_ref, o_ref, buf, sem):
    tm = x_ref.shape[0]
    i = pl.program_id(0)
    slot = lax.rem(i, 2)

    def gather(d_refs, s):
        def start(r, c):
            for k, d_ref in enumerate(d_refs):
                pltpu.make_async_copy(ys_ref.at[pl.ds(d_ref[0, r], 1), :],
                                      buf.at[s, k, pl.ds(r, 1), :], sem.at[s, k]).start(priority=k)
            return c

        lax.fori_loop(0, tm, start, 0, unroll=8)

    @pl.when(i == 0)
    def _():
        gather((da_ref, db_ref), 0)

    @pl.when(i + 1 < pl.num_programs(0))
    def _():
        gather((na_ref, nb_ref), 1 - slot)

    for k in range(MOE_TOPK):
        pltpu.make_async_copy(ys_ref.at[pl.ds(0, tm), :], buf.at[slot, k], sem.at[slot, k]).wait()
    mf = mf_ref[...]
    y = x_ref[...] + mf[:, 0:1] * buf[slot, 0] + mf[:, 1:2] * buf[slot, 1]
    o_ref[...] = _rms(y, g_ref[...])


def _combine(dest_a, dest_b, x2d, mf, norm_w, ys, tm):
    t, d = x2d.shape
    last = t // tm - 1
    cur = pl.BlockSpec((None, 1, tm), lambda i: (i, 0, 0), memory_space=pltpu.SMEM)
    nxt = pl.BlockSpec((None, 1, tm), lambda i: (jnp.minimum(i + 1, last), 0, 0), memory_space=pltpu.SMEM)
    return pl.pallas_call(
        _combine_body,
        grid=(t // tm,),
        in_specs=[cur, cur, nxt, nxt,
                  pl.BlockSpec((tm, d), lambda i: (i, 0)),
                  pl.BlockSpec((tm, LANES), lambda i: (i, 0)),
                  pl.BlockSpec((1, d), lambda i: (0, 0)),
                  pl.BlockSpec(memory_space=pl.ANY)],
        out_specs=pl.BlockSpec((tm, d), lambda i: (i, 0)),
        out_shape=jax.ShapeDtypeStruct((t, d), F32),
        scratch_shapes=[pltpu.VMEM((2, MOE_TOPK, tm, ys.shape[1]), ys.dtype), pltpu.SemaphoreType.DMA((2, MOE_TOPK))],
        compiler_params=_cparams("arbitrary"),
        name="moe_combine",
    )(dest_a, dest_b, dest_a, dest_b, x2d, mf, norm_w, ys)


MOE_ROW_TILE = 512
MOE_TOKEN_TILE = 1024
MOE_ZERO_BLOCK = 256


def _moe(x2d, mi, mf, cnt, norm_w, w_gate, w_up, w_down, norm_final):
    t, d = x2d.shape
    tm = MOE_ROW_TILE
    tok = MOE_TOKEN_TILE
    counts = cnt[ROUTE_EXPERT_LANE0:ROUTE_EXPERT_LANE0 + MOE_EXPERTS, 0].astype(I32)
    padded = ((counts + tm - 1) // tm) * tm
    ends = jnp.cumsum(padded)
    offsets = ends - padded
    n_tiles = (MOE_TOPK * t + MOE_EXPERTS * (tm - 1)) // tm
    tile_start = jnp.arange(n_tiles, dtype=I32) * tm
    tile_expert = jnp.minimum(jnp.sum((ends[None, :] <= tile_start[:, None]).astype(I32), axis=1), MOE_EXPERTS - 1)
    n_info = jnp.stack([ends[-1] // tm, jnp.sum((counts > 0).astype(I32))]).astype(I32)
    tz = MOE_ZERO_BLOCK
    pad = padded - counts
    zero_start = jnp.concatenate([jnp.where(pad > j * tz, ends - (j + 1) * tz, -1) for j in range(tm // tz)])
    zero_start = jnp.where(jnp.tile(counts, tm // tz) > 0, zero_start, -1).astype(I32)
    z_info = jnp.stack([ends[-1] // tz, jnp.sum((zero_start >= 0).astype(I32))]).astype(I32)
    offsets_col = jnp.broadcast_to(offsets[:, None], (MOE_EXPERTS, LANES))
    dest_a, dest_b = _dest_rows(mi, offsets_col, tok)
    xs = _dispatch(zero_start, z_info, dest_a, dest_b, x2d, n_tiles * tm, tok, tz)
    ys = _experts(tile_expert, n_info, xs, norm_w, w_gate, w_up, w_down, tm)
    return _combine(dest_a, dest_b, x2d, mf, norm_final, ys, tok)


def kernel(x, mem, norm_mix, w_in, w_out,
           s5_lam_re_f, s5_lam_im_f, s5_log_step_f, s5_b_re_f, s5_b_im_f, s5_c_re_f, s5_c_im_f,
           s5_lam_re_b, s5_lam_im_b, s5_log_step_b, s5_b_re_b, s5_b_im_b, s5_c_re_b, s5_c_im_b,
           s5_d, s5_w_glu, s5_b_glu, s5_norm,
           gdn_conv, gdn_a_log_f, gdn_dt_bias_f, gdn_a_log_b, gdn_dt_bias_b, gdn_norm,
           norm_xattn, norm_mem, xa_wq, xa_wk, xa_wv, xa_wo,
           norm_moe, router_group_w, router_group_b, router_expert_w, router_expert_b,
           moe_w_gate, moe_w_up, moe_w_down, norm_final):
    bsz, seq, d = x.shape
    t = bsz * seq
    l = 0
    x2d = x.reshape(t, d)
    wi = w_in[l]
    wg = wi[:, S5_WIDTH + 4 * GDN_WIDTH:].reshape(d, 4, GDN_HEADS)
    wg = jnp.pad(jnp.swapaxes(wg, 1, 2), ((0, 0), (0, 0), (0, 4))).reshape(d, GDN_HEADS * 8)
    wgt = wg.T
    u3, qkvz, gates = _in_proj(x2d, norm_mix[l][None], wi, wgt)
    s5p = dict(lam_re_f=s5_lam_re_f[l], lam_im_f=s5_lam_im_f[l], log_step_f=s5_log_step_f[l],
               b_re_f=s5_b_re_f[l], b_im_f=s5_b_im_f[l], c_re_f=s5_c_re_f[l], c_im_f=s5_c_im_f[l],
               lam_re_b=s5_lam_re_b[l], lam_im_b=s5_lam_im_b[l], log_step_b=s5_log_step_b[l],
               b_re_b=s5_b_re_b[l], b_im_b=s5_b_im_b[l], c_re_b=s5_c_re_b[l], c_im_b=s5_c_im_b[l], d=s5_d[l])
    g3 = _s5_scan(u3, s5p, seq // S5_CHUNK)
    y_s5 = _s5_post(g3, s5_w_glu[l].T.astype(BF16), s5_b_glu[l][:, None], s5_norm[l][:, None])

    head_par = jnp.stack([gdn_a_log_f[l], gdn_dt_bias_f[l], gdn_a_log_b[l], gdn_dt_bias_b[l]], axis=1)
    head_par = jnp.broadcast_to(jnp.pad(head_par, ((0, 0), (0, 4)))[:, :, None], (GDN_HEADS, 8, LANES))
    y_gdn = _gdn(qkvz, gates, gdn_conv[l], head_par, gdn_norm[l][None], bsz, seq)

    mem_len = mem.shape[1]
    kmem, vmem = _mem_kv(mem.reshape(bsz * mem_len, d), norm_mem[l][None],
                         xa_wk[l].astype(BF16), xa_wv[l].astype(BF16))
    n_pad = LANES - MOE_GROUPS - MOE_EXPERTS
    w_route = jnp.pad(jnp.concatenate([router_group_w[l], router_expert_w[l]], axis=1), ((0, 0), (0, n_pad))).T
    b_route = jnp.pad(jnp.concatenate([router_group_b[l], router_expert_b[l]]), (0, n_pad))
    b_route = jnp.broadcast_to(b_route[:, None], (LANES, LANES))
    x2, mi, mf, cnt = _mix_xattn(x2d, y_s5, y_gdn, w_out[l], norm_xattn[l][None], xa_wq[l], kmem, vmem, xa_wo[l],
                                 norm_moe[l][None], w_route, b_route, seq, mem_len)
    y = _moe(x2, mi, mf, cnt, norm_moe[l][None], moe_w_gate[l], moe_w_up[l], moe_w_down[l], norm_final[None])
    return y.reshape(bsz, seq, d)
```

```python
import functools
import math

import jax
import jax.numpy as jnp
from jax import lax
from jax.experimental import pallas as pl
from jax.experimental.pallas import tpu as pltpu

F32 = jnp.float32
BF16 = jnp.bfloat16
I32 = jnp.int32

D_MODEL = 1024
S5_WIDTH = 512
S5_GROUP = 16
S5_GROUPS = 32
S5_STATE = 64
S5_CHUNK = 128
GDN_HEADS = 4
GDN_HEAD_DIM = 128
GDN_WIDTH = 512
GDN_CONV = 5
GDN_CHUNK = 64
XA_HEADS = 4
XA_HEAD_DIM = 256
MOE_GROUPS = 4
MOE_PER_GROUP = 8
MOE_EXPERTS = 32
MOE_TOPK = 2
D_EXPERT = 256
RMS_EPS = 1e-6
L2_EPS = 1e-6
LANES = 128
VMEM_LIMIT = 56 * 1024 * 1024


def _cparams(*sem):
    return pltpu.CompilerParams(dimension_semantics=tuple(sem), vmem_limit_bytes=VMEM_LIMIT)


def _rms(x, gain):
    return x * lax.rsqrt(jnp.mean(x * x, axis=-1, keepdims=True) + RMS_EPS) * gain


def _dot(a, b):
    return jnp.dot(a, b, preferred_element_type=F32)


def _dot_nt(a, b):
    return lax.dot_general(a, b, (((1,), (1,)), ((), ())), preferred_element_type=F32)


def _dot_tn(a, b):
    return lax.dot_general(a, b, (((0,), (0,)), ((), ())), preferred_element_type=F32)


def _in_proj_body(x_ref, g_ref, w_ref, wgt_ref, u_ref, qkvz_ref, gates_ref, wut_b, wqkvz_b, wgt_b):
    @pl.when(pl.program_id(0) == 0)
    def _():
        wut_b[...] = w_ref[:, :S5_WIDTH].T.astype(BF16)
        wqkvz_b[...] = w_ref[:, S5_WIDTH:S5_WIDTH + 4 * GDN_WIDTH].astype(BF16)
        wgt_b[...] = wgt_ref[...].astype(BF16)

    h = _rms(x_ref[...], g_ref[...]).astype(BF16)
    ut = _dot_nt(wut_b[...], h)
    gt = _dot_nt(wgt_b[...], h)
    for j in range(u_ref.shape[0]):
        u_ref[j] = ut[:, j * S5_CHUNK:(j + 1) * S5_CHUNK]
        gates_ref[j] = gt[:, j * S5_CHUNK:(j + 1) * S5_CHUNK]
    qkvz_ref[...] = _dot(h, wqkvz_b[...]).astype(BF16)


def _in_proj(x2d, norm_w, w_in, wgt, tm=512):
    t = x2d.shape[0]
    nck = tm // S5_CHUNK
    nqkvz = 4 * GDN_WIDTH
    full = lambda shape: pl.BlockSpec(shape, lambda i: (0,) * len(shape))
    return pl.pallas_call(
        _in_proj_body,
        grid=(t // tm,),
        in_specs=[pl.BlockSpec((tm, D_MODEL), lambda i: (i, 0)),
                  full((1, D_MODEL)), full(w_in.shape), full(wgt.shape)],
        out_specs=[pl.BlockSpec((nck, S5_WIDTH, S5_CHUNK), lambda i: (i, 0, 0)),
                   pl.BlockSpec((tm, nqkvz), lambda i: (i, 0)),
                   pl.BlockSpec((nck, wgt.shape[0], S5_CHUNK), lambda i: (i, 0, 0))],
        out_shape=[jax.ShapeDtypeStruct((t // S5_CHUNK, S5_WIDTH, S5_CHUNK), F32),
                   jax.ShapeDtypeStruct((t, nqkvz), BF16),
                   jax.ShapeDtypeStruct((t // S5_CHUNK, wgt.shape[0], S5_CHUNK), F32)],
        scratch_shapes=[pltpu.VMEM((S5_WIDTH, D_MODEL), BF16), pltpu.VMEM((D_MODEL, nqkvz), BF16),
                        pltpu.VMEM(wgt.shape, BF16)],
        compiler_params=_cparams("arbitrary"),
        name="in_proj",
    )(x2d, norm_w, w_in, wgt)


def _cmul(ar, ai, br, bi):
    return ar * br - ai * bi, ar * bi + ai * br


def _cpow_int(lr, li, expo, nbits):
    res_r = jnp.ones(jnp.broadcast_shapes(lr.shape, expo.shape), F32)
    res_i = jnp.zeros_like(res_r)
    for b in range(nbits):
        bit = ((expo >> b) & 1) == 1
        nr, ni = _cmul(res_r, res_i, lr, li)
        res_r = jnp.where(bit, nr, res_r)
        res_i = jnp.where(bit, ni, res_i)
        if b + 1 < nbits:
            lr, li = _cmul(lr, li, lr, li)
    return res_r, res_i


def _lam_bar(re, im, step):
    er = jnp.exp(step * re)
    return er * jnp.cos(step * im), er * jnp.sin(step * im)


def _zoh_coef(re, im, lr, li):
    den = re * re + im * im
    return ((lr - 1.0) * re + li * im) / den, (li * re - (lr - 1.0) * im) / den


def _s5_body(u_ref, lrow_ref, lcol_ref, l256_ref, b_ref, bt_ref, c_ref, ct_ref, d_ref,
             o_ref, vf_ref, vb_ref, m_ref, win_ref, wout_ref, sf_ref, sb_ref, hf_ref, hb_ref, uc_ref, y_ref, *, nchunk):
    L = S5_CHUNK
    P = S5_STATE
    n_rows = u_ref.shape[0]
    nb = n_rows // nchunk
    lane_i = lax.broadcasted_iota(I32, (1, L), 1)

    lcol = lcol_ref[...]
    lbc_r, lbc_i = _lam_bar(lcol[:, 0:2], lcol[:, 2:4], lcol[:, 4:6])
    kc_r, kc_i = _zoh_coef(lcol[:, 0:2], lcol[:, 2:4], lbc_r, lbc_i)
    lrow = lrow_ref[...]
    lbr_r, lbr_i = _lam_bar(lrow[0:2], lrow[2:4], lrow[4:6])
    kr_r, kr_i = _zoh_coef(lrow[0:2], lrow[2:4], lbr_r, lbr_i)
    lf_r, lf_i, lb_r, lb_i = lbc_r[:, 0:1], lbc_i[:, 0:1], lbc_r[:, 1:2], lbc_i[:, 1:2]
    kfr_c, kfi_c, kbr_c, kbi_c = kc_r[:, 0:1], kc_i[:, 0:1], kc_r[:, 1:2], kc_i[:, 1:2]
    kfr_r, kfi_r, kbr_r, kbi_r = kr_r[0:1], kr_i[0:1], kr_r[1:2], kr_i[1:2]

    pwf_r, pwf_i = _cpow_int(lf_r, lf_i, lane_i, 7)
    rvf_r, rvf_i = _cpow_int(lf_r, lf_i, (L - 1) - lane_i, 7)
    pwb_r, pwb_i = _cpow_int(lb_r, lb_i, lane_i, 7)
    rvb_r, rvb_i = _cpow_int(lb_r, lb_i, L - lane_i, 8)
    nxf_r, nxf_i = _cmul(pwf_r, pwf_i, lf_r, lf_i)

    bf_r = kfr_c * b_ref[0] - kfi_c * b_ref[1]
    bf_i = kfr_c * b_ref[1] + kfi_c * b_ref[0]
    bb_r = kbr_c * b_ref[2] - kbi_c * b_ref[3]
    bb_i = kbr_c * b_ref[3] + kbi_c * b_ref[2]
    btf_r = kfr_r * bt_ref[0] - kfi_r * bt_ref[1]
    btf_i = kfr_r * bt_ref[1] + kfi_r * bt_ref[0]
    btb_r = kbr_r * bt_ref[2] - kbi_r * bt_ref[3]
    btb_i = kbr_r * bt_ref[3] + kbi_r * bt_ref[2]

    def taps(c_r, c_i, bt_r, bt_i, pw_r, pw_i):
        cb_r = (bt_r[:, None, :] * c_r[None, :, :] - bt_i[:, None, :] * c_i[None, :, :]).reshape(256, P)
        cb_i = (bt_r[:, None, :] * c_i[None, :, :] + bt_i[:, None, :] * c_r[None, :, :]).reshape(256, P)
        k = (jnp.dot(cb_r, pw_r, preferred_element_type=F32, precision=lax.Precision.HIGHEST)
             - jnp.dot(cb_i, pw_i, preferred_element_type=F32, precision=lax.Precision.HIGHEST))
        return k, jnp.sum(cb_r, axis=1, keepdims=True)

    kf, _ = taps(c_ref[0], c_ref[1], btf_r, btf_i, pwf_r, pwf_i)
    kb, kb0 = taps(c_ref[2], c_ref[3], btb_r, btb_i, rvb_r, rvb_i)
    is0 = lane_i == 0
    vf_ref[...] = kf + jnp.where(is0, kb0, 0.0)
    vb_ref[...] = jnp.where(is0, 0.0, kb)

    row_i = lax.broadcasted_iota(I32, (L, L), 0)
    col_i = lax.broadcasted_iota(I32, (L, L), 1)
    fwd_lane = col_i + row_i < L

    def build_ci(ci, carry):
        for co in range(S5_GROUP):
            r = ci * S5_GROUP + co
            taps_rows = jnp.where(fwd_lane, jnp.broadcast_to(vf_ref[pl.ds(r, 1), :], (L, L)),
                                  jnp.broadcast_to(vb_ref[pl.ds(r, 1), :], (L, L)))
            m_ref[pl.ds(pl.multiple_of(ci * L, L), L), co * L:(co + 1) * L] = pltpu.roll(
                taps_rows, 0, 1, stride=1, stride_axis=0).astype(BF16)
        return carry

    lax.fori_loop(0, S5_GROUP, build_ci, 0, unroll=4)

    for ci in range(S5_GROUP):
        sl = slice(ci * L, (ci + 1) * L)
        br, bi = bf_r[:, ci:ci + 1], bf_i[:, ci:ci + 1]
        win_ref[0 * P:1 * P, sl] = (rvf_r * br - rvf_i * bi).astype(BF16)
        win_ref[1 * P:2 * P, sl] = (rvf_r * bi + rvf_i * br).astype(BF16)
        br, bi = bb_r[:, ci:ci + 1], bb_i[:, ci:ci + 1]
        win_ref[2 * P:3 * P, sl] = (pwb_r * br - pwb_i * bi).astype(BF16)
        win_ref[3 * P:4 * P, sl] = (pwb_r * bi + pwb_i * br).astype(BF16)
    for co in range(S5_GROUP):
        sl = slice(co * L, (co + 1) * L)
        cr, ci_ = ct_ref[0][:, co:co + 1], ct_ref[1][:, co:co + 1]
        wout_ref[0 * P:1 * P, sl] = (cr * nxf_r - ci_ * nxf_i).astype(BF16)
        wout_ref[1 * P:2 * P, sl] = (-(cr * nxf_i + ci_ * nxf_r)).astype(BF16)
        cr, ci_ = ct_ref[2][:, co:co + 1], ct_ref[3][:, co:co + 1]
        wout_ref[2 * P:3 * P, sl] = (cr * rvb_r - ci_ * rvb_i).astype(BF16)
        wout_ref[3 * P:4 * P, sl] = (-(cr * rvb_i + ci_ * rvb_r)).astype(BF16)

    for ci in range(S5_GROUP):
        uc_ref[ci] = u_ref[:, ci, :]
    ucat = jnp.concatenate([uc_ref[ci].astype(BF16) for ci in range(S5_GROUP)], axis=1)

    summ = _dot_nt(ucat, win_ref[...])
    sf_ref[...] = summ[:, :2 * P]
    sb_ref[...] = summ[:, 2 * P:]
    nblk = 2 * L
    for j in range(S5_GROUP * L // nblk):
        y_ref[:, j * nblk:(j + 1) * nblk] = _dot(ucat, m_ref[:, j * nblk:(j + 1) * nblk])
    l256 = l256_ref[...]
    a_mul, a_im = _lam_bar(l256[0:1], l256[1:2], l256[2:3])
    for _ in range(7):
        a_mul, a_im = _cmul(a_mul, a_im, a_mul, a_im)
    lane256 = lax.broadcasted_iota(I32, (1, 4 * P), 1)
    b_mul = jnp.where((lane256 // P) % 2 == 0, -a_im, a_im)

    hf = jnp.zeros((nb, 2 * P), F32)
    hb = jnp.zeros((nb, 2 * P), F32)
    for c in range(nchunk):
        cr = nchunk - 1 - c
        rows_f = pl.ds(c, nb, stride=nchunk)
        rows_b = pl.ds(cr, nb, stride=nchunk)
        hf_ref[rows_f, :] = hf
        hb_ref[rows_b, :] = hb
        hf = a_mul[:, :2 * P] * hf + b_mul[:, :2 * P] * pltpu.roll(hf, P, 1) + sf_ref[rows_f, :]
        hb = a_mul[:, 2 * P:] * hb + b_mul[:, 2 * P:] * pltpu.roll(hb, P, 1) + sb_ref[rows_b, :]
    hprev = jnp.concatenate([hf_ref[...], hb_ref[...]], axis=1).astype(BF16)

    for j in range(S5_GROUP * L // nblk):
        y = y_ref[:, j * nblk:(j + 1) * nblk] + _dot(hprev, wout_ref[:, j * nblk:(j + 1) * nblk])
        for q in range(nblk // L):
            co = j * (nblk // L) + q
            yc = y[:, q * L:(q + 1) * L] + d_ref[co:co + 1, :] * uc_ref[co]
            o_ref[:, co, :] = 0.5 * yc * (1.0 + lax.erf(yc * (2.0 ** -0.5)))


def _s5_scan(u3, p, nchunk):
    n = u3.shape[0]
    g, grp, st, L = S5_GROUPS, S5_GROUP, S5_STATE, S5_CHUNK
    step_f = jnp.exp(p["log_step_f"])[:, None] * jnp.ones((1, st), F32)
    step_b = jnp.exp(p["log_step_b"])[:, None] * jnp.ones((1, st), F32)
    zeros = jnp.zeros((g, st), F32)
    lrow = jnp.stack([p["lam_re_f"], p["lam_re_b"], p["lam_im_f"], p["lam_im_b"], step_f, step_b, zeros, zeros], axis=1)
    lcol = jnp.swapaxes(lrow, 1, 2)
    cat4 = lambda f, b: jnp.concatenate([f, f, b, b], axis=1)
    z256 = jnp.zeros((g, 4 * st), F32)
    l256 = jnp.stack([cat4(p["lam_re_f"], p["lam_re_b"]), cat4(p["lam_im_f"], p["lam_im_b"]),
                      cat4(step_f, step_b)] + [z256] * 5, axis=1)
    b4 = jnp.stack([p["b_re_f"], p["b_im_f"], p["b_re_b"], p["b_im_b"]], axis=1)
    bt4 = jnp.swapaxes(b4, 2, 3)
    c4 = jnp.stack([p["c_re_f"], p["c_im_f"], p["c_re_b"], p["c_im_b"]], axis=1)
    ct4 = jnp.swapaxes(c4, 2, 3)
    dbc = jnp.broadcast_to(p["d"].reshape(g, grp, 1), (g, grp, L))
    per_g = lambda *shape: pl.BlockSpec((None,) + shape, lambda i: (i,) + (0,) * len(shape))
    return pl.pallas_call(
        functools.partial(_s5_body, nchunk=nchunk),
        grid=(g,),
        in_specs=[pl.BlockSpec((n, grp, L), lambda i: (0, i, 0)),
                  per_g(8, st), per_g(st, 8), per_g(8, 4 * st), per_g(4, st, grp), per_g(4, grp, st),
                  per_g(4, grp, st), per_g(4, st, grp), per_g(grp, L)],
        out_specs=pl.BlockSpec((n, grp, L), lambda i: (0, i, 0)),
        out_shape=jax.ShapeDtypeStruct(u3.shape, F32),
        scratch_shapes=[pltpu.VMEM((grp * grp, L), F32), pltpu.VMEM((grp * grp, L), F32),
                        pltpu.VMEM((grp * L, grp * L), BF16),
                        pltpu.VMEM((4 * st, grp * L), BF16), pltpu.VMEM((4 * st, grp * L), BF16),
                        pltpu.VMEM((n, 2 * st), F32), pltpu.VMEM((n, 2 * st), F32),
                        pltpu.VMEM((n, 2 * st), F32), pltpu.VMEM((n, 2 * st), F32),
                        pltpu.VMEM((grp, n, L), F32), pltpu.VMEM((n, grp * L), F32)],
        compiler_params=_cparams("parallel"),
        name="s5_scan",
    )(u3, lrow, lcol, l256, b4, bt4, c4, ct4, dbc)


def _s5_post_body(g_ref, wt_ref, b_ref, nw_ref, o_ref):
    for j in range(g_ref.shape[0]):
        g = g_ref[j]
        z = _dot(wt_ref[...], g.astype(BF16)) + b_ref[...]
        y = g * jax.nn.sigmoid(z)
        y = y * lax.rsqrt(jnp.mean(y * y, axis=0, keepdims=True) + RMS_EPS) * nw_ref[...]
        o_ref[j * S5_CHUNK:(j + 1) * S5_CHUNK, :] = y.T.astype(BF16)


def _s5_post(g3, w_glu_t, b_glu_col, norm_col):
    n = g3.shape[0]
    nck = math.gcd(16, n)
    full = lambda shape: pl.BlockSpec(shape, lambda i: (0,) * len(shape))
    return pl.pallas_call(
        _s5_post_body,
        grid=(n // nck,),
        in_specs=[pl.BlockSpec((nck, S5_WIDTH, S5_CHUNK), lambda i: (i, 0, 0)),
                  full(w_glu_t.shape), full(b_glu_col.shape), full(norm_col.shape)],
        out_specs=pl.BlockSpec((nck * S5_CHUNK, S5_WIDTH), lambda i: (i, 0)),
        out_shape=jax.ShapeDtypeStruct((n * S5_CHUNK, S5_WIDTH), BF16),
        compiler_params=_cparams("parallel"),
        name="s5_post",
    )(g3, w_glu_t, b_glu_col, norm_col)


GDN_BLOCK = 128
GDN_PAIR = 2
GDN_PREP_BATCH = 8


def _packed_tri_inverse(lps, low, upp, bd16, rings):
    def pk(xs, ys):
        outs = []
        for a, b in zip(xs, ys):
            lhs = jnp.concatenate([jnp.where(low, a, 0.0), jnp.where(upp, a, 0.0)], axis=1).astype(BF16)
            rhs = jnp.concatenate([jnp.where(low, b, 0.0), jnp.where(upp, b, 0.0)], axis=0).astype(BF16)
            outs.append(_dot(lhs, rhs))
        return outs

    d = [jnp.where(bd16, lp, 0.0) for lp in lps]
    d2 = pk(d, d)
    d4 = pk(d2, d2)
    d8 = pk(d4, d4)
    a = [y - x - p for x, y, p in zip(d, d2, pk(d, d2))]
    a = [x + y + p for x, y, p in zip(a, d4, pk(a, d4))]
    a = [x + y + p for x, y, p in zip(a, d8, pk(a, d8))]
    for ring in rings:
        n = [jnp.where(ring, lp, 0.0) for lp in lps]
        t = [x + p for x, p in zip(n, pk(a, n))]
        a = [x - y - p for x, y, p in zip(a, t, pk(t, a))]
    return a


def _gdn_body(q_ref, k_ref, v_ref, z_ref, wq_ref, wk_ref, wv_ref, g_ref, hp_ref, nw_ref, o_ref,
              qs, ks, vs, os_, sg, cf, cb, uf, ub, wqf, wqb, qkf, qkb, kdtf, kdtb, eglf, eglb, xpad):
    seq = q_ref.shape[0]
    C = GDN_BLOCK
    hd = GDN_HEAD_DIM
    nck = seq // C
    heads = range(GDN_PAIR)

    pad = 8
    half = (GDN_CONV - 1) // 2
    xpad[0:pad, :] = jnp.zeros((pad, LANES), F32)
    xpad[pad + seq:2 * pad + seq, :] = jnp.zeros((pad, LANES), F32)

    def conv_silu(x_ref, w_ref, j):
        cols = pl.ds(pl.multiple_of(j * hd, hd), hd)
        xpad[pad:pad + seq, :] = x_ref[:, cols].astype(F32)
        w = w_ref[:, cols]
        acc = xpad[pad - half:pad - half + seq, :] * w[0:1]
        for tap in range(1, GDN_CONV):
            acc = acc + xpad[pad - half + tap:pad - half + tap + seq, :] * w[tap:tap + 1]
        return acc * jax.nn.sigmoid(acc)

    def l2n(x):
        return x * lax.rsqrt(jnp.sum(x * x, axis=-1, keepdims=True) + L2_EPS)

    def softplus(x):
        return jnp.maximum(x, 0.0) + jnp.log1p(jnp.exp(-jnp.abs(x)))

    lane = lax.broadcasted_iota(I32, (1, C), 1)

    def prologue(j, carry):
        qs[j] = l2n(conv_silu(q_ref, wq_ref, j)) * (hd ** -0.5)
        ks[j] = l2n(conv_silu(k_ref, wk_ref, j))
        vs[j] = conv_silu(v_ref, wv_ref, j)
        os_[j] = jnp.zeros((seq, hd), F32)
        g = g_ref[:, pl.ds(pl.multiple_of(8 * j, 8), 8), :].reshape(nck * 8, C)
        hp = hp_ref[j]
        sg[j] = jax.nn.sigmoid(g)
        gl_f = -jnp.exp(hp[0:1]) * softplus(g + hp[1:2])
        gl_b = -jnp.exp(hp[2:3]) * softplus(g + hp[3:4])
        sh = 1
        while sh < C:
            gl_f = gl_f + jnp.where(lane >= sh, pltpu.roll(gl_f, sh, 1), 0.0)
            gl_b = gl_b + jnp.where(lane < C - sh, pltpu.roll(gl_b, C - sh, 1), 0.0)
            sh *= 2
        cf[j] = gl_f
        cb[j] = gl_b
        return carry

    lax.fori_loop(0, GDN_PAIR, prologue, 0)

    ri = lax.broadcasted_iota(I32, (C, C), 0)
    ci = lax.broadcasted_iota(I32, (C, C), 1)
    low, upp = ri > ci, ri < ci
    low_i, upp_i = ri >= ci, ri <= ci
    same = lambda w: (ri // w) == (ci // w)
    bd16 = same(16)
    rings = []
    w = 32
    while w <= C:
        rings.append(jnp.logical_and(same(w), jnp.logical_not(same(w // 2))))
        w *= 2

    nbatch = math.gcd(GDN_PREP_BATCH // GDN_PAIR, nck)

    def column(ref, r):
        rows = jnp.broadcast_to(ref[pl.ds(r, 1), :], (C, C))
        return rows.T, rows

    def prepare(it, carry):
        items = [(j, it * nbatch + i) for i in range(nbatch) for j in heads]
        sls = [pl.ds(pl.multiple_of(c * C, C), C) for _, c in items]
        g_f, g_b, bt_f, bt_b, dec_f, dec_b, kb_f, kb_b, aq = [], [], [], [], [], [], [], [], []
        for (j, c), sl in zip(items, sls):
            k = ks[j, sl, :]
            gfc, gfr = column(cf.at[j], c * 8 + 2)
            gbc, gbr = column(cb.at[j], c * 8 + 3)
            g_f.append(gfc)
            g_b.append(gbc)
            bt_f.append(column(sg.at[j], c * 8)[0])
            bt_b.append(column(sg.at[j], c * 8 + 1)[0])
            dec_f.append(jnp.where(low_i, jnp.exp(jnp.where(low_i, gfc - gfr, 0.0)), 0.0))
            dec_b.append(jnp.where(upp_i, jnp.exp(jnp.where(upp_i, gbc - gbr, 0.0)), 0.0))
            kb_f.append(k * bt_f[-1])
            kb_b.append(k * bt_b[-1])
            aq.append(_dot_nt(jnp.concatenate([kb_f[-1], kb_b[-1], qs[j, sl, :]], axis=0).astype(BF16),
                              k.astype(BF16)))
        lps = [jnp.where(low, x[:C] * df, 0.0) + jnp.where(upp, x[C:2 * C] * db, 0.0)
               for x, df, db in zip(aq, dec_f, dec_b)]
        inv = _packed_tri_inverse(lps, low, upp, bd16, rings)
        for n, ((j, c), sl) in enumerate(zip(items, sls)):
            q, k, v = qs[j, sl, :], ks[j, sl, :], vs[j, sl, :]
            for rev, g_c, kb, beta, dec, msk, u_s, wq_s, qk_s, kdt_s, egl_s in (
                    (False, g_f[n], kb_f[n], bt_f[n], dec_f[n], low, uf, wqf, qkf, kdtf, eglf),
                    (True, g_b[n], kb_b[n], bt_b[n], dec_b[n], upp, ub, wqb, qkb, kdtb, eglb)):
                eg = jnp.exp(g_c)
                rhs = jnp.concatenate([v * beta, kb * eg], axis=1)
                uw = rhs + _dot(jnp.where(msk, inv[n], 0.0).astype(BF16), rhs.astype(BF16))
                glast = g_c[0:1] if rev else g_c[C - 1:C]
                u_s[j, sl, :] = uw[:, :C]
                wq_s[j, pl.ds(pl.multiple_of(c * 2 * C, 2 * C), 2 * C), :] = jnp.concatenate(
                    [uw[:, C:], q * eg], axis=0).astype(BF16)
                qk_s[j, sl, :] = (aq[n][2 * C:] * dec).astype(BF16)
                kdt_s[j, sl, :] = (k * jnp.exp(glast - g_c)).T.astype(BF16)
                egl_s[j, pl.ds(c, 1), :] = jnp.exp(glast)
        return carry

    lax.fori_loop(0, nck // nbatch, prepare, 0)

    def body(i, carry):
        chains = []
        for j in heads:
            chains.append((j, i, uf, wqf, qkf, kdtf, eglf))
            chains.append((j, nck - 1 - i, ub, wqb, qkb, kdtb, eglb))
        sls = [pl.ds(pl.multiple_of(c * C, C), C) for _, c, *_ in chains]
        ws_qs = [_dot(wq_s[j, pl.ds(pl.multiple_of(c * 2 * C, 2 * C), 2 * C), :], st.astype(BF16))
                 for (j, c, _, wq_s, *_), st in zip(chains, carry)]
        vnb = [(u_s[j, sl, :] - x[:C]).astype(BF16) for (j, _, u_s, *_), sl, x in zip(chains, sls, ws_qs)]
        new = [st * egl_s[j, pl.ds(c, 1), :] + _dot(kdt_s[j, sl, :], v)
               for (j, c, _, _, _, kdt_s, egl_s), sl, st, v in zip(chains, sls, carry, vnb)]
        for (j, _, _, _, qk_s, _, _), sl, x, v in zip(chains, sls, ws_qs, vnb):
            os_[j, sl, :] += x[C:] + _dot(qk_s[j, sl, :], v)
        return tuple(new)

    zero = jnp.zeros((hd, hd), F32)
    lax.fori_loop(0, nck, body, (zero,) * (2 * GDN_PAIR), unroll=4)

    for j in heads:
        o = os_[j]
        o = o * lax.rsqrt(jnp.mean(o * o, axis=-1, keepdims=True) + RMS_EPS) * nw_ref[...]
        z = z_ref[:, j * hd:(j + 1) * hd].astype(F32)
        o_ref[:, j * hd:(j + 1) * hd] = (o * (z * jax.nn.sigmoid(z))).astype(BF16)


def _gdn(qkvz, gates3, conv_w, head_par, norm_w, bsz, seq):
    t = bsz * seq
    hd = GDN_HEAD_DIM
    nh = GDN_HEADS
    np_ = GDN_PAIR
    wd = np_ * hd
    npairs = nh // np_
    nck = seq // GDN_BLOCK
    col = lambda off: pl.BlockSpec((seq, wd), lambda b, p: (b, off * npairs + p))
    wcol = lambda off: pl.BlockSpec((GDN_CONV, wd), lambda b, p: (0, off * npairs + p))
    per_head = lambda rows, dt: pltpu.VMEM((np_, rows, hd), dt)
    return pl.pallas_call(
        _gdn_body,
        grid=(bsz, npairs),
        in_specs=[col(0), col(1), col(2), col(3), wcol(0), wcol(1), wcol(2),
                  pl.BlockSpec((nck, 8 * np_, GDN_BLOCK), lambda b, p: (b, p, 0)),
                  pl.BlockSpec((np_, 8, LANES), lambda b, p: (p, 0, 0)),
                  pl.BlockSpec((1, hd), lambda b, p: (0, 0))],
        out_specs=pl.BlockSpec((seq, wd), lambda b, p: (b, p)),
        out_shape=jax.ShapeDtypeStruct((t, nh * hd), BF16),
        scratch_shapes=([per_head(seq, F32)] * 4
                        + [per_head(nck * 8, F32)] * 3
                        + [per_head(seq, F32)] * 2
                        + [per_head(2 * seq, BF16)] * 2
                        + [per_head(seq, BF16)] * 2
                        + [per_head(seq, BF16)] * 2
                        + [per_head(nck, F32)] * 2
                        + [pltpu.VMEM((seq + 16, hd), F32)]),
        compiler_params=_cparams("parallel", "parallel"),
        name="gdn",
    )(qkvz, qkvz, qkvz, qkvz, conv_w, conv_w, conv_w, gates3, head_par, norm_w)


def _kv_body(m_ref, g_ref, wk_ref, wv_ref, k_ref, v_ref):
    mn = _rms(m_ref[...], g_ref[...]).astype(BF16)
    k_ref[...] = _dot(mn, wk_ref[...]).astype(BF16)
    v_ref[...] = _dot(mn, wv_ref[...]).astype(BF16)


def _mem_kv(mem2d, norm_w, wk, wv, tm=512):
    r, d = mem2d.shape
    full = lambda shape: pl.BlockSpec(shape, lambda i: (0,) * len(shape))
    tile = pl.BlockSpec((tm, d), lambda i: (i, 0))
    return pl.pallas_call(
        _kv_body,
        grid=(r // tm,),
        in_specs=[tile, full((1, d)), full(wk.shape), full(wv.shape)],
        out_specs=[tile, tile],
        out_shape=[jax.ShapeDtypeStruct((r, d), BF16)] * 2,
        compiler_params=_cparams("parallel"),
        name="mem_kv",
    )(mem2d, norm_w, wk, wv)


def _mix_xattn_body(x_ref, y5_ref, yg_ref, wmix_ref, g_ref, wq_ref, k_ref, v_ref, wo_ref,
                    gm_ref, whi_ref, wlo_ref, br_ref, tri_ref,
                    o_ref, mi_ref, mf_ref, cnt_ref, wmix_b, wq_b, wo_b, carry):
    @pl.when(pl.program_id(0) == 0)
    def _():
        wmix_b[...] = wmix_ref[...].astype(BF16)
        wq_b[...] = wq_ref[...].astype(BF16)
        wo_b[...] = wo_ref[...].astype(BF16)
        carry[...] = jnp.zeros_like(carry)

    x1 = (x_ref[...] + _dot(y5_ref[...], wmix_b[:S5_WIDTH, :]) + _dot(yg_ref[...], wmix_b[S5_WIDTH:, :]))
    xn = _rms(x1, g_ref[...]).astype(BF16)
    q = (_dot(xn, wq_b[...]) * (XA_HEAD_DIM ** -0.5)).astype(BF16)
    hsl = [slice(h * XA_HEAD_DIM, (h + 1) * XA_HEAD_DIM) for h in range(XA_HEADS)]
    sc = [_dot_nt(q[:, sl], k_ref[:, sl]) for sl in hsl]
    pr = [jnp.exp(s - jnp.max(s, axis=-1, keepdims=True)) for s in sc]
    pr = [p / jnp.sum(p, axis=-1, keepdims=True) for p in pr]
    heads = [_dot(p.astype(BF16), v_ref[:, sl]).astype(BF16) for p, sl in zip(pr, hsl)]
    x2 = x1 + _dot(jnp.concatenate(heads, axis=1), wo_b[...])
    o_ref[...] = x2
    mi, mf = _route(_rms(x2, gm_ref[...]), whi_ref[...], wlo_ref[...], br_ref[...], tri_ref[...], carry)
    mi_ref[...] = mi
    mf_ref[...] = mf
    cnt_ref[...] = carry[...]


def _mix_xattn(x2d, y5, yg, w_mix, norm_w, wq, kmem, vmem, wo, norm_moe, w_route, b_route, seq, mem_len, tm=512):
    t, d = x2d.shape
    per_b = seq // tm
    tri = jnp.triu(jnp.ones((tm, tm), BF16))
    w_hi = w_route.astype(BF16)
    w_lo = (w_route - w_hi.astype(F32)).astype(BF16)
    full = lambda shape: pl.BlockSpec(shape, lambda i: (0,) * len(shape))
    tile = lambda w: pl.BlockSpec((tm, w), lambda i: (i, 0))
    return pl.pallas_call(
        _mix_xattn_body,
        grid=(t // tm,),
        in_specs=[tile(d), tile(S5_WIDTH), tile(GDN_WIDTH),
                  full(w_mix.shape), full((1, d)), full(wq.shape),
                  pl.BlockSpec((mem_len, d), lambda i: (i // per_b, 0)),
                  pl.BlockSpec((mem_len, d), lambda i: (i // per_b, 0)),
                  full(wo.shape),
                  full((1, d)), full(w_route.shape), full(w_route.shape), full(b_route.shape), full((tm, tm))],
        out_specs=[tile(d), pl.BlockSpec((8, tm), lambda i: (0, i)), tile(LANES),
                   pl.BlockSpec((ROUTE_ROWS, LANES), lambda i: (0, 0))],
        out_shape=[jax.ShapeDtypeStruct((t, d), F32),
                   jax.ShapeDtypeStruct((8, t), I32),
                   jax.ShapeDtypeStruct((t, LANES), F32),
                   jax.ShapeDtypeStruct((ROUTE_ROWS, LANES), F32)],
        scratch_shapes=[pltpu.VMEM(w_mix.shape, BF16), pltpu.VMEM(wq.shape, BF16), pltpu.VMEM(wo.shape, BF16),
                        pltpu.VMEM((ROUTE_ROWS, LANES), F32)],
        compiler_params=_cparams("arbitrary"),
        name="mix_xattn",
    )(x2d, y5, yg, w_mix, norm_w, wq, kmem, vmem, wo, norm_moe, w_hi, w_lo, b_route, tri)


ROUTE_EXPERT_LANE0 = 4


ROUTE_ROWS = 40


def _route(xn, w_hi, w_lo, bias, tri, carry):
    x_hi = xn.astype(BF16)
    x_lo = (xn - x_hi.astype(F32)).astype(BF16)
    logits = (_dot_nt(w_hi, x_hi) + _dot_nt(w_hi, x_lo) + _dot_nt(w_lo, x_hi))[:ROUTE_ROWS] + bias[:ROUTE_ROWS, 0:1]
    tm = logits.shape[1]
    row = lax.broadcasted_iota(I32, (ROUTE_ROWS, tm), 0)
    neg = jnp.float32(-jnp.inf)
    big = jnp.int32(LANES)

    def top(vals):
        m = jnp.max(vals, axis=0, keepdims=True)
        idx = jnp.min(jnp.where(vals == m, row, big), axis=0, keepdims=True)
        return m, idx

    is_g = row < MOE_GROUPS
    gl = jnp.where(is_g, logits, neg)
    gmax, gidx = top(gl)
    p_top = 1.0 / jnp.sum(jnp.where(is_g, jnp.exp(gl - gmax), 0.0), axis=0, keepdims=True)
    erow = row - ROUTE_EXPERT_LANE0
    in_grp = jnp.logical_and(jnp.logical_and(erow >= 0, erow < MOE_EXPERTS), (erow // MOE_PER_GROUP) == gidx)
    es = jnp.where(in_grp, logits, neg)
    m1, i1 = top(es)
    m2, i2 = top(jnp.where(row == i1, neg, es))
    e21 = jnp.exp(m2 - m1)
    w1 = p_top / (1.0 + e21)
    w2 = p_top * e21 / (1.0 + e21)

    a1 = (row == i1).astype(F32)
    a2 = (row == i2).astype(F32)
    both = a1 + a2
    before = _dot(both.astype(BF16), tri) - both + carry[:, 0:1]
    r1 = jnp.sum(a1 * before, axis=0, keepdims=True).astype(I32)
    r2 = jnp.sum(a2 * before, axis=0, keepdims=True).astype(I32)
    carry[...] = carry[...] + jnp.sum(both, axis=1, keepdims=True)
    row8 = lax.broadcasted_iota(I32, (8, tm), 0)
    mi = jnp.where(row8 == 0, i1 - ROUTE_EXPERT_LANE0,
                   jnp.where(row8 == 1, i2 - ROUTE_EXPERT_LANE0, jnp.where(row8 == 2, r1, jnp.where(row8 == 3, r2, 0))))
    rowl = lax.broadcasted_iota(I32, (LANES, tm), 0)
    wt = jnp.where(rowl == 0, w1, jnp.where(rowl == 1, w2, 0.0))
    mf = jnp.concatenate([wt[:, j * LANES:(j + 1) * LANES].T for j in range(tm // LANES)], axis=0)
    return mi, mf


def _dest_body(mi_ref, off_ref, da_ref, db_ref):
    tm = da_ref.shape[2]
    mi = mi_ref[...]
    n = mi.shape[1]
    row = lax.broadcasted_iota(I32, (MOE_EXPERTS, n), 0)
    off = off_ref[:, 0:1]
    d0 = jnp.sum(jnp.where(row == mi[0:1], off, 0), axis=0, keepdims=True) + mi[2:3]
    d1 = jnp.sum(jnp.where(row == mi[1:2], off, 0), axis=0, keepdims=True) + mi[3:4]
    for s in range(da_ref.shape[0]):
        da_ref[s] = d0[:, s * tm:(s + 1) * tm]
        db_ref[s] = d1[:, s * tm:(s + 1) * tm]


def _dest_rows(mi, offsets_col, tm):
    t = mi.shape[1]
    tiles_per_step = math.gcd(8, t // tm)
    out = pl.BlockSpec((tiles_per_step, 1, tm), lambda i: (i, 0, 0))
    return pl.pallas_call(
        _dest_body,
        grid=(t // (tm * tiles_per_step),),
        in_specs=[pl.BlockSpec((8, tm * tiles_per_step), lambda i: (0, i)),
                  pl.BlockSpec((MOE_EXPERTS, LANES), lambda i: (0, 0))],
        out_specs=[out, out],
        out_shape=[jax.ShapeDtypeStruct((t // tm, 1, tm), I32)] * 2,
        compiler_params=_cparams("parallel"),
        name="moe_dest",
    )(mi, offsets_col)


def _dispatch_body(zs_ref, na_ref, da_ref, db_ref, xn_ref, xs_ref, zbuf, sem, zsem):
    tm = xn_ref.shape[0]
    tz = zbuf.shape[0]
    n_blocks = xs_ref.shape[0] // tz

    @pl.when(pl.program_id(0) == 0)
    def _():
        zbuf[...] = jnp.zeros_like(zbuf)

        def fill(row0):
            return pltpu.make_async_copy(zbuf, xs_ref.at[pl.ds(pl.multiple_of(row0, tz), tz), :], zsem)

        def tail(e, c):
            @pl.when(zs_ref[e] >= 0)
            def _():
                fill(zs_ref[e]).start()
            return c

        def unused(j, c):
            fill(j * tz).start()
            return c

        def drain(j, c):
            fill(0).wait()
            return c

        lax.fori_loop(0, zs_ref.shape[0], tail, 0)
        lax.fori_loop(na_ref[0], n_blocks, unused, 0)
        lax.fori_loop(0, na_ref[1] + n_blocks - na_ref[0], drain, 0)

    def start(r, c):
        for k, d_ref in enumerate((da_ref, db_ref)):
            pltpu.make_async_copy(xn_ref.at[pl.ds(r, 1), :], xs_ref.at[pl.ds(d_ref[0, r], 1), :],
                                  sem.at[k]).start(priority=k)
        return c

    lax.fori_loop(0, tm, start, 0, unroll=8)
    for k in range(MOE_TOPK):
        pltpu.make_async_copy(xn_ref, xs_ref.at[pl.ds(0, tm), :], sem.at[k]).wait()


def _dispatch(zero_start, n_active, dest_a, dest_b, xn, n_rows, tm, te):
    t, dw = xn.shape
    smem_row = pl.BlockSpec((None, 1, tm), lambda i, zs, na: (i, 0, 0), memory_space=pltpu.SMEM)
    grid_spec = pltpu.PrefetchScalarGridSpec(
        num_scalar_prefetch=2,
        grid=(t // tm,),
        in_specs=[smem_row, smem_row, pl.BlockSpec((tm, dw), lambda i, zs, na: (i, 0))],
        out_specs=pl.BlockSpec(memory_space=pl.ANY),
        scratch_shapes=[pltpu.VMEM((te, dw), xn.dtype), pltpu.SemaphoreType.DMA((MOE_TOPK,)),
                        pltpu.SemaphoreType.DMA(())],
    )
    return pl.pallas_call(
        _dispatch_body,
        grid_spec=grid_spec,
        out_shape=jax.ShapeDtypeStruct((n_rows, dw), xn.dtype),
        compiler_params=_cparams("arbitrary"),
        name="moe_dispatch",
    )(zero_start, n_active, dest_a, dest_b, xn)


def _experts_body(te_ref, na_ref, x_ref, g_ref, wg_ref, wu_ref, wd_ref, y_ref, wg_b, wu_b, wd_b):
    i = pl.program_id(0)

    @pl.when(i < na_ref[0])
    def _():
        @pl.when(jnp.logical_or(i == 0, te_ref[i] != te_ref[jnp.maximum(i - 1, 0)]))
        def _():
            wg_b[...] = wg_ref[...].astype(BF16)
            wu_b[...] = wu_ref[...].astype(BF16)
            wd_b[...] = wd_ref[...].astype(BF16)

        x = _rms(x_ref[...], g_ref[...]).astype(BF16)
        gt = _dot(x, wg_b[...])
        up = _dot(x, wu_b[...])
        hid = (gt * jax.nn.sigmoid(gt) * up).astype(BF16)
        y_ref[...] = _dot(hid, wd_b[...])

    @pl.when(i >= na_ref[0])
    def _():
        y_ref[...] = jnp.zeros_like(y_ref)


def _experts(tile_expert, n_active, xs, norm_w, w_gate, w_up, w_down, tm):
    r, d = xs.shape
    f = w_gate.shape[2]
    row_tile = lambda i, te, na: (jnp.minimum(i, na[0] - 1), 0)
    grid_spec = pltpu.PrefetchScalarGridSpec(
        num_scalar_prefetch=2,
        grid=(r // tm,),
        in_specs=[pl.BlockSpec((tm, d), row_tile),
                  pl.BlockSpec((1, d), lambda i, te, na: (0, 0)),
                  pl.BlockSpec((None, d, f), lambda i, te, na: (te[i], 0, 0)),
                  pl.BlockSpec((None, d, f), lambda i, te, na: (te[i], 0, 0)),
                  pl.BlockSpec((None, f, d), lambda i, te, na: (te[i], 0, 0))],
        out_specs=pl.BlockSpec((tm, d), lambda i, te, na: (i, 0)),
        scratch_shapes=[pltpu.VMEM((d, f), BF16), pltpu.VMEM((d, f), BF16), pltpu.VMEM((f, d), BF16)],
    )
    return pl.pallas_call(
        _experts_body,
        grid_spec=grid_spec,
        out_shape=jax.ShapeDtypeStruct((r, d), xs.dtype),
        compiler_params=_cparams("arbitrary"),
        name="moe_experts",
    )(tile_expert, n_active, xs, norm_w, w_gate, w_up, w_down)


def _combine_body(da_ref, db_ref, na_ref, nb_ref, x_ref, mf_ref, g_ref, ys_ref, o_ref, buf, sem):
    tm = x_ref.shape[0]
    i = pl.program_id(0)
    slot = lax.rem(i, 2)

    def gather(d_refs, s):
        def start(r, c):
            for k, d_ref in enumerate(d_refs):
                pltpu.make_async_copy(ys_ref.at[pl.ds(d_ref[0, r], 1), :],
                                      buf.at[s, k, pl.ds(r, 1), :], sem.at[s, k]).start(priority=k)
            return c

        lax.fori_loop(0, tm, start, 0, unroll=8)

    @pl.when(i == 0)
    def _():
        gather((da_ref, db_ref), 0)

    @pl.when(i + 1 < pl.num_programs(0))
    def _():
        gather((na_ref, nb_ref), 1 - slot)

    for k in range(MOE_TOPK):
        pltpu.make_async_copy(ys_ref.at[pl.ds(0, tm), :], buf.at[slot, k], sem.at[slot, k]).wait()
    mf = mf_ref[...]
    y = x_ref[...] + mf[:, 0:1] * buf[slot, 0] + mf[:, 1:2] * buf[slot, 1]
    o_ref[...] = _rms(y, g_ref[...])


def _combine(dest_a, dest_b, x2d, mf, norm_w, ys, tm):
    t, d = x2d.shape
    last = t // tm - 1
    cur = pl.BlockSpec((None, 1, tm), lambda i: (i, 0, 0), memory_space=pltpu.SMEM)
    nxt = pl.BlockSpec((None, 1, tm), lambda i: (jnp.minimum(i + 1, last), 0, 0), memory_space=pltpu.SMEM)
    return pl.pallas_call(
        _combine_body,
        grid=(t // tm,),
        in_specs=[cur, cur, nxt, nxt,
                  pl.BlockSpec((tm, d), lambda i: (i, 0)),
                  pl.BlockSpec((tm, LANES), lambda i: (i, 0)),
                  pl.BlockSpec((1, d), lambda i: (0, 0)),
                  pl.BlockSpec(memory_space=pl.ANY)],
        out_specs=pl.BlockSpec((tm, d), lambda i: (i, 0)),
        out_shape=jax.ShapeDtypeStruct((t, d), F32),
        scratch_shapes=[pltpu.VMEM((2, MOE_TOPK, tm, ys.shape[1]), ys.dtype), pltpu.SemaphoreType.DMA((2, MOE_TOPK))],
        compiler_params=_cparams("arbitrary"),
        name="moe_combine",
    )(dest_a, dest_b, dest_a, dest_b, x2d, mf, norm_w, ys)


MOE_ROW_TILE = 512
MOE_TOKEN_TILE = 1024
MOE_ZERO_BLOCK = 256


def _moe(x2d, mi, mf, cnt, norm_w, w_gate, w_up, w_down, norm_final):
    t, d = x2d.shape
    tm = MOE_ROW_TILE
    tok = MOE_TOKEN_TILE
    counts = cnt[ROUTE_EXPERT_LANE0:ROUTE_EXPERT_LANE0 + MOE_EXPERTS, 0].astype(I32)
    padded = ((counts + tm - 1) // tm) * tm
    ends = jnp.cumsum(padded)
    offsets = ends - padded
    n_tiles = (MOE_TOPK * t + MOE_EXPERTS * (tm - 1)) // tm
    tile_start = jnp.arange(n_tiles, dtype=I32) * tm
    tile_expert = jnp.minimum(jnp.sum((ends[None, :] <= tile_start[:, None]).astype(I32), axis=1), MOE_EXPERTS - 1)
    n_info = jnp.stack([ends[-1] // tm, jnp.sum((counts > 0).astype(I32))]).astype(I32)
    tz = MOE_ZERO_BLOCK
    pad = padded - counts
    zero_start = jnp.concatenate([jnp.where(pad > j * tz, ends - (j + 1) * tz, -1) for j in range(tm // tz)])
    zero_start = jnp.where(jnp.tile(counts, tm // tz) > 0, zero_start, -1).astype(I32)
    z_info = jnp.stack([ends[-1] // tz, jnp.sum((zero_start >= 0).astype(I32))]).astype(I32)
    offsets_col = jnp.broadcast_to(offsets[:, None], (MOE_EXPERTS, LANES))
    dest_a, dest_b = _dest_rows(mi, offsets_col, tok)
    xs = _dispatch(zero_start, z_info, dest_a, dest_b, x2d, n_tiles * tm, tok, tz)
    ys = _experts(tile_expert, n_info, xs, norm_w, w_gate, w_up, w_down, tm)
    return _combine(dest_a, dest_b, x2d, mf, norm_final, ys, tok)


def kernel(x, mem, norm_mix, w_in, w_out,
           s5_lam_re_f, s5_lam_im_f, s5_log_step_f, s5_b_re_f, s5_b_im_f, s5_c_re_f, s5_c_im_f,
           s5_lam_re_b, s5_lam_im_b, s5_log_step_b, s5_b_re_b, s5_b_im_b, s5_c_re_b, s5_c_im_b,
           s5_d, s5_w_glu, s5_b_glu, s5_norm,
           gdn_conv, gdn_a_log_f, gdn_dt_bias_f, gdn_a_log_b, gdn_dt_bias_b, gdn_norm,
           norm_xattn, norm_mem, xa_wq, xa_wk, xa_wv, xa_wo,
           norm_moe, router_group_w, router_group_b, router_expert_w, router_expert_b,
           moe_w_gate, moe_w_up, moe_w_down, norm_final):
    bsz, seq, d = x.shape
    t = bsz * seq
    l = 0
    x2d = x.reshape(t, d)
    wi = w_in[l]
    wg = wi[:, S5_WIDTH + 4 * GDN_WIDTH:].reshape(d, 4, GDN_HEADS)
    wg = jnp.pad(jnp.swapaxes(wg, 1, 2), ((0, 0), (0, 0), (0, 4))).reshape(d, GDN_HEADS * 8)
    wgt = wg.T
    u3, qkvz, gates = _in_proj(x2d, norm_mix[l][None], wi, wgt)
    s5p = dict(lam_re_f=s5_lam_re_f[l], lam_im_f=s5_lam_im_f[l], log_step_f=s5_log_step_f[l],
               b_re_f=s5_b_re_f[l], b_im_f=s5_b_im_f[l], c_re_f=s5_c_re_f[l], c_im_f=s5_c_im_f[l],
               lam_re_b=s5_lam_re_b[l], lam_im_b=s5_lam_im_b[l], log_step_b=s5_log_step_b[l],
               b_re_b=s5_b_re_b[l], b_im_b=s5_b_im_b[l], c_re_b=s5_c_re_b[l], c_im_b=s5_c_im_b[l], d=s5_d[l])
    g3 = _s5_scan(u3, s5p, seq // S5_CHUNK)
    y_s5 = _s5_post(g3, s5_w_glu[l].T.astype(BF16), s5_b_glu[l][:, None], s5_norm[l][:, None])

    head_par = jnp.stack([gdn_a_log_f[l], gdn_dt_bias_f[l], gdn_a_log_b[l], gdn_dt_bias_b[l]], axis=1)
    head_par = jnp.broadcast_to(jnp.pad(head_par, ((0, 0), (0, 4)))[:, :, None], (GDN_HEADS, 8, LANES))
    y_gdn = _gdn(qkvz, gates, gdn_conv[l], head_par, gdn_norm[l][None], bsz, seq)

    mem_len = mem.shape[1]
    kmem, vmem = _mem_kv(mem.reshape(bsz * mem_len, d), norm_mem[l][None],
                         xa_wk[l].astype(BF16), xa_wv[l].astype(BF16))
    n_pad = LANES - MOE_GROUPS - MOE_EXPERTS
    w_route = jnp.pad(jnp.concatenate([router_group_w[l], router_expert_w[l]], axis=1), ((0, 0), (0, n_pad))).T
    b_route = jnp.pad(jnp.concatenate([router_group_b[l], router_expert_b[l]]), (0, n_pad))
    b_route = jnp.broadcast_to(b_route[:, None], (LANES, LANES))
    x2, mi, mf, cnt = _mix_xattn(x2d, y_s5, y_gdn, w_out[l], norm_xattn[l][None], xa_wq[l], kmem, vmem, xa_wo[l],
                                 norm_moe[l][None], w_route, b_route, seq, mem_len)
    y = _moe(x2, mi, mf, cnt, norm_moe[l][None], moe_w_gate[l], moe_w_up[l], moe_w_down[l], norm_final[None])
    return y.reshape(bsz, seq, d)
```

```python
import functools
import math

import jax
import jax.numpy as jnp
from jax import lax
from jax.experimental import pallas as pl
from jax.experimental.pallas import tpu as pltpu

F32 = jnp.float32
BF16 = jnp.bfloat16
I32 = jnp.int32

D_MODEL = 1024
S5_WIDTH = 512
S5_GROUP = 16
S5_GROUPS = 32
S5_STATE = 64
S5_CHUNK = 128
GDN_HEADS = 4
GDN_HEAD_DIM = 128
GDN_WIDTH = 512
GDN_CONV = 5
GDN_CHUNK = 64
XA_HEADS = 4
XA_HEAD_DIM = 256
MOE_GROUPS = 4
MOE_PER_GROUP = 8
MOE_EXPERTS = 32
MOE_TOPK = 2
D_EXPERT = 256
RMS_EPS = 1e-6
L2_EPS = 1e-6
LANES = 128
VMEM_LIMIT = 56 * 1024 * 1024


def _cparams(*sem):
    return pltpu.CompilerParams(dimension_semantics=tuple(sem), vmem_limit_bytes=VMEM_LIMIT)


def _rms(x, gain):
    return x * lax.rsqrt(jnp.mean(x * x, axis=-1, keepdims=True) + RMS_EPS) * gain


def _dot(a, b):
    return jnp.dot(a, b, preferred_element_type=F32)


def _dot_nt(a, b):
    return lax.dot_general(a, b, (((1,), (1,)), ((), ())), preferred_element_type=F32)


def _dot_tn(a, b):
    return lax.dot_general(a, b, (((0,), (0,)), ((), ())), preferred_element_type=F32)


def _in_proj_body(x_ref, g_ref, w_ref, wgt_ref, u_ref, qkvz_ref, gates_ref, wut_b, wqkvz_b, wgt_b):
    @pl.when(pl.program_id(0) == 0)
    def _():
        wut_b[...] = w_ref[:, :S5_WIDTH].T.astype(BF16)
        wqkvz_b[...] = w_ref[:, S5_WIDTH:S5_WIDTH + 4 * GDN_WIDTH].astype(BF16)
        wgt_b[...] = wgt_ref[...].astype(BF16)

    h = _rms(x_ref[...], g_ref[...]).astype(BF16)
    ut = _dot_nt(wut_b[...], h)
    gt = _dot_nt(wgt_b[...], h)
    for j in range(u_ref.shape[0]):
        u_ref[j] = ut[:, j * S5_CHUNK:(j + 1) * S5_CHUNK]
        gates_ref[j] = gt[:, j * S5_CHUNK:(j + 1) * S5_CHUNK]
    qkvz_ref[...] = _dot(h, wqkvz_b[...]).astype(BF16)


def _in_proj(x2d, norm_w, w_in, wgt, tm=1024):
    t = x2d.shape[0]
    tm = math.gcd(tm, t)
    nck = tm // S5_CHUNK
    nqkvz = 4 * GDN_WIDTH
    full = lambda shape: pl.BlockSpec(shape, lambda i: (0,) * len(shape), pipeline_mode=pl.Buffered(1))
    return pl.pallas_call(
        _in_proj_body,
        grid=(t // tm,),
        in_specs=[pl.BlockSpec((tm, D_MODEL), lambda i: (i, 0)),
                  full((1, D_MODEL)), full(w_in.shape), full(wgt.shape)],
        out_specs=[pl.BlockSpec((nck, S5_WIDTH, S5_CHUNK), lambda i: (i, 0, 0)),
                   pl.BlockSpec((tm, nqkvz), lambda i: (i, 0)),
                   pl.BlockSpec((nck, wgt.shape[0], S5_CHUNK), lambda i: (i, 0, 0))],
        out_shape=[jax.ShapeDtypeStruct((t // S5_CHUNK, S5_WIDTH, S5_CHUNK), F32),
                   jax.ShapeDtypeStruct((t, nqkvz), BF16),
                   jax.ShapeDtypeStruct((t // S5_CHUNK, wgt.shape[0], S5_CHUNK), F32)],
        scratch_shapes=[pltpu.VMEM((S5_WIDTH, D_MODEL), BF16), pltpu.VMEM((D_MODEL, nqkvz), BF16),
                        pltpu.VMEM(wgt.shape, BF16)],
        compiler_params=_cparams("arbitrary"),
        name="in_proj",
    )(x2d, norm_w, w_in, wgt)


def _cmul(ar, ai, br, bi):
    return ar * br - ai * bi, ar * bi + ai * br


def _cpow_int(lr, li, expo, nbits):
    res_r = jnp.ones(jnp.broadcast_shapes(lr.shape, expo.shape), F32)
    res_i = jnp.zeros_like(res_r)
    for b in range(nbits):
        bit = ((expo >> b) & 1) == 1
        nr, ni = _cmul(res_r, res_i, lr, li)
        res_r = jnp.where(bit, nr, res_r)
        res_i = jnp.where(bit, ni, res_i)
        if b + 1 < nbits:
            lr, li = _cmul(lr, li, lr, li)
    return res_r, res_i


def _lam_bar(re, im, step):
    er = jnp.exp(step * re)
    return er * jnp.cos(step * im), er * jnp.sin(step * im)


def _zoh_coef(re, im, lr, li):
    den = re * re + im * im
    return ((lr - 1.0) * re + li * im) / den, (li * re - (lr - 1.0) * im) / den


def _s5_body(u_ref, lrow_ref, lcol_ref, l256_ref, b_ref, bt_ref, c_ref, ct_ref, d_ref,
             o_ref, vf_ref, vb_ref, m_ref, win_ref, wout_ref, sf_ref, sb_ref, hf_ref, hb_ref, uc_ref, y_ref, *, nchunk):
    L = S5_CHUNK
    P = S5_STATE
    n_rows = u_ref.shape[0]
    nb = n_rows // nchunk
    lane_i = lax.broadcasted_iota(I32, (1, L), 1)

    lcol = lcol_ref[...]
    lbc_r, lbc_i = _lam_bar(lcol[:, 0:2], lcol[:, 2:4], lcol[:, 4:6])
    kc_r, kc_i = _zoh_coef(lcol[:, 0:2], lcol[:, 2:4], lbc_r, lbc_i)
    lrow = lrow_ref[...]
    lbr_r, lbr_i = _lam_bar(lrow[0:2], lrow[2:4], lrow[4:6])
    kr_r, kr_i = _zoh_coef(lrow[0:2], lrow[2:4], lbr_r, lbr_i)
    lf_r, lf_i, lb_r, lb_i = lbc_r[:, 0:1], lbc_i[:, 0:1], lbc_r[:, 1:2], lbc_i[:, 1:2]
    kfr_c, kfi_c, kbr_c, kbi_c = kc_r[:, 0:1], kc_i[:, 0:1], kc_r[:, 1:2], kc_i[:, 1:2]
    kfr_r, kfi_r, kbr_r, kbi_r = kr_r[0:1], kr_i[0:1], kr_r[1:2], kr_i[1:2]

    pwf_r, pwf_i = _cpow_int(lf_r, lf_i, lane_i, 7)
    rvf_r, rvf_i = _cpow_int(lf_r, lf_i, (L - 1) - lane_i, 7)
    pwb_r, pwb_i = _cpow_int(lb_r, lb_i, lane_i, 7)
    rvb_r, rvb_i = _cpow_int(lb_r, lb_i, L - lane_i, 8)
    nxf_r, nxf_i = _cmul(pwf_r, pwf_i, lf_r, lf_i)

    bf_r = kfr_c * b_ref[0] - kfi_c * b_ref[1]
    bf_i = kfr_c * b_ref[1] + kfi_c * b_ref[0]
    bb_r = kbr_c * b_ref[2] - kbi_c * b_ref[3]
    bb_i = kbr_c * b_ref[3] + kbi_c * b_ref[2]
    btf_r = kfr_r * bt_ref[0] - kfi_r * bt_ref[1]
    btf_i = kfr_r * bt_ref[1] + kfi_r * bt_ref[0]
    btb_r = kbr_r * bt_ref[2] - kbi_r * bt_ref[3]
    btb_i = kbr_r * bt_ref[3] + kbi_r * bt_ref[2]

    def taps(c_r, c_i, bt_r, bt_i, pw_r, pw_i):
        cb_r = (bt_r[:, None, :] * c_r[None, :, :] - bt_i[:, None, :] * c_i[None, :, :]).reshape(256, P)
        cb_i = (bt_r[:, None, :] * c_i[None, :, :] + bt_i[:, None, :] * c_r[None, :, :]).reshape(256, P)
        k = (jnp.dot(cb_r, pw_r, preferred_element_type=F32, precision=lax.Precision.HIGHEST)
             - jnp.dot(cb_i, pw_i, preferred_element_type=F32, precision=lax.Precision.HIGHEST))
        return k, jnp.sum(cb_r, axis=1, keepdims=True)

    kf, _ = taps(c_ref[0], c_ref[1], btf_r, btf_i, pwf_r, pwf_i)
    kb, kb0 = taps(c_ref[2], c_ref[3], btb_r, btb_i, rvb_r, rvb_i)
    is0 = lane_i == 0
    vf_ref[...] = kf + jnp.where(is0, kb0, 0.0)
    vb_ref[...] = jnp.where(is0, 0.0, kb)

    row_i = lax.broadcasted_iota(I32, (L, L), 0)
    col_i = lax.broadcasted_iota(I32, (L, L), 1)
    fwd_lane = col_i + row_i < L

    def build_ci(ci, carry):
        for co in range(S5_GROUP):
            r = ci * S5_GROUP + co
            taps_rows = jnp.where(fwd_lane, jnp.broadcast_to(vf_ref[pl.ds(r, 1), :], (L, L)),
                                  jnp.broadcast_to(vb_ref[pl.ds(r, 1), :], (L, L)))
            m_ref[pl.ds(pl.multiple_of(ci * L, L), L), co * L:(co + 1) * L] = pltpu.roll(
                taps_rows, 0, 1, stride=1, stride_axis=0).astype(BF16)
        return carry

    lax.fori_loop(0, S5_GROUP, build_ci, 0, unroll=4)

    for ci in range(S5_GROUP):
        sl = slice(ci * L, (ci + 1) * L)
        br, bi = bf_r[:, ci:ci + 1], bf_i[:, ci:ci + 1]
        win_ref[0 * P:1 * P, sl] = (rvf_r * br - rvf_i * bi).astype(BF16)
        win_ref[1 * P:2 * P, sl] = (rvf_r * bi + rvf_i * br).astype(BF16)
        br, bi = bb_r[:, ci:ci + 1], bb_i[:, ci:ci + 1]
        win_ref[2 * P:3 * P, sl] = (pwb_r * br - pwb_i * bi).astype(BF16)
        win_ref[3 * P:4 * P, sl] = (pwb_r * bi + pwb_i * br).astype(BF16)
    for co in range(S5_GROUP):
        sl = slice(co * L, (co + 1) * L)
        cr, ci_ = ct_ref[0][:, co:co + 1], ct_ref[1][:, co:co + 1]
        wout_ref[0 * P:1 * P, sl] = (cr * nxf_r - ci_ * nxf_i).astype(BF16)
        wout_ref[1 * P:2 * P, sl] = (-(cr * nxf_i + ci_ * nxf_r)).astype(BF16)
        cr, ci_ = ct_ref[2][:, co:co + 1], ct_ref[3][:, co:co + 1]
        wout_ref[2 * P:3 * P, sl] = (cr * rvb_r - ci_ * rvb_i).astype(BF16)
        wout_ref[3 * P:4 * P, sl] = (-(cr * rvb_i + ci_ * rvb_r)).astype(BF16)

    for ci in range(S5_GROUP):
        uc_ref[ci] = u_ref[:, ci, :]
    ucat = jnp.concatenate([uc_ref[ci].astype(BF16) for ci in range(S5_GROUP)], axis=1)

    summ = _dot_nt(ucat, win_ref[...])
    sf_ref[...] = summ[:, :2 * P]
    sb_ref[...] = summ[:, 2 * P:]
    nblk = 2 * L
    for j in range(S5_GROUP * L // nblk):
        y_ref[:, j * nblk:(j + 1) * nblk] = _dot(ucat, m_ref[:, j * nblk:(j + 1) * nblk])
    l256 = l256_ref[...]
    a_mul, a_im = _lam_bar(l256[0:1], l256[1:2], l256[2:3])
    for _ in range(7):
        a_mul, a_im = _cmul(a_mul, a_im, a_mul, a_im)
    lane256 = lax.broadcasted_iota(I32, (1, 4 * P), 1)
    b_mul = jnp.where((lane256 // P) % 2 == 0, -a_im, a_im)

    hf = jnp.zeros((nb, 2 * P), F32)
    hb = jnp.zeros((nb, 2 * P), F32)
    for c in range(nchunk):
        cr = nchunk - 1 - c
        rows_f = pl.ds(c, nb, stride=nchunk)
        rows_b = pl.ds(cr, nb, stride=nchunk)
        hf_ref[rows_f, :] = hf
        hb_ref[rows_b, :] = hb
        hf = a_mul[:, :2 * P] * hf + b_mul[:, :2 * P] * pltpu.roll(hf, P, 1) + sf_ref[rows_f, :]
        hb = a_mul[:, 2 * P:] * hb + b_mul[:, 2 * P:] * pltpu.roll(hb, P, 1) + sb_ref[rows_b, :]
    hprev = jnp.concatenate([hf_ref[...], hb_ref[...]], axis=1).astype(BF16)

    for j in range(S5_GROUP * L // nblk):
        y = y_ref[:, j * nblk:(j + 1) * nblk] + _dot(hprev, wout_ref[:, j * nblk:(j + 1) * nblk])
        for q in range(nblk // L):
            co = j * (nblk // L) + q
            yc = y[:, q * L:(q + 1) * L] + d_ref[co:co + 1, :] * uc_ref[co]
            o_ref[:, co, :] = 0.5 * yc * (1.0 + lax.erf(yc * (2.0 ** -0.5)))


def _s5_scan(u3, p, nchunk):
    n = u3.shape[0]
    g, grp, st, L = S5_GROUPS, S5_GROUP, S5_STATE, S5_CHUNK
    step_f = jnp.exp(p["log_step_f"])[:, None] * jnp.ones((1, st), F32)
    step_b = jnp.exp(p["log_step_b"])[:, None] * jnp.ones((1, st), F32)
    zeros = jnp.zeros((g, st), F32)
    lrow = jnp.stack([p["lam_re_f"], p["lam_re_b"], p["lam_im_f"], p["lam_im_b"], step_f, step_b, zeros, zeros], axis=1)
    lcol = jnp.swapaxes(lrow, 1, 2)
    cat4 = lambda f, b: jnp.concatenate([f, f, b, b], axis=1)
    z256 = jnp.zeros((g, 4 * st), F32)
    l256 = jnp.stack([cat4(p["lam_re_f"], p["lam_re_b"]), cat4(p["lam_im_f"], p["lam_im_b"]),
                      cat4(step_f, step_b)] + [z256] * 5, axis=1)
    b4 = jnp.stack([p["b_re_f"], p["b_im_f"], p["b_re_b"], p["b_im_b"]], axis=1)
    bt4 = jnp.swapaxes(b4, 2, 3)
    c4 = jnp.stack([p["c_re_f"], p["c_im_f"], p["c_re_b"], p["c_im_b"]], axis=1)
    ct4 = jnp.swapaxes(c4, 2, 3)
    dbc = jnp.broadcast_to(p["d"].reshape(g, grp, 1), (g, grp, L))
    per_g = lambda *shape: pl.BlockSpec((None,) + shape, lambda i: (i,) + (0,) * len(shape))
    return pl.pallas_call(
        functools.partial(_s5_body, nchunk=nchunk),
        grid=(g,),
        in_specs=[pl.BlockSpec((n, grp, L), lambda i: (0, i, 0)),
                  per_g(8, st), per_g(st, 8), per_g(8, 4 * st), per_g(4, st, grp), per_g(4, grp, st),
                  per_g(4, grp, st), per_g(4, st, grp), per_g(grp, L)],
        out_specs=pl.BlockSpec((n, grp, L), lambda i: (0, i, 0)),
        out_shape=jax.ShapeDtypeStruct(u3.shape, F32),
        scratch_shapes=[pltpu.VMEM((grp * grp, L), F32), pltpu.VMEM((grp * grp, L), F32),
                        pltpu.VMEM((grp * L, grp * L), BF16),
                        pltpu.VMEM((4 * st, grp * L), BF16), pltpu.VMEM((4 * st, grp * L), BF16),
                        pltpu.VMEM((n, 2 * st), F32), pltpu.VMEM((n, 2 * st), F32),
                        pltpu.VMEM((n, 2 * st), F32), pltpu.VMEM((n, 2 * st), F32),
                        pltpu.VMEM((grp, n, L), F32), pltpu.VMEM((n, grp * L), F32)],
        compiler_params=_cparams("parallel"),
        name="s5_scan",
    )(u3, lrow, lcol, l256, b4, bt4, c4, ct4, dbc)


def _s5_post_body(g_ref, wt_ref, b_ref, nw_ref, o_ref):
    for j in range(g_ref.shape[0]):
        g = g_ref[j]
        z = _dot(wt_ref[...], g.astype(BF16)) + b_ref[...]
        y = g * jax.nn.sigmoid(z)
        y = y * lax.rsqrt(jnp.mean(y * y, axis=0, keepdims=True) + RMS_EPS) * nw_ref[...]
        o_ref[j * S5_CHUNK:(j + 1) * S5_CHUNK, :] = y.T.astype(BF16)


def _s5_post(g3, w_glu_t, b_glu_col, norm_col):
    n = g3.shape[0]
    nck = math.gcd(16, n)
    full = lambda shape: pl.BlockSpec(shape, lambda i: (0,) * len(shape))
    return pl.pallas_call(
        _s5_post_body,
        grid=(n // nck,),
        in_specs=[pl.BlockSpec((nck, S5_WIDTH, S5_CHUNK), lambda i: (i, 0, 0)),
                  full(w_glu_t.shape), full(b_glu_col.shape), full(norm_col.shape)],
        out_specs=pl.BlockSpec((nck * S5_CHUNK, S5_WIDTH), lambda i: (i, 0)),
        out_shape=jax.ShapeDtypeStruct((n * S5_CHUNK, S5_WIDTH), BF16),
        compiler_params=_cparams("parallel"),
        name="s5_post",
    )(g3, w_glu_t, b_glu_col, norm_col)


GDN_BLOCK = 128
GDN_PAIR = 2
GDN_PREP_BATCH = 8


def _packed_tri_inverse(lps, low, upp, bd16, rings):
    def pk(xs, ys):
        outs = []
        for a, b in zip(xs, ys):
            lhs = jnp.concatenate([jnp.where(low, a, 0.0), jnp.where(upp, a, 0.0)], axis=1).astype(BF16)
            rhs = jnp.concatenate([jnp.where(low, b, 0.0), jnp.where(upp, b, 0.0)], axis=0).astype(BF16)
            outs.append(_dot(lhs, rhs))
        return outs

    d = [jnp.where(bd16, lp, 0.0) for lp in lps]
    d2 = pk(d, d)
    d4 = pk(d2, d2)
    d8 = pk(d4, d4)
    a = [y - x - p for x, y, p in zip(d, d2, pk(d, d2))]
    a = [x + y + p for x, y, p in zip(a, d4, pk(a, d4))]
    a = [x + y + p for x, y, p in zip(a, d8, pk(a, d8))]
    for ring in rings:
        n = [jnp.where(ring, lp, 0.0) for lp in lps]
        t = [x + p for x, p in zip(n, pk(a, n))]
        a = [x - y - p for x, y, p in zip(a, t, pk(t, a))]
    return a


def _gdn_body(q_ref, k_ref, v_ref, z_ref, wq_ref, wk_ref, wv_ref, g_ref, hp_ref, nw_ref, o_ref,
              qs, ks, vs, os_, sg, cf, cb, uf, ub, wqf, wqb, qkf, qkb, kdtf, kdtb, eglf, eglb, xpad):
    seq = q_ref.shape[0]
    C = GDN_BLOCK
    hd = GDN_HEAD_DIM
    nck = seq // C
    heads = range(GDN_PAIR)

    pad = 8
    half = (GDN_CONV - 1) // 2
    xpad[0:pad, :] = jnp.zeros((pad, LANES), F32)
    xpad[pad + seq:2 * pad + seq, :] = jnp.zeros((pad, LANES), F32)

    def conv_silu(x_ref, w_ref, j):
        cols = pl.ds(pl.multiple_of(j * hd, hd), hd)
        xpad[pad:pad + seq, :] = x_ref[:, cols].astype(F32)
        w = w_ref[:, cols]
        acc = xpad[pad - half:pad - half + seq, :] * w[0:1]
        for tap in range(1, GDN_CONV):
            acc = acc + xpad[pad - half + tap:pad - half + tap + seq, :] * w[tap:tap + 1]
        return acc * jax.nn.sigmoid(acc)

    def l2n(x):
        return x * lax.rsqrt(jnp.sum(x * x, axis=-1, keepdims=True) + L2_EPS)

    def softplus(x):
        return jnp.maximum(x, 0.0) + jnp.log1p(jnp.exp(-jnp.abs(x)))

    lane = lax.broadcasted_iota(I32, (1, C), 1)

    def prologue(j, carry):
        qs[j] = l2n(conv_silu(q_ref, wq_ref, j)) * (hd ** -0.5)
        ks[j] = l2n(conv_silu(k_ref, wk_ref, j))
        vs[j] = conv_silu(v_ref, wv_ref, j)
        os_[j] = jnp.zeros((seq, hd), F32)
        g = g_ref[:, pl.ds(pl.multiple_of(8 * j, 8), 8), :].reshape(nck * 8, C)
        hp = hp_ref[j]
        sg[j] = jax.nn.sigmoid(g)
        gl_f = -jnp.exp(hp[0:1]) * softplus(g + hp[1:2])
        gl_b = -jnp.exp(hp[2:3]) * softplus(g + hp[3:4])
        sh = 1
        while sh < C:
            gl_f = gl_f + jnp.where(lane >= sh, pltpu.roll(gl_f, sh, 1), 0.0)
            gl_b = gl_b + jnp.where(lane < C - sh, pltpu.roll(gl_b, C - sh, 1), 0.0)
            sh *= 2
        cf[j] = gl_f
        cb[j] = gl_b
        return carry

    lax.fori_loop(0, GDN_PAIR, prologue, 0)

    ri = lax.broadcasted_iota(I32, (C, C), 0)
    ci = lax.broadcasted_iota(I32, (C, C), 1)
    low, upp = ri > ci, ri < ci
    low_i, upp_i = ri >= ci, ri <= ci
    same = lambda w: (ri // w) == (ci // w)
    bd16 = same(16)
    rings = []
    w = 32
    while w <= C:
        rings.append(jnp.logical_and(same(w), jnp.logical_not(same(w // 2))))
        w *= 2

    nbatch = math.gcd(GDN_PREP_BATCH // GDN_PAIR, nck)

    def column(ref, r):
        rows = jnp.broadcast_to(ref[pl.ds(r, 1), :], (C, C))
        return rows.T, rows

    def prepare(it, carry):
        items = [(j, it * nbatch + i) for i in range(nbatch) for j in heads]
        sls = [pl.ds(pl.multiple_of(c * C, C), C) for _, c in items]
        g_f, g_b, bt_f, bt_b, dec_f, dec_b, kb_f, kb_b, aq = [], [], [], [], [], [], [], [], []
        for (j, c), sl in zip(items, sls):
            k = ks[j, sl, :]
            gfc, gfr = column(cf.at[j], c * 8 + 2)
            gbc, gbr = column(cb.at[j], c * 8 + 3)
            g_f.append(gfc)
            g_b.append(gbc)
            bt_f.append(column(sg.at[j], c * 8)[0])
            bt_b.append(column(sg.at[j], c * 8 + 1)[0])
            dec_f.append(jnp.where(low_i, jnp.exp(jnp.where(low_i, gfc - gfr, 0.0)), 0.0))
            dec_b.append(jnp.where(upp_i, jnp.exp(jnp.where(upp_i, gbc - gbr, 0.0)), 0.0))
            kb_f.append(k * bt_f[-1])
            kb_b.append(k * bt_b[-1])
            aq.append(_dot_nt(jnp.concatenate([kb_f[-1], kb_b[-1], qs[j, sl, :]], axis=0).astype(BF16),
                              k.astype(BF16)))
        lps = [jnp.where(low, x[:C] * df, 0.0) + jnp.where(upp, x[C:2 * C] * db, 0.0)
               for x, df, db in zip(aq, dec_f, dec_b)]
        inv = _packed_tri_inverse(lps, low, upp, bd16, rings)
        for n, ((j, c), sl) in enumerate(zip(items, sls)):
            q, k, v = qs[j, sl, :], ks[j, sl, :], vs[j, sl, :]
            for rev, g_c, kb, beta, dec, msk, u_s, wq_s, qk_s, kdt_s, egl_s in (
                    (False, g_f[n], kb_f[n], bt_f[n], dec_f[n], low, uf, wqf, qkf, kdtf, eglf),
                    (True, g_b[n], kb_b[n], bt_b[n], dec_b[n], upp, ub, wqb, qkb, kdtb, eglb)):
                eg = jnp.exp(g_c)
                rhs = jnp.concatenate([v * beta, kb * eg], axis=1)
                uw = rhs + _dot(jnp.where(msk, inv[n], 0.0).astype(BF16), rhs.astype(BF16))
                glast = g_c[0:1] if rev else g_c[C - 1:C]
                u_s[j, sl, :] = uw[:, :C]
                wq_s[j, pl.ds(pl.multiple_of(c * 2 * C, 2 * C), 2 * C), :] = jnp.concatenate(
                    [uw[:, C:], q * eg], axis=0).astype(BF16)
                qk_s[j, sl, :] = (aq[n][2 * C:] * dec).astype(BF16)
                kdt_s[j, sl, :] = (k * jnp.exp(glast - g_c)).T.astype(BF16)
                egl_s[j, pl.ds(c, 1), :] = jnp.exp(glast)
        return carry

    lax.fori_loop(0, nck // nbatch, prepare, 0)

    def body(i, carry):
        chains = []
        for j in heads:
            chains.append((j, i, uf, wqf, qkf, kdtf, eglf))
            chains.append((j, nck - 1 - i, ub, wqb, qkb, kdtb, eglb))
        sls = [pl.ds(pl.multiple_of(c * C, C), C) for _, c, *_ in chains]
        ws_qs = [_dot(wq_s[j, pl.ds(pl.multiple_of(c * 2 * C, 2 * C), 2 * C), :], st.astype(BF16))
                 for (j, c, _, wq_s, *_), st in zip(chains, carry)]
        vnb = [(u_s[j, sl, :] - x[:C]).astype(BF16) for (j, _, u_s, *_), sl, x in zip(chains, sls, ws_qs)]
        new = [st * egl_s[j, pl.ds(c, 1), :] + _dot(kdt_s[j, sl, :], v)
               for (j, c, _, _, _, kdt_s, egl_s), sl, st, v in zip(chains, sls, carry, vnb)]
        for (j, _, _, _, qk_s, _, _), sl, x, v in zip(chains, sls, ws_qs, vnb):
            os_[j, sl, :] += x[C:] + _dot(qk_s[j, sl, :], v)
        return tuple(new)

    zero = jnp.zeros((hd, hd), F32)
    lax.fori_loop(0, nck, body, (zero,) * (2 * GDN_PAIR), unroll=4)

    for j in heads:
        o = os_[j]
        o = o * lax.rsqrt(jnp.mean(o * o, axis=-1, keepdims=True) + RMS_EPS) * nw_ref[...]
        z = z_ref[:, j * hd:(j + 1) * hd].astype(F32)
        o_ref[:, j * hd:(j + 1) * hd] = (o * (z * jax.nn.sigmoid(z))).astype(BF16)


def _gdn(qkvz, gates3, conv_w, head_par, norm_w, bsz, seq):
    t = bsz * seq
    hd = GDN_HEAD_DIM
    nh = GDN_HEADS
    np_ = GDN_PAIR
    wd = np_ * hd
    npairs = nh // np_
    nck = seq // GDN_BLOCK
    col = lambda off: pl.BlockSpec((seq, wd), lambda b, p: (b, off * npairs + p))
    wcol = lambda off: pl.BlockSpec((GDN_CONV, wd), lambda b, p: (0, off * npairs + p))
    per_head = lambda rows, dt: pltpu.VMEM((np_, rows, hd), dt)
    return pl.pallas_call(
        _gdn_body,
        grid=(bsz, npairs),
        in_specs=[col(0), col(1), col(2), col(3), wcol(0), wcol(1), wcol(2),
                  pl.BlockSpec((nck, 8 * np_, GDN_BLOCK), lambda b, p: (b, p, 0)),
                  pl.BlockSpec((np_, 8, LANES), lambda b, p: (p, 0, 0)),
                  pl.BlockSpec((1, hd), lambda b, p: (0, 0))],
        out_specs=pl.BlockSpec((seq, wd), lambda b, p: (b, p)),
        out_shape=jax.ShapeDtypeStruct((t, nh * hd), BF16),
        scratch_shapes=([per_head(seq, F32)] * 4
                        + [per_head(nck * 8, F32)] * 3
                        + [per_head(seq, F32)] * 2
                        + [per_head(2 * seq, BF16)] * 2
                        + [per_head(seq, BF16)] * 2
                        + [per_head(seq, BF16)] * 2
                        + [per_head(nck, F32)] * 2
                        + [pltpu.VMEM((seq + 16, hd), F32)]),
        compiler_params=_cparams("parallel", "parallel"),
        name="gdn",
    )(qkvz, qkvz, qkvz, qkvz, conv_w, conv_w, conv_w, gates3, head_par, norm_w)


def _kv_body(m_ref, g_ref, wk_ref, wv_ref, k_ref, v_ref):
    mn = _rms(m_ref[...], g_ref[...]).astype(BF16)
    k_ref[...] = _dot(mn, wk_ref[...]).astype(BF16)
    v_ref[...] = _dot(mn, wv_ref[...]).astype(BF16)


def _mem_kv(mem2d, norm_w, wk, wv, tm=512):
    r, d = mem2d.shape
    full = lambda shape: pl.BlockSpec(shape, lambda i: (0,) * len(shape))
    tile = pl.BlockSpec((tm, d), lambda i: (i, 0))
    return pl.pallas_call(
        _kv_body,
        grid=(r // tm,),
        in_specs=[tile, full((1, d)), full(wk.shape), full(wv.shape)],
        out_specs=[tile, tile],
        out_shape=[jax.ShapeDtypeStruct((r, d), BF16)] * 2,
        compiler_params=_cparams("parallel"),
        name="mem_kv",
    )(mem2d, norm_w, wk, wv)


def _mix_xattn_body(x_ref, y5_ref, yg_ref, wmix_ref, g_ref, wq_ref, k_ref, v_ref, wo_ref,
                    gm_ref, whi_ref, wlo_ref, br_ref, tri_ref,
                    o_ref, mi_ref, mf_ref, cnt_ref, wmix_b, wq_b, wo_b, carry):
    @pl.when(pl.program_id(0) == 0)
    def _():
        wmix_b[...] = wmix_ref[...].astype(BF16)
        wq_b[...] = wq_ref[...].astype(BF16)
        wo_b[...] = wo_ref[...].astype(BF16)
        carry[...] = jnp.zeros_like(carry)

    x1 = (x_ref[...] + _dot(y5_ref[...], wmix_b[:S5_WIDTH, :]) + _dot(yg_ref[...], wmix_b[S5_WIDTH:, :]))
    xn = _rms(x1, g_ref[...]).astype(BF16)
    q = (_dot(xn, wq_b[...]) * (XA_HEAD_DIM ** -0.5)).astype(BF16)
    hsl = [slice(h * XA_HEAD_DIM, (h + 1) * XA_HEAD_DIM) for h in range(XA_HEADS)]
    sc = [_dot_nt(q[:, sl], k_ref[:, sl]) for sl in hsl]
    pr = [jnp.exp(s - jnp.max(s, axis=-1, keepdims=True)) for s in sc]
    pr = [p / jnp.sum(p, axis=-1, keepdims=True) for p in pr]
    heads = [_dot(p.astype(BF16), v_ref[:, sl]).astype(BF16) for p, sl in zip(pr, hsl)]
    x2 = x1 + _dot(jnp.concatenate(heads, axis=1), wo_b[...])
    o_ref[...] = x2
    mi, mf = _route(_rms(x2, gm_ref[...]), whi_ref[...], wlo_ref[...], br_ref[...], tri_ref[...], carry)
    mi_ref[...] = mi
    mf_ref[...] = mf
    cnt_ref[...] = carry[...]


def _mix_xattn(x2d, y5, yg, w_mix, norm_w, wq, kmem, vmem, wo, norm_moe, w_route, b_route, seq, mem_len, tm=512):
    t, d = x2d.shape
    per_b = seq // tm
    tri = jnp.triu(jnp.ones((tm, tm), BF16))
    w_hi = w_route.astype(BF16)
    w_lo = (w_route - w_hi.astype(F32)).astype(BF16)
    full = lambda shape: pl.BlockSpec(shape, lambda i: (0,) * len(shape))
    tile = lambda w: pl.BlockSpec((tm, w), lambda i: (i, 0))
    return pl.pallas_call(
        _mix_xattn_body,
        grid=(t // tm,),
        in_specs=[tile(d), tile(S5_WIDTH), tile(GDN_WIDTH),
                  full(w_mix.shape), full((1, d)), full(wq.shape),
                  pl.BlockSpec((mem_len, d), lambda i: (i // per_b, 0)),
                  pl.BlockSpec((mem_len, d), lambda i: (i // per_b, 0)),
                  full(wo.shape),
                  full((1, d)), full(w_route.shape), full(w_route.shape), full(b_route.shape), full((tm, tm))],
        out_specs=[tile(d), pl.BlockSpec((8, tm), lambda i: (0, i)), tile(LANES),
                   pl.BlockSpec((ROUTE_ROWS, LANES), lambda i: (0, 0))],
        out_shape=[jax.ShapeDtypeStruct((t, d), F32),
                   jax.ShapeDtypeStruct((8, t), I32),
                   jax.ShapeDtypeStruct((t, LANES), F32),
                   jax.ShapeDtypeStruct((ROUTE_ROWS, LANES), F32)],
        scratch_shapes=[pltpu.VMEM(w_mix.shape, BF16), pltpu.VMEM(wq.shape, BF16), pltpu.VMEM(wo.shape, BF16),
                        pltpu.VMEM((ROUTE_ROWS, LANES), F32)],
        compiler_params=_cparams("arbitrary"),
        name="mix_xattn",
    )(x2d, y5, yg, w_mix, norm_w, wq, kmem, vmem, wo, norm_moe, w_hi, w_lo, b_route, tri)


ROUTE_EXPERT_LANE0 = 4


ROUTE_ROWS = 40


def _route(xn, w_hi, w_lo, bias, tri, carry):
    x_hi = xn.astype(BF16)
    x_lo = (xn - x_hi.astype(F32)).astype(BF16)
    logits = (_dot_nt(w_hi, x_hi) + _dot_nt(w_hi, x_lo) + _dot_nt(w_lo, x_hi))[:ROUTE_ROWS] + bias[:ROUTE_ROWS, 0:1]
    tm = logits.shape[1]
    row = lax.broadcasted_iota(I32, (ROUTE_ROWS, tm), 0)
    neg = jnp.float32(-jnp.inf)
    big = jnp.int32(LANES)

    def top(vals):
        m = jnp.max(vals, axis=0, keepdims=True)
        idx = jnp.min(jnp.where(vals == m, row, big), axis=0, keepdims=True)
        return m, idx

    is_g = row < MOE_GROUPS
    gl = jnp.where(is_g, logits, neg)
    gmax, gidx = top(gl)
    p_top = 1.0 / jnp.sum(jnp.where(is_g, jnp.exp(gl - gmax), 0.0), axis=0, keepdims=True)
    erow = row - ROUTE_EXPERT_LANE0
    in_grp = jnp.logical_and(jnp.logical_and(erow >= 0, erow < MOE_EXPERTS), (erow // MOE_PER_GROUP) == gidx)
    es = jnp.where(in_grp, logits, neg)
    m1, i1 = top(es)
    m2, i2 = top(jnp.where(row == i1, neg, es))
    e21 = jnp.exp(m2 - m1)
    w1 = p_top / (1.0 + e21)
    w2 = p_top * e21 / (1.0 + e21)

    a1 = (row == i1).astype(F32)
    a2 = (row == i2).astype(F32)
    both = a1 + a2
    before = _dot(both.astype(BF16), tri) - both + carry[:, 0:1]
    r1 = jnp.sum(a1 * before, axis=0, keepdims=True).astype(I32)
    r2 = jnp.sum(a2 * before, axis=0, keepdims=True).astype(I32)
    carry[...] = carry[...] + jnp.sum(both, axis=1, keepdims=True)
    row8 = lax.broadcasted_iota(I32, (8, tm), 0)
    mi = jnp.where(row8 == 0, i1 - ROUTE_EXPERT_LANE0,
                   jnp.where(row8 == 1, i2 - ROUTE_EXPERT_LANE0, jnp.where(row8 == 2, r1, jnp.where(row8 == 3, r2, 0))))
    rowl = lax.broadcasted_iota(I32, (LANES, tm), 0)
    wt = jnp.where(rowl == 0, w1, jnp.where(rowl == 1, w2, 0.0))
    mf = jnp.concatenate([wt[:, j * LANES:(j + 1) * LANES].T for j in range(tm // LANES)], axis=0)
    return mi, mf


def _dest_body(mi_ref, off_ref, da_ref, db_ref):
    tm = da_ref.shape[2]
    mi = mi_ref[...]
    n = mi.shape[1]
    row = lax.broadcasted_iota(I32, (MOE_EXPERTS, n), 0)
    off = off_ref[:, 0:1]
    d0 = jnp.sum(jnp.where(row == mi[0:1], off, 0), axis=0, keepdims=True) + mi[2:3]
    d1 = jnp.sum(jnp.where(row == mi[1:2], off, 0), axis=0, keepdims=True) + mi[3:4]
    for s in range(da_ref.shape[0]):
        da_ref[s] = d0[:, s * tm:(s + 1) * tm]
        db_ref[s] = d1[:, s * tm:(s + 1) * tm]


def _dest_rows(mi, offsets_col, tm):
    t = mi.shape[1]
    tiles_per_step = math.gcd(8, t // tm)
    out = pl.BlockSpec((tiles_per_step, 1, tm), lambda i: (i, 0, 0))
    return pl.pallas_call(
        _dest_body,
        grid=(t // (tm * tiles_per_step),),
        in_specs=[pl.BlockSpec((8, tm * tiles_per_step), lambda i: (0, i)),
                  pl.BlockSpec((MOE_EXPERTS, LANES), lambda i: (0, 0))],
        out_specs=[out, out],
        out_shape=[jax.ShapeDtypeStruct((t // tm, 1, tm), I32)] * 2,
        compiler_params=_cparams("parallel"),
        name="moe_dest",
    )(mi, offsets_col)


def _dispatch_body(zs_ref, na_ref, da_ref, db_ref, xn_ref, xs_ref, zbuf, sem, zsem):
    tm = xn_ref.shape[0]
    tz = zbuf.shape[0]
    n_blocks = xs_ref.shape[0] // tz

    @pl.when(pl.program_id(0) == 0)
    def _():
        zbuf[...] = jnp.zeros_like(zbuf)

        def fill(row0):
            return pltpu.make_async_copy(zbuf, xs_ref.at[pl.ds(pl.multiple_of(row0, tz), tz), :], zsem)

        def tail(e, c):
            @pl.when(zs_ref[e] >= 0)
            def _():
                fill(zs_ref[e]).start()
            return c

        def unused(j, c):
            fill(j * tz).start()
            return c

        def drain(j, c):
            fill(0).wait()
            return c

        lax.fori_loop(0, zs_ref.shape[0], tail, 0)
        lax.fori_loop(na_ref[0], n_blocks, unused, 0)
        lax.fori_loop(0, na_ref[1] + n_blocks - na_ref[0], drain, 0)

    def start(r, c):
        for k, d_ref in enumerate((da_ref, db_ref)):
            pltpu.make_async_copy(xn_ref.at[pl.ds(r, 1), :], xs_ref.at[pl.ds(d_ref[0, r], 1), :],
                                  sem.at[k]).start(priority=k)
        return c

    lax.fori_loop(0, tm, start, 0, unroll=8)
    for k in range(MOE_TOPK):
        pltpu.make_async_copy(xn_ref, xs_ref.at[pl.ds(0, tm), :], sem.at[k]).wait()


def _dispatch(zero_start, n_active, dest_a, dest_b, xn, n_rows, tm, te):
    t, dw = xn.shape
    smem_row = pl.BlockSpec((None, 1, tm), lambda i, zs, na: (i, 0, 0), memory_space=pltpu.SMEM)
    grid_spec = pltpu.PrefetchScalarGridSpec(
        num_scalar_prefetch=2,
        grid=(t // tm,),
        in_specs=[smem_row, smem_row, pl.BlockSpec((tm, dw), lambda i, zs, na: (i, 0))],
        out_specs=pl.BlockSpec(memory_space=pl.ANY),
        scratch_shapes=[pltpu.VMEM((te, dw), xn.dtype), pltpu.SemaphoreType.DMA((MOE_TOPK,)),
                        pltpu.SemaphoreType.DMA(())],
    )
    return pl.pallas_call(
        _dispatch_body,
        grid_spec=grid_spec,
        out_shape=jax.ShapeDtypeStruct((n_rows, dw), xn.dtype),
        compiler_params=_cparams("arbitrary"),
        name="moe_dispatch",
    )(zero_start, n_active, dest_a, dest_b, xn)


def _experts_body(te_ref, na_ref, x_ref, g_ref, wg_ref, wu_ref, wd_ref, y_ref, wg_b, wu_b, wd_b):
    i = pl.program_id(0)

    @pl.when(i < na_ref[0])
    def _():
        @pl.when(jnp.logical_or(i == 0, te_ref[i] != te_ref[jnp.maximum(i - 1, 0)]))
        def _():
            wg_b[...] = wg_ref[...].astype(BF16)
            wu_b[...] = wu_ref[...].astype(BF16)
            wd_b[...] = wd_ref[...].astype(BF16)

        x = _rms(x_ref[...], g_ref[...]).astype(BF16)
        gt = _dot(x, wg_b[...])
        up = _dot(x, wu_b[...])
        hid = (gt * jax.nn.sigmoid(gt) * up).astype(BF16)
        y_ref[...] = _dot(hid, wd_b[...])

    @pl.when(i >= na_ref[0])
    def _():
        y_ref[...] = jnp.zeros_like(y_ref)


def _experts(tile_expert, n_active, xs, norm_w, w_gate, w_up, w_down, tm):
    r, d = xs.shape
    f = w_gate.shape[2]
    row_tile = lambda i, te, na: (jnp.minimum(i, na[0] - 1), 0)
    grid_spec = pltpu.PrefetchScalarGridSpec(
        num_scalar_prefetch=2,
        grid=(r // tm,),
        in_specs=[pl.BlockSpec((tm, d), row_tile),
                  pl.BlockSpec((1, d), lambda i, te, na: (0, 0)),
                  pl.BlockSpec((None, d, f), lambda i, te, na: (te[i], 0, 0)),
                  pl.BlockSpec((None, d, f), lambda i, te, na: (te[i], 0, 0)),
                  pl.BlockSpec((None, f, d), lambda i, te, na: (te[i], 0, 0))],
        out_specs=pl.BlockSpec((tm, d), lambda i, te, na: (i, 0)),
        scratch_shapes=[pltpu.VMEM((d, f), BF16), pltpu.VMEM((d, f), BF16), pltpu.VMEM((f, d), BF16)],
    )
    return pl.pallas_call(
        _experts_body,
        grid_spec=grid_spec,
        out_shape=jax.ShapeDtypeStruct((r, d), xs.dtype),
        compiler_params=_cparams("arbitrary"),
        name="moe_experts",
    )(tile_expert, n_active, xs, norm_w, w_gate, w_up, w_down)


def _combine_body(da_ref, db_ref, na_ref, nb_ref, x_ref, mf_ref, g_ref, ys_ref, o_ref, buf, sem):
    tm = x_ref.shape[0]
    i = pl.program_id(0)
    slot = lax.rem(i, 2)

    def gather(d_refs, s):
        def start(r, c):
            for k, d_ref in enumerate(d_refs):
                pltpu.make_async_copy(ys_ref.at[pl.ds(d_ref[0, r], 1), :],
                                      buf.at[s, k, pl.ds(r, 1), :], sem.at[s, k]).start(priority=k)
            return c

        lax.fori_loop(0, tm, start, 0, unroll=8)

    @pl.when(i == 0)
    def _():
        gather((da_ref, db_ref), 0)

    @pl.when(i + 1 < pl.num_programs(0))
    def _():
        gather((na_ref, nb_ref), 1 - slot)

    for k in range(MOE_TOPK):
        pltpu.make_async_copy(ys_ref.at[pl.ds(0, tm), :], buf.at[slot, k], sem.at[slot, k]).wait()
    mf = mf_ref[...]
    y = x_ref[...] + mf[:, 0:1] * buf[slot, 0] + mf[:, 1:2] * buf[slot, 1]
    o_ref[...] = _rms(y, g_ref[...])


def _combine(dest_a, dest_b, x2d, mf, norm_w, ys, tm):
    t, d = x2d.shape
    last = t // tm - 1
    cur = pl.BlockSpec((None, 1, tm), lambda i: (i, 0, 0), memory_space=pltpu.SMEM)
    nxt = pl.BlockSpec((None, 1, tm), lambda i: (jnp.minimum(i + 1, last), 0, 0), memory_space=pltpu.SMEM)
    return pl.pallas_call(
        _combine_body,
        grid=(t // tm,),
        in_specs=[cur, cur, nxt, nxt,
                  pl.BlockSpec((tm, d), lambda i: (i, 0)),
                  pl.BlockSpec((tm, LANES), lambda i: (i, 0)),
                  pl.BlockSpec((1, d), lambda i: (0, 0)),
                  pl.BlockSpec(memory_space=pl.ANY)],
        out_specs=pl.BlockSpec((tm, d), lambda i: (i, 0)),
        out_shape=jax.ShapeDtypeStruct((t, d), F32),
        scratch_shapes=[pltpu.VMEM((2, MOE_TOPK, tm, ys.shape[1]), ys.dtype), pltpu.SemaphoreType.DMA((2, MOE_TOPK))],
        compiler_params=_cparams("arbitrary"),
        name="moe_combine",
    )(dest_a, dest_b, dest_a, dest_b, x2d, mf, norm_w, ys)


MOE_ROW_TILE = 512
MOE_TOKEN_TILE = 1024
MOE_ZERO_BLOCK = 256


def _moe(x2d, mi, mf, cnt, norm_w, w_gate, w_up, w_down, norm_final):
    t, d = x2d.shape
    tm = MOE_ROW_TILE
    tok = MOE_TOKEN_TILE
    counts = cnt[ROUTE_EXPERT_LANE0:ROUTE_EXPERT_LANE0 + MOE_EXPERTS, 0].astype(I32)
    padded = ((counts + tm - 1) // tm) * tm
    ends = jnp.cumsum(padded)
    offsets = ends - padded
    n_tiles = (MOE_TOPK * t + MOE_EXPERTS * (tm - 1)) // tm
    tile_start = jnp.arange(n_tiles, dtype=I32) * tm
    tile_expert = jnp.minimum(jnp.sum((ends[None, :] <= tile_start[:, None]).astype(I32), axis=1), MOE_EXPERTS - 1)
    n_info = jnp.stack([ends[-1] // tm, jnp.sum((counts > 0).astype(I32))]).astype(I32)
    tz = MOE_ZERO_BLOCK
    pad = padded - counts
    zero_start = jnp.concatenate([jnp.where(pad > j * tz, ends - (j + 1) * tz, -1) for j in range(tm // tz)])
    zero_start = jnp.where(jnp.tile(counts, tm // tz) > 0, zero_start, -1).astype(I32)
    z_info = jnp.stack([ends[-1] // tz, jnp.sum((zero_start >= 0).astype(I32))]).astype(I32)
    offsets_col = jnp.broadcast_to(offsets[:, None], (MOE_EXPERTS, LANES))
    dest_a, dest_b = _dest_rows(mi, offsets_col, tok)
    xs = _dispatch(zero_start, z_info, dest_a, dest_b, x2d, n_tiles * tm, tok, tz)
    ys = _experts(tile_expert, n_info, xs, norm_w, w_gate, w_up, w_down, tm)
    return _combine(dest_a, dest_b, x2d, mf, norm_final, ys, tok)


def kernel(x, mem, norm_mix, w_in, w_out,
           s5_lam_re_f, s5_lam_im_f, s5_log_step_f, s5_b_re_f, s5_b_im_f, s5_c_re_f, s5_c_im_f,
           s5_lam_re_b, s5_lam_im_b, s5_log_step_b, s5_b_re_b, s5_b_im_b, s5_c_re_b, s5_c_im_b,
           s5_d, s5_w_glu, s5_b_glu, s5_norm,
           gdn_conv, gdn_a_log_f, gdn_dt_bias_f, gdn_a_log_b, gdn_dt_bias_b, gdn_norm,
           norm_xattn, norm_mem, xa_wq, xa_wk, xa_wv, xa_wo,
           norm_moe, router_group_w, router_group_b, router_expert_w, router_expert_b,
           moe_w_gate, moe_w_up, moe_w_down, norm_final):
    bsz, seq, d = x.shape
    t = bsz * seq
    l = 0
    x2d = x.reshape(t, d)
    wi = w_in[l]
    wg = wi[:, S5_WIDTH + 4 * GDN_WIDTH:].reshape(d, 4, GDN_HEADS)
    wg = jnp.pad(jnp.swapaxes(wg, 1, 2), ((0, 0), (0, 0), (0, 4))).reshape(d, GDN_HEADS * 8)
    wgt = wg.T
    u3, qkvz, gates = _in_proj(x2d, norm_mix[l][None], wi, wgt)
    s5p = dict(lam_re_f=s5_lam_re_f[l], lam_im_f=s5_lam_im_f[l], log_step_f=s5_log_step_f[l],
               b_re_f=s5_b_re_f[l], b_im_f=s5_b_im_f[l], c_re_f=s5_c_re_f[l], c_im_f=s5_c_im_f[l],
               lam_re_b=s5_lam_re_b[l], lam_im_b=s5_lam_im_b[l], log_step_b=s5_log_step_b[l],
               b_re_b=s5_b_re_b[l], b_im_b=s5_b_im_b[l], c_re_b=s5_c_re_b[l], c_im_b=s5_c_im_b[l], d=s5_d[l])
    g3 = _s5_scan(u3, s5p, seq // S5_CHUNK)
    y_s5 = _s5_post(g3, s5_w_glu[l].T.astype(BF16), s5_b_glu[l][:, None], s5_norm[l][:, None])

    head_par = jnp.stack([gdn_a_log_f[l], gdn_dt_bias_f[l], gdn_a_log_b[l], gdn_dt_bias_b[l]], axis=1)
    head_par = jnp.broadcast_to(jnp.pad(head_par, ((0, 0), (0, 4)))[:, :, None], (GDN_HEADS, 8, LANES))
    y_gdn = _gdn(qkvz, gates, gdn_conv[l], head_par, gdn_norm[l][None], bsz, seq)

    mem_len = mem.shape[1]
    kmem, vmem = _mem_kv(mem.reshape(bsz * mem_len, d), norm_mem[l][None],
                         xa_wk[l].astype(BF16), xa_wv[l].astype(BF16))
    n_pad = LANES - MOE_GROUPS - MOE_EXPERTS
    w_route = jnp.pad(jnp.concatenate([router_group_w[l], router_expert_w[l]], axis=1), ((0, 0), (0, n_pad))).T
    b_route = jnp.pad(jnp.concatenate([router_group_b[l], router_expert_b[l]]), (0, n_pad))
    b_route = jnp.broadcast_to(b_route[:, None], (LANES, LANES))
    x2, mi, mf, cnt = _mix_xattn(x2d, y_s5, y_gdn, w_out[l], norm_xattn[l][None], xa_wq[l], kmem, vmem, xa_wo[l],
                                 norm_moe[l][None], w_route, b_route, seq, mem_len)
    y = _moe(x2, mi, mf, cnt, norm_moe[l][None], moe_w_gate[l], moe_w_up[l], moe_w_down[l], norm_final[None])
    return y.reshape(bsz, seq, d)
```

```python
import functools
import math

import jax
import jax.numpy as jnp
from jax import lax
from jax.experimental import pallas as pl
from jax.experimental.pallas import tpu as pltpu

F32 = jnp.float32
BF16 = jnp.bfloat16
I32 = jnp.int32

D_MODEL = 1024
S5_WIDTH = 512
S5_GROUP = 16
S5_GROUPS = 32
S5_STATE = 64
S5_CHUNK = 128
GDN_HEADS = 4
GDN_HEAD_DIM = 128
GDN_WIDTH = 512
GDN_CONV = 5
GDN_CHUNK = 64
XA_HEADS = 4
XA_HEAD_DIM = 256
MOE_GROUPS = 4
MOE_PER_GROUP = 8
MOE_EXPERTS = 32
MOE_TOPK = 2
D_EXPERT = 256
RMS_EPS = 1e-6
L2_EPS = 1e-6
LANES = 128
VMEM_LIMIT = 56 * 1024 * 1024


def _cparams(*sem):
    return pltpu.CompilerParams(dimension_semantics=tuple(sem), vmem_limit_bytes=VMEM_LIMIT)


def _rms(x, gain):
    return x * lax.rsqrt(jnp.mean(x * x, axis=-1, keepdims=True) + RMS_EPS) * gain


def _dot(a, b):
    return jnp.dot(a, b, preferred_element_type=F32)


def _dot_nt(a, b):
    return lax.dot_general(a, b, (((1,), (1,)), ((), ())), preferred_element_type=F32)


def _dot_tn(a, b):
    return lax.dot_general(a, b, (((0,), (0,)), ((), ())), preferred_element_type=F32)


def _in_proj_body(x_ref, g_ref, w_ref, wgt_ref, u_ref, qkvz_ref, gates_ref, wut_b, wqkvz_b, wgt_b):
    @pl.when(pl.program_id(0) == 0)
    def _():
        wut_b[...] = w_ref[:, :S5_WIDTH].T.astype(BF16)
        wqkvz_b[...] = w_ref[:, S5_WIDTH:S5_WIDTH + 4 * GDN_WIDTH].astype(BF16)
        wgt_b[...] = wgt_ref[...].astype(BF16)

    h = _rms(x_ref[...], g_ref[...]).astype(BF16)
    ut = _dot_nt(wut_b[...], h)
    gt = _dot_nt(wgt_b[...], h)
    for j in range(u_ref.shape[0]):
        u_ref[j] = ut[:, j * S5_CHUNK:(j + 1) * S5_CHUNK]
        gates_ref[j] = gt[:, j * S5_CHUNK:(j + 1) * S5_CHUNK]
    qkvz_ref[...] = _dot(h, wqkvz_b[...]).astype(BF16)


def _in_proj(x2d, norm_w, w_in, wgt, tm=1024):
    t = x2d.shape[0]
    tm = math.gcd(tm, t)
    nck = tm // S5_CHUNK
    nqkvz = 4 * GDN_WIDTH
    full = lambda shape: pl.BlockSpec(shape, lambda i: (0,) * len(shape), pipeline_mode=pl.Buffered(1))
    return pl.pallas_call(
        _in_proj_body,
        grid=(t // tm,),
        in_specs=[pl.BlockSpec((tm, D_MODEL), lambda i: (i, 0)),
                  full((1, D_MODEL)), full(w_in.shape), full(wgt.shape)],
        out_specs=[pl.BlockSpec((nck, S5_WIDTH, S5_CHUNK), lambda i: (i, 0, 0)),
                   pl.BlockSpec((tm, nqkvz), lambda i: (i, 0)),
                   pl.BlockSpec((nck, wgt.shape[0], S5_CHUNK), lambda i: (i, 0, 0))],
        out_shape=[jax.ShapeDtypeStruct((t // S5_CHUNK, S5_WIDTH, S5_CHUNK), F32),
                   jax.ShapeDtypeStruct((t, nqkvz), BF16),
                   jax.ShapeDtypeStruct((t // S5_CHUNK, wgt.shape[0], S5_CHUNK), F32)],
        scratch_shapes=[pltpu.VMEM((S5_WIDTH, D_MODEL), BF16), pltpu.VMEM((D_MODEL, nqkvz), BF16),
                        pltpu.VMEM(wgt.shape, BF16)],
        compiler_params=_cparams("arbitrary"),
        name="in_proj",
    )(x2d, norm_w, w_in, wgt)


def _cmul(ar, ai, br, bi):
    return ar * br - ai * bi, ar * bi + ai * br


def _cpow_int(lr, li, expo, nbits):
    res_r = jnp.ones(jnp.broadcast_shapes(lr.shape, expo.shape), F32)
    res_i = jnp.zeros_like(res_r)
    for b in range(nbits):
        bit = ((expo >> b) & 1) == 1
        nr, ni = _cmul(res_r, res_i, lr, li)
        res_r = jnp.where(bit, nr, res_r)
        res_i = jnp.where(bit, ni, res_i)
        if b + 1 < nbits:
            lr, li = _cmul(lr, li, lr, li)
    return res_r, res_i


def _lam_bar(re, im, step):
    er = jnp.exp(step * re)
    return er * jnp.cos(step * im), er * jnp.sin(step * im)


def _zoh_coef(re, im, lr, li):
    den = re * re + im * im
    return ((lr - 1.0) * re + li * im) / den, (li * re - (lr - 1.0) * im) / den


def _s5_body(u_ref, lrow_ref, lcol_ref, l256_ref, b_ref, bt_ref, c_ref, ct_ref, d_ref,
             o_ref, vf_ref, vb_ref, m_ref, win_ref, wout_ref, sf_ref, sb_ref, hf_ref, hb_ref, uc_ref, y_ref, *, nchunk):
    L = S5_CHUNK
    P = S5_STATE
    n_rows = u_ref.shape[0]
    nb = n_rows // nchunk
    lane_i = lax.broadcasted_iota(I32, (1, L), 1)

    lcol = lcol_ref[...]
    lbc_r, lbc_i = _lam_bar(lcol[:, 0:2], lcol[:, 2:4], lcol[:, 4:6])
    kc_r, kc_i = _zoh_coef(lcol[:, 0:2], lcol[:, 2:4], lbc_r, lbc_i)
    lrow = lrow_ref[...]
    lbr_r, lbr_i = _lam_bar(lrow[0:2], lrow[2:4], lrow[4:6])
    kr_r, kr_i = _zoh_coef(lrow[0:2], lrow[2:4], lbr_r, lbr_i)
    lf_r, lf_i, lb_r, lb_i = lbc_r[:, 0:1], lbc_i[:, 0:1], lbc_r[:, 1:2], lbc_i[:, 1:2]
    kfr_c, kfi_c, kbr_c, kbi_c = kc_r[:, 0:1], kc_i[:, 0:1], kc_r[:, 1:2], kc_i[:, 1:2]
    kfr_r, kfi_r, kbr_r, kbi_r = kr_r[0:1], kr_i[0:1], kr_r[1:2], kr_i[1:2]

    pwf_r, pwf_i = _cpow_int(lf_r, lf_i, lane_i, 7)
    rvf_r, rvf_i = _cpow_int(lf_r, lf_i, (L - 1) - lane_i, 7)
    pwb_r, pwb_i = _cpow_int(lb_r, lb_i, lane_i, 7)
    rvb_r, rvb_i = _cpow_int(lb_r, lb_i, L - lane_i, 8)
    nxf_r, nxf_i = _cmul(pwf_r, pwf_i, lf_r, lf_i)

    bf_r = kfr_c * b_ref[0] - kfi_c * b_ref[1]
    bf_i = kfr_c * b_ref[1] + kfi_c * b_ref[0]
    bb_r = kbr_c * b_ref[2] - kbi_c * b_ref[3]
    bb_i = kbr_c * b_ref[3] + kbi_c * b_ref[2]
    btf_r = kfr_r * bt_ref[0] - kfi_r * bt_ref[1]
    btf_i = kfr_r * bt_ref[1] + kfi_r * bt_ref[0]
    btb_r = kbr_r * bt_ref[2] - kbi_r * bt_ref[3]
    btb_i = kbr_r * bt_ref[3] + kbi_r * bt_ref[2]

    def taps(c_r, c_i, bt_r, bt_i, pw_r, pw_i):
        cb_r = (bt_r[:, None, :] * c_r[None, :, :] - bt_i[:, None, :] * c_i[None, :, :]).reshape(256, P)
        cb_i = (bt_r[:, None, :] * c_i[None, :, :] + bt_i[:, None, :] * c_r[None, :, :]).reshape(256, P)
        k = (jnp.dot(cb_r, pw_r, preferred_element_type=F32, precision=lax.Precision.HIGHEST)
             - jnp.dot(cb_i, pw_i, preferred_element_type=F32, precision=lax.Precision.HIGHEST))
        return k, jnp.sum(cb_r, axis=1, keepdims=True)

    kf, _ = taps(c_ref[0], c_ref[1], btf_r, btf_i, pwf_r, pwf_i)
    kb, kb0 = taps(c_ref[2], c_ref[3], btb_r, btb_i, rvb_r, rvb_i)
    is0 = lane_i == 0
    vf_ref[...] = kf + jnp.where(is0, kb0, 0.0)
    vb_ref[...] = jnp.where(is0, 0.0, kb)

    row_i = lax.broadcasted_iota(I32, (L, L), 0)
    col_i = lax.broadcasted_iota(I32, (L, L), 1)
    fwd_lane = col_i + row_i < L

    def build_ci(ci, carry):
        for co in range(S5_GROUP):
            r = ci * S5_GROUP + co
            taps_rows = jnp.where(fwd_lane, jnp.broadcast_to(vf_ref[pl.ds(r, 1), :], (L, L)),
                                  jnp.broadcast_to(vb_ref[pl.ds(r, 1), :], (L, L)))
            m_ref[pl.ds(pl.multiple_of(ci * L, L), L), co * L:(co + 1) * L] = pltpu.roll(
                taps_rows, 0, 1, stride=1, stride_axis=0).astype(BF16)
        return carry

    lax.fori_loop(0, S5_GROUP, build_ci, 0, unroll=4)

    for ci in range(S5_GROUP):
        sl = slice(ci * L, (ci + 1) * L)
        br, bi = bf_r[:, ci:ci + 1], bf_i[:, ci:ci + 1]
        win_ref[0 * P:1 * P, sl] = (rvf_r * br - rvf_i * bi).astype(BF16)
        win_ref[1 * P:2 * P, sl] = (rvf_r * bi + rvf_i * br).astype(BF16)
        br, bi = bb_r[:, ci:ci + 1], bb_i[:, ci:ci + 1]
        win_ref[2 * P:3 * P, sl] = (pwb_r * br - pwb_i * bi).astype(BF16)
        win_ref[3 * P:4 * P, sl] = (pwb_r * bi + pwb_i * br).astype(BF16)
    for co in range(S5_GROUP):
        sl = slice(co * L, (co + 1) * L)
        cr, ci_ = ct_ref[0][:, co:co + 1], ct_ref[1][:, co:co + 1]
        wout_ref[0 * P:1 * P, sl] = (cr * nxf_r - ci_ * nxf_i).astype(BF16)
        wout_ref[1 * P:2 * P, sl] = (-(cr * nxf_i + ci_ * nxf_r)).astype(BF16)
        cr, ci_ = ct_ref[2][:, co:co + 1], ct_ref[3][:, co:co + 1]
        wout_ref[2 * P:3 * P, sl] = (cr * rvb_r - ci_ * rvb_i).astype(BF16)
        wout_ref[3 * P:4 * P, sl] = (-(cr * rvb_i + ci_ * rvb_r)).astype(BF16)

    for ci in range(S5_GROUP):
        uc_ref[ci] = u_ref[:, ci, :]
    ucat = jnp.concatenate([uc_ref[ci].astype(BF16) for ci in range(S5_GROUP)], axis=1)

    summ = _dot_nt(ucat, win_ref[...])
    sf_ref[...] = summ[:, :2 * P]
    sb_ref[...] = summ[:, 2 * P:]
    nblk = 2 * L
    for j in range(S5_GROUP * L // nblk):
        y_ref[:, j * nblk:(j + 1) * nblk] = _dot(ucat, m_ref[:, j * nblk:(j + 1) * nblk])
    l256 = l256_ref[...]
    a_mul, a_im = _lam_bar(l256[0:1], l256[1:2], l256[2:3])
    for _ in range(7):
        a_mul, a_im = _cmul(a_mul, a_im, a_mul, a_im)
    lane256 = lax.broadcasted_iota(I32, (1, 4 * P), 1)
    b_mul = jnp.where((lane256 // P) % 2 == 0, -a_im, a_im)

    hf = jnp.zeros((nb, 2 * P), F32)
    hb = jnp.zeros((nb, 2 * P), F32)
    for c in range(nchunk):
        cr = nchunk - 1 - c
        rows_f = pl.ds(c, nb, stride=nchunk)
        rows_b = pl.ds(cr, nb, stride=nchunk)
        hf_ref[rows_f, :] = hf
        hb_ref[rows_b, :] = hb
        hf = a_mul[:, :2 * P] * hf + b_mul[:, :2 * P] * pltpu.roll(hf, P, 1) + sf_ref[rows_f, :]
        hb = a_mul[:, 2 * P:] * hb + b_mul[:, 2 * P:] * pltpu.roll(hb, P, 1) + sb_ref[rows_b, :]
    hprev = jnp.concatenate([hf_ref[...], hb_ref[...]], axis=1).astype(BF16)

    for j in range(S5_GROUP * L // nblk):
        y = y_ref[:, j * nblk:(j + 1) * nblk] + _dot(hprev, wout_ref[:, j * nblk:(j + 1) * nblk])
        for q in range(nblk // L):
            co = j * (nblk // L) + q
            yc = y[:, q * L:(q + 1) * L] + d_ref[co:co + 1, :] * uc_ref[co]
            o_ref[:, co, :] = 0.5 * yc * (1.0 + lax.erf(yc * (2.0 ** -0.5)))


def _s5_scan(u3, p, nchunk):
    n = u3.shape[0]
    g, grp, st, L = S5_GROUPS, S5_GROUP, S5_STATE, S5_CHUNK
    step_f = jnp.exp(p["log_step_f"])[:, None] * jnp.ones((1, st), F32)
    step_b = jnp.exp(p["log_step_b"])[:, None] * jnp.ones((1, st), F32)
    zeros = jnp.zeros((g, st), F32)
    lrow = jnp.stack([p["lam_re_f"], p["lam_re_b"], p["lam_im_f"], p["lam_im_b"], step_f, step_b, zeros, zeros], axis=1)
    lcol = jnp.swapaxes(lrow, 1, 2)
    cat4 = lambda f, b: jnp.concatenate([f, f, b, b], axis=1)
    z256 = jnp.zeros((g, 4 * st), F32)
    l256 = jnp.stack([cat4(p["lam_re_f"], p["lam_re_b"]), cat4(p["lam_im_f"], p["lam_im_b"]),
                      cat4(step_f, step_b)] + [z256] * 5, axis=1)
    b4 = jnp.stack([p["b_re_f"], p["b_im_f"], p["b_re_b"], p["b_im_b"]], axis=1)
    bt4 = jnp.swapaxes(b4, 2, 3)
    c4 = jnp.stack([p["c_re_f"], p["c_im_f"], p["c_re_b"], p["c_im_b"]], axis=1)
    ct4 = jnp.swapaxes(c4, 2, 3)
    dbc = jnp.broadcast_to(p["d"].reshape(g, grp, 1), (g, grp, L))
    per_g = lambda *shape: pl.BlockSpec((None,) + shape, lambda i: (i,) + (0,) * len(shape))
    return pl.pallas_call(
        functools.partial(_s5_body, nchunk=nchunk),
        grid=(g,),
        in_specs=[pl.BlockSpec((n, grp, L), lambda i: (0, i, 0)),
                  per_g(8, st), per_g(st, 8), per_g(8, 4 * st), per_g(4, st, grp), per_g(4, grp, st),
                  per_g(4, grp, st), per_g(4, st, grp), per_g(grp, L)],
        out_specs=pl.BlockSpec((n, grp, L), lambda i: (0, i, 0)),
        out_shape=jax.ShapeDtypeStruct(u3.shape, F32),
        scratch_shapes=[pltpu.VMEM((grp * grp, L), F32), pltpu.VMEM((grp * grp, L), F32),
                        pltpu.VMEM((grp * L, grp * L), BF16),
                        pltpu.VMEM((4 * st, grp * L), BF16), pltpu.VMEM((4 * st, grp * L), BF16),
                        pltpu.VMEM((n, 2 * st), F32), pltpu.VMEM((n, 2 * st), F32),
                        pltpu.VMEM((n, 2 * st), F32), pltpu.VMEM((n, 2 * st), F32),
                        pltpu.VMEM((grp, n, L), F32), pltpu.VMEM((n, grp * L), F32)],
        compiler_params=_cparams("parallel"),
        name="s5_scan",
    )(u3, lrow, lcol, l256, b4, bt4, c4, ct4, dbc)


def _s5_post_body(g_ref, wt_ref, b_ref, nw_ref, o_ref):
    for j in range(g_ref.shape[0]):
        g = g_ref[j]
        z = _dot(wt_ref[...], g.astype(BF16)) + b_ref[...]
        y = g * jax.nn.sigmoid(z)
        y = y * lax.rsqrt(jnp.mean(y * y, axis=0, keepdims=True) + RMS_EPS) * nw_ref[...]
        o_ref[j * S5_CHUNK:(j + 1) * S5_CHUNK, :] = y.T.astype(BF16)


def _s5_post(g3, w_glu_t, b_glu_col, norm_col):
    n = g3.shape[0]
    nck = math.gcd(16, n)
    full = lambda shape: pl.BlockSpec(shape, lambda i: (0,) * len(shape))
    return pl.pallas_call(
        _s5_post_body,
        grid=(n // nck,),
        in_specs=[pl.BlockSpec((nck, S5_WIDTH, S5_CHUNK), lambda i: (i, 0, 0)),
                  full(w_glu_t.shape), full(b_glu_col.shape), full(norm_col.shape)],
        out_specs=pl.BlockSpec((nck * S5_CHUNK, S5_WIDTH), lambda i: (i, 0)),
        out_shape=jax.ShapeDtypeStruct((n * S5_CHUNK, S5_WIDTH), BF16),
        compiler_params=_cparams("parallel"),
        name="s5_post",
    )(g3, w_glu_t, b_glu_col, norm_col)


GDN_BLOCK = 128
GDN_PAIR = 2
GDN_PREP_BATCH = 8


def _packed_tri_inverse(lps, low, upp, bd16, rings):
    def pk(xs, ys):
        outs = []
        for a, b in zip(xs, ys):
            lhs = jnp.concatenate([jnp.where(low, a, 0.0), jnp.where(upp, a, 0.0)], axis=1).astype(BF16)
            rhs = jnp.concatenate([jnp.where(low, b, 0.0), jnp.where(upp, b, 0.0)], axis=0).astype(BF16)
            outs.append(_dot(lhs, rhs))
        return outs

    d = [jnp.where(bd16, lp, 0.0) for lp in lps]
    d2 = pk(d, d)
    d4 = pk(d2, d2)
    d8 = pk(d4, d4)
    a = [y - x - p for x, y, p in zip(d, d2, pk(d, d2))]
    a = [x + y + p for x, y, p in zip(a, d4, pk(a, d4))]
    a = [x + y + p for x, y, p in zip(a, d8, pk(a, d8))]
    for ring in rings:
        n = [jnp.where(ring, lp, 0.0) for lp in lps]
        t = [x + p for x, p in zip(n, pk(a, n))]
        a = [x - y - p for x, y, p in zip(a, t, pk(t, a))]
    return a


def _gdn_body(q_ref, k_ref, v_ref, z_ref, wq_ref, wk_ref, wv_ref, g_ref, hp_ref, nw_ref, o_ref,
              qs, ks, vs, os_, sg, cf, cb, uf, ub, wqf, wqb, qkf, qkb, kdtf, kdtb, eglf, eglb, xpad):
    seq = q_ref.shape[0]
    C = GDN_BLOCK
    hd = GDN_HEAD_DIM
    nck = seq // C
    heads = range(GDN_PAIR)

    pad = 8
    half = (GDN_CONV - 1) // 2
    xpad[0:pad, :] = jnp.zeros((pad, LANES), F32)
    xpad[pad + seq:2 * pad + seq, :] = jnp.zeros((pad, LANES), F32)

    def conv_silu(x_ref, w_ref, j):
        cols = pl.ds(pl.multiple_of(j * hd, hd), hd)
        xpad[pad:pad + seq, :] = x_ref[:, cols].astype(F32)
        w = w_ref[:, cols]
        acc = xpad[pad - half:pad - half + seq, :] * w[0:1]
        for tap in range(1, GDN_CONV):
            acc = acc + xpad[pad - half + tap:pad - half + tap + seq, :] * w[tap:tap + 1]
        return acc * jax.nn.sigmoid(acc)

    def l2n(x):
        return x * lax.rsqrt(jnp.sum(x * x, axis=-1, keepdims=True) + L2_EPS)

    def softplus(x):
        return jnp.maximum(x, 0.0) + jnp.log1p(jnp.exp(-jnp.abs(x)))

    lane = lax.broadcasted_iota(I32, (1, C), 1)

    def prologue(j, carry):
        qs[j] = l2n(conv_silu(q_ref, wq_ref, j)) * (hd ** -0.5)
        ks[j] = l2n(conv_silu(k_ref, wk_ref, j))
        vs[j] = conv_silu(v_ref, wv_ref, j)
        os_[j] = jnp.zeros((seq, hd), F32)
        g = g_ref[:, pl.ds(pl.multiple_of(8 * j, 8), 8), :].reshape(nck * 8, C)
        hp = hp_ref[j]
        sg[j] = jax.nn.sigmoid(g)
        gl_f = -jnp.exp(hp[0:1]) * softplus(g + hp[1:2])
        gl_b = -jnp.exp(hp[2:3]) * softplus(g + hp[3:4])
        sh = 1
        while sh < C:
            gl_f = gl_f + jnp.where(lane >= sh, pltpu.roll(gl_f, sh, 1), 0.0)
            gl_b = gl_b + jnp.where(lane < C - sh, pltpu.roll(gl_b, C - sh, 1), 0.0)
            sh *= 2
        cf[j] = gl_f
        cb[j] = gl_b
        return carry

    lax.fori_loop(0, GDN_PAIR, prologue, 0)

    ri = lax.broadcasted_iota(I32, (C, C), 0)
    ci = lax.broadcasted_iota(I32, (C, C), 1)
    low, upp = ri > ci, ri < ci
    low_i, upp_i = ri >= ci, ri <= ci
    same = lambda w: (ri // w) == (ci // w)
    bd16 = same(16)
    rings = []
    w = 32
    while w <= C:
        rings.append(jnp.logical_and(same(w), jnp.logical_not(same(w // 2))))
        w *= 2

    nbatch = math.gcd(GDN_PREP_BATCH // GDN_PAIR, nck)

    def column(ref, r):
        rows = jnp.broadcast_to(ref[pl.ds(r, 1), :], (C, C))
        return rows.T, rows

    def prepare(it, carry):
        items = [(j, it * nbatch + i) for i in range(nbatch) for j in heads]
        sls = [pl.ds(pl.multiple_of(c * C, C), C) for _, c in items]
        g_f, g_b, bt_f, bt_b, dec_f, dec_b, kb_f, kb_b, aq = [], [], [], [], [], [], [], [], []
        for (j, c), sl in zip(items, sls):
            k = ks[j, sl, :]
            gfc, gfr = column(cf.at[j], c * 8 + 2)
            gbc, gbr = column(cb.at[j], c * 8 + 3)
            g_f.append(gfc)
            g_b.append(gbc)
            bt_f.append(column(sg.at[j], c * 8)[0])
            bt_b.append(column(sg.at[j], c * 8 + 1)[0])
            dec_f.append(jnp.where(low_i, jnp.exp(jnp.where(low_i, gfc - gfr, 0.0)), 0.0))
            dec_b.append(jnp.where(upp_i, jnp.exp(jnp.where(upp_i, gbc - gbr, 0.0)), 0.0))
            kb_f.append(k * bt_f[-1])
            kb_b.append(k * bt_b[-1])
            aq.append(_dot_nt(jnp.concatenate([kb_f[-1], kb_b[-1], qs[j, sl, :]], axis=0).astype(BF16),
                              k.astype(BF16)))
        lps = [jnp.where(low, x[:C] * df, 0.0) + jnp.where(upp, x[C:2 * C] * db, 0.0)
               for x, df, db in zip(aq, dec_f, dec_b)]
        inv = _packed_tri_inverse(lps, low, upp, bd16, rings)
        for n, ((j, c), sl) in enumerate(zip(items, sls)):
            q, k, v = qs[j, sl, :], ks[j, sl, :], vs[j, sl, :]
            for rev, g_c, kb, beta, dec, msk, u_s, wq_s, qk_s, kdt_s, egl_s in (
                    (False, g_f[n], kb_f[n], bt_f[n], dec_f[n], low, uf, wqf, qkf, kdtf, eglf),
                    (True, g_b[n], kb_b[n], bt_b[n], dec_b[n], upp, ub, wqb, qkb, kdtb, eglb)):
                eg = jnp.exp(g_c)
                rhs = jnp.concatenate([v * beta, kb * eg], axis=1)
                uw = rhs + _dot(jnp.where(msk, inv[n], 0.0).astype(BF16), rhs.astype(BF16))
                glast = g_c[0:1] if rev else g_c[C - 1:C]
                u_s[j, sl, :] = uw[:, :C]
                wq_s[j, pl.ds(pl.multiple_of(c * 2 * C, 2 * C), 2 * C), :] = jnp.concatenate(
                    [uw[:, C:], q * eg], axis=0).astype(BF16)
                qk_s[j, sl, :] = (aq[n][2 * C:] * dec).astype(BF16)
                kdt_s[j, sl, :] = (k * jnp.exp(glast - g_c)).T.astype(BF16)
                egl_s[j, pl.ds(c, 1), :] = jnp.exp(glast)
        return carry

    lax.fori_loop(0, nck // nbatch, prepare, 0)

    def body(i, carry):
        chains = []
        for j in heads:
            chains.append((j, i, uf, wqf, qkf, kdtf, eglf))
            chains.append((j, nck - 1 - i, ub, wqb, qkb, kdtb, eglb))
        sls = [pl.ds(pl.multiple_of(c * C, C), C) for _, c, *_ in chains]
        ws_qs = [_dot(wq_s[j, pl.ds(pl.multiple_of(c * 2 * C, 2 * C), 2 * C), :], st.astype(BF16))
                 for (j, c, _, wq_s, *_), st in zip(chains, carry)]
        vnb = [(u_s[j, sl, :] - x[:C]).astype(BF16) for (j, _, u_s, *_), sl, x in zip(chains, sls, ws_qs)]
        new = [st * egl_s[j, pl.ds(c, 1), :] + _dot(kdt_s[j, sl, :], v)
               for (j, c, _, _, _, kdt_s, egl_s), sl, st, v in zip(chains, sls, carry, vnb)]
        for (j, _, _, _, qk_s, _, _), sl, x, v in zip(chains, sls, ws_qs, vnb):
            os_[j, sl, :] += x[C:] + _dot(qk_s[j, sl, :], v)
        return tuple(new)

    zero = jnp.zeros((hd, hd), F32)
    lax.fori_loop(0, nck, body, (zero,) * (2 * GDN_PAIR), unroll=4)

    for j in heads:
        o = os_[j]
        o = o * lax.rsqrt(jnp.mean(o * o, axis=-1, keepdims=True) + RMS_EPS) * nw_ref[...]
        z = z_ref[:, j * hd:(j + 1) * hd].astype(F32)
        o_ref[:, j * hd:(j + 1) * hd] = (o * (z * jax.nn.sigmoid(z))).astype(BF16)


def _gdn(qkvz, gates3, conv_w, head_par, norm_w, bsz, seq):
    t = bsz * seq
    hd = GDN_HEAD_DIM
    nh = GDN_HEADS
    np_ = GDN_PAIR
    wd = np_ * hd
    npairs = nh // np_
    nck = seq // GDN_BLOCK
    col = lambda off: pl.BlockSpec((seq, wd), lambda b, p: (b, off * npairs + p))
    wcol = lambda off: pl.BlockSpec((GDN_CONV, wd), lambda b, p: (0, off * npairs + p))
    per_head = lambda rows, dt: pltpu.VMEM((np_, rows, hd), dt)
    return pl.pallas_call(
        _gdn_body,
        grid=(bsz, npairs),
        in_specs=[col(0), col(1), col(2), col(3), wcol(0), wcol(1), wcol(2),
                  pl.BlockSpec((nck, 8 * np_, GDN_BLOCK), lambda b, p: (b, p, 0)),
                  pl.BlockSpec((np_, 8, LANES), lambda b, p: (p, 0, 0)),
                  pl.BlockSpec((1, hd), lambda b, p: (0, 0))],
        out_specs=pl.BlockSpec((seq, wd), lambda b, p: (b, p)),
        out_shape=jax.ShapeDtypeStruct((t, nh * hd), BF16),
        scratch_shapes=([per_head(seq, F32)] * 4
                        + [per_head(nck * 8, F32)] * 3
                        + [per_head(seq, F32)] * 2
                        + [per_head(2 * seq, BF16)] * 2
                        + [per_head(seq, BF16)] * 2
                        + [per_head(seq, BF16)] * 2
                        + [per_head(nck, F32)] * 2
                        + [pltpu.VMEM((seq + 16, hd), F32)]),
        compiler_params=_cparams("parallel", "parallel"),
        name="gdn",
    )(qkvz, qkvz, qkvz, qkvz, conv_w, conv_w, conv_w, gates3, head_par, norm_w)


def _kv_body(m_ref, g_ref, wk_ref, wv_ref, k_ref, v_ref):
    mn = _rms(m_ref[...], g_ref[...]).astype(BF16)
    k_ref[...] = _dot(mn, wk_ref[...]).astype(BF16)
    v_ref[...] = _dot(mn, wv_ref[...]).astype(BF16)


def _mem_kv(mem2d, norm_w, wk, wv, tm=512):
    r, d = mem2d.shape
    full = lambda shape: pl.BlockSpec(shape, lambda i: (0,) * len(shape))
    tile = pl.BlockSpec((tm, d), lambda i: (i, 0))
    return pl.pallas_call(
        _kv_body,
        grid=(r // tm,),
        in_specs=[tile, full((1, d)), full(wk.shape), full(wv.shape)],
        out_specs=[tile, tile],
        out_shape=[jax.ShapeDtypeStruct((r, d), BF16)] * 2,
        compiler_params=_cparams("parallel"),
        name="mem_kv",
    )(mem2d, norm_w, wk, wv)


def _mix_xattn_body(x_ref, y5_ref, yg_ref, wmix_ref, g_ref, wq_ref, k_ref, v_ref, wo_ref,
                    gm_ref, whi_ref, wlo_ref, br_ref, tri_ref,
                    o_ref, mi_ref, mf_ref, cnt_ref, wmix_b, wq_b, wo_b, carry):
    @pl.when(pl.program_id(0) == 0)
    def _():
        wmix_b[...] = wmix_ref[...].astype(BF16)
        wq_b[...] = wq_ref[...].astype(BF16)
        wo_b[...] = wo_ref[...].astype(BF16)
        carry[...] = jnp.zeros_like(carry)

    x1 = (x_ref[...] + _dot(y5_ref[...], wmix_b[:S5_WIDTH, :]) + _dot(yg_ref[...], wmix_b[S5_WIDTH:, :]))
    xn = _rms(x1, g_ref[...]).astype(BF16)
    q = (_dot(xn, wq_b[...]) * (XA_HEAD_DIM ** -0.5)).astype(BF16)
    hsl = [slice(h * XA_HEAD_DIM, (h + 1) * XA_HEAD_DIM) for h in range(XA_HEADS)]
    sc = [_dot_nt(q[:, sl], k_ref[:, sl]) for sl in hsl]
    pr = [jnp.exp(s - jnp.max(s, axis=-1, keepdims=True)) for s in sc]
    pr = [p / jnp.sum(p, axis=-1, keepdims=True) for p in pr]
    heads = [_dot(p.astype(BF16), v_ref[:, sl]).astype(BF16) for p, sl in zip(pr, hsl)]
    x2 = x1 + _dot(jnp.concatenate(heads, axis=1), wo_b[...])
    o_ref[...] = x2
    mi, mf = _route(_rms(x2, gm_ref[...]), whi_ref[...], wlo_ref[...], br_ref[...], tri_ref[...], carry)
    mi_ref[...] = mi
    mf_ref[...] = mf
    cnt_ref[...] = carry[...]


def _mix_xattn(x2d, y5, yg, w_mix, norm_w, wq, kmem, vmem, wo, norm_moe, w_route, b_route, seq, mem_len, tm=1024):
    t, d = x2d.shape
    tm = math.gcd(tm, seq)
    per_b = seq // tm
    tri = jnp.triu(jnp.ones((tm, tm), BF16))
    w_hi = w_route.astype(BF16)
    w_lo = (w_route - w_hi.astype(F32)).astype(BF16)
    full = lambda shape: pl.BlockSpec(shape, lambda i: (0,) * len(shape), pipeline_mode=pl.Buffered(1))
    tile = lambda w: pl.BlockSpec((tm, w), lambda i: (i, 0))
    return pl.pallas_call(
        _mix_xattn_body,
        grid=(t // tm,),
        in_specs=[tile(d), tile(S5_WIDTH), tile(GDN_WIDTH),
                  full(w_mix.shape), full((1, d)), full(wq.shape),
                  pl.BlockSpec((mem_len, d), lambda i: (i // per_b, 0)),
                  pl.BlockSpec((mem_len, d), lambda i: (i // per_b, 0)),
                  full(wo.shape),
                  full((1, d)), full(w_route.shape), full(w_route.shape), full(b_route.shape), full((tm, tm))],
        out_specs=[tile(d), pl.BlockSpec((8, tm), lambda i: (0, i)), tile(LANES),
                   pl.BlockSpec((ROUTE_ROWS, LANES), lambda i: (0, 0))],
        out_shape=[jax.ShapeDtypeStruct((t, d), F32),
                   jax.ShapeDtypeStruct((8, t), I32),
                   jax.ShapeDtypeStruct((t, LANES), F32),
                   jax.ShapeDtypeStruct((ROUTE_ROWS, LANES), F32)],
        scratch_shapes=[pltpu.VMEM(w_mix.shape, BF16), pltpu.VMEM(wq.shape, BF16), pltpu.VMEM(wo.shape, BF16),
                        pltpu.VMEM((ROUTE_ROWS, LANES), F32)],
        compiler_params=_cparams("arbitrary"),
        name="mix_xattn",
    )(x2d, y5, yg, w_mix, norm_w, wq, kmem, vmem, wo, norm_moe, w_hi, w_lo, b_route, tri)


ROUTE_EXPERT_LANE0 = 4


ROUTE_ROWS = 40


def _route(xn, w_hi, w_lo, bias, tri, carry):
    x_hi = xn.astype(BF16)
    x_lo = (xn - x_hi.astype(F32)).astype(BF16)
    logits = (_dot_nt(w_hi, x_hi) + _dot_nt(w_hi, x_lo) + _dot_nt(w_lo, x_hi))[:ROUTE_ROWS] + bias[:ROUTE_ROWS, 0:1]
    tm = logits.shape[1]
    row = lax.broadcasted_iota(I32, (ROUTE_ROWS, tm), 0)
    neg = jnp.float32(-jnp.inf)
    big = jnp.int32(LANES)

    def top(vals):
        m = jnp.max(vals, axis=0, keepdims=True)
        idx = jnp.min(jnp.where(vals == m, row, big), axis=0, keepdims=True)
        return m, idx

    is_g = row < MOE_GROUPS
    gl = jnp.where(is_g, logits, neg)
    gmax, gidx = top(gl)
    p_top = 1.0 / jnp.sum(jnp.where(is_g, jnp.exp(gl - gmax), 0.0), axis=0, keepdims=True)
    erow = row - ROUTE_EXPERT_LANE0
    in_grp = jnp.logical_and(jnp.logical_and(erow >= 0, erow < MOE_EXPERTS), (erow // MOE_PER_GROUP) == gidx)
    es = jnp.where(in_grp, logits, neg)
    m1, i1 = top(es)
    m2, i2 = top(jnp.where(row == i1, neg, es))
    e21 = jnp.exp(m2 - m1)
    w1 = p_top / (1.0 + e21)
    w2 = p_top * e21 / (1.0 + e21)

    a1 = (row == i1).astype(F32)
    a2 = (row == i2).astype(F32)
    both = a1 + a2
    before = _dot(both.astype(BF16), tri) - both + carry[:, 0:1]
    r1 = jnp.sum(a1 * before, axis=0, keepdims=True).astype(I32)
    r2 = jnp.sum(a2 * before, axis=0, keepdims=True).astype(I32)
    carry[...] = carry[...] + jnp.sum(both, axis=1, keepdims=True)
    row8 = lax.broadcasted_iota(I32, (8, tm), 0)
    mi = jnp.where(row8 == 0, i1 - ROUTE_EXPERT_LANE0,
                   jnp.where(row8 == 1, i2 - ROUTE_EXPERT_LANE0, jnp.where(row8 == 2, r1, jnp.where(row8 == 3, r2, 0))))
    rowl = lax.broadcasted_iota(I32, (LANES, tm), 0)
    wt = jnp.where(rowl == 0, w1, jnp.where(rowl == 1, w2, 0.0))
    mf = jnp.concatenate([wt[:, j * LANES:(j + 1) * LANES].T for j in range(tm // LANES)], axis=0)
    return mi, mf


def _dest_body(mi_ref, off_ref, da_ref, db_ref):
    tm = da_ref.shape[2]
    mi = mi_ref[...]
    n = mi.shape[1]
    row = lax.broadcasted_iota(I32, (MOE_EXPERTS, n), 0)
    off = off_ref[:, 0:1]
    d0 = jnp.sum(jnp.where(row == mi[0:1], off, 0), axis=0, keepdims=True) + mi[2:3]
    d1 = jnp.sum(jnp.where(row == mi[1:2], off, 0), axis=0, keepdims=True) + mi[3:4]
    for s in range(da_ref.shape[0]):
        da_ref[s] = d0[:, s * tm:(s + 1) * tm]
        db_ref[s] = d1[:, s * tm:(s + 1) * tm]


def _dest_rows(mi, offsets_col, tm):
    t = mi.shape[1]
    tiles_per_step = math.gcd(8, t // tm)
    out = pl.BlockSpec((tiles_per_step, 1, tm), lambda i: (i, 0, 0))
    return pl.pallas_call(
        _dest_body,
        grid=(t // (tm * tiles_per_step),),
        in_specs=[pl.BlockSpec((8, tm * tiles_per_step), lambda i: (0, i)),
                  pl.BlockSpec((MOE_EXPERTS, LANES), lambda i: (0, 0))],
        out_specs=[out, out],
        out_shape=[jax.ShapeDtypeStruct((t // tm, 1, tm), I32)] * 2,
        compiler_params=_cparams("parallel"),
        name="moe_dest",
    )(mi, offsets_col)


def _dispatch_body(zs_ref, na_ref, da_ref, db_ref, xn_ref, xs_ref, zbuf, sem, zsem):
    tm = xn_ref.shape[0]
    tz = zbuf.shape[0]
    n_blocks = xs_ref.shape[0] // tz

    @pl.when(pl.program_id(0) == 0)
    def _():
        zbuf[...] = jnp.zeros_like(zbuf)

        def fill(row0):
            return pltpu.make_async_copy(zbuf, xs_ref.at[pl.ds(pl.multiple_of(row0, tz), tz), :], zsem)

        def tail(e, c):
            @pl.when(zs_ref[e] >= 0)
            def _():
                fill(zs_ref[e]).start()
            return c

        def unused(j, c):
            fill(j * tz).start()
            return c

        def drain(j, c):
            fill(0).wait()
            return c

        lax.fori_loop(0, zs_ref.shape[0], tail, 0)
        lax.fori_loop(na_ref[0], n_blocks, unused, 0)
        lax.fori_loop(0, na_ref[1] + n_blocks - na_ref[0], drain, 0)

    def start(r, c):
        for k, d_ref in enumerate((da_ref, db_ref)):
            pltpu.make_async_copy(xn_ref.at[pl.ds(r, 1), :], xs_ref.at[pl.ds(d_ref[0, r], 1), :],
                                  sem.at[k]).start(priority=k)
        return c

    lax.fori_loop(0, tm, start, 0, unroll=8)
    for k in range(MOE_TOPK):
        pltpu.make_async_copy(xn_ref, xs_ref.at[pl.ds(0, tm), :], sem.at[k]).wait()


def _dispatch(zero_start, n_active, dest_a, dest_b, xn, n_rows, tm, te):
    t, dw = xn.shape
    smem_row = pl.BlockSpec((None, 1, tm), lambda i, zs, na: (i, 0, 0), memory_space=pltpu.SMEM)
    grid_spec = pltpu.PrefetchScalarGridSpec(
        num_scalar_prefetch=2,
        grid=(t // tm,),
        in_specs=[smem_row, smem_row, pl.BlockSpec((tm, dw), lambda i, zs, na: (i, 0))],
        out_specs=pl.BlockSpec(memory_space=pl.ANY),
        scratch_shapes=[pltpu.VMEM((te, dw), xn.dtype), pltpu.SemaphoreType.DMA((MOE_TOPK,)),
                        pltpu.SemaphoreType.DMA(())],
    )
    return pl.pallas_call(
        _dispatch_body,
        grid_spec=grid_spec,
        out_shape=jax.ShapeDtypeStruct((n_rows, dw), xn.dtype),
        compiler_params=_cparams("arbitrary"),
        name="moe_dispatch",
    )(zero_start, n_active, dest_a, dest_b, xn)


def _experts_body(te_ref, na_ref, x_ref, g_ref, wg_ref, wu_ref, wd_ref, y_ref, wg_b, wu_b, wd_b):
    i = pl.program_id(0)

    @pl.when(i < na_ref[0])
    def _():
        @pl.when(jnp.logical_or(i == 0, te_ref[i] != te_ref[jnp.maximum(i - 1, 0)]))
        def _():
            wg_b[...] = wg_ref[...].astype(BF16)
            wu_b[...] = wu_ref[...].astype(BF16)
            wd_b[...] = wd_ref[...].astype(BF16)

        x = _rms(x_ref[...], g_ref[...]).astype(BF16)
        gt = _dot(x, wg_b[...])
        up = _dot(x, wu_b[...])
        hid = (gt * jax.nn.sigmoid(gt) * up).astype(BF16)
        y_ref[...] = _dot(hid, wd_b[...])

    @pl.when(i >= na_ref[0])
    def _():
        y_ref[...] = jnp.zeros_like(y_ref)


def _experts(tile_expert, n_active, xs, norm_w, w_gate, w_up, w_down, tm):
    r, d = xs.shape
    f = w_gate.shape[2]
    row_tile = lambda i, te, na: (jnp.minimum(i, na[0] - 1), 0)
    grid_spec = pltpu.PrefetchScalarGridSpec(
        num_scalar_prefetch=2,
        grid=(r // tm,),
        in_specs=[pl.BlockSpec((tm, d), row_tile),
                  pl.BlockSpec((1, d), lambda i, te, na: (0, 0)),
                  pl.BlockSpec((None, d, f), lambda i, te, na: (te[i], 0, 0)),
                  pl.BlockSpec((None, d, f), lambda i, te, na: (te[i], 0, 0)),
                  pl.BlockSpec((None, f, d), lambda i, te, na: (te[i], 0, 0))],
        out_specs=pl.BlockSpec((tm, d), lambda i, te, na: (i, 0)),
        scratch_shapes=[pltpu.VMEM((d, f), BF16), pltpu.VMEM((d, f), BF16), pltpu.VMEM((f, d), BF16)],
    )
    return pl.pallas_call(
        _experts_body,
        grid_spec=grid_spec,
        out_shape=jax.ShapeDtypeStruct((r, d), xs.dtype),
        compiler_params=_cparams("arbitrary"),
        name="moe_experts",
    )(tile_expert, n_active, xs, norm_w, w_gate, w_up, w_down)


def _combine_body(da_ref, db_ref, na_ref, nb_ref, x_ref, mf_ref, g_ref, ys_ref, o_ref, buf, sem):
    tm = x_ref.shape[0]
    i = pl.program_id(0)
    slot = lax.rem(i, 2)

    def gather(d_refs, s):
        def start(r, c):
            for k, d_ref in enumerate(d_refs):
                pltpu.make_async_copy(ys_ref.at[pl.ds(d_ref[0, r], 1), :],
                                      buf.at[s, k, pl.ds(r, 1), :], sem.at[s, k]).start(priority=k)
            return c

        lax.fori_loop(0, tm, start, 0, unroll=8)

    @pl.when(i == 0)
    def _():
        gather((da_ref, db_ref), 0)

    @pl.when(i + 1 < pl.num_programs(0))
    def _():
        gather((na_ref, nb_ref), 1 - slot)

    for k in range(MOE_TOPK):
        pltpu.make_async_copy(ys_ref.at[pl.ds(0, tm), :], buf.at[slot, k], sem.at[slot, k]).wait()
    mf = mf_ref[...]
    y = x_ref[...] + mf[:, 0:1] * buf[slot, 0] + mf[:, 1:2] * buf[slot, 1]
    o_ref[...] = _rms(y, g_ref[...])


def _combine(dest_a, dest_b, x2d, mf, norm_w, ys, tm):
    t, d = x2d.shape
    last = t // tm - 1
    cur = pl.BlockSpec((None, 1, tm), lambda i: (i, 0, 0), memory_space=pltpu.SMEM)
    nxt = pl.BlockSpec((None, 1, tm), lambda i: (jnp.minimum(i + 1, last), 0, 0), memory_space=pltpu.SMEM)
    return pl.pallas_call(
        _combine_body,
        grid=(t // tm,),
        in_specs=[cur, cur, nxt, nxt,
                  pl.BlockSpec((tm, d), lambda i: (i, 0)),
                  pl.BlockSpec((tm, LANES), lambda i: (i, 0)),
                  pl.BlockSpec((1, d), lambda i: (0, 0)),
                  pl.BlockSpec(memory_space=pl.ANY)],
        out_specs=pl.BlockSpec((tm, d), lambda i: (i, 0)),
        out_shape=jax.ShapeDtypeStruct((t, d), F32),
        scratch_shapes=[pltpu.VMEM((2, MOE_TOPK, tm, ys.shape[1]), ys.dtype), pltpu.SemaphoreType.DMA((2, MOE_TOPK))],
        compiler_params=_cparams("arbitrary"),
        name="moe_combine",
    )(dest_a, dest_b, dest_a, dest_b, x2d, mf, norm_w, ys)


MOE_ROW_TILE = 512
MOE_TOKEN_TILE = 1024
MOE_ZERO_BLOCK = 256


def _moe(x2d, mi, mf, cnt, norm_w, w_gate, w_up, w_down, norm_final):
    t, d = x2d.shape
    tm = MOE_ROW_TILE
    tok = MOE_TOKEN_TILE
    counts = cnt[ROUTE_EXPERT_LANE0:ROUTE_EXPERT_LANE0 + MOE_EXPERTS, 0].astype(I32)
    padded = ((counts + tm - 1) // tm) * tm
    ends = jnp.cumsum(padded)
    offsets = ends - padded
    n_tiles = (MOE_TOPK * t + MOE_EXPERTS * (tm - 1)) // tm
    tile_start = jnp.arange(n_tiles, dtype=I32) * tm
    tile_expert = jnp.minimum(jnp.sum((ends[None, :] <= tile_start[:, None]).astype(I32), axis=1), MOE_EXPERTS - 1)
    n_info = jnp.stack([ends[-1] // tm, jnp.sum((counts > 0).astype(I32))]).astype(I32)
    tz = MOE_ZERO_BLOCK
    pad = padded - counts
    zero_start = jnp.concatenate([jnp.where(pad > j * tz, ends - (j + 1) * tz, -1) for j in range(tm // tz)])
    zero_start = jnp.where(jnp.tile(counts, tm // tz) > 0, zero_start, -1).astype(I32)
    z_info = jnp.stack([ends[-1] // tz, jnp.sum((zero_start >= 0).astype(I32))]).astype(I32)
    offsets_col = jnp.broadcast_to(offsets[:, None], (MOE_EXPERTS, LANES))
    dest_a, dest_b = _dest_rows(mi, offsets_col, tok)
    xs = _dispatch(zero_start, z_info, dest_a, dest_b, x2d, n_tiles * tm, tok, tz)
    ys = _experts(tile_expert, n_info, xs, norm_w, w_gate, w_up, w_down, tm)
    return _combine(dest_a, dest_b, x2d, mf, norm_final, ys, tok)


def kernel(x, mem, norm_mix, w_in, w_out,
           s5_lam_re_f, s5_lam_im_f, s5_log_step_f, s5_b_re_f, s5_b_im_f, s5_c_re_f, s5_c_im_f,
           s5_lam_re_b, s5_lam_im_b, s5_log_step_b, s5_b_re_b, s5_b_im_b, s5_c_re_b, s5_c_im_b,
           s5_d, s5_w_glu, s5_b_glu, s5_norm,
           gdn_conv, gdn_a_log_f, gdn_dt_bias_f, gdn_a_log_b, gdn_dt_bias_b, gdn_norm,
           norm_xattn, norm_mem, xa_wq, xa_wk, xa_wv, xa_wo,
           norm_moe, router_group_w, router_group_b, router_expert_w, router_expert_b,
           moe_w_gate, moe_w_up, moe_w_down, norm_final):
    bsz, seq, d = x.shape
    t = bsz * seq
    l = 0
    x2d = x.reshape(t, d)
    wi = w_in[l]
    wg = wi[:, S5_WIDTH + 4 * GDN_WIDTH:].reshape(d, 4, GDN_HEADS)
    wg = jnp.pad(jnp.swapaxes(wg, 1, 2), ((0, 0), (0, 0), (0, 4))).reshape(d, GDN_HEADS * 8)
    wgt = wg.T
    u3, qkvz, gates = _in_proj(x2d, norm_mix[l][None], wi, wgt)
    s5p = dict(lam_re_f=s5_lam_re_f[l], lam_im_f=s5_lam_im_f[l], log_step_f=s5_log_step_f[l],
               b_re_f=s5_b_re_f[l], b_im_f=s5_b_im_f[l], c_re_f=s5_c_re_f[l], c_im_f=s5_c_im_f[l],
               lam_re_b=s5_lam_re_b[l], lam_im_b=s5_lam_im_b[l], log_step_b=s5_log_step_b[l],
               b_re_b=s5_b_re_b[l], b_im_b=s5_b_im_b[l], c_re_b=s5_c_re_b[l], c_im_b=s5_c_im_b[l], d=s5_d[l])
    g3 = _s5_scan(u3, s5p, seq // S5_CHUNK)
    y_s5 = _s5_post(g3, s5_w_glu[l].T.astype(BF16), s5_b_glu[l][:, None], s5_norm[l][:, None])

    head_par = jnp.stack([gdn_a_log_f[l], gdn_dt_bias_f[l], gdn_a_log_b[l], gdn_dt_bias_b[l]], axis=1)
    head_par = jnp.broadcast_to(jnp.pad(head_par, ((0, 0), (0, 4)))[:, :, None], (GDN_HEADS, 8, LANES))
    y_gdn = _gdn(qkvz, gates, gdn_conv[l], head_par, gdn_norm[l][None], bsz, seq)

    mem_len = mem.shape[1]
    kmem, vmem = _mem_kv(mem.reshape(bsz * mem_len, d), norm_mem[l][None],
                         xa_wk[l].astype(BF16), xa_wv[l].astype(BF16))
    n_pad = LANES - MOE_GROUPS - MOE_EXPERTS
    w_route = jnp.pad(jnp.concatenate([router_group_w[l], router_expert_w[l]], axis=1), ((0, 0), (0, n_pad))).T
    b_route = jnp.pad(jnp.concatenate([router_group_b[l], router_expert_b[l]]), (0, n_pad))
    b_route = jnp.broadcast_to(b_route[:, None], (LANES, LANES))
    x2, mi, mf, cnt = _mix_xattn(x2d, y_s5, y_gdn, w_out[l], norm_xattn[l][None], xa_wq[l], kmem, vmem, xa_wo[l],
                                 norm_moe[l][None], w_route, b_route, seq, mem_len)
    y = _moe(x2, mi, mf, cnt, norm_moe[l][None], moe_w_gate[l], moe_w_up[l], moe_w_down[l], norm_final[None])
    return y.reshape(bsz, seq, d)
```
